```python
import math
import jax
import jax.numpy as jnp
from jax import lax
import numpy as np

D_MODEL = 1024
BATCH = 32
SEQ = 256
DEPTH = 2
DEC_BATCH = 4
DEC_SEQ = 1024
PAST_LEN = 256

GRID_W = 64
N_EVEN = (DEPTH + 1) // 2
N_ODD = DEPTH // 2
HEAD_DIM = 64
H_A = 8
QK_NOPE = 64
QK_ROPE = 32
V_A = 64
Q_LORA = 256
KV_LORA = 128
H_B = 4
DH_B = HEAD_DIM
H_C = 8
KV_C = 2
DH_C = HEAD_DIM
H_D = 8
DH_D = HEAD_DIM
NA_WIN_ROWS = 8
NA_WIN_COLS = 16
MIX_EVEN = H_A * V_A + H_B * 2 * DH_B
MIX_ODD = H_C * DH_C + H_D * DH_D
EVEN_SIZES = (Q_LORA, KV_LORA, QK_ROPE, H_B * 2 * DH_B, H_B * 2 * DH_B, H_B * 2 * DH_B)
ODD_SIZES = (H_C * DH_C, KV_C * DH_C, KV_C * DH_C, H_D * DH_D, H_D * DH_D, H_D * DH_D)
D_FF = 2816
CONV_WIDTH = 3
Q_BLOCK = 128
ROPE_THETA = 10000.0
EPS = 1e-6
NEG_INF = -1e30

kernel_name = "hybrid_mla_diff_gqa_natten_dit_step"


def rmsnorm(x, w):
    xf = x.astype(jnp.float32)
    y = xf * lax.rsqrt(jnp.mean(xf * xf, axis=-1, keepdims=True) + EPS)
    return (y * w.astype(jnp.float32)).astype(x.dtype)


def split_cols(z, sizes):
    return jnp.split(z, np.cumsum(sizes)[:-1].tolist(), axis=-1)


def adaln(cvec, w, b):
    m = jax.nn.silu(cvec) @ w + b
    return jnp.split(m, 6, axis=-1)


def modulate(x, shift, scale):
    return x * (1.0 + scale[:, None, :]) + shift[:, None, :]


def axial_rope_table(n_tokens, rot_dim):
    t = np.arange(n_tokens)
    n_freq = rot_dim // 4
    inv = 1.0 / (ROPE_THETA ** (np.arange(n_freq) / n_freq))
    ang = np.concatenate([(t // GRID_W)[:, None] * inv[None, :],
                          (t % GRID_W)[:, None] * inv[None, :]], axis=-1)
    return jnp.asarray(np.cos(ang), jnp.float32), jnp.asarray(np.sin(ang), jnp.float32)


def apply_rope(x, cos, sin):
    shape = (cos.shape[0],) + (1,) * (x.ndim - 3) + (cos.shape[1],)
    cos = cos.reshape(shape).astype(x.dtype)
    sin = sin.reshape(shape).astype(x.dtype)
    half = x.shape[-1] // 2
    x1, x2 = x[..., :half], x[..., half:]
    return jnp.concatenate([x1 * cos - x2 * sin, x1 * sin + x2 * cos], axis=-1)


def map_query_blocks(fn, q):
    b, s = q.shape[0], q.shape[1]
    nb = s // Q_BLOCK
    qb = jnp.moveaxis(q.reshape((b, nb, Q_BLOCK) + q.shape[2:]), 1, 0)
    out = jnp.moveaxis(lax.map(fn, qb), 0, 1)
    return out.reshape((b, s) + out.shape[3:])


def mla_attend(q, k_nope, k_pe, v):
    scale = (QK_NOPE + QK_ROPE) ** -0.5
    def block(qb):
        s = (jnp.einsum('bqhd,bthd->bhqt', qb[..., :QK_NOPE], k_nope)
             + jnp.einsum('bqhr,btr->bhqt', qb[..., QK_NOPE:], k_pe)).astype(jnp.float32) * scale
        p = jax.nn.softmax(s, axis=-1).astype(v.dtype)
        return jnp.einsum('bhqt,bthd->bqhd', p, v)
    return map_query_blocks(block, q)


def diff_attend(q, k, v, lam):
    scale = q.shape[-1] ** -0.5
    def block(qb):
        s = jnp.einsum('bqhcd,bthcd->bhcqt', qb, k).astype(jnp.float32) * scale
        p = jax.nn.softmax(s, axis=-1)
        a = (p[:, :, 0] - lam * p[:, :, 1]).astype(v.dtype)
        return jnp.einsum('bhqt,bthd->bqhd', a, v)
    return map_query_blocks(block, q)


def gqa_attend(q, k, v):
    scale = q.shape[-1] ** -0.5
    def block(qb):
        s = jnp.einsum('bqkgd,btkd->bkgqt', qb, k).astype(jnp.float32) * scale
        p = jax.nn.softmax(s, axis=-1).astype(v.dtype)
        return jnp.einsum('bkgqt,btkd->bqkgd', p, v)
    return map_query_blocks(block, q)


def neighbourhood_attend(q, k, v, k_ctx, v_ctx, rpb):
    b, s, h, dh = q.shape
    rows = s // GRID_W
    kr = min(NA_WIN_ROWS, rows)
    kc = NA_WIN_COLS
    r = np.arange(rows)
    rs = np.clip(r - kr // 2, 0, rows - kr)
    band = rs[:, None] + np.arange(kr)[None, :]
    col = np.arange(GRID_W)
    cs = np.clip(col - kc // 2, 0, GRID_W - kc)
    col_ok = (col[None, :] >= cs[:, None]) & (col[None, :] < cs[:, None] + kc)
    dr = band - r[:, None] + NA_WIN_ROWS - 1
    dc = np.clip(col[None, :] - col[:, None] + kc - 1, 0, 2 * kc - 2)
    bias = rpb[:, dr[:, None, :, None], dc[None, :, None, :]].astype(jnp.float32)
    scale = dh ** -0.5
    qg = q.reshape(b, rows, GRID_W, h, dh)
    kg = k.reshape(b, rows, GRID_W, h, dh)[:, band]
    vg = v.reshape(b, rows, GRID_W, h, dh)[:, band]
    s_loc = jnp.einsum('brqhd,brkwhd->bhrqkw', qg, kg).astype(jnp.float32) * scale + bias[None]
    s_loc = jnp.where(col_ok[None, None, None, :, None, :], s_loc, NEG_INF)
    s_ctx = jnp.einsum('brqhd,bthd->bhrqt', qg, k_ctx).astype(jnp.float32) * scale
    n_loc = kr * GRID_W
    sc = jnp.concatenate([s_loc.reshape(b, h, rows, GRID_W, n_loc), s_ctx], axis=-1)
    p = jax.nn.softmax(sc, axis=-1).astype(v.dtype)
    p_loc = p[..., :n_loc].reshape(b, h, rows, GRID_W, kr, GRID_W)
    p_ctx = p[..., n_loc:]
    out = (jnp.einsum('bhrqkw,brkwhd->brqhd', p_loc, vg)
           + jnp.einsum('bhrqt,bthd->brqhd', p_ctx, v_ctx))
    return out.reshape(b, s, h, dh)


def even_mixer(u, w_in, w_out, w_uq, q_norm_w, kv_norm_w, w_uk, w_uv, diff_lam, diff_subln_w,
               lam_init, ctx=None, rope=None):
    b, s, _ = u.shape
    cq, ckv, kpe, qd, kd, vd = split_cols(u @ w_in, EVEN_SIZES)
    q = (rmsnorm(cq, q_norm_w) @ w_uq).reshape(b, s, H_A, QK_NOPE + QK_ROPE)
    ckv = rmsnorm(ckv, kv_norm_w)
    qd = qd.reshape(b, s, H_B, 2, DH_B)
    kd = kd.reshape(b, s, H_B, 2 * DH_B)
    vd = vd.reshape(b, s, H_B, 2 * DH_B)
    state = (ckv, kpe, kd, vd)
    if ctx is None:
        ckv_all, kpe_all, kd_all, vd_all = state
    else:
        cos_a, sin_a, cos_h, sin_h = rope
        q = jnp.concatenate([q[..., :QK_NOPE], apply_rope(q[..., QK_NOPE:], cos_a, sin_a)], axis=-1)
        qd = apply_rope(qd, cos_h, sin_h)
        kd_rot = apply_rope(kd.reshape(b, s, H_B, 2, DH_B), cos_h, sin_h).reshape(b, s, H_B, 2 * DH_B)
        ckv_all = jnp.concatenate([ckv, ctx[0]], axis=1)
        kpe_all = jnp.concatenate([apply_rope(kpe, cos_a, sin_a), ctx[1]], axis=1)
        kd_all = jnp.concatenate([kd_rot, ctx[2]], axis=1)
        vd_all = jnp.concatenate([vd, ctx[3]], axis=1)
    t = ckv_all.shape[1]
    k_nope = (ckv_all @ w_uk).reshape(b, t, H_A, QK_NOPE)
    v_a = (ckv_all @ w_uv).reshape(b, t, H_A, V_A)
    o_a = mla_attend(q, k_nope, kpe_all, v_a)
    lf = diff_lam.astype(jnp.float32)
    lam = jnp.exp(jnp.sum(lf[0] * lf[1])) - jnp.exp(jnp.sum(lf[2] * lf[3])) + lam_init
    o_b = diff_attend(qd, kd_all.reshape(b, t, H_B, 2, DH_B), vd_all, lam)
    o_b = rmsnorm(o_b, diff_subln_w) * (1.0 - lam_init)
    o = jnp.concatenate([o_a.reshape(b, s, H_A * V_A), o_b.reshape(b, s, H_B * 2 * DH_B)], axis=-1)
    return o @ w_out, state


def odd_mixer(u, w_in, w_out, qk_norm_w, rpb, ctx=None, rope=None):
    b, s, _ = u.shape
    qc, kc, vc, qn, kn, vn = split_cols(u @ w_in, ODD_SIZES)
    qc = rmsnorm(qc.reshape(b, s, H_C, DH_C), qk_norm_w[0])
    kc = rmsnorm(kc.reshape(b, s, KV_C, DH_C), qk_norm_w[1])
    vc = vc.reshape(b, s, KV_C, DH_C)
    qn = qn.reshape(b, s, H_D, DH_D)
    kn = kn.reshape(b, s, H_D, DH_D)
    vn = vn.reshape(b, s, H_D, DH_D)
    state = (kc, vc, kn, vn)
    if ctx is None:
        o_c = gqa_attend(qc.reshape(b, s, KV_C, H_C // KV_C, DH_C), kc, vc)
        o_d = gqa_attend(qn[:, :, :, None], kn, vn)[:, :, :, 0]
    else:
        cos_h, sin_h = rope
        qc = apply_rope(qc, cos_h, sin_h)
        k_all = jnp.concatenate([apply_rope(kc, cos_h, sin_h), ctx[0]], axis=1)
        v_all = jnp.concatenate([vc, ctx[1]], axis=1)
        o_c = gqa_attend(qc.reshape(b, s, KV_C, H_C // KV_C, DH_C), k_all, v_all)
        o_d = neighbourhood_attend(qn, kn, vn, ctx[2], ctx[3], rpb)
    o = jnp.concatenate([o_c.reshape(b, s, H_C * DH_C), o_d.reshape(b, s, H_D * DH_D)], axis=-1)
    return o @ w_out, state


def conv_ffn(u, w_up, conv_w, conv_b, w_down):
    z = u @ w_up
    zp = jnp.pad(z, ((0, 0), (1, 1), (0, 0)))
    z = zp[:, :-2] * conv_w[0] + zp[:, 1:-1] * conv_w[1] + zp[:, 2:] * conv_w[2] + conv_b
    gate, val = jnp.split(z, 2, axis=-1)
    return (jax.nn.silu(gate) * val) @ w_down


def setup_inputs(seed: int = 0) -> dict:
    key = jax.random.key(seed)
    ks = iter(jax.random.split(key, 40))

    def nrm(shape, scale=1.0):
        return jax.random.normal(next(ks), shape, jnp.float32) * scale

    def gain(shape):
        return 1.0 + nrm(shape, 0.05)

    return {
        "x_prompt": nrm((BATCH, SEQ, D_MODEL)),
        "x_sample": nrm((DEC_BATCH, DEC_SEQ, D_MODEL)),
        "c": nrm((DEC_BATCH, D_MODEL)),
        "cache_mla_ckv": nrm((DEC_BATCH, N_EVEN, PAST_LEN, KV_LORA)),
        "cache_mla_kpe": nrm((DEC_BATCH, N_EVEN, PAST_LEN, QK_ROPE)),
        "cache_diff_k": nrm((DEC_BATCH, N_EVEN, PAST_LEN, H_B, 2 * DH_B)),
        "cache_diff_v": nrm((DEC_BATCH, N_EVEN, PAST_LEN, H_B, 2 * DH_B)),
        "cache_gqa_k": nrm((DEC_BATCH, N_ODD, PAST_LEN, KV_C, DH_C)),
        "cache_gqa_v": nrm((DEC_BATCH, N_ODD, PAST_LEN, KV_C, DH_C)),
        "cache_na_k": nrm((DEC_BATCH, N_ODD, PAST_LEN, H_D, DH_D)),
        "cache_na_v": nrm((DEC_BATCH, N_ODD, PAST_LEN, H_D, DH_D)),
        "c_ctx": nrm((D_MODEL,)),
        "norm_w": gain((DEPTH, 4, D_MODEL)),
        "w_mod": nrm((DEPTH, D_MODEL, 6 * D_MODEL), 0.5 * D_MODEL ** -0.5),
        "b_mod": nrm((DEPTH, 6 * D_MODEL), 0.01),
        "w_in_even": nrm((N_EVEN, D_MODEL, sum(EVEN_SIZES)), D_MODEL ** -0.5),
        "w_out_even": nrm((N_EVEN, MIX_EVEN, D_MODEL), MIX_EVEN ** -0.5),
        "w_uq": nrm((N_EVEN, Q_LORA, H_A * (QK_NOPE + QK_ROPE)), Q_LORA ** -0.5),
        "q_norm_w": gain((N_EVEN, Q_LORA)),
        "kv_norm_w": gain((N_EVEN, KV_LORA)),
        "w_uk": nrm((N_EVEN, KV_LORA, H_A * QK_NOPE), KV_LORA ** -0.5),
        "w_uv": nrm((N_EVEN, KV_LORA, H_A * V_A), KV_LORA ** -0.5),
        "diff_lam": nrm((N_EVEN, 4, DH_B), 0.1),
        "diff_subln_w": gain((N_EVEN, 2 * DH_B)),
        "w_in_odd": nrm((N_ODD, D_MODEL, sum(ODD_SIZES)), D_MODEL ** -0.5),
        "w_out_odd": nrm((N_ODD, MIX_ODD, D_MODEL), MIX_ODD ** -0.5),
        "qk_norm_w": gain((N_ODD, 2, DH_C)),
        "na_rpb": nrm((N_ODD, H_D, 2 * NA_WIN_ROWS - 1, 2 * NA_WIN_COLS - 1), 0.1),
        "w_up": nrm((DEPTH, D_MODEL, 2 * D_FF), D_MODEL ** -0.5),
        "conv_w": nrm((DEPTH, CONV_WIDTH, 2 * D_FF), CONV_WIDTH ** -0.5),
        "conv_b": nrm((DEPTH, 2 * D_FF), 0.01),
        "w_down": nrm((DEPTH, D_FF, D_MODEL), D_FF ** -0.5),
    }


def reference(x_prompt, x_sample, c, cache_mla_ckv, cache_mla_kpe, cache_diff_k, cache_diff_v,
              cache_gqa_k, cache_gqa_v, cache_na_k, cache_na_v, c_ctx, norm_w, w_mod, b_mod,
              w_in_even, w_out_even, w_uq, q_norm_w, kv_norm_w, w_uk, w_uv, diff_lam, diff_subln_w,
              w_in_odd, w_out_odd, qk_norm_w, na_rpb, w_up, conv_w, conv_b, w_down):
    n_lat = x_sample.shape[1]
    cos_a, sin_a = axial_rope_table(n_lat, QK_ROPE)
    cos_h, sin_h = axial_rope_table(n_lat, HEAD_DIM)
    hp, hs = x_prompt, x_sample
    even_states, odd_states = [], []
    for l in range(DEPTH):
        mp = adaln(c_ctx[None, :], w_mod[l], b_mod[l])
        ms = adaln(c, w_mod[l], b_mod[l])
        up = modulate(rmsnorm(hp, norm_w[l, 0]), mp[0], mp[1])
        us = modulate(rmsnorm(hs, norm_w[l, 0]), ms[0], ms[1])
        i = l // 2
        if l % 2 == 0:
            lam_init = 0.8 - 0.6 * math.exp(-0.3 * l)
            args = (w_in_even[i], w_out_even[i], w_uq[i], q_norm_w[i], kv_norm_w[i], w_uk[i],
                    w_uv[i], diff_lam[i], diff_subln_w[i], lam_init)
            yp, st = even_mixer(up, *args)
            ys, _ = even_mixer(us, *args,
                               ctx=(cache_mla_ckv[:, i], cache_mla_kpe[:, i], cache_diff_k[:, i], cache_diff_v[:, i]),
                               rope=(cos_a, sin_a, cos_h, sin_h))
            even_states.append(st)
        else:
            args = (w_in_odd[i], w_out_odd[i], qk_norm_w[i], na_rpb[i])
            yp, st = odd_mixer(up, *args)
            ys, _ = odd_mixer(us, *args,
                              ctx=(cache_gqa_k[:, i], cache_gqa_v[:, i], cache_na_k[:, i], cache_na_v[:, i]),
                              rope=(cos_h, sin_h))
            odd_states.append(st)
        hp = hp + mp[2][:, None, :] * rmsnorm(yp, norm_w[l, 1])
        hs = hs + ms[2][:, None, :] * rmsnorm(ys, norm_w[l, 1])
        up = modulate(rmsnorm(hp, norm_w[l, 2]), mp[3], mp[4])
        us = modulate(rmsnorm(hs, norm_w[l, 2]), ms[3], ms[4])
        hp = hp + mp[5][:, None, :] * rmsnorm(conv_ffn(up, w_up[l], conv_w[l], conv_b[l], w_down[l]), norm_w[l, 3])
        hs = hs + ms[5][:, None, :] * rmsnorm(conv_ffn(us, w_up[l], conv_w[l], conv_b[l], w_down[l]), norm_w[l, 3])
    new_mla_ckv = jnp.stack([st[0] for st in even_states], axis=1)
    new_mla_kpe = jnp.stack([st[1] for st in even_states], axis=1)
    new_diff_k = jnp.stack([st[2] for st in even_states], axis=1)
    new_diff_v = jnp.stack([st[3] for st in even_states], axis=1)
    new_gqa_k = jnp.stack([st[0] for st in odd_states], axis=1)
    new_gqa_v = jnp.stack([st[1] for st in odd_states], axis=1)
    new_na_k = jnp.stack([st[2] for st in odd_states], axis=1)
    new_na_v = jnp.stack([st[3] for st in odd_states], axis=1)
    return (hp, hs, new_mla_ckv, new_mla_kpe, new_diff_k, new_diff_v, new_gqa_k, new_gqa_v, new_na_k, new_na_v)
```

```python
import functools
import math

import numpy as np
import jax
import jax.numpy as jnp
from jax import lax
from jax.experimental import pallas as pl
from jax.experimental.pallas import tpu as pltpu

D_MODEL = 1024
BATCH = 32
SEQ = 256
DEPTH = 2
DEC_BATCH = 4
DEC_SEQ = 1024
PAST_LEN = 256
GRID_W = 64
HEAD_DIM = 64
H_A = 8
QK_NOPE = 64
QK_ROPE = 32
V_A = 64
Q_LORA = 256
KV_LORA = 128
H_B = 4
DH_B = HEAD_DIM
H_C = 8
KV_C = 2
DH_C = HEAD_DIM
H_D = 8
DH_D = HEAD_DIM
NA_WIN_ROWS = 8
NA_WIN_COLS = 16
D_FF = 2816
ROPE_THETA = 10000.0
EPS = 1e-6
NEG_INF = -1e30

LANES = 128
MOD_ROWS = 8
ROW_TILE = 512
FFN_ROW_TILE = 1024
FF_CHUNK = 256
Q_TILE = 256
VMEM_LIMIT = 48 * 1024 * 1024

F32 = jnp.float32
BF16 = jnp.bfloat16


def _cparams(*sem):
    return pltpu.CompilerParams(dimension_semantics=sem, vmem_limit_bytes=VMEM_LIMIT)


def _dot(a, b):
    return jnp.dot(a, b, preferred_element_type=F32)


def _dot_nt(a, b):
    return lax.dot_general(a, b, (((1,), (1,)), ((), ())), preferred_element_type=F32)


def _rms(x, w):
    return x * lax.rsqrt(jnp.mean(x * x, axis=-1, keepdims=True) + EPS) * w


def _group_rms(x, w, gmat):
    x2 = x * x
    hi = x2.astype(BF16)
    lo = (x2 - hi.astype(F32)).astype(BF16)
    ms = _dot(hi, gmat) + _dot(lo, gmat)
    return x * lax.rsqrt(ms + EPS) * w


def _rope(x, cos, sin_signed, half):
    outs = []
    for j in range(x.shape[1] // LANES):
        xc = x[:, j * LANES:(j + 1) * LANES]
        lane = lax.broadcasted_iota(jnp.int32, xc.shape, 1)
        first = (lane % (2 * half)) < half
        partner = jnp.where(first, pltpu.roll(xc, LANES - half, 1), pltpu.roll(xc, half, 1))
        outs.append(xc * cos + partner * sin_signed)
    return outs[0] if len(outs) == 1 else jnp.concatenate(outs, axis=1)


def _softmax_pv(scores, values):
    m = scores[0].max(axis=-1, keepdims=True)
    for s in scores[1:]:
        m = jnp.maximum(m, s.max(axis=-1, keepdims=True))
    den = None
    acc = None
    for s, v in zip(scores, values):
        e = jnp.exp(s - m)
        d = e.sum(axis=-1, keepdims=True)
        o = _dot(e.astype(BF16), v)
        den = d if den is None else den + d
        acc = o if acc is None else acc + o
    return acc, den


def _mod_kernel(c_ref, w_ref, b_ref, o_ref):
    cv = c_ref[...]
    act = cv / (1.0 + jnp.exp(-cv))
    o_ref[...] = _dot(act.astype(BF16), w_ref[...].astype(BF16)) + b_ref[...]


def _modulation(cvecs, w_mod, b_mod):
    tn = 1024
    n = 6 * D_MODEL
    return pl.pallas_call(
        _mod_kernel,
        grid=(DEPTH, n // tn),
        in_specs=[
            pl.BlockSpec((MOD_ROWS, D_MODEL), lambda l, j: (0, 0)),
            pl.BlockSpec((None, D_MODEL, tn), lambda l, j: (l, 0, j)),
            pl.BlockSpec((None, 1, tn), lambda l, j: (l, 0, j)),
        ],
        out_specs=pl.BlockSpec((None, MOD_ROWS, tn), lambda l, j: (l, 0, j)),
        out_shape=jax.ShapeDtypeStruct((DEPTH, MOD_ROWS, n), F32),
        compiler_params=_cparams("parallel", "parallel"),
        name="adaln_mod",
    )(cvecs, w_mod, b_mod.reshape(DEPTH, 1, n))


def _mod_spec(layer, sample, tm):
    if sample:
        per = DEC_SEQ // tm
        return pl.BlockSpec((None, None, 6, D_MODEL), lambda i, *_: (layer, 1 + i // per, 0, 0))
    return pl.BlockSpec((None, None, 6, D_MODEL), lambda i, *_: (layer, 0, 0, 0))


def _full(shape):
    nd = len(shape)
    return pl.BlockSpec(shape, lambda *_: (0,) * nd)


def _rows(tm, width):
    return pl.BlockSpec((tm, width), lambda i, *_: (i, 0))


def _rope_spec(tm):
    per = DEC_SEQ // tm
    return pl.BlockSpec((tm, LANES), lambda i, *_: (i % per, 0))


EVEN_IN = 2048


def _proj_even_kernel(sample, *refs):
    if sample:
        (h_ref, mod_ref, nw_ref, win_ref, qnw_ref, kvnw_ref, wuqn_ref, wuqr_ref, wuk_ref, wuv_ref,
         ca_ref, sa_ref, ch_ref, sh_ref,
         qn_ref, qr_ref, kn_ref, va_ref, kpe_ref, qd_ref, kd_ref, vd_ref) = refs
    else:
        (h_ref, mod_ref, nw_ref, win_ref, qnw_ref, kvnw_ref, wuqn_ref, wuqr_ref, wuk_ref, wuv_ref,
         qn_ref, qr_ref, kn_ref, va_ref, kpe_ref, qd_ref, kd_ref, vd_ref,
         ckv_st_ref, kpe_st_ref, kd_st_ref, vd_st_ref) = refs
    mod = mod_ref[...]
    u = _rms(h_ref[...], nw_ref[...]) * (1.0 + mod[1:2]) + mod[0:1]
    z = _dot(u.astype(BF16), win_ref[...])
    cq = _rms(z[:, 0:256], qnw_ref[...]).astype(BF16)
    qn_ref[...] = _dot(cq, wuqn_ref[...]).astype(BF16)
    qr = _dot(cq, wuqr_ref[...])
    ckv = _rms(z[:, 256:384], kvnw_ref[...])
    ckv_b = ckv.astype(BF16)
    kn_ref[...] = _dot(ckv_b, wuk_ref[...]).astype(BF16)
    va_ref[...] = _dot(ckv_b, wuv_ref[...]).astype(BF16)
    kpe = z[:, 384:512]
    qd = z[:, 512:1024]
    kd = z[:, 1024:1536]
    vd = z[:, 1536:2048]
    vd_ref[...] = vd.astype(BF16)
    if sample:
        ca, sa, ch, sh = ca_ref[...], sa_ref[...], ch_ref[...], sh_ref[...]
        qr_ref[...] = _rope(qr, ca, sa, QK_ROPE // 2).astype(BF16)
        kpe_ref[...] = _rope(kpe, ca, sa, QK_ROPE // 2).astype(BF16)
        qd_ref[...] = _rope(qd, ch, sh, DH_B // 2).astype(BF16)
        kd_ref[...] = _rope(kd, ch, sh, DH_B // 2).astype(BF16)
    else:
        qr_ref[...] = qr.astype(BF16)
        kpe_ref[...] = kpe.astype(BF16)
        qd_ref[...] = qd.astype(BF16)
        kd_ref[...] = kd.astype(BF16)
        ckv_st_ref[...] = ckv
        kpe_st_ref[...] = kpe[:, :QK_ROPE]
        kd_st_ref[...] = kd
        vd_st_ref[...] = vd


def _proj_even(h, mod, layer, sample, nw, w, rope):
    n = h.shape[0]
    tm = ROW_TILE
    ins = [h, mod, nw, w["w_in"], w["q_norm_w"], w["kv_norm_w"], w["w_uq_n"], w["w_uq_r"], w["w_uk"], w["w_uv"]]
    in_specs = [_rows(tm, D_MODEL), _mod_spec(layer, sample, tm), _full((1, D_MODEL)), _full((D_MODEL, EVEN_IN)),
                _full((1, Q_LORA)), _full((1, KV_LORA)), _full((Q_LORA, 512)), _full((Q_LORA, 256)),
                _full((KV_LORA, 512)), _full((KV_LORA, 512))]
    widths = [512, 256, 512, 512, 128, 512, 512, 512]
    out_shape = [jax.ShapeDtypeStruct((n, wd), BF16) for wd in widths]
    out_specs = [_rows(tm, wd) for wd in widths]
    if sample:
        ins += [rope["ca"], rope["sa"], rope["ch"], rope["sh"]]
        in_specs += [_rope_spec(tm)] * 4
    else:
        for wd in (KV_LORA, QK_ROPE, 512, 512):
            out_shape.append(jax.ShapeDtypeStruct((n, wd), F32))
            out_specs.append(_rows(tm, wd))
    return pl.pallas_call(
        functools.partial(_proj_even_kernel, sample),
        grid=(n // tm,),
        in_specs=in_specs,
        out_specs=out_specs,
        out_shape=out_shape,
        compiler_params=_cparams("parallel"),
        name="proj_even_s" if sample else "proj_even_p",
    )(*ins)


def _attn_even_kernel(sample, lam_init, *refs):
    if sample:
        (qn_ref, qr_ref, kn_ref, va_ref, kpe_ref, qd_ref, kd_ref, vd_ref,
         cckv_ref, ckpe_ref, cdk_ref, cdv_ref, wuk_ref, wuv_ref, lam_ref, sub_ref, o_ref) = refs
    else:
        (qn_ref, qr_ref, kn_ref, va_ref, kpe_ref, qd_ref, kd_ref, vd_ref, lam_ref, sub_ref, o_ref) = refs
    lf = lam_ref[...]
    lam = (jnp.exp(jnp.sum(lf[0:1] * lf[1:2], axis=-1, keepdims=True))
           - jnp.exp(jnp.sum(lf[2:3] * lf[3:4], axis=-1, keepdims=True)) + lam_init)
    if sample:
        cckv = cckv_ref[...].astype(BF16)
        kn_ctx = _dot(cckv, wuk_ref[...]).astype(BF16)
        va_ctx = _dot(cckv, wuv_ref[...]).astype(BF16)
        kpe_ctx = ckpe_ref[...].astype(BF16)
        kd_ctx = cdk_ref[...].astype(BF16)
        vd_ctx = cdv_ref[...].astype(BF16)
    scale_a = (QK_NOPE + QK_ROPE) ** -0.5
    kpe = kpe_ref[:, 0:QK_ROPE]
    for hd in range(H_A):
        c0, c1 = hd * QK_NOPE, (hd + 1) * QK_NOPE
        qn = qn_ref[:, c0:c1]
        qr = qr_ref[:, hd * QK_ROPE:(hd + 1) * QK_ROPE]
        scores = [(_dot_nt(qn, kn_ref[:, c0:c1]) + _dot_nt(qr, kpe)) * scale_a]
        values = [va_ref[:, hd * V_A:(hd + 1) * V_A]]
        if sample:
            scores.append((_dot_nt(qn, kn_ctx[:, c0:c1]) + _dot_nt(qr, kpe_ctx)) * scale_a)
            values.append(va_ctx[:, hd * V_A:(hd + 1) * V_A])
        acc, den = _softmax_pv(scores, values)
        o_ref[:, hd * V_A:(hd + 1) * V_A] = (acc / den).astype(BF16)
    scale_b = DH_B ** -0.5
    base = H_A * V_A
    sub_w = sub_ref[...]
    for hd in range(H_B):
        g0 = hd * 2 * DH_B
        outs = []
        for comp in range(2):
            c0, c1 = g0 + comp * DH_B, g0 + (comp + 1) * DH_B
            q = qd_ref[:, c0:c1]
            scores = [_dot_nt(q, kd_ref[:, c0:c1]) * scale_b]
            values = [vd_ref[:, g0:g0 + 2 * DH_B]]
            if sample:
                scores.append(_dot_nt(q, kd_ctx[:, c0:c1]) * scale_b)
                values.append(vd_ctx[:, g0:g0 + 2 * DH_B])
            acc, den = _softmax_pv(scores, values)
            outs.append(acc / den)
        ob = outs[0] - lam * outs[1]
        ob = _rms(ob, sub_w) * (1.0 - lam_init)
        o_ref[:, base + g0:base + g0 + 2 * DH_B] = ob.astype(BF16)


def _attn_even(p, sample, lam_init, caches, w):
    qn, qr, kn, va, kpe, qd, kd, vd = p
    n = qn.shape[0]
    if sample:
        tq, per = Q_TILE, DEC_SEQ // Q_TILE
        grid = (DEC_BATCH, per)
        qspec = lambda wd: pl.BlockSpec((tq, wd), lambda b, i: (b * per + i, 0))
        kspec = lambda wd: pl.BlockSpec((DEC_SEQ, wd), lambda b, i: (b, 0))
        cspec = lambda wd: pl.BlockSpec((None, PAST_LEN, wd), lambda b, i: (b, 0, 0))
        ins = [qn, qr, kn, va, kpe, qd, kd, vd, *caches, w["w_uk"], w["w_uv"], w["diff_lam"], w["diff_subln_w"]]
        in_specs = [qspec(512), qspec(256), kspec(512), kspec(512), kspec(128), qspec(512), kspec(512), kspec(512),
                    cspec(KV_LORA), cspec(QK_ROPE), cspec(512), cspec(512),
                    _full((KV_LORA, 512)), _full((KV_LORA, 512)), _full((4, DH_B)), _full((1, 2 * DH_B))]
        out_spec = qspec(D_MODEL)
        sem = ("parallel", "parallel")
    else:
        grid = (BATCH,)
        spec = lambda wd: pl.BlockSpec((SEQ, wd), lambda b: (b, 0))
        ins = [qn, qr, kn, va, kpe, qd, kd, vd, w["diff_lam"], w["diff_subln_w"]]
        in_specs = [spec(512), spec(256), spec(512), spec(512), spec(128), spec(512), spec(512), spec(512),
                    _full((4, DH_B)), _full((1, 2 * DH_B))]
        out_spec = spec(D_MODEL)
        sem = ("parallel",)
    return pl.pallas_call(
        functools.partial(_attn_even_kernel, sample, lam_init),
        grid=grid,
        in_specs=in_specs,
        out_specs=out_spec,
        out_shape=jax.ShapeDtypeStruct((n, D_MODEL), BF16),
        compiler_params=_cparams(*sem),
        name="attn_even_s" if sample else "attn_even_p",
    )(*ins)


ODD_IN = 2304


def _proj_odd_kernel(sample, *refs):
    if sample:
        (h_ref, mod_ref, nw_ref, win_ref, qw_ref, kw_ref, gq_ref, gk_ref, ch_ref, sh_ref,
         qc_ref, kc_ref, vc_ref, qn_ref, kn_ref, vn_ref) = refs
    else:
        (h_ref, mod_ref, nw_ref, win_ref, qw_ref, kw_ref, gq_ref, gk_ref,
         qc_ref, kc_ref, vc_ref, qn_ref, kn_ref, vn_ref,
         kc_st_ref, vc_st_ref, kn_st_ref, vn_st_ref) = refs
    mod = mod_ref[...]
    u = _rms(h_ref[...], nw_ref[...]) * (1.0 + mod[1:2]) + mod[0:1]
    z = _dot(u.astype(BF16), win_ref[...])
    qc = _group_rms(z[:, 0:512], qw_ref[...], gq_ref[...])
    kc = _group_rms(z[:, 512:640], kw_ref[...], gk_ref[...])
    vc = z[:, 640:768]
    kn = z[:, 1280:1792]
    vn = z[:, 1792:2304]
    vc_ref[...] = vc.astype(BF16)
    qn_ref[...] = z[:, 768:1280].astype(BF16)
    kn_ref[...] = kn.astype(BF16)
    vn_ref[...] = vn.astype(BF16)
    if sample:
        ch, sh = ch_ref[...], sh_ref[...]
        qc_ref[...] = _rope(qc, ch, sh, DH_C // 2).astype(BF16)
        kc_ref[...] = _rope(kc, ch, sh, DH_C // 2).astype(BF16)
    else:
        qc_ref[...] = qc.astype(BF16)
        kc_ref[...] = kc.astype(BF16)
        kc_st_ref[...] = kc
        vc_st_ref[...] = vc
        kn_st_ref[...] = kn
        vn_st_ref[...] = vn


def _proj_odd(h, mod, layer, sample, nw, w, rope):
    n = h.shape[0]
    tm = ROW_TILE
    ins = [h, mod, nw, w["w_in"], w["q_w"], w["k_w"], w["gq"], w["gk"]]
    in_specs = [_rows(tm, D_MODEL), _mod_spec(layer, sample, tm), _full((1, D_MODEL)), _full((D_MODEL, ODD_IN)),
                _full((1, 512)), _full((1, 128)), _full((512, 512)), _full((128, 128))]
    widths = [512, 128, 128, 512, 512, 512]
    out_shape = [jax.ShapeDtypeStruct((n, wd), BF16) for wd in widths]
    out_specs = [_rows(tm, wd) for wd in widths]
    if sample:
        ins += [rope["ch"], rope["sh"]]
        in_specs += [_rope_spec(tm)] * 2
    else:
        for wd in (128, 128, 512, 512):
            out_shape.append(jax.ShapeDtypeStruct((n, wd), F32))
            out_specs.append(_rows(tm, wd))
    return pl.pallas_call(
        functools.partial(_proj_odd_kernel, sample),
        grid=(n // tm,),
        in_specs=in_specs,
        out_specs=out_specs,
        out_shape=out_shape,
        compiler_params=_cparams("parallel"),
        name="proj_odd_s" if sample else "proj_odd_p",
    )(*ins)


def _attn_odd_prompt_kernel(qc_ref, kc_ref, vc_ref, qn_ref, kn_ref, vn_ref, o_ref):
    scale = DH_C ** -0.5
    group = H_C // KV_C
    for hd in range(H_C):
        kv = hd // group
        q = qc_ref[:, hd * DH_C:(hd + 1) * DH_C]
        s = _dot_nt(q, kc_ref[:, kv * DH_C:(kv + 1) * DH_C]) * scale
        acc, den = _softmax_pv([s], [vc_ref[:, kv * DH_C:(kv + 1) * DH_C]])
        o_ref[:, hd * DH_C:(hd + 1) * DH_C] = (acc / den).astype(BF16)
    base = H_C * DH_C
    scale = DH_D ** -0.5
    for hd in range(H_D):
        c0, c1 = hd * DH_D, (hd + 1) * DH_D
        s = _dot_nt(qn_ref[:, c0:c1], kn_ref[:, c0:c1]) * scale
        acc, den = _softmax_pv([s], [vn_ref[:, c0:c1]])
        o_ref[:, base + c0:base + c1] = (acc / den).astype(BF16)


def _attn_odd_prompt(p):
    qc, kc, vc, qn, kn, vn = p
    spec = lambda wd: pl.BlockSpec((SEQ, wd), lambda b: (b, 0))
    return pl.pallas_call(
        _attn_odd_prompt_kernel,
        grid=(BATCH,),
        in_specs=[spec(512), spec(128), spec(128), spec(512), spec(512), spec(512)],
        out_specs=spec(D_MODEL),
        out_shape=jax.ShapeDtypeStruct((qc.shape[0], D_MODEL), BF16),
        compiler_params=_cparams("parallel"),
        name="attn_odd_p",
    )(qc, kc, vc, qn, kn, vn)


def _gqa_sample_kernel(q_ref, k_ref, v_ref, ck_ref, cv_ref, o_ref):
    scale = DH_C ** -0.5
    group = H_C // KV_C
    k_ctx = ck_ref[...].astype(BF16)
    v_ctx = cv_ref[...].astype(BF16)
    for hd in range(H_C):
        kv = hd // group
        c0, c1 = kv * DH_C, (kv + 1) * DH_C
        q = q_ref[:, hd * DH_C:(hd + 1) * DH_C]
        scores = [_dot_nt(q, k_ref[:, c0:c1]) * scale, _dot_nt(q, k_ctx[:, c0:c1]) * scale]
        acc, den = _softmax_pv(scores, [v_ref[:, c0:c1], v_ctx[:, c0:c1]])
        o_ref[:, hd * DH_C:(hd + 1) * DH_C] = (acc / den).astype(BF16)


def _gqa_sample(qc, kc, vc, cache_k, cache_v):
    tq, per = Q_TILE, DEC_SEQ // Q_TILE
    return pl.pallas_call(
        _gqa_sample_kernel,
        grid=(DEC_BATCH, per),
        in_specs=[pl.BlockSpec((tq, 512), lambda b, i: (b * per + i, 0)),
                  pl.BlockSpec((DEC_SEQ, 128), lambda b, i: (b, 0)),
                  pl.BlockSpec((DEC_SEQ, 128), lambda b, i: (b, 0)),
                  pl.BlockSpec((None, PAST_LEN, 128), lambda b, i: (b, 0, 0)),
                  pl.BlockSpec((None, PAST_LEN, 128), lambda b, i: (b, 0, 0))],
        out_specs=pl.BlockSpec((tq, 512), lambda b, i: (b * per + i, 0)),
        out_shape=jax.ShapeDtypeStruct((qc.shape[0], 512), BF16),
        compiler_params=_cparams("parallel", "parallel"),
        name="gqa_s",
    )(qc, kc, vc, cache_k, cache_v)


NA_ROWS = DEC_SEQ // GRID_W
NA_KR = min(NA_WIN_ROWS, NA_ROWS)
NA_LOC = NA_KR * GRID_W


def _na_sample_kernel(q_ref, k_ref, v_ref, ck_ref, cv_ref, bias_ref, o_ref):
    r = pl.program_id(1)
    rs = jnp.clip(r - NA_KR // 2, 0, NA_ROWS - NA_KR)
    start = pl.multiple_of(rs * GRID_W, GRID_W)
    scale = DH_D ** -0.5
    wq = lax.broadcasted_iota(jnp.int32, (GRID_W, NA_LOC), 0)
    wk = lax.broadcasted_iota(jnp.int32, (GRID_W, NA_LOC), 1) % GRID_W
    cs = jnp.clip(wq - NA_WIN_COLS // 2, 0, GRID_W - NA_WIN_COLS)
    col_ok = (wk >= cs) & (wk < cs + NA_WIN_COLS)
    k_loc = k_ref[pl.ds(start, NA_LOC), :]
    v_loc = v_ref[pl.ds(start, NA_LOC), :]
    k_ctx = ck_ref[...].astype(BF16)
    v_ctx = cv_ref[...].astype(BF16)
    for hd in range(H_D):
        c0, c1 = hd * DH_D, (hd + 1) * DH_D
        q = q_ref[:, c0:c1]
        s_loc = _dot_nt(q, k_loc[:, c0:c1]) * scale + bias_ref[hd]
        s_loc = jnp.where(col_ok, s_loc, NEG_INF)
        s_ctx = _dot_nt(q, k_ctx[:, c0:c1]) * scale
        acc, den = _softmax_pv([s_loc, s_ctx], [v_loc[:, c0:c1], v_ctx[:, c0:c1]])
        o_ref[:, c0:c1] = (acc / den).astype(BF16)


def _na_sample(qn, kn, vn, cache_k, cache_v, bias):
    def bias_map(b, r):
        rs = jnp.clip(r - NA_KR // 2, 0, NA_ROWS - NA_KR)
        return (0, rs - r + NA_WIN_ROWS - 1, 0, 0)
    return pl.pallas_call(
        _na_sample_kernel,
        grid=(DEC_BATCH, NA_ROWS),
        in_specs=[pl.BlockSpec((GRID_W, 512), lambda b, r: (b * NA_ROWS + r, 0)),
                  pl.BlockSpec((DEC_SEQ, 512), lambda b, r: (b, 0)),
                  pl.BlockSpec((DEC_SEQ, 512), lambda b, r: (b, 0)),
                  pl.BlockSpec((None, PAST_LEN, 512), lambda b, r: (b, 0, 0)),
                  pl.BlockSpec((None, PAST_LEN, 512), lambda b, r: (b, 0, 0)),
                  pl.BlockSpec((H_D, None, GRID_W, NA_LOC), bias_map)],
        out_specs=pl.BlockSpec((GRID_W, 512), lambda b, r: (b * NA_ROWS + r, 0)),
        out_shape=jax.ShapeDtypeStruct((qn.shape[0], 512), BF16),
        compiler_params=_cparams("parallel", "parallel"),
        name="na_s",
    )(qn, kn, vn, cache_k, cache_v, bias)


def _na_bias_table(rpb):
    kc = NA_WIN_COLS
    col = np.arange(GRID_W)
    dc = np.clip(col[None, :] - col[:, None] + kc - 1, 0, 2 * kc - 2)
    d0 = np.arange(NA_WIN_ROWS)
    dr = d0[:, None] + np.arange(NA_KR)[None, :]
    t = rpb[:, dr[:, None, :, None], dc[None, :, None, :]]
    return t.reshape(H_D, NA_WIN_ROWS, GRID_W, NA_LOC).astype(F32)


def _post_attn_kernel(n_parts, *refs):
    o_refs = refs[:n_parts]
    h_ref, wout_ref, mod_ref, nw1_ref, nw2_ref, h1_ref, u2_ref = refs[n_parts:]
    y = None
    off = 0
    for o_ref in o_refs:
        wd = o_ref.shape[1]
        part = _dot(o_ref[...], wout_ref[off:off + wd, :])
        y = part if y is None else y + part
        off += wd
    mod = mod_ref[...]
    h1 = h_ref[...] + mod[2:3] * _rms(y, nw1_ref[...])
    h1_ref[...] = h1
    u2 = _rms(h1, nw2_ref[...]) * (1.0 + mod[4:5]) + mod[3:4]
    u2_ref[...] = u2.astype(BF16)


def _post_attn(o_parts, h, w_out, mod, layer, sample, nw1, nw2):
    n = h.shape[0]
    tm = ROW_TILE
    in_specs = [_rows(tm, o.shape[1]) for o in o_parts]
    in_specs += [_rows(tm, D_MODEL), _full((D_MODEL, D_MODEL)), _mod_spec(layer, sample, tm),
                 _full((1, D_MODEL)), _full((1, D_MODEL))]
    return pl.pallas_call(
        functools.partial(_post_attn_kernel, len(o_parts)),
        grid=(n // tm,),
        in_specs=in_specs,
        out_specs=[_rows(tm, D_MODEL), _rows(tm, D_MODEL)],
        out_shape=[jax.ShapeDtypeStruct((n, D_MODEL), F32), jax.ShapeDtypeStruct((n, D_MODEL), BF16)],
        compiler_params=_cparams("parallel"),
        name="post_attn_s" if sample else "post_attn_p",
    )(*o_parts, h, w_out, mod, nw1, nw2)


def _ffn_kernel(seq_len, u_ref, h_ref, wg_ref, wv_ref, cwg_ref, cwv_ref, cbg_ref, cbv_ref, wd_ref,
                mod_ref, nw_ref, o_ref, acc_ref):
    j = pl.program_id(1)
    u = u_ref[...]
    tm = u.shape[0]
    row = lax.broadcasted_iota(jnp.int32, (tm, FF_CHUNK), 0) % seq_len
    is_first = row == 0
    is_last = row == seq_len - 1

    def conv(z, cw_ref, cb_ref):
        cw = cw_ref[...]
        prev = jnp.where(is_first, 0.0, pltpu.roll(z, 1, 0))
        nxt = jnp.where(is_last, 0.0, pltpu.roll(z, tm - 1, 0))
        return prev * cw[0:1] + z * cw[1:2] + nxt * cw[2:3] + cb_ref[...]

    g = conv(_dot(u, wg_ref[...]), cwg_ref, cbg_ref)
    v = conv(_dot(u, wv_ref[...]), cwv_ref, cbv_ref)
    a = (g / (1.0 + jnp.exp(-g))) * v
    part = _dot(a.astype(BF16), wd_ref[...])

    @pl.when(j == 0)
    def _():
        acc_ref[...] = part

    @pl.when(j > 0)
    def _():
        acc_ref[...] += part

    @pl.when(j == pl.num_programs(1) - 1)
    def _():
        mod = mod_ref[...]
        o_ref[...] = h_ref[...] + mod[5:6] * _rms(acc_ref[...], nw_ref[...])


def _ffn(u2, h1, w, mod, layer, sample, nw):
    n = h1.shape[0]
    tm = FFN_ROW_TILE
    nf = D_FF // FF_CHUNK
    seq_len = DEC_SEQ if sample else SEQ
    in_specs = [
        pl.BlockSpec((tm, D_MODEL), lambda i, j: (i, 0)),
        pl.BlockSpec((tm, D_MODEL), lambda i, j: (i, 0)),
        pl.BlockSpec((D_MODEL, FF_CHUNK), lambda i, j: (0, j)),
        pl.BlockSpec((D_MODEL, FF_CHUNK), lambda i, j: (0, nf + j)),
        pl.BlockSpec((3, FF_CHUNK), lambda i, j: (0, j)),
        pl.BlockSpec((3, FF_CHUNK), lambda i, j: (0, nf + j)),
        pl.BlockSpec((1, FF_CHUNK), lambda i, j: (0, j)),
        pl.BlockSpec((1, FF_CHUNK), lambda i, j: (0, nf + j)),
        pl.BlockSpec((FF_CHUNK, D_MODEL), lambda i, j: (j, 0)),
        _mod_spec(layer, sample, tm),
        _full((1, D_MODEL)),
    ]
    return pl.pallas_call(
        functools.partial(_ffn_kernel, seq_len),
        grid=(n // tm, nf),
        in_specs=in_specs,
        out_specs=pl.BlockSpec((tm, D_MODEL), lambda i, j: (i, 0)),
        out_shape=jax.ShapeDtypeStruct((n, D_MODEL), F32),
        scratch_shapes=[pltpu.VMEM((tm, D_MODEL), F32)],
        compiler_params=_cparams("parallel", "arbitrary"),
        name="ffn_s" if sample else "ffn_p",
    )(u2, h1, w["w_up"], w["w_up"], w["conv_w"], w["conv_w"], w["conv_b"], w["conv_b"], w["w_down"], mod, nw)


def _rope_tables():
    def table(rot_dim):
        t = np.arange(DEC_SEQ)
        n_freq = rot_dim // 4
        inv = 1.0 / (ROPE_THETA ** (np.arange(n_freq) / n_freq))
        ang = np.concatenate([(t // GRID_W)[:, None] * inv[None, :], (t % GRID_W)[:, None] * inv[None, :]], axis=-1)
        cos = np.cos(ang).astype(np.float32)
        sin = np.sin(ang).astype(np.float32)
        reps = LANES // rot_dim
        return (jnp.asarray(np.tile(np.concatenate([cos, cos], axis=-1), (1, reps))),
                jnp.asarray(np.tile(np.concatenate([-sin, sin], axis=-1), (1, reps))))
    ca, sa = table(QK_ROPE)
    ch, sh = table(HEAD_DIM)
    return {"ca": ca, "sa": sa, "ch": ch, "sh": sh}


def _group_mean_matrix(width):
    idx = np.arange(width) // HEAD_DIM
    return jnp.asarray((idx[:, None] == idx[None, :]).astype(np.float32) / HEAD_DIM, BF16)


def kernel(x_prompt, x_sample, c, cache_mla_ckv, cache_mla_kpe, cache_diff_k, cache_diff_v, cache_gqa_k, cache_gqa_v, cache_na_k, cache_na_v, c_ctx, norm_w, w_mod, b_mod, w_in_even, w_out_even, w_uq, q_norm_w, kv_norm_w, w_uk, w_uv, diff_lam, diff_subln_w, w_in_odd, w_out_odd, qk_norm_w, na_rpb, w_up, conv_w, conv_b, w_down):
    rope = _rope_tables()
    n_p = BATCH * SEQ
    n_s = DEC_BATCH * DEC_SEQ
    cvecs = jnp.concatenate([c_ctx[None, :], c, jnp.zeros((MOD_ROWS - 1 - DEC_BATCH, D_MODEL), F32)], axis=0)
    mod = _modulation(cvecs, w_mod, b_mod).reshape(DEPTH, MOD_ROWS, 6, D_MODEL)
    hp = x_prompt.reshape(n_p, D_MODEL)
    hs = x_sample.reshape(n_s, D_MODEL)
    even_states, odd_states = [], []
    for l in range(DEPTH):
        i = l // 2
        nw = [norm_w[l, k][None, :] for k in range(4)]
        if l % 2 == 0:
            lam_init = 0.8 - 0.6 * math.exp(-0.3 * l)
            wi = w_in_even[i]
            w_uq3 = w_uq[i].reshape(Q_LORA, H_A, QK_NOPE + QK_ROPE)
            w = {
                "w_in": jnp.concatenate([wi[:, :416], jnp.zeros((D_MODEL, 96), F32), wi[:, 416:]], axis=1).astype(BF16),
                "q_norm_w": q_norm_w[i][None, :],
                "kv_norm_w": kv_norm_w[i][None, :],
                "w_uq_n": w_uq3[:, :, :QK_NOPE].reshape(Q_LORA, H_A * QK_NOPE).astype(BF16),
                "w_uq_r": w_uq3[:, :, QK_NOPE:].reshape(Q_LORA, H_A * QK_ROPE).astype(BF16),
                "w_uk": w_uk[i].astype(BF16),
                "w_uv": w_uv[i].astype(BF16),
                "diff_lam": diff_lam[i],
                "diff_subln_w": diff_subln_w[i][None, :],
            }
            outs_p = _proj_even(hp, mod, l, False, nw[0], w, rope)
            outs_s = _proj_even(hs, mod, l, True, nw[0], w, rope)
            even_states.append(outs_p[8:])
            o_p = [_attn_even(outs_p[:8], False, lam_init, None, w)]
            caches = (cache_mla_ckv[:, i], cache_mla_kpe[:, i],
                      cache_diff_k[:, i].reshape(DEC_BATCH, PAST_LEN, 512),
                      cache_diff_v[:, i].reshape(DEC_BATCH, PAST_LEN, 512))
            o_s = [_attn_even(outs_s, True, lam_init, caches, w)]
            w_out = w_out_even[i].astype(BF16)
        else:
            q_w = jnp.tile(qk_norm_w[i, 0], H_C)[None, :]
            k_w = jnp.tile(qk_norm_w[i, 1], KV_C)[None, :]
            w = {"w_in": w_in_odd[i].astype(BF16), "q_w": q_w, "k_w": k_w,
                 "gq": _group_mean_matrix(512), "gk": _group_mean_matrix(128)}
            outs_p = _proj_odd(hp, mod, l, False, nw[0], w, rope)
            outs_s = _proj_odd(hs, mod, l, True, nw[0], w, rope)
            odd_states.append(outs_p[6:])
            o_p = [_attn_odd_prompt(outs_p[:6])]
            qc, kc, vc, qn, kn, vn = outs_s
            o_c = _gqa_sample(qc, kc, vc, cache_gqa_k[:, i].reshape(DEC_BATCH, PAST_LEN, 128),
                              cache_gqa_v[:, i].reshape(DEC_BATCH, PAST_LEN, 128))
            o_d = _na_sample(qn, kn, vn, cache_na_k[:, i].reshape(DEC_BATCH, PAST_LEN, 512),
                             cache_na_v[:, i].reshape(DEC_BATCH, PAST_LEN, 512), _na_bias_table(na_rpb[i]))
            o_s = [o_c, o_d]
            w_out = w_out_odd[i].astype(BF16)
        wf = {"w_up": w_up[l].astype(BF16), "conv_w": conv_w[l], "conv_b": conv_b[l][None, :],
              "w_down": w_down[l].astype(BF16)}
        h1p, u2p = _post_attn(o_p, hp, w_out, mod, l, False, nw[1], nw[2])
        h1s, u2s = _post_attn(o_s, hs, w_out, mod, l, True, nw[1], nw[2])
        hp = _ffn(u2p, h1p, wf, mod, l, False, nw[3])
        hs = _ffn(u2s, h1s, wf, mod, l, True, nw[3])

    def stack(states, k, shape):
        return jnp.stack([st[k].reshape((BATCH, SEQ) + shape) for st in states], axis=1)

    new_mla_ckv = stack(even_states, 0, (KV_LORA,))
    new_mla_kpe = stack(even_states, 1, (QK_ROPE,))
    new_diff_k = stack(even_states, 2, (H_B, 2 * DH_B))
    new_diff_v = stack(even_states, 3, (H_B, 2 * DH_B))
    new_gqa_k = stack(odd_states, 0, (KV_C, DH_C))
    new_gqa_v = stack(odd_states, 1, (KV_C, DH_C))
    new_na_k = stack(odd_states, 2, (H_D, DH_D))
    new_na_v = stack(odd_states, 3, (H_D, DH_D))
    return (hp.reshape(BATCH, SEQ, D_MODEL), hs.reshape(DEC_BATCH, DEC_SEQ, D_MODEL),
            new_mla_ckv, new_mla_kpe, new_diff_k, new_diff_v, new_gqa_k, new_gqa_v, new_na_k, new_na_v)
```

```python
import functools
import math

import numpy as np
import jax
import jax.numpy as jnp
from jax import lax
from jax.experimental import pallas as pl
from jax.experimental.pallas import tpu as pltpu

D_MODEL = 1024
BATCH = 32
SEQ = 256
DEPTH = 2
DEC_BATCH = 4
DEC_SEQ = 1024
PAST_LEN = 256
GRID_W = 64
HEAD_DIM = 64
H_A = 8
QK_NOPE = 64
QK_ROPE = 32
V_A = 64
Q_LORA = 256
KV_LORA = 128
H_B = 4
DH_B = HEAD_DIM
H_C = 8
KV_C = 2
DH_C = HEAD_DIM
H_D = 8
DH_D = HEAD_DIM
NA_WIN_ROWS = 8
NA_WIN_COLS = 16
D_FF = 2816
ROPE_THETA = 10000.0
EPS = 1e-6
NEG_INF = -1e30

LANES = 128
MOD_ROWS = 8
ROW_TILE = 512
FFN_ROW_TILE = 1024
FF_CHUNK = 256
Q_TILE = 256
VMEM_LIMIT = 48 * 1024 * 1024

F32 = jnp.float32
BF16 = jnp.bfloat16


def _cparams(*sem):
    return pltpu.CompilerParams(dimension_semantics=sem, vmem_limit_bytes=VMEM_LIMIT)


def _dot(a, b):
    return jnp.dot(a, b, preferred_element_type=F32)


def _dot_nt(a, b):
    return lax.dot_general(a, b, (((1,), (1,)), ((), ())), preferred_element_type=F32)


def _rms(x, w):
    return x * lax.rsqrt(jnp.mean(x * x, axis=-1, keepdims=True) + EPS) * w


def _group_rms(x, w, gmat):
    x2 = x * x
    hi = x2.astype(BF16)
    lo = (x2 - hi.astype(F32)).astype(BF16)
    ms = _dot(hi, gmat) + _dot(lo, gmat)
    return x * lax.rsqrt(ms + EPS) * w


def _rope(x, cos, sin_signed, half):
    outs = []
    for j in range(x.shape[1] // LANES):
        xc = x[:, j * LANES:(j + 1) * LANES]
        lane = lax.broadcasted_iota(jnp.int32, xc.shape, 1)
        first = (lane % (2 * half)) < half
        partner = jnp.where(first, pltpu.roll(xc, LANES - half, 1), pltpu.roll(xc, half, 1))
        outs.append(xc * cos + partner * sin_signed)
    return outs[0] if len(outs) == 1 else jnp.concatenate(outs, axis=1)


def _softmax_pv(scores, values):
    m = scores[0].max(axis=-1, keepdims=True)
    for s in scores[1:]:
        m = jnp.maximum(m, s.max(axis=-1, keepdims=True))
    den = None
    acc = None
    for s, v in zip(scores, values):
        e = jnp.exp(s - m)
        d = e.sum(axis=-1, keepdims=True)
        o = _dot(e.astype(BF16), v)
        den = d if den is None else den + d
        acc = o if acc is None else acc + o
    return acc, den


def _mod_kernel(c_ref, w_ref, b_ref, o_ref):
    cv = c_ref[...]
    act = cv / (1.0 + jnp.exp(-cv))
    o_ref[...] = _dot(act.astype(BF16), w_ref[...].astype(BF16)) + b_ref[...]


def _modulation(cvecs, w_mod, b_mod):
    tn = 1024
    n = 6 * D_MODEL
    return pl.pallas_call(
        _mod_kernel,
        grid=(DEPTH, n // tn),
        in_specs=[
            pl.BlockSpec((MOD_ROWS, D_MODEL), lambda l, j: (0, 0)),
            pl.BlockSpec((None, D_MODEL, tn), lambda l, j: (l, 0, j)),
            pl.BlockSpec((None, 1, tn), lambda l, j: (l, 0, j)),
        ],
        out_specs=pl.BlockSpec((None, MOD_ROWS, tn), lambda l, j: (l, 0, j)),
        out_shape=jax.ShapeDtypeStruct((DEPTH, MOD_ROWS, n), F32),
        compiler_params=_cparams("parallel", "parallel"),
        name="adaln_mod",
    )(cvecs, w_mod, b_mod.reshape(DEPTH, 1, n))


def _mod_spec(layer, sample, tm):
    if sample:
        per = DEC_SEQ // tm
        return pl.BlockSpec((None, None, 6, D_MODEL), lambda i, *_: (layer, 1 + i // per, 0, 0))
    return pl.BlockSpec((None, None, 6, D_MODEL), lambda i, *_: (layer, 0, 0, 0))


def _full(shape):
    nd = len(shape)
    return pl.BlockSpec(shape, lambda *_: (0,) * nd)


def _rows(tm, width):
    return pl.BlockSpec((tm, width), lambda i, *_: (i, 0))


def _rope_spec(tm):
    per = DEC_SEQ // tm
    return pl.BlockSpec((tm, LANES), lambda i, *_: (i % per, 0))


EVEN_IN = 2048


def _proj_even_kernel(sample, *refs):
    if sample:
        (h_ref, mod_ref, nw_ref, win_ref, qnw_ref, kvnw_ref, wuqn_ref, wuqr_ref, wuk_ref, wuv_ref,
         ca_ref, sa_ref, ch_ref, sh_ref,
         qn_ref, qr_ref, kn_ref, va_ref, kpe_ref, qd_ref, kd_ref, vd_ref) = refs
    else:
        (h_ref, mod_ref, nw_ref, win_ref, qnw_ref, kvnw_ref, wuqn_ref, wuqr_ref, wuk_ref, wuv_ref,
         qn_ref, qr_ref, kn_ref, va_ref, kpe_ref, qd_ref, kd_ref, vd_ref,
         ckv_st_ref, kpe_st_ref, kd_st_ref, vd_st_ref) = refs
    mod = mod_ref[...]
    u = _rms(h_ref[...], nw_ref[...]) * (1.0 + mod[1:2]) + mod[0:1]
    z = _dot(u.astype(BF16), win_ref[...])
    cq = _rms(z[:, 0:256], qnw_ref[...]).astype(BF16)
    qn_ref[...] = _dot(cq, wuqn_ref[...]).astype(BF16)
    qr = _dot(cq, wuqr_ref[...])
    ckv = _rms(z[:, 256:384], kvnw_ref[...])
    ckv_b = ckv.astype(BF16)
    kn_ref[...] = _dot(ckv_b, wuk_ref[...]).astype(BF16)
    va_ref[...] = _dot(ckv_b, wuv_ref[...]).astype(BF16)
    kpe = z[:, 384:512]
    qd = z[:, 512:1024]
    kd = z[:, 1024:1536]
    vd = z[:, 1536:2048]
    vd_ref[...] = vd.astype(BF16)
    if sample:
        ca, sa, ch, sh = ca_ref[...], sa_ref[...], ch_ref[...], sh_ref[...]
        qr_ref[...] = _rope(qr, ca, sa, QK_ROPE // 2).astype(BF16)
        kpe_ref[...] = _rope(kpe, ca, sa, QK_ROPE // 2).astype(BF16)
        qd_ref[...] = _rope(qd, ch, sh, DH_B // 2).astype(BF16)
        kd_ref[...] = _rope(kd, ch, sh, DH_B // 2).astype(BF16)
    else:
        qr_ref[...] = qr.astype(BF16)
        kpe_ref[...] = kpe.astype(BF16)
        qd_ref[...] = qd.astype(BF16)
        kd_ref[...] = kd.astype(BF16)
        ckv_st_ref[...] = ckv
        kpe_st_ref[...] = kpe[:, :QK_ROPE]
        kd_st_ref[...] = kd
        vd_st_ref[...] = vd


def _proj_even(h, mod, layer, sample, nw, w, rope):
    n = h.shape[0]
    tm = ROW_TILE
    ins = [h, mod, nw, w["w_in"], w["q_norm_w"], w["kv_norm_w"], w["w_uq_n"], w["w_uq_r"], w["w_uk"], w["w_uv"]]
    in_specs = [_rows(tm, D_MODEL), _mod_spec(layer, sample, tm), _full((1, D_MODEL)), _full((D_MODEL, EVEN_IN)),
                _full((1, Q_LORA)), _full((1, KV_LORA)), _full((Q_LORA, 512)), _full((Q_LORA, 256)),
                _full((KV_LORA, 512)), _full((KV_LORA, 512))]
    widths = [512, 256, 512, 512, 128, 512, 512, 512]
    out_shape = [jax.ShapeDtypeStruct((n, wd), BF16) for wd in widths]
    out_specs = [_rows(tm, wd) for wd in widths]
    if sample:
        ins += [rope["ca"], rope["sa"], rope["ch"], rope["sh"]]
        in_specs += [_rope_spec(tm)] * 4
    else:
        for wd in (KV_LORA, QK_ROPE, 512, 512):
            out_shape.append(jax.ShapeDtypeStruct((n, wd), F32))
            out_specs.append(_rows(tm, wd))
    return pl.pallas_call(
        functools.partial(_proj_even_kernel, sample),
        grid=(n // tm,),
        in_specs=in_specs,
        out_specs=out_specs,
        out_shape=out_shape,
        compiler_params=_cparams("parallel"),
        name="proj_even_s" if sample else "proj_even_p",
    )(*ins)


def _attn_even_kernel(sample, lam_init, *refs):
    if sample:
        (qn_ref, qr_ref, kn_ref, va_ref, kpe_ref, qd_ref, kd_ref, vd_ref,
         cckv_ref, ckpe_ref, cdk_ref, cdv_ref, wuk_ref, wuv_ref, lam_ref, sub_ref, o_ref) = refs
    else:
        (qn_ref, qr_ref, kn_ref, va_ref, kpe_ref, qd_ref, kd_ref, vd_ref, lam_ref, sub_ref, o_ref) = refs
    lf = lam_ref[...]
    lam = (jnp.exp(jnp.sum(lf[0:1] * lf[1:2], axis=-1, keepdims=True))
           - jnp.exp(jnp.sum(lf[2:3] * lf[3:4], axis=-1, keepdims=True)) + lam_init)
    if sample:
        cckv = cckv_ref[...].astype(BF16)
        kn_ctx = _dot(cckv, wuk_ref[...]).astype(BF16)
        va_ctx = _dot(cckv, wuv_ref[...]).astype(BF16)
        kpe_ctx = ckpe_ref[...].astype(BF16)
        kd_ctx = cdk_ref[...].astype(BF16)
        vd_ctx = cdv_ref[...].astype(BF16)
    scale_a = (QK_NOPE + QK_ROPE) ** -0.5
    kpe = kpe_ref[:, 0:QK_ROPE]
    for hd in range(H_A):
        c0, c1 = hd * QK_NOPE, (hd + 1) * QK_NOPE
        qn = qn_ref[:, c0:c1]
        qr = qr_ref[:, hd * QK_ROPE:(hd + 1) * QK_ROPE]
        scores = [(_dot_nt(qn, kn_ref[:, c0:c1]) + _dot_nt(qr, kpe)) * scale_a]
        values = [va_ref[:, hd * V_A:(hd + 1) * V_A]]
        if sample:
            scores.append((_dot_nt(qn, kn_ctx[:, c0:c1]) + _dot_nt(qr, kpe_ctx)) * scale_a)
            values.append(va_ctx[:, hd * V_A:(hd + 1) * V_A])
        acc, den = _softmax_pv(scores, values)
        o_ref[:, hd * V_A:(hd + 1) * V_A] = (acc / den).astype(BF16)
    scale_b = DH_B ** -0.5
    base = H_A * V_A
    sub_w = sub_ref[...]
    for hd in range(H_B):
        g0 = hd * 2 * DH_B
        outs = []
        for comp in range(2):
            c0, c1 = g0 + comp * DH_B, g0 + (comp + 1) * DH_B
            q = qd_ref[:, c0:c1]
            scores = [_dot_nt(q, kd_ref[:, c0:c1]) * scale_b]
            values = [vd_ref[:, g0:g0 + 2 * DH_B]]
            if sample:
                scores.append(_dot_nt(q, kd_ctx[:, c0:c1]) * scale_b)
                values.append(vd_ctx[:, g0:g0 + 2 * DH_B])
            acc, den = _softmax_pv(scores, values)
            outs.append(acc / den)
        ob = outs[0] - lam * outs[1]
        ob = _rms(ob, sub_w) * (1.0 - lam_init)
        o_ref[:, base + g0:base + g0 + 2 * DH_B] = ob.astype(BF16)


def _attn_even(p, sample, lam_init, caches, w):
    qn, qr, kn, va, kpe, qd, kd, vd = p
    n = qn.shape[0]
    if sample:
        tq, per = Q_TILE, DEC_SEQ // Q_TILE
        grid = (DEC_BATCH, per)
        qspec = lambda wd: pl.BlockSpec((tq, wd), lambda b, i: (b * per + i, 0))
        kspec = lambda wd: pl.BlockSpec((DEC_SEQ, wd), lambda b, i: (b, 0))
        cspec = lambda wd: pl.BlockSpec((None, PAST_LEN, wd), lambda b, i: (b, 0, 0))
        ins = [qn, qr, kn, va, kpe, qd, kd, vd, *caches, w["w_uk"], w["w_uv"], w["diff_lam"], w["diff_subln_w"]]
        in_specs = [qspec(512), qspec(256), kspec(512), kspec(512), kspec(128), qspec(512), kspec(512), kspec(512),
                    cspec(KV_LORA), cspec(QK_ROPE), cspec(512), cspec(512),
                    _full((KV_LORA, 512)), _full((KV_LORA, 512)), _full((4, DH_B)), _full((1, 2 * DH_B))]
        out_spec = qspec(D_MODEL)
        sem = ("parallel", "parallel")
    else:
        grid = (BATCH,)
        spec = lambda wd: pl.BlockSpec((SEQ, wd), lambda b: (b, 0))
        ins = [qn, qr, kn, va, kpe, qd, kd, vd, w["diff_lam"], w["diff_subln_w"]]
        in_specs = [spec(512), spec(256), spec(512), spec(512), spec(128), spec(512), spec(512), spec(512),
                    _full((4, DH_B)), _full((1, 2 * DH_B))]
        out_spec = spec(D_MODEL)
        sem = ("parallel",)
    return pl.pallas_call(
        functools.partial(_attn_even_kernel, sample, lam_init),
        grid=grid,
        in_specs=in_specs,
        out_specs=out_spec,
        out_shape=jax.ShapeDtypeStruct((n, D_MODEL), BF16),
        compiler_params=_cparams(*sem),
        name="attn_even_s" if sample else "attn_even_p",
    )(*ins)


ODD_IN = 2304


def _proj_odd_kernel(sample, *refs):
    if sample:
        (h_ref, mod_ref, nw_ref, win_ref, qw_ref, kw_ref, gq_ref, gk_ref, ch_ref, sh_ref,
         qc_ref, kc_ref, vc_ref, qn_ref, kn_ref, vn_ref) = refs
    else:
        (h_ref, mod_ref, nw_ref, win_ref, qw_ref, kw_ref, gq_ref, gk_ref,
         qc_ref, kc_ref, vc_ref, qn_ref, kn_ref, vn_ref,
         kc_st_ref, vc_st_ref, kn_st_ref, vn_st_ref) = refs
    mod = mod_ref[...]
    u = _rms(h_ref[...], nw_ref[...]) * (1.0 + mod[1:2]) + mod[0:1]
    z = _dot(u.astype(BF16), win_ref[...])
    qc = _group_rms(z[:, 0:512], qw_ref[...], gq_ref[...])
    kc = _group_rms(z[:, 512:640], kw_ref[...], gk_ref[...])
    vc = z[:, 640:768]
    kn = z[:, 1280:1792]
    vn = z[:, 1792:2304]
    vc_ref[...] = vc.astype(BF16)
    qn_ref[...] = z[:, 768:1280].astype(BF16)
    kn_ref[...] = kn.astype(BF16)
    vn_ref[...] = vn.astype(BF16)
    if sample:
        ch, sh = ch_ref[...], sh_ref[...]
        qc_ref[...] = _rope(qc, ch, sh, DH_C // 2).astype(BF16)
        kc_ref[...] = _rope(kc, ch, sh, DH_C // 2).astype(BF16)
    else:
        qc_ref[...] = qc.astype(BF16)
        kc_ref[...] = kc.astype(BF16)
        kc_st_ref[...] = kc
        vc_st_ref[...] = vc
        kn_st_ref[...] = kn
        vn_st_ref[...] = vn


def _proj_odd(h, mod, layer, sample, nw, w, rope):
    n = h.shape[0]
    tm = ROW_TILE
    ins = [h, mod, nw, w["w_in"], w["q_w"], w["k_w"], w["gq"], w["gk"]]
    in_specs = [_rows(tm, D_MODEL), _mod_spec(layer, sample, tm), _full((1, D_MODEL)), _full((D_MODEL, ODD_IN)),
                _full((1, 512)), _full((1, 128)), _full((512, 512)), _full((128, 128))]
    widths = [512, 128, 128, 512, 512, 512]
    out_shape = [jax.ShapeDtypeStruct((n, wd), BF16) for wd in widths]
    out_specs = [_rows(tm, wd) for wd in widths]
    if sample:
        ins += [rope["ch"], rope["sh"]]
        in_specs += [_rope_spec(tm)] * 2
    else:
        for wd in (128, 128, 512, 512):
            out_shape.append(jax.ShapeDtypeStruct((n, wd), F32))
            out_specs.append(_rows(tm, wd))
    return pl.pallas_call(
        functools.partial(_proj_odd_kernel, sample),
        grid=(n // tm,),
        in_specs=in_specs,
        out_specs=out_specs,
        out_shape=out_shape,
        compiler_params=_cparams("parallel"),
        name="proj_odd_s" if sample else "proj_odd_p",
    )(*ins)


def _attn_odd_prompt_kernel(qc_ref, kc_ref, vc_ref, qn_ref, kn_ref, vn_ref, o_ref):
    scale = DH_C ** -0.5
    group = H_C // KV_C
    for hd in range(H_C):
        kv = hd // group
        q = qc_ref[:, hd * DH_C:(hd + 1) * DH_C]
        s = _dot_nt(q, kc_ref[:, kv * DH_C:(kv + 1) * DH_C]) * scale
        acc, den = _softmax_pv([s], [vc_ref[:, kv * DH_C:(kv + 1) * DH_C]])
        o_ref[:, hd * DH_C:(hd + 1) * DH_C] = (acc / den).astype(BF16)
    base = H_C * DH_C
    scale = DH_D ** -0.5
    for hd in range(H_D):
        c0, c1 = hd * DH_D, (hd + 1) * DH_D
        s = _dot_nt(qn_ref[:, c0:c1], kn_ref[:, c0:c1]) * scale
        acc, den = _softmax_pv([s], [vn_ref[:, c0:c1]])
        o_ref[:, base + c0:base + c1] = (acc / den).astype(BF16)


def _attn_odd_prompt(p):
    qc, kc, vc, qn, kn, vn = p
    spec = lambda wd: pl.BlockSpec((SEQ, wd), lambda b: (b, 0))
    return pl.pallas_call(
        _attn_odd_prompt_kernel,
        grid=(BATCH,),
        in_specs=[spec(512), spec(128), spec(128), spec(512), spec(512), spec(512)],
        out_specs=spec(D_MODEL),
        out_shape=jax.ShapeDtypeStruct((qc.shape[0], D_MODEL), BF16),
        compiler_params=_cparams("parallel"),
        name="attn_odd_p",
    )(qc, kc, vc, qn, kn, vn)


def _gqa_sample_kernel(q_ref, k_ref, v_ref, ck_ref, cv_ref, o_ref):
    scale = DH_C ** -0.5
    group = H_C // KV_C
    k_ctx = ck_ref[...].astype(BF16)
    v_ctx = cv_ref[...].astype(BF16)
    for hd in range(H_C):
        kv = hd // group
        c0, c1 = kv * DH_C, (kv + 1) * DH_C
        q = q_ref[:, hd * DH_C:(hd + 1) * DH_C]
        scores = [_dot_nt(q, k_ref[:, c0:c1]) * scale, _dot_nt(q, k_ctx[:, c0:c1]) * scale]
        acc, den = _softmax_pv(scores, [v_ref[:, c0:c1], v_ctx[:, c0:c1]])
        o_ref[:, hd * DH_C:(hd + 1) * DH_C] = (acc / den).astype(BF16)


def _gqa_sample(qc, kc, vc, cache_k, cache_v):
    tq, per = Q_TILE, DEC_SEQ // Q_TILE
    return pl.pallas_call(
        _gqa_sample_kernel,
        grid=(DEC_BATCH, per),
        in_specs=[pl.BlockSpec((tq, 512), lambda b, i: (b * per + i, 0)),
                  pl.BlockSpec((DEC_SEQ, 128), lambda b, i: (b, 0)),
                  pl.BlockSpec((DEC_SEQ, 128), lambda b, i: (b, 0)),
                  pl.BlockSpec((None, PAST_LEN, 128), lambda b, i: (b, 0, 0)),
                  pl.BlockSpec((None, PAST_LEN, 128), lambda b, i: (b, 0, 0))],
        out_specs=pl.BlockSpec((tq, 512), lambda b, i: (b * per + i, 0)),
        out_shape=jax.ShapeDtypeStruct((qc.shape[0], 512), BF16),
        compiler_params=_cparams("parallel", "parallel"),
        name="gqa_s",
    )(qc, kc, vc, cache_k, cache_v)


NA_ROWS = DEC_SEQ // GRID_W
NA_KR = min(NA_WIN_ROWS, NA_ROWS)
NA_LOC = NA_KR * GRID_W


def _na_sample_kernel(q_ref, k_ref, v_ref, ck_ref, cv_ref, bias_ref, o_ref):
    r = pl.program_id(1)
    rs = jnp.clip(r - NA_KR // 2, 0, NA_ROWS - NA_KR)
    start = pl.multiple_of(rs * GRID_W, GRID_W)
    scale = DH_D ** -0.5
    wq = lax.broadcasted_iota(jnp.int32, (GRID_W, NA_LOC), 0)
    wk = lax.broadcasted_iota(jnp.int32, (GRID_W, NA_LOC), 1) % GRID_W
    cs = jnp.clip(wq - NA_WIN_COLS // 2, 0, GRID_W - NA_WIN_COLS)
    col_ok = (wk >= cs) & (wk < cs + NA_WIN_COLS)
    k_loc = k_ref[pl.ds(start, NA_LOC), :]
    v_loc = v_ref[pl.ds(start, NA_LOC), :]
    k_ctx = ck_ref[...].astype(BF16)
    v_ctx = cv_ref[...].astype(BF16)
    for hd in range(H_D):
        c0, c1 = hd * DH_D, (hd + 1) * DH_D
        q = q_ref[:, c0:c1]
        s_loc = _dot_nt(q, k_loc[:, c0:c1]) * scale + bias_ref[hd]
        s_loc = jnp.where(col_ok, s_loc, NEG_INF)
        s_ctx = _dot_nt(q, k_ctx[:, c0:c1]) * scale
        acc, den = _softmax_pv([s_loc, s_ctx], [v_loc[:, c0:c1], v_ctx[:, c0:c1]])
        o_ref[:, c0:c1] = (acc / den).astype(BF16)


def _na_sample(qn, kn, vn, cache_k, cache_v, bias):
    def bias_map(b, r):
        rs = jnp.clip(r - NA_KR // 2, 0, NA_ROWS - NA_KR)
        return (0, rs - r + NA_WIN_ROWS - 1, 0, 0)
    return pl.pallas_call(
        _na_sample_kernel,
        grid=(DEC_BATCH, NA_ROWS),
        in_specs=[pl.BlockSpec((GRID_W, 512), lambda b, r: (b * NA_ROWS + r, 0)),
                  pl.BlockSpec((DEC_SEQ, 512), lambda b, r: (b, 0)),
                  pl.BlockSpec((DEC_SEQ, 512), lambda b, r: (b, 0)),
                  pl.BlockSpec((None, PAST_LEN, 512), lambda b, r: (b, 0, 0)),
                  pl.BlockSpec((None, PAST_LEN, 512), lambda b, r: (b, 0, 0)),
                  pl.BlockSpec((H_D, None, GRID_W, NA_LOC), bias_map)],
        out_specs=pl.BlockSpec((GRID_W, 512), lambda b, r: (b * NA_ROWS + r, 0)),
        out_shape=jax.ShapeDtypeStruct((qn.shape[0], 512), BF16),
        compiler_params=_cparams("parallel", "parallel"),
        name="na_s",
    )(qn, kn, vn, cache_k, cache_v, bias)


def _na_bias_table(rpb):
    edge = GRID_W - NA_WIN_COLS
    ext = jnp.pad(rpb.astype(F32), ((0, 0), (0, 0), (edge, edge)), mode="edge")
    toep = jnp.stack([ext[:, :, GRID_W - 1 - wq:2 * GRID_W - 1 - wq] for wq in range(GRID_W)], axis=1)
    flat = toep.reshape(H_D, GRID_W, (2 * NA_WIN_ROWS - 1) * GRID_W)
    return jnp.stack([flat[:, :, d0 * GRID_W:d0 * GRID_W + NA_LOC] for d0 in range(NA_WIN_ROWS)], axis=1)


def _post_attn_kernel(n_parts, *refs):
    o_refs = refs[:n_parts]
    h_ref, wout_ref, mod_ref, nw1_ref, nw2_ref, h1_ref, u2_ref = refs[n_parts:]
    y = None
    off = 0
    for o_ref in o_refs:
        wd = o_ref.shape[1]
        part = _dot(o_ref[...], wout_ref[off:off + wd, :])
        y = part if y is None else y + part
        off += wd
    mod = mod_ref[...]
    h1 = h_ref[...] + mod[2:3] * _rms(y, nw1_ref[...])
    h1_ref[...] = h1
    u2 = _rms(h1, nw2_ref[...]) * (1.0 + mod[4:5]) + mod[3:4]
    u2_ref[...] = u2.astype(BF16)


def _post_attn(o_parts, h, w_out, mod, layer, sample, nw1, nw2):
    n = h.shape[0]
    tm = ROW_TILE
    in_specs = [_rows(tm, o.shape[1]) for o in o_parts]
    in_specs += [_rows(tm, D_MODEL), _full((D_MODEL, D_MODEL)), _mod_spec(layer, sample, tm),
                 _full((1, D_MODEL)), _full((1, D_MODEL))]
    return pl.pallas_call(
        functools.partial(_post_attn_kernel, len(o_parts)),
        grid=(n // tm,),
        in_specs=in_specs,
        out_specs=[_rows(tm, D_MODEL), _rows(tm, D_MODEL)],
        out_shape=[jax.ShapeDtypeStruct((n, D_MODEL), F32), jax.ShapeDtypeStruct((n, D_MODEL), BF16)],
        compiler_params=_cparams("parallel"),
        name="post_attn_s" if sample else "post_attn_p",
    )(*o_parts, h, w_out, mod, nw1, nw2)


def _ffn_kernel(seq_len, u_ref, h_ref, wg_ref, wv_ref, cwg_ref, cwv_ref, cbg_ref, cbv_ref, wd_ref,
                mod_ref, nw_ref, o_ref, acc_ref):
    j = pl.program_id(1)
    u = u_ref[...]
    tm = u.shape[0]
    row = lax.broadcasted_iota(jnp.int32, (tm, FF_CHUNK), 0) % seq_len
    is_first = row == 0
    is_last = row == seq_len - 1

    def conv(z, cw_ref, cb_ref):
        cw = cw_ref[...]
        prev = jnp.where(is_first, 0.0, pltpu.roll(z, 1, 0))
        nxt = jnp.where(is_last, 0.0, pltpu.roll(z, tm - 1, 0))
        return prev * cw[0:1] + z * cw[1:2] + nxt * cw[2:3] + cb_ref[...]

    g = conv(_dot(u, wg_ref[...]), cwg_ref, cbg_ref)
    v = conv(_dot(u, wv_ref[...]), cwv_ref, cbv_ref)
    a = (g / (1.0 + jnp.exp(-g))) * v
    part = _dot(a.astype(BF16), wd_ref[...])

    @pl.when(j == 0)
    def _():
        acc_ref[...] = part

    @pl.when(j > 0)
    def _():
        acc_ref[...] += part

    @pl.when(j == pl.num_programs(1) - 1)
    def _():
        mod = mod_ref[...]
        o_ref[...] = h_ref[...] + mod[5:6] * _rms(acc_ref[...], nw_ref[...])


def _ffn(u2, h1, w, mod, layer, sample, nw):
    n = h1.shape[0]
    tm = FFN_ROW_TILE
    nf = D_FF // FF_CHUNK
    seq_len = DEC_SEQ if sample else SEQ
    in_specs = [
        pl.BlockSpec((tm, D_MODEL), lambda i, j: (i, 0)),
        pl.BlockSpec((tm, D_MODEL), lambda i, j: (i, 0)),
        pl.BlockSpec((D_MODEL, FF_CHUNK), lambda i, j: (0, j)),
        pl.BlockSpec((D_MODEL, FF_CHUNK), lambda i, j: (0, nf + j)),
        pl.BlockSpec((3, FF_CHUNK), lambda i, j: (0, j)),
        pl.BlockSpec((3, FF_CHUNK), lambda i, j: (0, nf + j)),
        pl.BlockSpec((1, FF_CHUNK), lambda i, j: (0, j)),
        pl.BlockSpec((1, FF_CHUNK), lambda i, j: (0, nf + j)),
        pl.BlockSpec((FF_CHUNK, D_MODEL), lambda i, j: (j, 0)),
        _mod_spec(layer, sample, tm),
        _full((1, D_MODEL)),
    ]
    return pl.pallas_call(
        functools.partial(_ffn_kernel, seq_len),
        grid=(n // tm, nf),
        in_specs=in_specs,
        out_specs=pl.BlockSpec((tm, D_MODEL), lambda i, j: (i, 0)),
        out_shape=jax.ShapeDtypeStruct((n, D_MODEL), F32),
        scratch_shapes=[pltpu.VMEM((tm, D_MODEL), F32)],
        compiler_params=_cparams("parallel", "arbitrary"),
        name="ffn_s" if sample else "ffn_p",
    )(u2, h1, w["w_up"], w["w_up"], w["conv_w"], w["conv_w"], w["conv_b"], w["conv_b"], w["w_down"], mod, nw)


def _rope_tables():
    def table(rot_dim):
        t = np.arange(DEC_SEQ)
        n_freq = rot_dim // 4
        inv = 1.0 / (ROPE_THETA ** (np.arange(n_freq) / n_freq))
        ang = np.concatenate([(t // GRID_W)[:, None] * inv[None, :], (t % GRID_W)[:, None] * inv[None, :]], axis=-1)
        cos = np.cos(ang).astype(np.float32)
        sin = np.sin(ang).astype(np.float32)
        reps = LANES // rot_dim
        return (jnp.asarray(np.tile(np.concatenate([cos, cos], axis=-1), (1, reps))),
                jnp.asarray(np.tile(np.concatenate([-sin, sin], axis=-1), (1, reps))))
    ca, sa = table(QK_ROPE)
    ch, sh = table(HEAD_DIM)
    return {"ca": ca, "sa": sa, "ch": ch, "sh": sh}


def _group_mean_matrix(width):
    idx = np.arange(width) // HEAD_DIM
    return jnp.asarray((idx[:, None] == idx[None, :]).astype(np.float32) / HEAD_DIM, BF16)


def kernel(x_prompt, x_sample, c, cache_mla_ckv, cache_mla_kpe, cache_diff_k, cache_diff_v, cache_gqa_k, cache_gqa_v, cache_na_k, cache_na_v, c_ctx, norm_w, w_mod, b_mod, w_in_even, w_out_even, w_uq, q_norm_w, kv_norm_w, w_uk, w_uv, diff_lam, diff_subln_w, w_in_odd, w_out_odd, qk_norm_w, na_rpb, w_up, conv_w, conv_b, w_down):
    rope = _rope_tables()
    n_p = BATCH * SEQ
    n_s = DEC_BATCH * DEC_SEQ
    cvecs = jnp.concatenate([c_ctx[None, :], c, jnp.zeros((MOD_ROWS - 1 - DEC_BATCH, D_MODEL), F32)], axis=0)
    mod = _modulation(cvecs, w_mod, b_mod).reshape(DEPTH, MOD_ROWS, 6, D_MODEL)
    hp = x_prompt.reshape(n_p, D_MODEL)
    hs = x_sample.reshape(n_s, D_MODEL)
    even_states, odd_states = [], []
    for l in range(DEPTH):
        i = l // 2
        nw = [norm_w[l, k][None, :] for k in range(4)]
        if l % 2 == 0:
            lam_init = 0.8 - 0.6 * math.exp(-0.3 * l)
            wi = w_in_even[i]
            w_uq3 = w_uq[i].reshape(Q_LORA, H_A, QK_NOPE + QK_ROPE)
            w = {
                "w_in": jnp.concatenate([wi[:, :416], jnp.zeros((D_MODEL, 96), F32), wi[:, 416:]], axis=1).astype(BF16),
                "q_norm_w": q_norm_w[i][None, :],
                "kv_norm_w": kv_norm_w[i][None, :],
                "w_uq_n": w_uq3[:, :, :QK_NOPE].reshape(Q_LORA, H_A * QK_NOPE).astype(BF16),
                "w_uq_r": w_uq3[:, :, QK_NOPE:].reshape(Q_LORA, H_A * QK_ROPE).astype(BF16),
                "w_uk": w_uk[i].astype(BF16),
                "w_uv": w_uv[i].astype(BF16),
                "diff_lam": diff_lam[i],
                "diff_subln_w": diff_subln_w[i][None, :],
            }
            outs_p = _proj_even(hp, mod, l, False, nw[0], w, rope)
            outs_s = _proj_even(hs, mod, l, True, nw[0], w, rope)
            even_states.append(outs_p[8:])
            o_p = [_attn_even(outs_p[:8], False, lam_init, None, w)]
            caches = (cache_mla_ckv[:, i], cache_mla_kpe[:, i],
                      cache_diff_k[:, i].reshape(DEC_BATCH, PAST_LEN, 512),
                      cache_diff_v[:, i].reshape(DEC_BATCH, PAST_LEN, 512))
            o_s = [_attn_even(outs_s, True, lam_init, caches, w)]
            w_out = w_out_even[i].astype(BF16)
        else:
            q_w = jnp.tile(qk_norm_w[i, 0], H_C)[None, :]
            k_w = jnp.tile(qk_norm_w[i, 1], KV_C)[None, :]
            w = {"w_in": w_in_odd[i].astype(BF16), "q_w": q_w, "k_w": k_w,
                 "gq": _group_mean_matrix(512), "gk": _group_mean_matrix(128)}
            outs_p = _proj_odd(hp, mod, l, False, nw[0], w, rope)
            outs_s = _proj_odd(hs, mod, l, True, nw[0], w, rope)
            odd_states.append(outs_p[6:])
            o_p = [_attn_odd_prompt(outs_p[:6])]
            qc, kc, vc, qn, kn, vn = outs_s
            o_c = _gqa_sample(qc, kc, vc, cache_gqa_k[:, i].reshape(DEC_BATCH, PAST_LEN, 128),
                              cache_gqa_v[:, i].reshape(DEC_BATCH, PAST_LEN, 128))
            o_d = _na_sample(qn, kn, vn, cache_na_k[:, i].reshape(DEC_BATCH, PAST_LEN, 512),
                             cache_na_v[:, i].reshape(DEC_BATCH, PAST_LEN, 512), _na_bias_table(na_rpb[i]))
            o_s = [o_c, o_d]
            w_out = w_out_odd[i].astype(BF16)
        wf = {"w_up": w_up[l].astype(BF16), "conv_w": conv_w[l], "conv_b": conv_b[l][None, :],
              "w_down": w_down[l].astype(BF16)}
        h1p, u2p = _post_attn(o_p, hp, w_out, mod, l, False, nw[1], nw[2])
        h1s, u2s = _post_attn(o_s, hs, w_out, mod, l, True, nw[1], nw[2])
        hp = _ffn(u2p, h1p, wf, mod, l, False, nw[3])
        hs = _ffn(u2s, h1s, wf, mod, l, True, nw[3])

    def stack(states, k, shape):
        return jnp.stack([st[k].reshape((BATCH, SEQ) + shape) for st in states], axis=1)

    new_mla_ckv = stack(even_states, 0, (KV_LORA,))
    new_mla_kpe = stack(even_states, 1, (QK_ROPE,))
    new_diff_k = stack(even_states, 2, (H_B, 2 * DH_B))
    new_diff_v = stack(even_states, 3, (H_B, 2 * DH_B))
    new_gqa_k = stack(odd_states, 0, (KV_C, DH_C))
    new_gqa_v = stack(odd_states, 1, (KV_C, DH_C))
    new_na_k = stack(odd_states, 2, (H_D, DH_D))
    new_na_v = stack(odd_states, 3, (H_D, DH_D))
    return (hp.reshape(BATCH, SEQ, D_MODEL), hs.reshape(DEC_BATCH, DEC_SEQ, D_MODEL),
            new_mla_ckv, new_mla_kpe, new_diff_k, new_diff_v, new_gqa_k, new_gqa_v, new_na_k, new_na_v)
```

```python
import functools
import math

import numpy as np
import jax
import jax.numpy as jnp
from jax import lax
from jax.experimental import pallas as pl
from jax.experimental.pallas import tpu as pltpu

D_MODEL = 1024
BATCH = 32
SEQ = 256
DEPTH = 2
DEC_BATCH = 4
DEC_SEQ = 1024
PAST_LEN = 256
GRID_W = 64
HEAD_DIM = 64
H_A = 8
QK_NOPE = 64
QK_ROPE = 32
V_A = 64
Q_LORA = 256
KV_LORA = 128
H_B = 4
DH_B = HEAD_DIM
H_C = 8
KV_C = 2
DH_C = HEAD_DIM
H_D = 8
DH_D = HEAD_DIM
NA_WIN_ROWS = 8
NA_WIN_COLS = 16
D_FF = 2816
ROPE_THETA = 10000.0
EPS = 1e-6
NEG_INF = -1e30

LANES = 128
MOD_ROWS = 8
ROW_TILE = 512
FFN_ROW_TILE = 1024
FF_CHUNK = 256
Q_TILE = 256
VMEM_LIMIT = 48 * 1024 * 1024
FFN_VMEM_LIMIT = 56 * 1024 * 1024

F32 = jnp.float32
BF16 = jnp.bfloat16


def _cparams(*sem):
    return pltpu.CompilerParams(dimension_semantics=sem, vmem_limit_bytes=VMEM_LIMIT)


def _dot(a, b):
    return jnp.dot(a, b, preferred_element_type=F32)


def _dot_nt(a, b):
    return lax.dot_general(a, b, (((1,), (1,)), ((), ())), preferred_element_type=F32)


def _rms(x, w):
    return x * lax.rsqrt(jnp.mean(x * x, axis=-1, keepdims=True) + EPS) * w


def _group_rms(x, w, gmat):
    x2 = x * x
    hi = x2.astype(BF16)
    lo = (x2 - hi.astype(F32)).astype(BF16)
    ms = _dot(hi, gmat) + _dot(lo, gmat)
    return x * lax.rsqrt(ms + EPS) * w


def _rope(x, cos, sin_signed, half):
    outs = []
    for j in range(x.shape[1] // LANES):
        xc = x[:, j * LANES:(j + 1) * LANES]
        lane = lax.broadcasted_iota(jnp.int32, xc.shape, 1)
        first = (lane % (2 * half)) < half
        partner = jnp.where(first, pltpu.roll(xc, LANES - half, 1), pltpu.roll(xc, half, 1))
        outs.append(xc * cos + partner * sin_signed)
    return outs[0] if len(outs) == 1 else jnp.concatenate(outs, axis=1)


def _softmax_pv(scores, values):
    m = scores[0].max(axis=-1, keepdims=True)
    for s in scores[1:]:
        m = jnp.maximum(m, s.max(axis=-1, keepdims=True))
    den = None
    acc = None
    for s, v in zip(scores, values):
        e = jnp.exp(s - m)
        d = e.sum(axis=-1, keepdims=True)
        o = _dot(e.astype(BF16), v)
        den = d if den is None else den + d
        acc = o if acc is None else acc + o
    return acc, den


def _mod_kernel(c_ref, w_ref, b_ref, o_ref):
    cv = c_ref[...]
    act = cv / (1.0 + jnp.exp(-cv))
    o_ref[...] = _dot(act.astype(BF16), w_ref[...].astype(BF16)) + b_ref[...]


def _modulation(cvecs, w_mod, b_mod):
    tn = 1024
    n = 6 * D_MODEL
    return pl.pallas_call(
        _mod_kernel,
        grid=(DEPTH, n // tn),
        in_specs=[
            pl.BlockSpec((MOD_ROWS, D_MODEL), lambda l, j: (0, 0)),
            pl.BlockSpec((None, D_MODEL, tn), lambda l, j: (l, 0, j)),
            pl.BlockSpec((None, 1, tn), lambda l, j: (l, 0, j)),
        ],
        out_specs=pl.BlockSpec((None, MOD_ROWS, tn), lambda l, j: (l, 0, j)),
        out_shape=jax.ShapeDtypeStruct((DEPTH, MOD_ROWS, n), F32),
        compiler_params=_cparams("parallel", "parallel"),
        name="adaln_mod",
    )(cvecs, w_mod, b_mod.reshape(DEPTH, 1, n))


def _mod_spec(layer, sample, tm):
    if sample:
        per = DEC_SEQ // tm
        return pl.BlockSpec((None, None, 6, D_MODEL), lambda i, *_: (layer, 1 + i // per, 0, 0))
    return pl.BlockSpec((None, None, 6, D_MODEL), lambda i, *_: (layer, 0, 0, 0))


def _full(shape):
    nd = len(shape)
    return pl.BlockSpec(shape, lambda *_: (0,) * nd)


def _rows(tm, width):
    return pl.BlockSpec((tm, width), lambda i, *_: (i, 0))


def _rope_spec(tm):
    per = DEC_SEQ // tm
    return pl.BlockSpec((tm, LANES), lambda i, *_: (i % per, 0))


EVEN_IN = 2048


def _proj_even_kernel(sample, *refs):
    if sample:
        (h_ref, mod_ref, nw_ref, win_ref, qnw_ref, kvnw_ref, wuqn_ref, wuqr_ref, wuk_ref, wuv_ref,
         ca_ref, sa_ref, ch_ref, sh_ref,
         qn_ref, qr_ref, kn_ref, va_ref, kpe_ref, qd_ref, kd_ref, vd_ref) = refs
    else:
        (h_ref, mod_ref, nw_ref, win_ref, qnw_ref, kvnw_ref, wuqn_ref, wuqr_ref, wuk_ref, wuv_ref,
         qn_ref, qr_ref, kn_ref, va_ref, kpe_ref, qd_ref, kd_ref, vd_ref,
         ckv_st_ref, kpe_st_ref, kd_st_ref, vd_st_ref) = refs
    mod = mod_ref[...]
    u = _rms(h_ref[...], nw_ref[...]) * (1.0 + mod[1:2]) + mod[0:1]
    z = _dot(u.astype(BF16), win_ref[...])
    cq = _rms(z[:, 0:256], qnw_ref[...]).astype(BF16)
    qn_ref[...] = _dot(cq, wuqn_ref[...]).astype(BF16)
    qr = _dot(cq, wuqr_ref[...])
    ckv = _rms(z[:, 256:384], kvnw_ref[...])
    ckv_b = ckv.astype(BF16)
    kn_ref[...] = _dot(ckv_b, wuk_ref[...]).astype(BF16)
    va_ref[...] = _dot(ckv_b, wuv_ref[...]).astype(BF16)
    kpe = z[:, 384:512]
    qd = z[:, 512:1024]
    kd = z[:, 1024:1536]
    vd = z[:, 1536:2048]
    vd_ref[...] = vd.astype(BF16)
    if sample:
        ca, sa, ch, sh = ca_ref[...], sa_ref[...], ch_ref[...], sh_ref[...]
        qr_ref[...] = _rope(qr, ca, sa, QK_ROPE // 2).astype(BF16)
        kpe_ref[...] = _rope(kpe, ca, sa, QK_ROPE // 2).astype(BF16)
        qd_ref[...] = _rope(qd, ch, sh, DH_B // 2).astype(BF16)
        kd_ref[...] = _rope(kd, ch, sh, DH_B // 2).astype(BF16)
    else:
        qr_ref[...] = qr.astype(BF16)
        kpe_ref[...] = kpe.astype(BF16)
        qd_ref[...] = qd.astype(BF16)
        kd_ref[...] = kd.astype(BF16)
        ckv_st_ref[...] = ckv
        kpe_st_ref[...] = kpe[:, :QK_ROPE]
        kd_st_ref[...] = kd
        vd_st_ref[...] = vd


def _proj_even(h, mod, layer, sample, nw, w, rope):
    n = h.shape[0]
    tm = ROW_TILE
    ins = [h, mod, nw, w["w_in"], w["q_norm_w"], w["kv_norm_w"], w["w_uq_n"], w["w_uq_r"], w["w_uk"], w["w_uv"]]
    in_specs = [_rows(tm, D_MODEL), _mod_spec(layer, sample, tm), _full((1, D_MODEL)), _full((D_MODEL, EVEN_IN)),
                _full((1, Q_LORA)), _full((1, KV_LORA)), _full((Q_LORA, 512)), _full((Q_LORA, 256)),
                _full((KV_LORA, 512)), _full((KV_LORA, 512))]
    widths = [512, 256, 512, 512, 128, 512, 512, 512]
    out_shape = [jax.ShapeDtypeStruct((n, wd), BF16) for wd in widths]
    out_specs = [_rows(tm, wd) for wd in widths]
    if sample:
        ins += [rope["ca"], rope["sa"], rope["ch"], rope["sh"]]
        in_specs += [_rope_spec(tm)] * 4
    else:
        for wd in (KV_LORA, QK_ROPE, 512, 512):
            out_shape.append(jax.ShapeDtypeStruct((n, wd), F32))
            out_specs.append(_rows(tm, wd))
    return pl.pallas_call(
        functools.partial(_proj_even_kernel, sample),
        grid=(n // tm,),
        in_specs=in_specs,
        out_specs=out_specs,
        out_shape=out_shape,
        compiler_params=_cparams("parallel"),
        name="proj_even_s" if sample else "proj_even_p",
    )(*ins)


def _attn_even_kernel(sample, lam_init, *refs):
    if sample:
        (qn_ref, qr_ref, kn_ref, va_ref, kpe_ref, qd_ref, kd_ref, vd_ref,
         cckv_ref, ckpe_ref, cdk_ref, cdv_ref, wuk_ref, wuv_ref, lam_ref, sub_ref, o_ref) = refs
    else:
        (qn_ref, qr_ref, kn_ref, va_ref, kpe_ref, qd_ref, kd_ref, vd_ref, lam_ref, sub_ref, o_ref) = refs
    lf = lam_ref[...]
    lam = (jnp.exp(jnp.sum(lf[0:1] * lf[1:2], axis=-1, keepdims=True))
           - jnp.exp(jnp.sum(lf[2:3] * lf[3:4], axis=-1, keepdims=True)) + lam_init)
    if sample:
        cckv = cckv_ref[...].astype(BF16)
        kn_ctx = _dot(cckv, wuk_ref[...]).astype(BF16)
        va_ctx = _dot(cckv, wuv_ref[...]).astype(BF16)
        kpe_ctx = ckpe_ref[...].astype(BF16)
        kd_ctx = cdk_ref[...].astype(BF16)
        vd_ctx = cdv_ref[...].astype(BF16)
    scale_a = (QK_NOPE + QK_ROPE) ** -0.5
    kpe = kpe_ref[:, 0:QK_ROPE]
    for hd in range(H_A):
        c0, c1 = hd * QK_NOPE, (hd + 1) * QK_NOPE
        qn = qn_ref[:, c0:c1]
        qr = qr_ref[:, hd * QK_ROPE:(hd + 1) * QK_ROPE]
        scores = [(_dot_nt(qn, kn_ref[:, c0:c1]) + _dot_nt(qr, kpe)) * scale_a]
        values = [va_ref[:, hd * V_A:(hd + 1) * V_A]]
        if sample:
            scores.append((_dot_nt(qn, kn_ctx[:, c0:c1]) + _dot_nt(qr, kpe_ctx)) * scale_a)
            values.append(va_ctx[:, hd * V_A:(hd + 1) * V_A])
        acc, den = _softmax_pv(scores, values)
        o_ref[:, hd * V_A:(hd + 1) * V_A] = (acc / den).astype(BF16)
    scale_b = DH_B ** -0.5
    base = H_A * V_A
    sub_w = sub_ref[...]
    for hd in range(H_B):
        g0 = hd * 2 * DH_B
        outs = []
        for comp in range(2):
            c0, c1 = g0 + comp * DH_B, g0 + (comp + 1) * DH_B
            q = qd_ref[:, c0:c1]
            scores = [_dot_nt(q, kd_ref[:, c0:c1]) * scale_b]
            values = [vd_ref[:, g0:g0 + 2 * DH_B]]
            if sample:
                scores.append(_dot_nt(q, kd_ctx[:, c0:c1]) * scale_b)
                values.append(vd_ctx[:, g0:g0 + 2 * DH_B])
            acc, den = _softmax_pv(scores, values)
            outs.append(acc / den)
        ob = outs[0] - lam * outs[1]
        ob = _rms(ob, sub_w) * (1.0 - lam_init)
        o_ref[:, base + g0:base + g0 + 2 * DH_B] = ob.astype(BF16)


def _attn_even(p, sample, lam_init, caches, w):
    qn, qr, kn, va, kpe, qd, kd, vd = p
    n = qn.shape[0]
    if sample:
        tq, per = Q_TILE, DEC_SEQ // Q_TILE
        grid = (DEC_BATCH, per)
        qspec = lambda wd: pl.BlockSpec((tq, wd), lambda b, i: (b * per + i, 0))
        kspec = lambda wd: pl.BlockSpec((DEC_SEQ, wd), lambda b, i: (b, 0))
        cspec = lambda wd: pl.BlockSpec((None, PAST_LEN, wd), lambda b, i: (b, 0, 0))
        ins = [qn, qr, kn, va, kpe, qd, kd, vd, *caches, w["w_uk"], w["w_uv"], w["diff_lam"], w["diff_subln_w"]]
        in_specs = [qspec(512), qspec(256), kspec(512), kspec(512), kspec(128), qspec(512), kspec(512), kspec(512),
                    cspec(KV_LORA), cspec(QK_ROPE), cspec(512), cspec(512),
                    _full((KV_LORA, 512)), _full((KV_LORA, 512)), _full((4, DH_B)), _full((1, 2 * DH_B))]
        out_spec = qspec(D_MODEL)
        sem = ("parallel", "parallel")
    else:
        grid = (BATCH,)
        spec = lambda wd: pl.BlockSpec((SEQ, wd), lambda b: (b, 0))
        ins = [qn, qr, kn, va, kpe, qd, kd, vd, w["diff_lam"], w["diff_subln_w"]]
        in_specs = [spec(512), spec(256), spec(512), spec(512), spec(128), spec(512), spec(512), spec(512),
                    _full((4, DH_B)), _full((1, 2 * DH_B))]
        out_spec = spec(D_MODEL)
        sem = ("parallel",)
    return pl.pallas_call(
        functools.partial(_attn_even_kernel, sample, lam_init),
        grid=grid,
        in_specs=in_specs,
        out_specs=out_spec,
        out_shape=jax.ShapeDtypeStruct((n, D_MODEL), BF16),
        compiler_params=_cparams(*sem),
        name="attn_even_s" if sample else "attn_even_p",
    )(*ins)


ODD_IN = 2304


def _proj_odd_kernel(sample, *refs):
    if sample:
        (h_ref, mod_ref, nw_ref, win_ref, qw_ref, kw_ref, gq_ref, gk_ref, ch_ref, sh_ref,
         qc_ref, kc_ref, vc_ref, qn_ref, kn_ref, vn_ref) = refs
    else:
        (h_ref, mod_ref, nw_ref, win_ref, qw_ref, kw_ref, gq_ref, gk_ref,
         qc_ref, kc_ref, vc_ref, qn_ref, kn_ref, vn_ref,
         kc_st_ref, vc_st_ref, kn_st_ref, vn_st_ref) = refs
    mod = mod_ref[...]
    u = _rms(h_ref[...], nw_ref[...]) * (1.0 + mod[1:2]) + mod[0:1]
    z = _dot(u.astype(BF16), win_ref[...])
    qc = _group_rms(z[:, 0:512], qw_ref[...], gq_ref[...])
    kc = _group_rms(z[:, 512:640], kw_ref[...], gk_ref[...])
    vc = z[:, 640:768]
    kn = z[:, 1280:1792]
    vn = z[:, 1792:2304]
    vc_ref[...] = vc.astype(BF16)
    qn_ref[...] = z[:, 768:1280].astype(BF16)
    kn_ref[...] = kn.astype(BF16)
    vn_ref[...] = vn.astype(BF16)
    if sample:
        ch, sh = ch_ref[...], sh_ref[...]
        qc_ref[...] = _rope(qc, ch, sh, DH_C // 2).astype(BF16)
        kc_ref[...] = _rope(kc, ch, sh, DH_C // 2).astype(BF16)
    else:
        qc_ref[...] = qc.astype(BF16)
        kc_ref[...] = kc.astype(BF16)
        kc_st_ref[...] = kc
        vc_st_ref[...] = vc
        kn_st_ref[...] = kn
        vn_st_ref[...] = vn


def _proj_odd(h, mod, layer, sample, nw, w, rope):
    n = h.shape[0]
    tm = ROW_TILE
    ins = [h, mod, nw, w["w_in"], w["q_w"], w["k_w"], w["gq"], w["gk"]]
    in_specs = [_rows(tm, D_MODEL), _mod_spec(layer, sample, tm), _full((1, D_MODEL)), _full((D_MODEL, ODD_IN)),
                _full((1, 512)), _full((1, 128)), _full((512, 512)), _full((128, 128))]
    widths = [512, 128, 128, 512, 512, 512]
    out_shape = [jax.ShapeDtypeStruct((n, wd), BF16) for wd in widths]
    out_specs = [_rows(tm, wd) for wd in widths]
    if sample:
        ins += [rope["ch"], rope["sh"]]
        in_specs += [_rope_spec(tm)] * 2
    else:
        for wd in (128, 128, 512, 512):
            out_shape.append(jax.ShapeDtypeStruct((n, wd), F32))
            out_specs.append(_rows(tm, wd))
    return pl.pallas_call(
        functools.partial(_proj_odd_kernel, sample),
        grid=(n // tm,),
        in_specs=in_specs,
        out_specs=out_specs,
        out_shape=out_shape,
        compiler_params=_cparams("parallel"),
        name="proj_odd_s" if sample else "proj_odd_p",
    )(*ins)


def _attn_odd_prompt_kernel(qc_ref, kc_ref, vc_ref, qn_ref, kn_ref, vn_ref, o_ref):
    scale = DH_C ** -0.5
    group = H_C // KV_C
    for hd in range(H_C):
        kv = hd // group
        q = qc_ref[:, hd * DH_C:(hd + 1) * DH_C]
        s = _dot_nt(q, kc_ref[:, kv * DH_C:(kv + 1) * DH_C]) * scale
        acc, den = _softmax_pv([s], [vc_ref[:, kv * DH_C:(kv + 1) * DH_C]])
        o_ref[:, hd * DH_C:(hd + 1) * DH_C] = (acc / den).astype(BF16)
    base = H_C * DH_C
    scale = DH_D ** -0.5
    for hd in range(H_D):
        c0, c1 = hd * DH_D, (hd + 1) * DH_D
        s = _dot_nt(qn_ref[:, c0:c1], kn_ref[:, c0:c1]) * scale
        acc, den = _softmax_pv([s], [vn_ref[:, c0:c1]])
        o_ref[:, base + c0:base + c1] = (acc / den).astype(BF16)


def _attn_odd_prompt(p):
    qc, kc, vc, qn, kn, vn = p
    spec = lambda wd: pl.BlockSpec((SEQ, wd), lambda b: (b, 0))
    return pl.pallas_call(
        _attn_odd_prompt_kernel,
        grid=(BATCH,),
        in_specs=[spec(512), spec(128), spec(128), spec(512), spec(512), spec(512)],
        out_specs=spec(D_MODEL),
        out_shape=jax.ShapeDtypeStruct((qc.shape[0], D_MODEL), BF16),
        compiler_params=_cparams("parallel"),
        name="attn_odd_p",
    )(qc, kc, vc, qn, kn, vn)


def _gqa_sample_kernel(q_ref, k_ref, v_ref, ck_ref, cv_ref, o_ref):
    scale = DH_C ** -0.5
    group = H_C // KV_C
    k_ctx = ck_ref[...].astype(BF16)
    v_ctx = cv_ref[...].astype(BF16)
    for hd in range(H_C):
        kv = hd // group
        c0, c1 = kv * DH_C, (kv + 1) * DH_C
        q = q_ref[:, hd * DH_C:(hd + 1) * DH_C]
        scores = [_dot_nt(q, k_ref[:, c0:c1]) * scale, _dot_nt(q, k_ctx[:, c0:c1]) * scale]
        acc, den = _softmax_pv(scores, [v_ref[:, c0:c1], v_ctx[:, c0:c1]])
        o_ref[:, hd * DH_C:(hd + 1) * DH_C] = (acc / den).astype(BF16)


def _gqa_sample(qc, kc, vc, cache_k, cache_v):
    tq, per = Q_TILE, DEC_SEQ // Q_TILE
    return pl.pallas_call(
        _gqa_sample_kernel,
        grid=(DEC_BATCH, per),
        in_specs=[pl.BlockSpec((tq, 512), lambda b, i: (b * per + i, 0)),
                  pl.BlockSpec((DEC_SEQ, 128), lambda b, i: (b, 0)),
                  pl.BlockSpec((DEC_SEQ, 128), lambda b, i: (b, 0)),
                  pl.BlockSpec((None, PAST_LEN, 128), lambda b, i: (b, 0, 0)),
                  pl.BlockSpec((None, PAST_LEN, 128), lambda b, i: (b, 0, 0))],
        out_specs=pl.BlockSpec((tq, 512), lambda b, i: (b * per + i, 0)),
        out_shape=jax.ShapeDtypeStruct((qc.shape[0], 512), BF16),
        compiler_params=_cparams("parallel", "parallel"),
        name="gqa_s",
    )(qc, kc, vc, cache_k, cache_v)


NA_ROWS = DEC_SEQ // GRID_W
NA_KR = min(NA_WIN_ROWS, NA_ROWS)
NA_LOC = NA_KR * GRID_W


def _na_sample_kernel(q_ref, k_ref, v_ref, ck_ref, cv_ref, bias_ref, o_ref):
    r = pl.program_id(1)
    rs = jnp.clip(r - NA_KR // 2, 0, NA_ROWS - NA_KR)
    start = pl.multiple_of(rs * GRID_W, GRID_W)
    scale = DH_D ** -0.5
    wq = lax.broadcasted_iota(jnp.int32, (GRID_W, NA_LOC), 0)
    wk = lax.broadcasted_iota(jnp.int32, (GRID_W, NA_LOC), 1) % GRID_W
    cs = jnp.clip(wq - NA_WIN_COLS // 2, 0, GRID_W - NA_WIN_COLS)
    col_ok = (wk >= cs) & (wk < cs + NA_WIN_COLS)
    k_loc = k_ref[pl.ds(start, NA_LOC), :]
    v_loc = v_ref[pl.ds(start, NA_LOC), :]
    k_ctx = ck_ref[...].astype(BF16)
    v_ctx = cv_ref[...].astype(BF16)
    for hd in range(H_D):
        c0, c1 = hd * DH_D, (hd + 1) * DH_D
        q = q_ref[:, c0:c1]
        s_loc = _dot_nt(q, k_loc[:, c0:c1]) * scale + bias_ref[hd]
        s_loc = jnp.where(col_ok, s_loc, NEG_INF)
        s_ctx = _dot_nt(q, k_ctx[:, c0:c1]) * scale
        acc, den = _softmax_pv([s_loc, s_ctx], [v_loc[:, c0:c1], v_ctx[:, c0:c1]])
        o_ref[:, c0:c1] = (acc / den).astype(BF16)


def _na_sample(qn, kn, vn, cache_k, cache_v, bias):
    def bias_map(b, r):
        rs = jnp.clip(r - NA_KR // 2, 0, NA_ROWS - NA_KR)
        return (0, rs - r + NA_WIN_ROWS - 1, 0, 0)
    return pl.pallas_call(
        _na_sample_kernel,
        grid=(DEC_BATCH, NA_ROWS),
        in_specs=[pl.BlockSpec((GRID_W, 512), lambda b, r: (b * NA_ROWS + r, 0)),
                  pl.BlockSpec((DEC_SEQ, 512), lambda b, r: (b, 0)),
                  pl.BlockSpec((DEC_SEQ, 512), lambda b, r: (b, 0)),
                  pl.BlockSpec((None, PAST_LEN, 512), lambda b, r: (b, 0, 0)),
                  pl.BlockSpec((None, PAST_LEN, 512), lambda b, r: (b, 0, 0)),
                  pl.BlockSpec((H_D, None, GRID_W, NA_LOC), bias_map)],
        out_specs=pl.BlockSpec((GRID_W, 512), lambda b, r: (b * NA_ROWS + r, 0)),
        out_shape=jax.ShapeDtypeStruct((qn.shape[0], 512), BF16),
        compiler_params=_cparams("parallel", "parallel"),
        name="na_s",
    )(qn, kn, vn, cache_k, cache_v, bias)


def _na_bias_table(rpb):
    edge = GRID_W - NA_WIN_COLS
    ext = jnp.pad(rpb.astype(F32), ((0, 0), (0, 0), (edge, edge)), mode="edge")
    toep = jnp.stack([ext[:, :, GRID_W - 1 - wq:2 * GRID_W - 1 - wq] for wq in range(GRID_W)], axis=1)
    flat = toep.reshape(H_D, GRID_W, (2 * NA_WIN_ROWS - 1) * GRID_W)
    return jnp.stack([flat[:, :, d0 * GRID_W:d0 * GRID_W + NA_LOC] for d0 in range(NA_WIN_ROWS)], axis=1)


def _post_attn_kernel(n_parts, *refs):
    o_refs = refs[:n_parts]
    h_ref, wout_ref, mod_ref, nw1_ref, h1_ref = refs[n_parts:]
    y = None
    off = 0
    for o_ref in o_refs:
        wd = o_ref.shape[1]
        part = _dot(o_ref[...], wout_ref[off:off + wd, :])
        y = part if y is None else y + part
        off += wd
    mod = mod_ref[...]
    h1_ref[...] = h_ref[...] + mod[2:3] * _rms(y, nw1_ref[...])


def _post_attn(o_parts, h, w_out, mod, layer, sample, nw1):
    n = h.shape[0]
    tm = ROW_TILE
    in_specs = [_rows(tm, o.shape[1]) for o in o_parts]
    in_specs += [_rows(tm, D_MODEL), _full((D_MODEL, D_MODEL)), _mod_spec(layer, sample, tm), _full((1, D_MODEL))]
    return pl.pallas_call(
        functools.partial(_post_attn_kernel, len(o_parts)),
        grid=(n // tm,),
        in_specs=in_specs,
        out_specs=_rows(tm, D_MODEL),
        out_shape=jax.ShapeDtypeStruct((n, D_MODEL), F32),
        compiler_params=_cparams("parallel"),
        name="post_attn_s" if sample else "post_attn_p",
    )(*o_parts, h, w_out, mod, nw1)


FF_PAIR = 2 * FF_CHUNK
N_FF = D_FF // FF_CHUNK
SUBLANES = 8
ACT_TILES = 4
ROW_BLOCK = 256
NORM_ROWS = 32


def _ffn_kernel(seq_len, h_ref, wup_ref, cw_ref, cb_ref, wd_ref, mod_ref, nw2_ref, nw3_ref, o_ref,
                u_ref, z0_ref, z1_ref, a_ref):
    tm = h_ref.shape[0]
    n_blocks = tm // ROW_BLOCK
    rows = ACT_TILES * SUBLANES
    mod = mod_ref[...]
    sub = lax.broadcasted_iota(jnp.int32, (SUBLANES, LANES), 0)
    zero_rows = jnp.zeros((SUBLANES, FF_PAIR), F32)
    for z_ref in (z0_ref, z1_ref):
        z_ref[0:SUBLANES, :] = zero_rows
        z_ref[SUBLANES + tm:2 * SUBLANES + tm, :] = zero_rows

    nw2 = nw2_ref[...]
    for c in range(tm // NORM_ROWS):
        x = h_ref[c * NORM_ROWS:(c + 1) * NORM_ROWS, :]
        u = _rms(x, nw2) * (1.0 + mod[4:5]) + mod[3:4]
        u_ref[c * NORM_ROWS:(c + 1) * NORM_ROWS, :] = u.astype(BF16)

    def up(j, z_ref, blk):
        r0 = blk * ROW_BLOCK
        z_ref[SUBLANES + r0:SUBLANES + r0 + ROW_BLOCK, :] = _dot(u_ref[r0:r0 + ROW_BLOCK, :], wup_ref[j])

    def act(j, z_ref, col, blk):
        cw = cw_ref[j]
        cb = cb_ref[j]
        for lc in range(FF_CHUNK // LANES):
            taps = []
            for lane0 in (lc * LANES, FF_CHUNK + lc * LANES):
                lanes = slice(lane0, lane0 + LANES)
                taps.append([jnp.broadcast_to(cw[k:k + 1, lanes], (rows, LANES)) for k in range(3)]
                            + [jnp.broadcast_to(cb[:, lanes], (rows, LANES))])
            for c in range(ROW_BLOCK // rows):
                r = blk * ROW_BLOCK + c * rows
                first = r % seq_len == 0
                last = (r + rows) % seq_len == 0

                def conv(lane0, tap):
                    ext = z_ref[r:r + rows + 2 * SUBLANES, lane0:lane0 + LANES]
                    tiles = [ext[t * SUBLANES:(t + 1) * SUBLANES] for t in range(ACT_TILES + 2)]
                    down = [pltpu.roll(t, 1, 0) for t in tiles[:-1]]
                    up_ = [pltpu.roll(t, SUBLANES - 1, 0) for t in tiles[1:]]
                    prev, nxt = [], []
                    for t in range(ACT_TILES):
                        above = 0.0 if (first and t == 0) else down[t]
                        below = 0.0 if (last and t == ACT_TILES - 1) else up_[t + 1]
                        prev.append(jnp.where(sub == 0, above, down[t + 1]))
                        nxt.append(jnp.where(sub == SUBLANES - 1, below, up_[t]))
                    prev = jnp.concatenate(prev, axis=0)
                    nxt = jnp.concatenate(nxt, axis=0)
                    return prev * tap[0] + ext[SUBLANES:SUBLANES + rows] * tap[1] + nxt * tap[2] + tap[3]

                g = conv(lc * LANES, taps[0])
                v = conv(FF_CHUNK + lc * LANES, taps[1])
                a = (g / (1.0 + jnp.exp(-g))) * v
                lane = col + lc * LANES
                if not isinstance(lane, int):
                    lane = pl.multiple_of(lane, LANES)
                a_ref[r:r + rows, pl.ds(lane, LANES)] = a.astype(BF16)

    for blk in range(n_blocks):
        up(0, z0_ref, blk)

    def pair(i, carry):
        j = 2 * i
        col = pl.multiple_of(j * FF_CHUNK, FF_CHUNK)
        for blk in range(n_blocks):
            up(j + 1, z1_ref, blk)
            act(j, z0_ref, col, blk)
        for blk in range(n_blocks):
            up(j + 2, z0_ref, blk)
            act(j + 1, z1_ref, col + FF_CHUNK, blk)
        return carry

    lax.fori_loop(0, (N_FF - 1) // 2, pair, 0)
    for blk in range(n_blocks):
        act(N_FF - 1, z0_ref, (N_FF - 1) * FF_CHUNK, blk)
    nw3 = nw3_ref[...]
    for blk in range(n_blocks):
        r0 = blk * ROW_BLOCK
        y = _dot(a_ref[r0:r0 + ROW_BLOCK, :], wd_ref[...])
        for c in range(ROW_BLOCK // NORM_ROWS):
            rs = slice(r0 + c * NORM_ROWS, r0 + (c + 1) * NORM_ROWS)
            o_ref[rs, :] = h_ref[rs, :] + mod[5:6] * _rms(y[c * NORM_ROWS:(c + 1) * NORM_ROWS], nw3)


def _ffn(h1, w, mod, layer, sample, nw2, nw3):
    n = h1.shape[0]
    tm = FFN_ROW_TILE
    seq_len = DEC_SEQ if sample else SEQ
    once = pl.Buffered(1)
    in_specs = [
        _rows(tm, D_MODEL),
        pl.BlockSpec((N_FF, D_MODEL, FF_PAIR), lambda i: (0, 0, 0), pipeline_mode=once),
        pl.BlockSpec((N_FF, 3, FF_PAIR), lambda i: (0, 0, 0), pipeline_mode=once),
        pl.BlockSpec((N_FF, 1, FF_PAIR), lambda i: (0, 0, 0), pipeline_mode=once),
        pl.BlockSpec((D_FF, D_MODEL), lambda i: (0, 0), pipeline_mode=once),
        _mod_spec(layer, sample, tm),
        _full((1, D_MODEL)),
        _full((1, D_MODEL)),
    ]
    return pl.pallas_call(
        functools.partial(_ffn_kernel, seq_len),
        grid=(n // tm,),
        in_specs=in_specs,
        out_specs=_rows(tm, D_MODEL),
        out_shape=jax.ShapeDtypeStruct((n, D_MODEL), F32),
        scratch_shapes=[pltpu.VMEM((tm, D_MODEL), BF16), pltpu.VMEM((tm + 2 * SUBLANES, FF_PAIR), F32),
                        pltpu.VMEM((tm + 2 * SUBLANES, FF_PAIR), F32), pltpu.VMEM((tm, D_FF), BF16)],
        compiler_params=pltpu.CompilerParams(dimension_semantics=("parallel",), vmem_limit_bytes=FFN_VMEM_LIMIT),
        name="ffn_s" if sample else "ffn_p",
    )(h1, w["w_up"], w["conv_w"], w["conv_b"], w["w_down"], mod, nw2, nw3)


def _pair_chunks(x):
    lead = x.shape[:-1]
    x = x.reshape(lead + (2, N_FF, FF_CHUNK))
    x = jnp.moveaxis(x, -2, 0)
    return x.reshape((N_FF,) + lead + (FF_PAIR,))


def _rope_tables():
    def table(rot_dim):
        t = np.arange(DEC_SEQ)
        n_freq = rot_dim // 4
        inv = 1.0 / (ROPE_THETA ** (np.arange(n_freq) / n_freq))
        ang = np.concatenate([(t // GRID_W)[:, None] * inv[None, :], (t % GRID_W)[:, None] * inv[None, :]], axis=-1)
        cos = np.cos(ang).astype(np.float32)
        sin = np.sin(ang).astype(np.float32)
        reps = LANES // rot_dim
        return (jnp.asarray(np.tile(np.concatenate([cos, cos], axis=-1), (1, reps))),
                jnp.asarray(np.tile(np.concatenate([-sin, sin], axis=-1), (1, reps))))
    ca, sa = table(QK_ROPE)
    ch, sh = table(HEAD_DIM)
    return {"ca": ca, "sa": sa, "ch": ch, "sh": sh}


def _group_mean_matrix(width):
    idx = np.arange(width) // HEAD_DIM
    return jnp.asarray((idx[:, None] == idx[None, :]).astype(np.float32) / HEAD_DIM, BF16)


def kernel(x_prompt, x_sample, c, cache_mla_ckv, cache_mla_kpe, cache_diff_k, cache_diff_v, cache_gqa_k, cache_gqa_v, cache_na_k, cache_na_v, c_ctx, norm_w, w_mod, b_mod, w_in_even, w_out_even, w_uq, q_norm_w, kv_norm_w, w_uk, w_uv, diff_lam, diff_subln_w, w_in_odd, w_out_odd, qk_norm_w, na_rpb, w_up, conv_w, conv_b, w_down):
    rope = _rope_tables()
    n_p = BATCH * SEQ
    n_s = DEC_BATCH * DEC_SEQ
    cvecs = jnp.concatenate([c_ctx[None, :], c, jnp.zeros((MOD_ROWS - 1 - DEC_BATCH, D_MODEL), F32)], axis=0)
    mod = _modulation(cvecs, w_mod, b_mod).reshape(DEPTH, MOD_ROWS, 6, D_MODEL)
    hp = x_prompt.reshape(n_p, D_MODEL)
    hs = x_sample.reshape(n_s, D_MODEL)
    even_states, odd_states = [], []
    for l in range(DEPTH):
        i = l // 2
        nw = [norm_w[l, k][None, :] for k in range(4)]
        if l % 2 == 0:
            lam_init = 0.8 - 0.6 * math.exp(-0.3 * l)
            wi = w_in_even[i]
            w_uq3 = w_uq[i].reshape(Q_LORA, H_A, QK_NOPE + QK_ROPE)
            w = {
                "w_in": jnp.concatenate([wi[:, :416], jnp.zeros((D_MODEL, 96), F32), wi[:, 416:]], axis=1).astype(BF16),
                "q_norm_w": q_norm_w[i][None, :],
                "kv_norm_w": kv_norm_w[i][None, :],
                "w_uq_n": w_uq3[:, :, :QK_NOPE].reshape(Q_LORA, H_A * QK_NOPE).astype(BF16),
                "w_uq_r": w_uq3[:, :, QK_NOPE:].reshape(Q_LORA, H_A * QK_ROPE).astype(BF16),
                "w_uk": w_uk[i].astype(BF16),
                "w_uv": w_uv[i].astype(BF16),
                "diff_lam": diff_lam[i],
                "diff_subln_w": diff_subln_w[i][None, :],
            }
            outs_p = _proj_even(hp, mod, l, False, nw[0], w, rope)
            outs_s = _proj_even(hs, mod, l, True, nw[0], w, rope)
            even_states.append(outs_p[8:])
            o_p = [_attn_even(outs_p[:8], False, lam_init, None, w)]
            caches = (cache_mla_ckv[:, i], cache_mla_kpe[:, i],
                      cache_diff_k[:, i].reshape(DEC_BATCH, PAST_LEN, 512),
                      cache_diff_v[:, i].reshape(DEC_BATCH, PAST_LEN, 512))
            o_s = [_attn_even(outs_s, True, lam_init, caches, w)]
            w_out = w_out_even[i].astype(BF16)
        else:
            q_w = jnp.tile(qk_norm_w[i, 0], H_C)[None, :]
            k_w = jnp.tile(qk_norm_w[i, 1], KV_C)[None, :]
            w = {"w_in": w_in_odd[i].astype(BF16), "q_w": q_w, "k_w": k_w,
                 "gq": _group_mean_matrix(512), "gk": _group_mean_matrix(128)}
            outs_p = _proj_odd(hp, mod, l, False, nw[0], w, rope)
            outs_s = _proj_odd(hs, mod, l, True, nw[0], w, rope)
            odd_states.append(outs_p[6:])
            o_p = [_attn_odd_prompt(outs_p[:6])]
            qc, kc, vc, qn, kn, vn = outs_s
            o_c = _gqa_sample(qc, kc, vc, cache_gqa_k[:, i].reshape(DEC_BATCH, PAST_LEN, 128),
                              cache_gqa_v[:, i].reshape(DEC_BATCH, PAST_LEN, 128))
            o_d = _na_sample(qn, kn, vn, cache_na_k[:, i].reshape(DEC_BATCH, PAST_LEN, 512),
                             cache_na_v[:, i].reshape(DEC_BATCH, PAST_LEN, 512), _na_bias_table(na_rpb[i]))
            o_s = [o_c, o_d]
            w_out = w_out_odd[i].astype(BF16)
        wf = {"w_up": _pair_chunks(w_up[l].astype(BF16)), "conv_w": _pair_chunks(conv_w[l]),
              "conv_b": _pair_chunks(conv_b[l][None, :]), "w_down": w_down[l].astype(BF16)}
        h1p = _post_attn(o_p, hp, w_out, mod, l, False, nw[1])
        h1s = _post_attn(o_s, hs, w_out, mod, l, True, nw[1])
        hp = _ffn(h1p, wf, mod, l, False, nw[2], nw[3])
        hs = _ffn(h1s, wf, mod, l, True, nw[2], nw[3])

    def stack(states, k, shape):
        return jnp.stack([st[k].reshape((BATCH, SEQ) + shape) for st in states], axis=1)

    new_mla_ckv = stack(even_states, 0, (KV_LORA,))
    new_mla_kpe = stack(even_states, 1, (QK_ROPE,))
    new_diff_k = stack(even_states, 2, (H_B, 2 * DH_B))
    new_diff_v = stack(even_states, 3, (H_B, 2 * DH_B))
    new_gqa_k = stack(odd_states, 0, (KV_C, DH_C))
    new_gqa_v = stack(odd_states, 1, (KV_C, DH_C))
    new_na_k = stack(odd_states, 2, (H_D, DH_D))
    new_na_v = stack(odd_states, 3, (H_D, DH_D))
    return (hp.reshape(BATCH, SEQ, D_MODEL), hs.reshape(DEC_BATCH, DEC_SEQ, D_MODEL),
            new_mla_ckv, new_mla_kpe, new_diff_k, new_diff_v, new_gqa_k, new_gqa_v, new_na_k, new_na_v)
```

```python
import functools
import math

import numpy as np
import jax
import jax.numpy as jnp
from jax import lax
from jax.experimental import pallas as pl
from jax.experimental.pallas import tpu as pltpu

D_MODEL = 1024
BATCH = 32
SEQ = 256
DEPTH = 2
DEC_BATCH = 4
DEC_SEQ = 1024
PAST_LEN = 256
GRID_W = 64
HEAD_DIM = 64
H_A = 8
QK_NOPE = 64
QK_ROPE = 32
V_A = 64
Q_LORA = 256
KV_LORA = 128
H_B = 4
DH_B = HEAD_DIM
H_C = 8
KV_C = 2
DH_C = HEAD_DIM
H_D = 8
DH_D = HEAD_DIM
NA_WIN_ROWS = 8
NA_WIN_COLS = 16
D_FF = 2816
ROPE_THETA = 10000.0
EPS = 1e-6
NEG_INF = -1e30

LANES = 128
MOD_ROWS = 8
ROW_TILE = 512
FFN_ROW_TILE = 1024
FF_CHUNK = 256
Q_TILE = 256
VMEM_LIMIT = 48 * 1024 * 1024
FFN_VMEM_LIMIT = 56 * 1024 * 1024

F32 = jnp.float32
BF16 = jnp.bfloat16


def _cparams(*sem):
    return pltpu.CompilerParams(dimension_semantics=sem, vmem_limit_bytes=VMEM_LIMIT)


def _dot(a, b):
    return jnp.dot(a, b, preferred_element_type=F32)


def _dot_nt(a, b):
    return lax.dot_general(a, b, (((1,), (1,)), ((), ())), preferred_element_type=F32)


def _rms(x, w):
    return x * lax.rsqrt(jnp.mean(x * x, axis=-1, keepdims=True) + EPS) * w


def _group_rms(x, w, gmat):
    x2 = x * x
    hi = x2.astype(BF16)
    lo = (x2 - hi.astype(F32)).astype(BF16)
    ms = _dot(hi, gmat) + _dot(lo, gmat)
    return x * lax.rsqrt(ms + EPS) * w


def _rope(x, cos, sin_signed, half):
    outs = []
    for j in range(x.shape[1] // LANES):
        xc = x[:, j * LANES:(j + 1) * LANES]
        lane = lax.broadcasted_iota(jnp.int32, xc.shape, 1)
        first = (lane % (2 * half)) < half
        partner = jnp.where(first, pltpu.roll(xc, LANES - half, 1), pltpu.roll(xc, half, 1))
        outs.append(xc * cos + partner * sin_signed)
    return outs[0] if len(outs) == 1 else jnp.concatenate(outs, axis=1)


def _softmax_pv(scores, values):
    m = scores[0].max(axis=-1, keepdims=True)
    for s in scores[1:]:
        m = jnp.maximum(m, s.max(axis=-1, keepdims=True))
    den = None
    acc = None
    for s, v in zip(scores, values):
        e = jnp.exp(s - m)
        d = e.sum(axis=-1, keepdims=True)
        o = _dot(e.astype(BF16), v)
        den = d if den is None else den + d
        acc = o if acc is None else acc + o
    return acc, den


def _mod_kernel(c_ref, w_ref, b_ref, o_ref):
    cv = c_ref[...]
    act = cv / (1.0 + jnp.exp(-cv))
    o_ref[...] = _dot(act.astype(BF16), w_ref[...].astype(BF16)) + b_ref[...]


def _modulation(cvecs, w_mod, b_mod):
    tn = 1024
    n = 6 * D_MODEL
    return pl.pallas_call(
        _mod_kernel,
        grid=(DEPTH, n // tn),
        in_specs=[
            pl.BlockSpec((MOD_ROWS, D_MODEL), lambda l, j: (0, 0)),
            pl.BlockSpec((None, D_MODEL, tn), lambda l, j: (l, 0, j)),
            pl.BlockSpec((None, 1, tn), lambda l, j: (l, 0, j)),
        ],
        out_specs=pl.BlockSpec((None, MOD_ROWS, tn), lambda l, j: (l, 0, j)),
        out_shape=jax.ShapeDtypeStruct((DEPTH, MOD_ROWS, n), F32),
        compiler_params=_cparams("parallel", "parallel"),
        name="adaln_mod",
    )(cvecs, w_mod, b_mod.reshape(DEPTH, 1, n))


def _mod_spec(layer, sample, tm):
    if sample:
        per = DEC_SEQ // tm
        return pl.BlockSpec((None, None, 6, D_MODEL), lambda i, *_: (layer, 1 + i // per, 0, 0))
    return pl.BlockSpec((None, None, 6, D_MODEL), lambda i, *_: (layer, 0, 0, 0))


def _full(shape):
    nd = len(shape)
    return pl.BlockSpec(shape, lambda *_: (0,) * nd)


def _rows(tm, width):
    return pl.BlockSpec((tm, width), lambda i, *_: (i, 0))


def _rope_spec(tm):
    per = DEC_SEQ // tm
    return pl.BlockSpec((tm, LANES), lambda i, *_: (i % per, 0))


EVEN_IN = 2048


def _proj_even_kernel(sample, *refs):
    if sample:
        (h_ref, mod_ref, nw_ref, win_ref, qnw_ref, kvnw_ref, wuqn_ref, wuqr_ref, wuk_ref, wuv_ref,
         ca_ref, sa_ref, ch_ref, sh_ref,
         qn_ref, qr_ref, kn_ref, va_ref, kpe_ref, qd_ref, kd_ref, vd_ref) = refs
    else:
        (h_ref, mod_ref, nw_ref, win_ref, qnw_ref, kvnw_ref, wuqn_ref, wuqr_ref, wuk_ref, wuv_ref,
         qn_ref, qr_ref, kn_ref, va_ref, kpe_ref, qd_ref, kd_ref, vd_ref,
         ckv_st_ref, kpe_st_ref, kd_st_ref, vd_st_ref) = refs
    mod = mod_ref[...]
    u = _rms(h_ref[...], nw_ref[...]) * (1.0 + mod[1:2]) + mod[0:1]
    z = _dot(u.astype(BF16), win_ref[...])
    cq = _rms(z[:, 0:256], qnw_ref[...]).astype(BF16)
    qn_ref[...] = _dot(cq, wuqn_ref[...]).astype(BF16)
    qr = _dot(cq, wuqr_ref[...])
    ckv = _rms(z[:, 256:384], kvnw_ref[...])
    ckv_b = ckv.astype(BF16)
    kn_ref[...] = _dot(ckv_b, wuk_ref[...]).astype(BF16)
    va_ref[...] = _dot(ckv_b, wuv_ref[...]).astype(BF16)
    kpe = z[:, 384:512]
    qd = z[:, 512:1024]
    kd = z[:, 1024:1536]
    vd = z[:, 1536:2048]
    vd_ref[...] = vd.astype(BF16)
    if sample:
        ca, sa, ch, sh = ca_ref[...], sa_ref[...], ch_ref[...], sh_ref[...]
        qr_ref[...] = _rope(qr, ca, sa, QK_ROPE // 2).astype(BF16)
        kpe_ref[...] = _rope(kpe, ca, sa, QK_ROPE // 2).astype(BF16)
        qd_ref[...] = _rope(qd, ch, sh, DH_B // 2).astype(BF16)
        kd_ref[...] = _rope(kd, ch, sh, DH_B // 2).astype(BF16)
    else:
        qr_ref[...] = qr.astype(BF16)
        kpe_ref[...] = kpe.astype(BF16)
        qd_ref[...] = qd.astype(BF16)
        kd_ref[...] = kd.astype(BF16)
        ckv_st_ref[...] = ckv
        kpe_st_ref[...] = kpe[:, :QK_ROPE]
        kd_st_ref[...] = kd
        vd_st_ref[...] = vd


def _proj_even(h, mod, layer, sample, nw, w, rope):
    n = h.shape[0]
    tm = ROW_TILE
    ins = [h, mod, nw, w["w_in"], w["q_norm_w"], w["kv_norm_w"], w["w_uq_n"], w["w_uq_r"], w["w_uk"], w["w_uv"]]
    in_specs = [_rows(tm, D_MODEL), _mod_spec(layer, sample, tm), _full((1, D_MODEL)), _full((D_MODEL, EVEN_IN)),
                _full((1, Q_LORA)), _full((1, KV_LORA)), _full((Q_LORA, 512)), _full((Q_LORA, 256)),
                _full((KV_LORA, 512)), _full((KV_LORA, 512))]
    widths = [512, 256, 512, 512, 128, 512, 512, 512]
    out_shape = [jax.ShapeDtypeStruct((n, wd), BF16) for wd in widths]
    out_specs = [_rows(tm, wd) for wd in widths]
    if sample:
        ins += [rope["ca"], rope["sa"], rope["ch"], rope["sh"]]
        in_specs += [_rope_spec(tm)] * 4
    else:
        for wd in (KV_LORA, QK_ROPE, 512, 512):
            out_shape.append(jax.ShapeDtypeStruct((n, wd), F32))
            out_specs.append(_rows(tm, wd))
    return pl.pallas_call(
        functools.partial(_proj_even_kernel, sample),
        grid=(n // tm,),
        in_specs=in_specs,
        out_specs=out_specs,
        out_shape=out_shape,
        compiler_params=_cparams("parallel"),
        name="proj_even_s" if sample else "proj_even_p",
    )(*ins)


def _attn_even_kernel(sample, lam_init, *refs):
    if sample:
        (qn_ref, qr_ref, kn_ref, va_ref, kpe_ref, qd_ref, kd_ref, vd_ref,
         cckv_ref, ckpe_ref, cdk_ref, cdv_ref, wuk_ref, wuv_ref, lam_ref, sub_ref, o_ref) = refs
    else:
        (qn_ref, qr_ref, kn_ref, va_ref, kpe_ref, qd_ref, kd_ref, vd_ref, lam_ref, sub_ref, o_ref) = refs
    lf = lam_ref[...]
    lam = (jnp.exp(jnp.sum(lf[0:1] * lf[1:2], axis=-1, keepdims=True))
           - jnp.exp(jnp.sum(lf[2:3] * lf[3:4], axis=-1, keepdims=True)) + lam_init)
    if sample:
        cckv = cckv_ref[...].astype(BF16)
        kn_ctx = _dot(cckv, wuk_ref[...]).astype(BF16)
        va_ctx = _dot(cckv, wuv_ref[...]).astype(BF16)
        kpe_ctx = ckpe_ref[...].astype(BF16)
        kd_ctx = cdk_ref[...].astype(BF16)
        vd_ctx = cdv_ref[...].astype(BF16)
    scale_a = (QK_NOPE + QK_ROPE) ** -0.5
    kpe = kpe_ref[:, 0:QK_ROPE]
    for hd in range(H_A):
        c0, c1 = hd * QK_NOPE, (hd + 1) * QK_NOPE
        qn = qn_ref[:, c0:c1]
        qr = qr_ref[:, hd * QK_ROPE:(hd + 1) * QK_ROPE]
        scores = [(_dot_nt(qn, kn_ref[:, c0:c1]) + _dot_nt(qr, kpe)) * scale_a]
        values = [va_ref[:, hd * V_A:(hd + 1) * V_A]]
        if sample:
            scores.append((_dot_nt(qn, kn_ctx[:, c0:c1]) + _dot_nt(qr, kpe_ctx)) * scale_a)
            values.append(va_ctx[:, hd * V_A:(hd + 1) * V_A])
        acc, den = _softmax_pv(scores, values)
        o_ref[:, hd * V_A:(hd + 1) * V_A] = (acc / den).astype(BF16)
    scale_b = DH_B ** -0.5
    base = H_A * V_A
    sub_w = sub_ref[...]
    for hd in range(H_B):
        g0 = hd * 2 * DH_B
        outs = []
        for comp in range(2):
            c0, c1 = g0 + comp * DH_B, g0 + (comp + 1) * DH_B
            q = qd_ref[:, c0:c1]
            scores = [_dot_nt(q, kd_ref[:, c0:c1]) * scale_b]
            values = [vd_ref[:, g0:g0 + 2 * DH_B]]
            if sample:
                scores.append(_dot_nt(q, kd_ctx[:, c0:c1]) * scale_b)
                values.append(vd_ctx[:, g0:g0 + 2 * DH_B])
            acc, den = _softmax_pv(scores, values)
            outs.append(acc / den)
        ob = outs[0] - lam * outs[1]
        ob = _rms(ob, sub_w) * (1.0 - lam_init)
        o_ref[:, base + g0:base + g0 + 2 * DH_B] = ob.astype(BF16)


def _attn_even(p, sample, lam_init, caches, w):
    qn, qr, kn, va, kpe, qd, kd, vd = p
    n = qn.shape[0]
    if sample:
        tq, per = Q_TILE, DEC_SEQ // Q_TILE
        grid = (DEC_BATCH, per)
        qspec = lambda wd: pl.BlockSpec((tq, wd), lambda b, i: (b * per + i, 0))
        kspec = lambda wd: pl.BlockSpec((DEC_SEQ, wd), lambda b, i: (b, 0))
        cspec = lambda wd: pl.BlockSpec((None, PAST_LEN, wd), lambda b, i: (b, 0, 0))
        ins = [qn, qr, kn, va, kpe, qd, kd, vd, *caches, w["w_uk"], w["w_uv"], w["diff_lam"], w["diff_subln_w"]]
        in_specs = [qspec(512), qspec(256), kspec(512), kspec(512), kspec(128), qspec(512), kspec(512), kspec(512),
                    cspec(KV_LORA), cspec(QK_ROPE), cspec(512), cspec(512),
                    _full((KV_LORA, 512)), _full((KV_LORA, 512)), _full((4, DH_B)), _full((1, 2 * DH_B))]
        out_spec = qspec(D_MODEL)
        sem = ("parallel", "parallel")
    else:
        grid = (BATCH,)
        spec = lambda wd: pl.BlockSpec((SEQ, wd), lambda b: (b, 0))
        ins = [qn, qr, kn, va, kpe, qd, kd, vd, w["diff_lam"], w["diff_subln_w"]]
        in_specs = [spec(512), spec(256), spec(512), spec(512), spec(128), spec(512), spec(512), spec(512),
                    _full((4, DH_B)), _full((1, 2 * DH_B))]
        out_spec = spec(D_MODEL)
        sem = ("parallel",)
    return pl.pallas_call(
        functools.partial(_attn_even_kernel, sample, lam_init),
        grid=grid,
        in_specs=in_specs,
        out_specs=out_spec,
        out_shape=jax.ShapeDtypeStruct((n, D_MODEL), BF16),
        compiler_params=_cparams(*sem),
        name="attn_even_s" if sample else "attn_even_p",
    )(*ins)


ODD_IN = 2304


def _proj_odd_kernel(sample, *refs):
    if sample:
        (h_ref, mod_ref, nw_ref, win_ref, qw_ref, kw_ref, gq_ref, gk_ref, ch_ref, sh_ref,
         qc_ref, kc_ref, vc_ref, qn_ref, kn_ref, vn_ref) = refs
    else:
        (h_ref, mod_ref, nw_ref, win_ref, qw_ref, kw_ref, gq_ref, gk_ref,
         qc_ref, kc_ref, vc_ref, qn_ref, kn_ref, vn_ref,
         kc_st_ref, vc_st_ref, kn_st_ref, vn_st_ref) = refs
    mod = mod_ref[...]
    u = _rms(h_ref[...], nw_ref[...]) * (1.0 + mod[1:2]) + mod[0:1]
    z = _dot(u.astype(BF16), win_ref[...])
    qc = _group_rms(z[:, 0:512], qw_ref[...], gq_ref[...])
    kc = _group_rms(z[:, 512:640], kw_ref[...], gk_ref[...])
    vc = z[:, 640:768]
    kn = z[:, 1280:1792]
    vn = z[:, 1792:2304]
    vc_ref[...] = vc.astype(BF16)
    qn_ref[...] = z[:, 768:1280].astype(BF16)
    kn_ref[...] = kn.astype(BF16)
    vn_ref[...] = vn.astype(BF16)
    if sample:
        ch, sh = ch_ref[...], sh_ref[...]
        qc_ref[...] = _rope(qc, ch, sh, DH_C // 2).astype(BF16)
        kc_ref[...] = _rope(kc, ch, sh, DH_C // 2).astype(BF16)
    else:
        qc_ref[...] = qc.astype(BF16)
        kc_ref[...] = kc.astype(BF16)
        kc_st_ref[...] = kc
        vc_st_ref[...] = vc
        kn_st_ref[...] = kn
        vn_st_ref[...] = vn


def _proj_odd(h, mod, layer, sample, nw, w, rope):
    n = h.shape[0]
    tm = ROW_TILE
    ins = [h, mod, nw, w["w_in"], w["q_w"], w["k_w"], w["gq"], w["gk"]]
    in_specs = [_rows(tm, D_MODEL), _mod_spec(layer, sample, tm), _full((1, D_MODEL)), _full((D_MODEL, ODD_IN)),
                _full((1, 512)), _full((1, 128)), _full((512, 512)), _full((128, 128))]
    widths = [512, 128, 128, 512, 512, 512]
    out_shape = [jax.ShapeDtypeStruct((n, wd), BF16) for wd in widths]
    out_specs = [_rows(tm, wd) for wd in widths]
    if sample:
        ins += [rope["ch"], rope["sh"]]
        in_specs += [_rope_spec(tm)] * 2
    else:
        for wd in (128, 128, 512, 512):
            out_shape.append(jax.ShapeDtypeStruct((n, wd), F32))
            out_specs.append(_rows(tm, wd))
    return pl.pallas_call(
        functools.partial(_proj_odd_kernel, sample),
        grid=(n // tm,),
        in_specs=in_specs,
        out_specs=out_specs,
        out_shape=out_shape,
        compiler_params=_cparams("parallel"),
        name="proj_odd_s" if sample else "proj_odd_p",
    )(*ins)


def _attn_odd_prompt_kernel(qc_ref, kc_ref, vc_ref, qn_ref, kn_ref, vn_ref, o_ref):
    scale = DH_C ** -0.5
    group = H_C // KV_C
    for hd in range(H_C):
        kv = hd // group
        q = qc_ref[:, hd * DH_C:(hd + 1) * DH_C]
        s = _dot_nt(q, kc_ref[:, kv * DH_C:(kv + 1) * DH_C]) * scale
        acc, den = _softmax_pv([s], [vc_ref[:, kv * DH_C:(kv + 1) * DH_C]])
        o_ref[:, hd * DH_C:(hd + 1) * DH_C] = (acc / den).astype(BF16)
    base = H_C * DH_C
    scale = DH_D ** -0.5
    for hd in range(H_D):
        c0, c1 = hd * DH_D, (hd + 1) * DH_D
        s = _dot_nt(qn_ref[:, c0:c1], kn_ref[:, c0:c1]) * scale
        acc, den = _softmax_pv([s], [vn_ref[:, c0:c1]])
        o_ref[:, base + c0:base + c1] = (acc / den).astype(BF16)


def _attn_odd_prompt(p):
    qc, kc, vc, qn, kn, vn = p
    spec = lambda wd: pl.BlockSpec((SEQ, wd), lambda b: (b, 0))
    return pl.pallas_call(
        _attn_odd_prompt_kernel,
        grid=(BATCH,),
        in_specs=[spec(512), spec(128), spec(128), spec(512), spec(512), spec(512)],
        out_specs=spec(D_MODEL),
        out_shape=jax.ShapeDtypeStruct((qc.shape[0], D_MODEL), BF16),
        compiler_params=_cparams("parallel"),
        name="attn_odd_p",
    )(qc, kc, vc, qn, kn, vn)


def _gqa_sample_kernel(q_ref, k_ref, v_ref, ck_ref, cv_ref, o_ref):
    scale = DH_C ** -0.5
    group = H_C // KV_C
    k_ctx = ck_ref[...].astype(BF16)
    v_ctx = cv_ref[...].astype(BF16)
    for hd in range(H_C):
        kv = hd // group
        c0, c1 = kv * DH_C, (kv + 1) * DH_C
        q = q_ref[:, hd * DH_C:(hd + 1) * DH_C]
        scores = [_dot_nt(q, k_ref[:, c0:c1]) * scale, _dot_nt(q, k_ctx[:, c0:c1]) * scale]
        acc, den = _softmax_pv(scores, [v_ref[:, c0:c1], v_ctx[:, c0:c1]])
        o_ref[:, hd * DH_C:(hd + 1) * DH_C] = (acc / den).astype(BF16)


def _gqa_sample(qc, kc, vc, cache_k, cache_v):
    tq, per = Q_TILE, DEC_SEQ // Q_TILE
    return pl.pallas_call(
        _gqa_sample_kernel,
        grid=(DEC_BATCH, per),
        in_specs=[pl.BlockSpec((tq, 512), lambda b, i: (b * per + i, 0)),
                  pl.BlockSpec((DEC_SEQ, 128), lambda b, i: (b, 0)),
                  pl.BlockSpec((DEC_SEQ, 128), lambda b, i: (b, 0)),
                  pl.BlockSpec((None, PAST_LEN, 128), lambda b, i: (b, 0, 0)),
                  pl.BlockSpec((None, PAST_LEN, 128), lambda b, i: (b, 0, 0))],
        out_specs=pl.BlockSpec((tq, 512), lambda b, i: (b * per + i, 0)),
        out_shape=jax.ShapeDtypeStruct((qc.shape[0], 512), BF16),
        compiler_params=_cparams("parallel", "parallel"),
        name="gqa_s",
    )(qc, kc, vc, cache_k, cache_v)


NA_ROWS = DEC_SEQ // GRID_W
NA_KR = min(NA_WIN_ROWS, NA_ROWS)
NA_LOC = NA_KR * GRID_W


def _na_sample_kernel(q_ref, k_ref, v_ref, ck_ref, cv_ref, bias_ref, o_ref):
    r = pl.program_id(1)
    rs = jnp.clip(r - NA_KR // 2, 0, NA_ROWS - NA_KR)
    start = pl.multiple_of(rs * GRID_W, GRID_W)
    scale = DH_D ** -0.5
    wq = lax.broadcasted_iota(jnp.int32, (GRID_W, NA_LOC), 0)
    wk = lax.broadcasted_iota(jnp.int32, (GRID_W, NA_LOC), 1) % GRID_W
    cs = jnp.clip(wq - NA_WIN_COLS // 2, 0, GRID_W - NA_WIN_COLS)
    col_ok = (wk >= cs) & (wk < cs + NA_WIN_COLS)
    k_loc = k_ref[pl.ds(start, NA_LOC), :]
    v_loc = v_ref[pl.ds(start, NA_LOC), :]
    k_ctx = ck_ref[...].astype(BF16)
    v_ctx = cv_ref[...].astype(BF16)
    for hd in range(H_D):
        c0, c1 = hd * DH_D, (hd + 1) * DH_D
        q = q_ref[:, c0:c1]
        s_loc = _dot_nt(q, k_loc[:, c0:c1]) * scale + bias_ref[hd]
        s_loc = jnp.where(col_ok, s_loc, NEG_INF)
        s_ctx = _dot_nt(q, k_ctx[:, c0:c1]) * scale
        acc, den = _softmax_pv([s_loc, s_ctx], [v_loc[:, c0:c1], v_ctx[:, c0:c1]])
        o_ref[:, c0:c1] = (acc / den).astype(BF16)


def _na_sample(qn, kn, vn, cache_k, cache_v, bias):
    def bias_map(b, r):
        rs = jnp.clip(r - NA_KR // 2, 0, NA_ROWS - NA_KR)
        return (0, rs - r + NA_WIN_ROWS - 1, 0, 0)
    return pl.pallas_call(
        _na_sample_kernel,
        grid=(DEC_BATCH, NA_ROWS),
        in_specs=[pl.BlockSpec((GRID_W, 512), lambda b, r: (b * NA_ROWS + r, 0)),
                  pl.BlockSpec((DEC_SEQ, 512), lambda b, r: (b, 0)),
                  pl.BlockSpec((DEC_SEQ, 512), lambda b, r: (b, 0)),
                  pl.BlockSpec((None, PAST_LEN, 512), lambda b, r: (b, 0, 0)),
                  pl.BlockSpec((None, PAST_LEN, 512), lambda b, r: (b, 0, 0)),
                  pl.BlockSpec((H_D, None, GRID_W, NA_LOC), bias_map)],
        out_specs=pl.BlockSpec((GRID_W, 512), lambda b, r: (b * NA_ROWS + r, 0)),
        out_shape=jax.ShapeDtypeStruct((qn.shape[0], 512), BF16),
        compiler_params=_cparams("parallel", "parallel"),
        name="na_s",
    )(qn, kn, vn, cache_k, cache_v, bias)


def _na_bias_table(rpb):
    edge = GRID_W - NA_WIN_COLS
    ext = jnp.pad(rpb.astype(F32), ((0, 0), (0, 0), (edge, edge)), mode="edge")
    toep = jnp.stack([ext[:, :, GRID_W - 1 - wq:2 * GRID_W - 1 - wq] for wq in range(GRID_W)], axis=1)
    flat = toep.reshape(H_D, GRID_W, (2 * NA_WIN_ROWS - 1) * GRID_W)
    return jnp.stack([flat[:, :, d0 * GRID_W:d0 * GRID_W + NA_LOC] for d0 in range(NA_WIN_ROWS)], axis=1)


def _post_attn_kernel(n_parts, *refs):
    o_refs = refs[:n_parts]
    h_ref, wout_ref, mod_ref, nw1_ref, h1_ref = refs[n_parts:]
    y = None
    off = 0
    for o_ref in o_refs:
        wd = o_ref.shape[1]
        part = _dot(o_ref[...], wout_ref[off:off + wd, :])
        y = part if y is None else y + part
        off += wd
    mod = mod_ref[...]
    h1_ref[...] = h_ref[...] + mod[2:3] * _rms(y, nw1_ref[...])


def _post_attn(o_parts, h, w_out, mod, layer, sample, nw1):
    n = h.shape[0]
    tm = ROW_TILE
    in_specs = [_rows(tm, o.shape[1]) for o in o_parts]
    in_specs += [_rows(tm, D_MODEL), _full((D_MODEL, D_MODEL)), _mod_spec(layer, sample, tm), _full((1, D_MODEL))]
    return pl.pallas_call(
        functools.partial(_post_attn_kernel, len(o_parts)),
        grid=(n // tm,),
        in_specs=in_specs,
        out_specs=_rows(tm, D_MODEL),
        out_shape=jax.ShapeDtypeStruct((n, D_MODEL), F32),
        compiler_params=_cparams("parallel"),
        name="post_attn_s" if sample else "post_attn_p",
    )(*o_parts, h, w_out, mod, nw1)


FF_PAIR = 2 * FF_CHUNK
N_FF = D_FF // FF_CHUNK
SUBLANES = 8
ACT_TILES = 4
ROW_BLOCK = 256
NORM_ROWS = 32


def _ffn_kernel(seq_len, h_ref, wup_ref, cw_ref, cb_ref, wd_ref, mod_ref, nw2_ref, nw3_ref, o_ref,
                u_ref, z0_ref, z1_ref, a_ref):
    tm = h_ref.shape[0]
    n_blocks = tm // ROW_BLOCK
    rows = ACT_TILES * SUBLANES
    mod = mod_ref[...]
    sub = lax.broadcasted_iota(jnp.int32, (SUBLANES, LANES), 0)
    zero_rows = jnp.zeros((SUBLANES, FF_PAIR), F32)
    for z_ref in (z0_ref, z1_ref):
        z_ref[0:SUBLANES, :] = zero_rows
        z_ref[SUBLANES + tm:2 * SUBLANES + tm, :] = zero_rows

    nw2 = nw2_ref[...]
    for c in range(tm // NORM_ROWS):
        x = h_ref[c * NORM_ROWS:(c + 1) * NORM_ROWS, :]
        u = _rms(x, nw2) * (1.0 + mod[4:5]) + mod[3:4]
        u_ref[c * NORM_ROWS:(c + 1) * NORM_ROWS, :] = u.astype(BF16)

    def up(j, z_ref, blk):
        r0 = blk * ROW_BLOCK
        u = u_ref[r0:r0 + ROW_BLOCK, :]
        rows_ = slice(SUBLANES + r0, SUBLANES + r0 + ROW_BLOCK)
        for half in range(2):
            c0 = half * D_FF + j * FF_CHUNK
            if not isinstance(c0, int):
                c0 = pl.multiple_of(c0, FF_CHUNK)
            z_ref[rows_, half * FF_CHUNK:(half + 1) * FF_CHUNK] = _dot(u, wup_ref[:, pl.ds(c0, FF_CHUNK)])

    def act(j, z_ref, col, blk):
        cw = cw_ref[j]
        cb = cb_ref[j]
        for lc in range(FF_CHUNK // LANES):
            taps = []
            for lane0 in (lc * LANES, FF_CHUNK + lc * LANES):
                lanes = slice(lane0, lane0 + LANES)
                taps.append([jnp.broadcast_to(cw[k:k + 1, lanes], (rows, LANES)) for k in range(3)]
                            + [jnp.broadcast_to(cb[:, lanes], (rows, LANES))])
            for c in range(ROW_BLOCK // rows):
                r = blk * ROW_BLOCK + c * rows
                first = r % seq_len == 0
                last = (r + rows) % seq_len == 0

                def conv(lane0, tap):
                    ext = z_ref[r:r + rows + 2 * SUBLANES, lane0:lane0 + LANES]
                    tiles = [ext[t * SUBLANES:(t + 1) * SUBLANES] for t in range(ACT_TILES + 2)]
                    down = [pltpu.roll(t, 1, 0) for t in tiles[:-1]]
                    up_ = [pltpu.roll(t, SUBLANES - 1, 0) for t in tiles[1:]]
                    prev, nxt = [], []
                    for t in range(ACT_TILES):
                        above = 0.0 if (first and t == 0) else down[t]
                        below = 0.0 if (last and t == ACT_TILES - 1) else up_[t + 1]
                        prev.append(jnp.where(sub == 0, above, down[t + 1]))
                        nxt.append(jnp.where(sub == SUBLANES - 1, below, up_[t]))
                    prev = jnp.concatenate(prev, axis=0)
                    nxt = jnp.concatenate(nxt, axis=0)
                    return prev * tap[0] + ext[SUBLANES:SUBLANES + rows] * tap[1] + nxt * tap[2] + tap[3]

                g = conv(lc * LANES, taps[0])
                v = conv(FF_CHUNK + lc * LANES, taps[1])
                a = (g / (1.0 + jnp.exp(-g))) * v
                lane = col + lc * LANES
                if not isinstance(lane, int):
                    lane = pl.multiple_of(lane, LANES)
                a_ref[r:r + rows, pl.ds(lane, LANES)] = a.astype(BF16)

    for blk in range(n_blocks):
        up(0, z0_ref, blk)

    def pair(i, carry):
        j = 2 * i
        col = pl.multiple_of(j * FF_CHUNK, FF_CHUNK)
        for blk in range(n_blocks):
            up(j + 1, z1_ref, blk)
            act(j, z0_ref, col, blk)
        for blk in range(n_blocks):
            up(j + 2, z0_ref, blk)
            act(j + 1, z1_ref, col + FF_CHUNK, blk)
        return carry

    lax.fori_loop(0, (N_FF - 1) // 2, pair, 0)
    for blk in range(n_blocks):
        act(N_FF - 1, z0_ref, (N_FF - 1) * FF_CHUNK, blk)
    nw3 = nw3_ref[...]
    for blk in range(n_blocks):
        r0 = blk * ROW_BLOCK
        y = _dot(a_ref[r0:r0 + ROW_BLOCK, :], wd_ref[...])
        for c in range(ROW_BLOCK // NORM_ROWS):
            rs = slice(r0 + c * NORM_ROWS, r0 + (c + 1) * NORM_ROWS)
            o_ref[rs, :] = h_ref[rs, :] + mod[5:6] * _rms(y[c * NORM_ROWS:(c + 1) * NORM_ROWS], nw3)


def _ffn(h1, w, mod, layer, sample, nw2, nw3):
    n = h1.shape[0]
    tm = FFN_ROW_TILE
    seq_len = DEC_SEQ if sample else SEQ
    once = pl.Buffered(1)
    in_specs = [
        _rows(tm, D_MODEL),
        pl.BlockSpec((D_MODEL, 2 * D_FF), lambda i: (0, 0), pipeline_mode=once),
        pl.BlockSpec((N_FF, 3, FF_PAIR), lambda i: (0, 0, 0), pipeline_mode=once),
        pl.BlockSpec((N_FF, 1, FF_PAIR), lambda i: (0, 0, 0), pipeline_mode=once),
        pl.BlockSpec((D_FF, D_MODEL), lambda i: (0, 0), pipeline_mode=once),
        _mod_spec(layer, sample, tm),
        _full((1, D_MODEL)),
        _full((1, D_MODEL)),
    ]
    return pl.pallas_call(
        functools.partial(_ffn_kernel, seq_len),
        grid=(n // tm,),
        in_specs=in_specs,
        out_specs=_rows(tm, D_MODEL),
        out_shape=jax.ShapeDtypeStruct((n, D_MODEL), F32),
        scratch_shapes=[pltpu.VMEM((tm, D_MODEL), BF16), pltpu.VMEM((tm + 2 * SUBLANES, FF_PAIR), F32),
                        pltpu.VMEM((tm + 2 * SUBLANES, FF_PAIR), F32), pltpu.VMEM((tm, D_FF), BF16)],
        compiler_params=pltpu.CompilerParams(dimension_semantics=("parallel",), vmem_limit_bytes=FFN_VMEM_LIMIT),
        name="ffn_s" if sample else "ffn_p",
    )(h1, w["w_up"], w["conv_w"], w["conv_b"], w["w_down"], mod, nw2, nw3)


def _pair_chunks(x):
    lead = x.shape[:-1]
    x = x.reshape(lead + (2, N_FF, FF_CHUNK))
    x = jnp.moveaxis(x, -2, 0)
    return x.reshape((N_FF,) + lead + (FF_PAIR,))


def _rope_tables():
    def table(rot_dim):
        t = np.arange(DEC_SEQ)
        n_freq = rot_dim // 4
        inv = 1.0 / (ROPE_THETA ** (np.arange(n_freq) / n_freq))
        ang = np.concatenate([(t // GRID_W)[:, None] * inv[None, :], (t % GRID_W)[:, None] * inv[None, :]], axis=-1)
        cos = np.cos(ang).astype(np.float32)
        sin = np.sin(ang).astype(np.float32)
        reps = LANES // rot_dim
        return (jnp.asarray(np.tile(np.concatenate([cos, cos], axis=-1), (1, reps))),
                jnp.asarray(np.tile(np.concatenate([-sin, sin], axis=-1), (1, reps))))
    ca, sa = table(QK_ROPE)
    ch, sh = table(HEAD_DIM)
    return {"ca": ca, "sa": sa, "ch": ch, "sh": sh}


def _group_mean_matrix(width):
    idx = np.arange(width) // HEAD_DIM
    return jnp.asarray((idx[:, None] == idx[None, :]).astype(np.float32) / HEAD_DIM, BF16)


def kernel(x_prompt, x_sample, c, cache_mla_ckv, cache_mla_kpe, cache_diff_k, cache_diff_v, cache_gqa_k, cache_gqa_v, cache_na_k, cache_na_v, c_ctx, norm_w, w_mod, b_mod, w_in_even, w_out_even, w_uq, q_norm_w, kv_norm_w, w_uk, w_uv, diff_lam, diff_subln_w, w_in_odd, w_out_odd, qk_norm_w, na_rpb, w_up, conv_w, conv_b, w_down):
    rope = _rope_tables()
    n_p = BATCH * SEQ
    n_s = DEC_BATCH * DEC_SEQ
    cvecs = jnp.concatenate([c_ctx[None, :], c, jnp.zeros((MOD_ROWS - 1 - DEC_BATCH, D_MODEL), F32)], axis=0)
    mod = _modulation(cvecs, w_mod, b_mod).reshape(DEPTH, MOD_ROWS, 6, D_MODEL)
    hp = x_prompt.reshape(n_p, D_MODEL)
    hs = x_sample.reshape(n_s, D_MODEL)
    even_states, odd_states = [], []
    for l in range(DEPTH):
        i = l // 2
        nw = [norm_w[l, k][None, :] for k in range(4)]
        if l % 2 == 0:
            lam_init = 0.8 - 0.6 * math.exp(-0.3 * l)
            wi = w_in_even[i]
            w_uq3 = w_uq[i].reshape(Q_LORA, H_A, QK_NOPE + QK_ROPE)
            w = {
                "w_in": jnp.concatenate([wi[:, :416], jnp.zeros((D_MODEL, 96), F32), wi[:, 416:]], axis=1).astype(BF16),
                "q_norm_w": q_norm_w[i][None, :],
                "kv_norm_w": kv_norm_w[i][None, :],
                "w_uq_n": w_uq3[:, :, :QK_NOPE].reshape(Q_LORA, H_A * QK_NOPE).astype(BF16),
                "w_uq_r": w_uq3[:, :, QK_NOPE:].reshape(Q_LORA, H_A * QK_ROPE).astype(BF16),
                "w_uk": w_uk[i].astype(BF16),
                "w_uv": w_uv[i].astype(BF16),
                "diff_lam": diff_lam[i],
                "diff_subln_w": diff_subln_w[i][None, :],
            }
            outs_p = _proj_even(hp, mod, l, False, nw[0], w, rope)
            outs_s = _proj_even(hs, mod, l, True, nw[0], w, rope)
            even_states.append(outs_p[8:])
            o_p = [_attn_even(outs_p[:8], False, lam_init, None, w)]
            caches = (cache_mla_ckv[:, i], cache_mla_kpe[:, i],
                      cache_diff_k[:, i].reshape(DEC_BATCH, PAST_LEN, 512),
                      cache_diff_v[:, i].reshape(DEC_BATCH, PAST_LEN, 512))
            o_s = [_attn_even(outs_s, True, lam_init, caches, w)]
            w_out = w_out_even[i].astype(BF16)
        else:
            q_w = jnp.tile(qk_norm_w[i, 0], H_C)[None, :]
            k_w = jnp.tile(qk_norm_w[i, 1], KV_C)[None, :]
            w = {"w_in": w_in_odd[i].astype(BF16), "q_w": q_w, "k_w": k_w,
                 "gq": _group_mean_matrix(512), "gk": _group_mean_matrix(128)}
            outs_p = _proj_odd(hp, mod, l, False, nw[0], w, rope)
            outs_s = _proj_odd(hs, mod, l, True, nw[0], w, rope)
            odd_states.append(outs_p[6:])
            o_p = [_attn_odd_prompt(outs_p[:6])]
            qc, kc, vc, qn, kn, vn = outs_s
            o_c = _gqa_sample(qc, kc, vc, cache_gqa_k[:, i].reshape(DEC_BATCH, PAST_LEN, 128),
                              cache_gqa_v[:, i].reshape(DEC_BATCH, PAST_LEN, 128))
            o_d = _na_sample(qn, kn, vn, cache_na_k[:, i].reshape(DEC_BATCH, PAST_LEN, 512),
                             cache_na_v[:, i].reshape(DEC_BATCH, PAST_LEN, 512), _na_bias_table(na_rpb[i]))
            o_s = [o_c, o_d]
            w_out = w_out_odd[i].astype(BF16)
        wf = {"w_up": w_up[l].astype(BF16), "conv_w": _pair_chunks(conv_w[l]),
              "conv_b": _pair_chunks(conv_b[l][None, :]), "w_down": w_down[l].astype(BF16)}
        h1p = _post_attn(o_p, hp, w_out, mod, l, False, nw[1])
        h1s = _post_attn(o_s, hs, w_out, mod, l, True, nw[1])
        hp = _ffn(h1p, wf, mod, l, False, nw[2], nw[3])
        hs = _ffn(h1s, wf, mod, l, True, nw[2], nw[3])

    def stack(states, k, shape):
        return jnp.stack([st[k].reshape((BATCH, SEQ) + shape) for st in states], axis=1)

    new_mla_ckv = stack(even_states, 0, (KV_LORA,))
    new_mla_kpe = stack(even_states, 1, (QK_ROPE,))
    new_diff_k = stack(even_states, 2, (H_B, 2 * DH_B))
    new_diff_v = stack(even_states, 3, (H_B, 2 * DH_B))
    new_gqa_k = stack(odd_states, 0, (KV_C, DH_C))
    new_gqa_v = stack(odd_states, 1, (KV_C, DH_C))
    new_na_k = stack(odd_states, 2, (H_D, DH_D))
    new_na_v = stack(odd_states, 3, (H_D, DH_D))
    return (hp.reshape(BATCH, SEQ, D_MODEL), hs.reshape(DEC_BATCH, DEC_SEQ, D_MODEL),
            new_mla_ckv, new_mla_kpe, new_diff_k, new_diff_v, new_gqa_k, new_gqa_v, new_na_k, new_na_v)
```

```python
import functools
import math

import numpy as np
import jax
import jax.numpy as jnp
from jax import lax
from jax.experimental import pallas as pl
from jax.experimental.pallas import tpu as pltpu

D_MODEL = 1024
BATCH = 32
SEQ = 256
DEPTH = 2
DEC_BATCH = 4
DEC_SEQ = 1024
PAST_LEN = 256
GRID_W = 64
HEAD_DIM = 64
H_A = 8
QK_NOPE = 64
QK_ROPE = 32
V_A = 64
Q_LORA = 256
KV_LORA = 128
H_B = 4
DH_B = HEAD_DIM
H_C = 8
KV_C = 2
DH_C = HEAD_DIM
H_D = 8
DH_D = HEAD_DIM
NA_WIN_ROWS = 8
NA_WIN_COLS = 16
D_FF = 2816
ROPE_THETA = 10000.0
EPS = 1e-6
NEG_INF = -1e30

LANES = 128
MOD_ROWS = 8
ROW_TILE = 512
FFN_ROW_TILE = 1024
FF_CHUNK = 256
Q_TILE = 256
LATENT_GROUP = 1
CONTEXT_GROUP = 4
VMEM_LIMIT = 48 * 1024 * 1024
FFN_VMEM_LIMIT = 56 * 1024 * 1024

F32 = jnp.float32
BF16 = jnp.bfloat16
LOG2E = math.log2(math.e)


def _cparams(*sem):
    return pltpu.CompilerParams(dimension_semantics=sem, vmem_limit_bytes=VMEM_LIMIT)


def _dot(a, b):
    return jnp.dot(a, b, preferred_element_type=F32)


def _dot_nt(a, b):
    return lax.dot_general(a, b, (((1,), (1,)), ((), ())), preferred_element_type=F32)


def _rms(x, w):
    return x * lax.rsqrt(jnp.mean(x * x, axis=-1, keepdims=True) + EPS) * w


def _group_rms(x, w, gmat):
    x2 = x * x
    hi = x2.astype(BF16)
    lo = (x2 - hi.astype(F32)).astype(BF16)
    ms = _dot(hi, gmat) + _dot(lo, gmat)
    return x * lax.rsqrt(ms + EPS) * w


def _rope(x, cos, sin_signed, half):
    outs = []
    for j in range(x.shape[1] // LANES):
        xc = x[:, j * LANES:(j + 1) * LANES]
        lane = lax.broadcasted_iota(jnp.int32, xc.shape, 1)
        first = (lane % (2 * half)) < half
        partner = jnp.where(first, pltpu.roll(xc, LANES - half, 1), pltpu.roll(xc, half, 1))
        outs.append(xc * cos + partner * sin_signed)
    return outs[0] if len(outs) == 1 else jnp.concatenate(outs, axis=1)


def _softmax_block_rows(n_keys):
    return max(16, min(64, (16 * 1280 // n_keys) // 16 * 16))


def _attention(maps, s_scr, p_scr, group):
    slots = s_scr.shape[0]
    staged = {}

    def stage(i):
        s_ref = s_scr.at[i % slots]
        offs, off = [], 0
        for s in maps[i][0]():
            s_ref[:, off:off + s.shape[1]] = s
            offs.append(off)
            off += s.shape[1]
        staged[i] = (offs, off)

    def softmax(i):
        _, _, c, fix, _ = maps[i]
        n_keys = staged[i][1]
        s_ref, p_ref = s_scr.at[i % slots], p_scr.at[i % slots]
        rb = _softmax_block_rows(n_keys)
        sums = []
        for r0 in range(0, s_ref.shape[0], rb):
            s = s_ref[r0:r0 + rb, 0:n_keys]
            if fix is not None:
                s = fix(s, r0)
            m = jnp.max(s, axis=-1, keepdims=True)
            p = jnp.exp2((s - m) * c)
            sums.append(jnp.sum(p, axis=-1, keepdims=True))
            p_ref[r0:r0 + rb, 0:n_keys] = p.astype(BF16)
        return jnp.concatenate(sums, axis=0)

    def weighted_values(i, den):
        _, values, _, _, sink = maps[i]
        p_ref = p_scr.at[i % slots]
        acc = None
        for o, v in zip(staged.pop(i)[0], values()):
            part = _dot(p_ref[:, o:o + v.shape[0]], v)
            acc = part if acc is None else acc + part
        sink(acc / den)

    groups = [range(g, min(g + group, len(maps))) for g in range(0, len(maps), group)]
    for i in groups[0]:
        stage(i)
    for gi, grp in enumerate(groups):
        if gi + 1 < len(groups):
            for i in groups[gi + 1]:
                stage(i)
        dens = [softmax(i) for i in grp]
        for i, den in zip(grp, dens):
            weighted_values(i, den)


def _attn_scratch(tq, n_keys, group):
    return [pltpu.VMEM((2 * group, tq, n_keys), F32), pltpu.VMEM((2 * group, tq, n_keys), BF16)]


def _store(o_ref, c0, c1):
    def sink(o):
        o_ref[:, c0:c1] = o.astype(BF16)
    return sink


def _upper_half(shape):
    return lax.broadcasted_iota(jnp.int32, shape, 1) >= HEAD_DIM


def _keep_half(x, half):
    upper = _upper_half(x.shape)
    return jnp.where(upper if half else ~upper, x, jnp.zeros_like(x))


def _swap_halves(x):
    return jnp.concatenate([x[:, HEAD_DIM:], x[:, :HEAD_DIM]], axis=1)


def _pair_store(o_ref, c0):
    got = {}

    def make(half):
        def sink(o):
            got[half] = o
            if len(got) == 2:
                o_ref[:, c0:c0 + LANES] = jnp.where(_upper_half(o.shape), got[1], got[0]).astype(BF16)
        return sink
    return make(0), make(1)


def _mod_kernel(c_ref, w_ref, b_ref, o_ref):
    cv = c_ref[...]
    act = cv / (1.0 + jnp.exp(-cv))
    o_ref[...] = _dot(act.astype(BF16), w_ref[...].astype(BF16)) + b_ref[...]


def _modulation(cvecs, w_mod, b_mod):
    tn = 1024
    n = 6 * D_MODEL
    return pl.pallas_call(
        _mod_kernel,
        grid=(DEPTH, n // tn),
        in_specs=[
            pl.BlockSpec((MOD_ROWS, D_MODEL), lambda l, j: (0, 0)),
            pl.BlockSpec((None, D_MODEL, tn), lambda l, j: (l, 0, j)),
            pl.BlockSpec((None, 1, tn), lambda l, j: (l, 0, j)),
        ],
        out_specs=pl.BlockSpec((None, MOD_ROWS, tn), lambda l, j: (l, 0, j)),
        out_shape=jax.ShapeDtypeStruct((DEPTH, MOD_ROWS, n), F32),
        compiler_params=_cparams("parallel", "parallel"),
        name="adaln_mod",
    )(cvecs, w_mod, b_mod.reshape(DEPTH, 1, n))


def _mod_spec(layer, sample, tm):
    if sample:
        per = DEC_SEQ // tm
        return pl.BlockSpec((None, None, 6, D_MODEL), lambda i, *_: (layer, 1 + i // per, 0, 0))
    return pl.BlockSpec((None, None, 6, D_MODEL), lambda i, *_: (layer, 0, 0, 0))


def _full(shape):
    nd = len(shape)
    return pl.BlockSpec(shape, lambda *_: (0,) * nd)


def _rows(tm, width):
    return pl.BlockSpec((tm, width), lambda i, *_: (i, 0))


def _rope_spec(tm):
    per = DEC_SEQ // tm
    return pl.BlockSpec((tm, LANES), lambda i, *_: (i % per, 0))


EVEN_IN = 2048
MLA_SLOT = 128


def _proj_even_kernel(sample, *refs):
    if sample:
        (h_ref, mod_ref, nw_ref, win_ref, qnw_ref, kvnw_ref, wuq_ref, wuk_ref, wuv_ref,
         ca_ref, sa_ref, ch_ref, sh_ref,
         qa_ref, ka_ref, va_ref, qd_ref, kd_ref, vd_ref) = refs
    else:
        (h_ref, mod_ref, nw_ref, win_ref, qnw_ref, kvnw_ref, wuq_ref, wuk_ref, wuv_ref,
         qa_ref, ka_ref, va_ref, qd_ref, kd_ref, vd_ref,
         ckv_st_ref, kpe_st_ref, kd_st_ref, vd_st_ref) = refs
    mod = mod_ref[...]
    u = _rms(h_ref[...], nw_ref[...]) * (1.0 + mod[1:2]) + mod[0:1]
    z = _dot(u.astype(BF16), win_ref[...])
    cq = _rms(z[:, 0:256], qnw_ref[...]).astype(BF16)
    qa = _dot(cq, wuq_ref[...])
    ckv = _rms(z[:, 256:384], kvnw_ref[...])
    ckv_b = ckv.astype(BF16)
    kn = _dot(ckv_b, wuk_ref[...])
    va_ref[...] = _dot(ckv_b, wuv_ref[...]).astype(BF16)
    kpe_slot = z[:, 384:512]
    qd = z[:, 512:1024]
    kd = z[:, 1024:1536]
    vd = z[:, 1536:2048]
    vd_ref[...] = vd.astype(BF16)
    if sample:
        ca, sa, ch, sh = ca_ref[...], sa_ref[...], ch_ref[...], sh_ref[...]
        qa = _rope(qa, ca, sa, QK_ROPE // 2)
        kpe_rot = _rope(kpe_slot, ca, sa, QK_ROPE // 2)
        qd_ref[...] = _rope(qd, ch, sh, DH_B // 2).astype(BF16)
        kd_ref[...] = _rope(kd, ch, sh, DH_B // 2).astype(BF16)
    else:
        kpe_rot = kpe_slot
        qd_ref[...] = qd.astype(BF16)
        kd_ref[...] = kd.astype(BF16)
        ckv_st_ref[...] = ckv
        kpe_st_ref[...] = kpe_slot[:, QK_NOPE:QK_NOPE + QK_ROPE]
        kd_st_ref[...] = kd
        vd_st_ref[...] = vd
    qa_ref[...] = qa.astype(BF16)
    for hd in range(H_A):
        sl = slice(hd * MLA_SLOT, (hd + 1) * MLA_SLOT)
        ka_ref[:, sl] = (kn[:, sl] + kpe_rot).astype(BF16)


def _proj_even(h, mod, layer, sample, nw, w, rope):
    n = h.shape[0]
    tm = ROW_TILE
    wide = H_A * MLA_SLOT
    ins = [h, mod, nw, w["w_in"], w["q_norm_w"], w["kv_norm_w"], w["w_uq"], w["w_uk"], w["w_uv"]]
    in_specs = [_rows(tm, D_MODEL), _mod_spec(layer, sample, tm), _full((1, D_MODEL)), _full((D_MODEL, EVEN_IN)),
                _full((1, Q_LORA)), _full((1, KV_LORA)), _full((Q_LORA, wide)), _full((KV_LORA, wide)),
                _full((KV_LORA, 512))]
    widths = [wide, wide, 512, 512, 512, 512]
    out_shape = [jax.ShapeDtypeStruct((n, wd), BF16) for wd in widths]
    out_specs = [_rows(tm, wd) for wd in widths]
    if sample:
        ins += [rope["ca"], rope["sa"], rope["ch"], rope["sh"]]
        in_specs += [_rope_spec(tm)] * 4
    else:
        for wd in (KV_LORA, QK_ROPE, 512, 512):
            out_shape.append(jax.ShapeDtypeStruct((n, wd), F32))
            out_specs.append(_rows(tm, wd))
    return pl.pallas_call(
        functools.partial(_proj_even_kernel, sample),
        grid=(n // tm,),
        in_specs=in_specs,
        out_specs=out_specs,
        out_shape=out_shape,
        compiler_params=_cparams("parallel"),
        name="proj_even_s" if sample else "proj_even_p",
    )(*ins)


def _attn_even_kernel(sample, lam_init, *refs):
    if sample:
        (qa_ref, ka_ref, va_ref, qd_ref, kd_ref, vd_ref,
         cckv_ref, ckpe_ref, cdk_ref, cdv_ref, wuk_ref, wuv_ref, lam_ref, sub_ref, o_ref, s_scr, p_scr) = refs
    else:
        (qa_ref, ka_ref, va_ref, qd_ref, kd_ref, vd_ref, lam_ref, sub_ref, o_ref, s_scr, p_scr) = refs
    lf = lam_ref[...]
    lam = (jnp.exp(jnp.sum(lf[0:1] * lf[1:2], axis=-1, keepdims=True))
           - jnp.exp(jnp.sum(lf[2:3] * lf[3:4], axis=-1, keepdims=True)) + lam_init)
    if sample:
        cckv = cckv_ref[...].astype(BF16)
        kn_ctx = _dot(cckv, wuk_ref[...])
        va_ctx = _dot(cckv, wuv_ref[...]).astype(BF16)
        n_ctx = cckv.shape[0]
        kpe_ctx = jnp.concatenate([jnp.zeros((n_ctx, QK_NOPE), F32), ckpe_ref[...],
                                   jnp.zeros((n_ctx, MLA_SLOT - QK_NOPE - QK_ROPE), F32)], axis=1)
        kd_ctx = cdk_ref[...].astype(BF16)
        vd_ctx = cdv_ref[...].astype(BF16)
    maps = []
    c_a = (QK_NOPE + QK_ROPE) ** -0.5 * LOG2E
    for hd in range(H_A):
        sl = slice(hd * MLA_SLOT, (hd + 1) * MLA_SLOT)
        if hd % 2 == 0:
            sinks = _pair_store(o_ref, hd * V_A)

        def scores(sl=sl):
            q = qa_ref[:, sl]
            out = [_dot_nt(q, ka_ref[:, sl])]
            if sample:
                out.append(_dot_nt(q, (kn_ctx[:, sl] + kpe_ctx).astype(BF16)))
            return out

        def values(vs=slice(hd // 2 * LANES, (hd // 2 + 1) * LANES)):
            return [va_ref[:, vs]] + ([va_ctx[:, vs]] if sample else [])

        maps.append((scores, values, c_a, None, sinks[hd % 2]))
    c_b = DH_B ** -0.5 * LOG2E
    base = H_A * V_A
    sub_w = sub_ref[...]
    for hd in range(H_B):
        hs = slice(hd * 2 * DH_B, (hd + 1) * 2 * DH_B)
        outs = []

        def sink(o, outs=outs, hs=hs):
            outs.append(o)
            if len(outs) == 2:
                ob = _rms(outs[0] - lam * outs[1], sub_w) * (1.0 - lam_init)
                o_ref[:, base + hs.start:base + hs.stop] = ob.astype(BF16)

        def values(hs=hs):
            return [vd_ref[:, hs]] + ([vd_ctx[:, hs]] if sample else [])

        for comp in range(2):
            def scores(hs=hs, comp=comp):
                q = _keep_half(qd_ref[:, hs], comp)
                out = [_dot_nt(q, kd_ref[:, hs])]
                if sample:
                    out.append(_dot_nt(q, kd_ctx[:, hs]))
                return out

            maps.append((scores, values, c_b, None, sink))
    _attention(maps, s_scr, p_scr, s_scr.shape[0] // 2)


def _attn_even(p, sample, lam_init, caches, w):
    qa, ka, va, qd, kd, vd = p
    n = qa.shape[0]
    wide = H_A * MLA_SLOT
    if sample:
        tq, per = Q_TILE, DEC_SEQ // Q_TILE
        grid = (DEC_BATCH, per)
        qspec = lambda wd: pl.BlockSpec((tq, wd), lambda b, i: (b * per + i, 0))
        kspec = lambda wd: pl.BlockSpec((DEC_SEQ, wd), lambda b, i: (b, 0))
        cspec = lambda wd: pl.BlockSpec((None, PAST_LEN, wd), lambda b, i: (b, 0, 0))
        ins = [qa, ka, va, qd, kd, vd, *caches, w["w_uk"], w["w_uv"], w["diff_lam"], w["diff_subln_w"]]
        in_specs = [qspec(wide), kspec(wide), kspec(512), qspec(512), kspec(512), kspec(512),
                    cspec(KV_LORA), cspec(QK_ROPE), cspec(512), cspec(512),
                    _full((KV_LORA, wide)), _full((KV_LORA, 512)), _full((4, DH_B)), _full((1, 2 * DH_B))]
        out_spec = qspec(D_MODEL)
        sem = ("parallel", "parallel")
        n_keys, group = DEC_SEQ + PAST_LEN, LATENT_GROUP
    else:
        tq = SEQ
        grid = (BATCH,)
        spec = lambda wd: pl.BlockSpec((SEQ, wd), lambda b: (b, 0))
        ins = [qa, ka, va, qd, kd, vd, w["diff_lam"], w["diff_subln_w"]]
        in_specs = [spec(wide), spec(wide), spec(512), spec(512), spec(512), spec(512),
                    _full((4, DH_B)), _full((1, 2 * DH_B))]
        out_spec = spec(D_MODEL)
        sem = ("parallel",)
        n_keys, group = SEQ, CONTEXT_GROUP
    return pl.pallas_call(
        functools.partial(_attn_even_kernel, sample, lam_init),
        grid=grid,
        in_specs=in_specs,
        out_specs=out_spec,
        out_shape=jax.ShapeDtypeStruct((n, D_MODEL), BF16),
        scratch_shapes=_attn_scratch(tq, n_keys, group),
        compiler_params=_cparams(*sem),
        name="attn_even_s" if sample else "attn_even_p",
    )(*ins)


ODD_IN = 2304


def _proj_odd_kernel(sample, *refs):
    if sample:
        (h_ref, mod_ref, nw_ref, win_ref, qw_ref, kw_ref, gq_ref, gk_ref, ch_ref, sh_ref,
         qc_ref, kc_ref, vc_ref, qn_ref, kn_ref, vn_ref) = refs
    else:
        (h_ref, mod_ref, nw_ref, win_ref, qw_ref, kw_ref, gq_ref, gk_ref,
         qc_ref, kc_ref, vc_ref, qn_ref, kn_ref, vn_ref,
         kc_st_ref, vc_st_ref, kn_st_ref, vn_st_ref) = refs
    mod = mod_ref[...]
    u = _rms(h_ref[...], nw_ref[...]) * (1.0 + mod[1:2]) + mod[0:1]
    z = _dot(u.astype(BF16), win_ref[...])
    qc = _group_rms(z[:, 0:512], qw_ref[...], gq_ref[...])
    kc = _group_rms(z[:, 512:640], kw_ref[...], gk_ref[...])
    vc = z[:, 640:768]
    kn = z[:, 1280:1792]
    vn = z[:, 1792:2304]
    vc_ref[...] = vc.astype(BF16)
    qn_ref[...] = z[:, 768:1280].astype(BF16)
    kn_ref[...] = kn.astype(BF16)
    vn_ref[...] = vn.astype(BF16)
    if sample:
        ch, sh = ch_ref[...], sh_ref[...]
        qc_ref[...] = _rope(qc, ch, sh, DH_C // 2).astype(BF16)
        kc_ref[...] = _rope(kc, ch, sh, DH_C // 2).astype(BF16)
    else:
        qc_ref[...] = qc.astype(BF16)
        kc_ref[...] = kc.astype(BF16)
        kc_st_ref[...] = kc
        vc_st_ref[...] = vc
        kn_st_ref[...] = kn
        vn_st_ref[...] = vn


def _proj_odd(h, mod, layer, sample, nw, w, rope):
    n = h.shape[0]
    tm = ROW_TILE
    ins = [h, mod, nw, w["w_in"], w["q_w"], w["k_w"], w["gq"], w["gk"]]
    in_specs = [_rows(tm, D_MODEL), _mod_spec(layer, sample, tm), _full((1, D_MODEL)), _full((D_MODEL, ODD_IN)),
                _full((1, 512)), _full((1, 128)), _full((512, 512)), _full((128, 128))]
    widths = [512, 128, 128, 512, 512, 512]
    out_shape = [jax.ShapeDtypeStruct((n, wd), BF16) for wd in widths]
    out_specs = [_rows(tm, wd) for wd in widths]
    if sample:
        ins += [rope["ch"], rope["sh"]]
        in_specs += [_rope_spec(tm)] * 2
    else:
        for wd in (128, 128, 512, 512):
            out_shape.append(jax.ShapeDtypeStruct((n, wd), F32))
            out_specs.append(_rows(tm, wd))
    return pl.pallas_call(
        functools.partial(_proj_odd_kernel, sample),
        grid=(n // tm,),
        in_specs=in_specs,
        out_specs=out_specs,
        out_shape=out_shape,
        compiler_params=_cparams("parallel"),
        name="proj_odd_s" if sample else "proj_odd_p",
    )(*ins)


def _attn_odd_prompt_kernel(qc_ref, kc_ref, vc_ref, qn_ref, kn_ref, vn_ref, o_ref, s_scr, p_scr):
    c = DH_C ** -0.5 * LOG2E
    group = H_C // KV_C
    kc = (kc_ref[...], _swap_halves(kc_ref[...]))
    vc = (vc_ref[...], _swap_halves(vc_ref[...]))
    maps = []
    for hd in range(H_C):
        half, swap = hd % 2, (hd // group) != (hd % 2)
        ps = slice(hd // 2 * LANES, (hd // 2 + 1) * LANES)
        if half == 0:
            sinks = _pair_store(o_ref, ps.start)
        maps.append((lambda ps=ps, half=half, swap=swap: [_dot_nt(_keep_half(qc_ref[:, ps], half), kc[swap])],
                     lambda swap=swap: [vc[swap]], c, None, sinks[half]))
    base = H_C * DH_C
    c = DH_D ** -0.5 * LOG2E
    for hd in range(H_D):
        half = hd % 2
        ps = slice(hd // 2 * LANES, (hd // 2 + 1) * LANES)
        if half == 0:
            sinks = _pair_store(o_ref, base + ps.start)
        maps.append((lambda ps=ps, half=half: [_dot_nt(_keep_half(qn_ref[:, ps], half), kn_ref[:, ps])],
                     lambda ps=ps: [vn_ref[:, ps]], c, None, sinks[half]))
    _attention(maps, s_scr, p_scr, s_scr.shape[0] // 2)


def _attn_odd_prompt(p):
    qc, kc, vc, qn, kn, vn = p
    spec = lambda wd: pl.BlockSpec((SEQ, wd), lambda b: (b, 0))
    return pl.pallas_call(
        _attn_odd_prompt_kernel,
        grid=(BATCH,),
        in_specs=[spec(512), spec(128), spec(128), spec(512), spec(512), spec(512)],
        out_specs=spec(D_MODEL),
        out_shape=jax.ShapeDtypeStruct((qc.shape[0], D_MODEL), BF16),
        scratch_shapes=_attn_scratch(SEQ, SEQ, CONTEXT_GROUP),
        compiler_params=_cparams("parallel"),
        name="attn_odd_p",
    )(qc, kc, vc, qn, kn, vn)


def _gqa_sample_kernel(q_ref, k_ref, v_ref, ck_ref, cv_ref, o_ref, s_scr, p_scr):
    c = DH_C ** -0.5 * LOG2E
    group = H_C // KV_C
    k_loc = (k_ref[...], _swap_halves(k_ref[...]))
    v_loc = (v_ref[...], _swap_halves(v_ref[...]))
    k_ctx = ck_ref[...].astype(BF16)
    v_ctx = cv_ref[...].astype(BF16)
    k_ctx = (k_ctx, _swap_halves(k_ctx))
    v_ctx = (v_ctx, _swap_halves(v_ctx))
    maps = []
    for hd in range(H_C):
        half, swap = hd % 2, (hd // group) != (hd % 2)
        ps = slice(hd // 2 * LANES, (hd // 2 + 1) * LANES)
        if half == 0:
            sinks = _pair_store(o_ref, ps.start)

        def scores(ps=ps, half=half, swap=swap):
            q = _keep_half(q_ref[:, ps], half)
            return [_dot_nt(q, k_loc[swap]), _dot_nt(q, k_ctx[swap])]

        maps.append((scores, lambda swap=swap: [v_loc[swap], v_ctx[swap]], c, None, sinks[half]))
    _attention(maps, s_scr, p_scr, s_scr.shape[0] // 2)


def _gqa_sample(qc, kc, vc, cache_k, cache_v):
    tq, per = Q_TILE, DEC_SEQ // Q_TILE
    return pl.pallas_call(
        _gqa_sample_kernel,
        grid=(DEC_BATCH, per),
        in_specs=[pl.BlockSpec((tq, 512), lambda b, i: (b * per + i, 0)),
                  pl.BlockSpec((DEC_SEQ, 128), lambda b, i: (b, 0)),
                  pl.BlockSpec((DEC_SEQ, 128), lambda b, i: (b, 0)),
                  pl.BlockSpec((None, PAST_LEN, 128), lambda b, i: (b, 0, 0)),
                  pl.BlockSpec((None, PAST_LEN, 128), lambda b, i: (b, 0, 0))],
        out_specs=pl.BlockSpec((tq, 512), lambda b, i: (b * per + i, 0)),
        out_shape=jax.ShapeDtypeStruct((qc.shape[0], 512), BF16),
        scratch_shapes=_attn_scratch(tq, DEC_SEQ + PAST_LEN, LATENT_GROUP),
        compiler_params=_cparams("parallel", "parallel"),
        name="gqa_s",
    )(qc, kc, vc, cache_k, cache_v)


NA_ROWS = DEC_SEQ // GRID_W
NA_KR = min(NA_WIN_ROWS, NA_ROWS)
NA_LOC = NA_KR * GRID_W


def _na_sample_kernel(q_ref, k_ref, v_ref, ck_ref, cv_ref, bias_ref, o_ref, s_scr, p_scr):
    r = pl.program_id(1)
    rs = jnp.clip(r - NA_KR // 2, 0, NA_ROWS - NA_KR)
    start = pl.multiple_of(rs * GRID_W, GRID_W)
    scale = DH_D ** -0.5
    k_loc = k_ref[pl.ds(start, NA_LOC), :]
    v_loc = v_ref[pl.ds(start, NA_LOC), :]
    k_ctx = ck_ref[...].astype(BF16)
    v_ctx = cv_ref[...].astype(BF16)
    rb = _softmax_block_rows(NA_LOC + PAST_LEN)
    col_ok = {}
    for r0 in range(0, GRID_W, rb):
        wq = lax.broadcasted_iota(jnp.int32, (rb, NA_LOC), 0) + r0
        wk = lax.broadcasted_iota(jnp.int32, (rb, NA_LOC), 1) % GRID_W
        cs = jnp.clip(wq - NA_WIN_COLS // 2, 0, GRID_W - NA_WIN_COLS)
        col_ok[r0] = (wk >= cs) & (wk < cs + NA_WIN_COLS)
    maps = []
    for hd in range(H_D):
        half = hd % 2
        ps = slice(hd // 2 * LANES, (hd // 2 + 1) * LANES)
        if half == 0:
            sinks = _pair_store(o_ref, ps.start)

        def fix(s, r0, hd=hd):
            loc = s[:, :NA_LOC] * scale + bias_ref[hd, r0:r0 + rb, :]
            return jnp.concatenate([jnp.where(col_ok[r0], loc, NEG_INF), s[:, NA_LOC:] * scale], axis=1)

        def scores(ps=ps, half=half):
            q = _keep_half(q_ref[:, ps], half)
            return [_dot_nt(q, k_loc[:, ps]), _dot_nt(q, k_ctx[:, ps])]

        maps.append((scores, lambda ps=ps: [v_loc[:, ps], v_ctx[:, ps]], LOG2E, fix, sinks[half]))
    _attention(maps, s_scr, p_scr, s_scr.shape[0] // 2)


def _na_sample(qn, kn, vn, cache_k, cache_v, bias):
    def bias_map(b, r):
        rs = jnp.clip(r - NA_KR // 2, 0, NA_ROWS - NA_KR)
        return (0, rs - r + NA_WIN_ROWS - 1, 0, 0)
    return pl.pallas_call(
        _na_sample_kernel,
        grid=(DEC_BATCH, NA_ROWS),
        in_specs=[pl.BlockSpec((GRID_W, 512), lambda b, r: (b * NA_ROWS + r, 0)),
                  pl.BlockSpec((DEC_SEQ, 512), lambda b, r: (b, 0)),
                  pl.BlockSpec((DEC_SEQ, 512), lambda b, r: (b, 0)),
                  pl.BlockSpec((None, PAST_LEN, 512), lambda b, r: (b, 0, 0)),
                  pl.BlockSpec((None, PAST_LEN, 512), lambda b, r: (b, 0, 0)),
                  pl.BlockSpec((H_D, None, GRID_W, NA_LOC), bias_map)],
        out_specs=pl.BlockSpec((GRID_W, 512), lambda b, r: (b * NA_ROWS + r, 0)),
        out_shape=jax.ShapeDtypeStruct((qn.shape[0], 512), BF16),
        scratch_shapes=_attn_scratch(GRID_W, NA_LOC + PAST_LEN, CONTEXT_GROUP),
        compiler_params=_cparams("parallel", "parallel"),
        name="na_s",
    )(qn, kn, vn, cache_k, cache_v, bias)


def _na_bias_table(rpb):
    edge = GRID_W - NA_WIN_COLS
    ext = jnp.pad(rpb.astype(F32), ((0, 0), (0, 0), (edge, edge)), mode="edge")
    toep = jnp.stack([ext[:, :, GRID_W - 1 - wq:2 * GRID_W - 1 - wq] for wq in range(GRID_W)], axis=1)
    flat = toep.reshape(H_D, GRID_W, (2 * NA_WIN_ROWS - 1) * GRID_W)
    return jnp.stack([flat[:, :, d0 * GRID_W:d0 * GRID_W + NA_LOC] for d0 in range(NA_WIN_ROWS)], axis=1)


def _post_attn_kernel(n_parts, *refs):
    o_refs = refs[:n_parts]
    h_ref, wout_ref, mod_ref, nw1_ref, h1_ref = refs[n_parts:]
    y = None
    off = 0
    for o_ref in o_refs:
        wd = o_ref.shape[1]
        part = _dot(o_ref[...], wout_ref[off:off + wd, :])
        y = part if y is None else y + part
        off += wd
    mod = mod_ref[...]
    h1_ref[...] = h_ref[...] + mod[2:3] * _rms(y, nw1_ref[...])


def _post_attn(o_parts, h, w_out, mod, layer, sample, nw1):
    n = h.shape[0]
    tm = ROW_TILE
    in_specs = [_rows(tm, o.shape[1]) for o in o_parts]
    in_specs += [_rows(tm, D_MODEL), _full((D_MODEL, D_MODEL)), _mod_spec(layer, sample, tm), _full((1, D_MODEL))]
    return pl.pallas_call(
        functools.partial(_post_attn_kernel, len(o_parts)),
        grid=(n // tm,),
        in_specs=in_specs,
        out_specs=_rows(tm, D_MODEL),
        out_shape=jax.ShapeDtypeStruct((n, D_MODEL), F32),
        compiler_params=_cparams("parallel"),
        name="post_attn_s" if sample else "post_attn_p",
    )(*o_parts, h, w_out, mod, nw1)


FF_PAIR = 2 * FF_CHUNK
N_FF = D_FF // FF_CHUNK
SUBLANES = 8
ACT_TILES = 4
ROW_BLOCK = 256
NORM_ROWS = 32


def _ffn_kernel(seq_len, h_ref, wup_ref, cw_ref, cb_ref, wd_ref, mod_ref, nw2_ref, nw3_ref, o_ref,
                u_ref, z0_ref, z1_ref, a_ref):
    tm = h_ref.shape[0]
    n_blocks = tm // ROW_BLOCK
    rows = ACT_TILES * SUBLANES
    mod = mod_ref[...]
    sub = lax.broadcasted_iota(jnp.int32, (SUBLANES, LANES), 0)
    zero_rows = jnp.zeros((SUBLANES, FF_PAIR), F32)
    for z_ref in (z0_ref, z1_ref):
        z_ref[0:SUBLANES, :] = zero_rows
        z_ref[SUBLANES + tm:2 * SUBLANES + tm, :] = zero_rows

    nw2 = nw2_ref[...]
    for c in range(tm // NORM_ROWS):
        x = h_ref[c * NORM_ROWS:(c + 1) * NORM_ROWS, :]
        u = _rms(x, nw2) * (1.0 + mod[4:5]) + mod[3:4]
        u_ref[c * NORM_ROWS:(c + 1) * NORM_ROWS, :] = u.astype(BF16)

    def up(j, z_ref, blk):
        r0 = blk * ROW_BLOCK
        u = u_ref[r0:r0 + ROW_BLOCK, :]
        rows_ = slice(SUBLANES + r0, SUBLANES + r0 + ROW_BLOCK)
        for half in range(2):
            c0 = half * D_FF + j * FF_CHUNK
            if not isinstance(c0, int):
                c0 = pl.multiple_of(c0, FF_CHUNK)
            z_ref[rows_, half * FF_CHUNK:(half + 1) * FF_CHUNK] = _dot(u, wup_ref[:, pl.ds(c0, FF_CHUNK)])

    def act(j, z_ref, col, blk):
        cw = cw_ref[j]
        cb = cb_ref[j]
        for lc in range(FF_CHUNK // LANES):
            taps = []
            for lane0 in (lc * LANES, FF_CHUNK + lc * LANES):
                lanes = slice(lane0, lane0 + LANES)
                taps.append([jnp.broadcast_to(cw[k:k + 1, lanes], (rows, LANES)) for k in range(3)]
                            + [jnp.broadcast_to(cb[:, lanes], (rows, LANES))])
            for c in range(ROW_BLOCK // rows):
                r = blk * ROW_BLOCK + c * rows
                first = r % seq_len == 0
                last = (r + rows) % seq_len == 0

                def conv(lane0, tap):
                    ext = z_ref[r:r + rows + 2 * SUBLANES, lane0:lane0 + LANES]
                    tiles = [ext[t * SUBLANES:(t + 1) * SUBLANES] for t in range(ACT_TILES + 2)]
                    down = [pltpu.roll(t, 1, 0) for t in tiles[:-1]]
                    up_ = [pltpu.roll(t, SUBLANES - 1, 0) for t in tiles[1:]]
                    prev, nxt = [], []
                    for t in range(ACT_TILES):
                        above = 0.0 if (first and t == 0) else down[t]
                        below = 0.0 if (last and t == ACT_TILES - 1) else up_[t + 1]
                        prev.append(jnp.where(sub == 0, above, down[t + 1]))
                        nxt.append(jnp.where(sub == SUBLANES - 1, below, up_[t]))
                    prev = jnp.concatenate(prev, axis=0)
                    nxt = jnp.concatenate(nxt, axis=0)
                    return prev * tap[0] + ext[SUBLANES:SUBLANES + rows] * tap[1] + nxt * tap[2] + tap[3]

                g = conv(lc * LANES, taps[0])
                v = conv(FF_CHUNK + lc * LANES, taps[1])
                a = (g / (1.0 + jnp.exp(-g))) * v
                lane = col + lc * LANES
                if not isinstance(lane, int):
                    lane = pl.multiple_of(lane, LANES)
                a_ref[r:r + rows, pl.ds(lane, LANES)] = a.astype(BF16)

    for blk in range(n_blocks):
        up(0, z0_ref, blk)

    def pair(i, carry):
        j = 2 * i
        col = pl.multiple_of(j * FF_CHUNK, FF_CHUNK)
        for blk in range(n_blocks):
            up(j + 1, z1_ref, blk)
            act(j, z0_ref, col, blk)
        for blk in range(n_blocks):
            up(j + 2, z0_ref, blk)
            act(j + 1, z1_ref, col + FF_CHUNK, blk)
        return carry

    lax.fori_loop(0, (N_FF - 1) // 2, pair, 0)
    for blk in range(n_blocks):
        act(N_FF - 1, z0_ref, (N_FF - 1) * FF_CHUNK, blk)
    nw3 = nw3_ref[...]
    for blk in range(n_blocks):
        r0 = blk * ROW_BLOCK
        y = _dot(a_ref[r0:r0 + ROW_BLOCK, :], wd_ref[...])
        for c in range(ROW_BLOCK // NORM_ROWS):
            rs = slice(r0 + c * NORM_ROWS, r0 + (c + 1) * NORM_ROWS)
            o_ref[rs, :] = h_ref[rs, :] + mod[5:6] * _rms(y[c * NORM_ROWS:(c + 1) * NORM_ROWS], nw3)


def _ffn(h1, w, mod, layer, sample, nw2, nw3):
    n = h1.shape[0]
    tm = FFN_ROW_TILE
    seq_len = DEC_SEQ if sample else SEQ
    once = pl.Buffered(1)
    in_specs = [
        _rows(tm, D_MODEL),
        pl.BlockSpec((D_MODEL, 2 * D_FF), lambda i: (0, 0), pipeline_mode=once),
        pl.BlockSpec((N_FF, 3, FF_PAIR), lambda i: (0, 0, 0), pipeline_mode=once),
        pl.BlockSpec((N_FF, 1, FF_PAIR), lambda i: (0, 0, 0), pipeline_mode=once),
        pl.BlockSpec((D_FF, D_MODEL), lambda i: (0, 0), pipeline_mode=once),
        _mod_spec(layer, sample, tm),
        _full((1, D_MODEL)),
        _full((1, D_MODEL)),
    ]
    return pl.pallas_call(
        functools.partial(_ffn_kernel, seq_len),
        grid=(n // tm,),
        in_specs=in_specs,
        out_specs=_rows(tm, D_MODEL),
        out_shape=jax.ShapeDtypeStruct((n, D_MODEL), F32),
        scratch_shapes=[pltpu.VMEM((tm, D_MODEL), BF16), pltpu.VMEM((tm + 2 * SUBLANES, FF_PAIR), F32),
                        pltpu.VMEM((tm + 2 * SUBLANES, FF_PAIR), F32), pltpu.VMEM((tm, D_FF), BF16)],
        compiler_params=pltpu.CompilerParams(dimension_semantics=("parallel",), vmem_limit_bytes=FFN_VMEM_LIMIT),
        name="ffn_s" if sample else "ffn_p",
    )(h1, w["w_up"], w["conv_w"], w["conv_b"], w["w_down"], mod, nw2, nw3)


def _pair_chunks(x):
    lead = x.shape[:-1]
    x = x.reshape(lead + (2, N_FF, FF_CHUNK))
    x = jnp.moveaxis(x, -2, 0)
    return x.reshape((N_FF,) + lead + (FF_PAIR,))


def _rope_tables():
    def table(rot_dim):
        t = np.arange(DEC_SEQ)
        n_freq = rot_dim // 4
        inv = 1.0 / (ROPE_THETA ** (np.arange(n_freq) / n_freq))
        ang = np.concatenate([(t // GRID_W)[:, None] * inv[None, :], (t % GRID_W)[:, None] * inv[None, :]], axis=-1)
        cos = np.cos(ang).astype(np.float32)
        sin = np.sin(ang).astype(np.float32)
        reps = LANES // rot_dim
        return (np.tile(np.concatenate([cos, cos], axis=-1), (1, reps)),
                np.tile(np.concatenate([-sin, sin], axis=-1), (1, reps)))
    ca, sa = table(QK_ROPE)
    ch, sh = table(HEAD_DIM)
    rope_lanes = (np.arange(MLA_SLOT) >= QK_NOPE) & (np.arange(MLA_SLOT) < QK_NOPE + QK_ROPE)
    ca = np.where(rope_lanes[None, :], ca, 1.0).astype(np.float32)
    sa = np.where(rope_lanes[None, :], sa, 0.0).astype(np.float32)
    return {"ca": jnp.asarray(ca), "sa": jnp.asarray(sa), "ch": jnp.asarray(ch), "sh": jnp.asarray(sh)}


def _group_mean_matrix(width):
    idx = np.arange(width) // HEAD_DIM
    return jnp.asarray((idx[:, None] == idx[None, :]).astype(np.float32) / HEAD_DIM, BF16)


def kernel(x_prompt, x_sample, c, cache_mla_ckv, cache_mla_kpe, cache_diff_k, cache_diff_v, cache_gqa_k, cache_gqa_v, cache_na_k, cache_na_v, c_ctx, norm_w, w_mod, b_mod, w_in_even, w_out_even, w_uq, q_norm_w, kv_norm_w, w_uk, w_uv, diff_lam, diff_subln_w, w_in_odd, w_out_odd, qk_norm_w, na_rpb, w_up, conv_w, conv_b, w_down):
    rope = _rope_tables()
    n_p = BATCH * SEQ
    n_s = DEC_BATCH * DEC_SEQ
    cvecs = jnp.concatenate([c_ctx[None, :], c, jnp.zeros((MOD_ROWS - 1 - DEC_BATCH, D_MODEL), F32)], axis=0)
    mod = _modulation(cvecs, w_mod, b_mod).reshape(DEPTH, MOD_ROWS, 6, D_MODEL)
    hp = x_prompt.reshape(n_p, D_MODEL)
    hs = x_sample.reshape(n_s, D_MODEL)
    even_states, odd_states = [], []
    for l in range(DEPTH):
        i = l // 2
        nw = [norm_w[l, k][None, :] for k in range(4)]
        if l % 2 == 0:
            lam_init = 0.8 - 0.6 * math.exp(-0.3 * l)
            wi = w_in_even[i]
            w_uq3 = w_uq[i].reshape(Q_LORA, H_A, QK_NOPE + QK_ROPE)
            w = {
                "w_in": jnp.concatenate([wi[:, :384], jnp.zeros((D_MODEL, QK_NOPE), F32), wi[:, 384:416],
                                         jnp.zeros((D_MODEL, MLA_SLOT - QK_NOPE - QK_ROPE), F32), wi[:, 416:]],
                                        axis=1).astype(BF16),
                "q_norm_w": q_norm_w[i][None, :],
                "kv_norm_w": kv_norm_w[i][None, :],
                "w_uq": jnp.pad(w_uq3, ((0, 0), (0, 0), (0, MLA_SLOT - QK_NOPE - QK_ROPE))
                                ).reshape(Q_LORA, H_A * MLA_SLOT).astype(BF16),
                "w_uk": jnp.pad(w_uk[i].reshape(KV_LORA, H_A, QK_NOPE), ((0, 0), (0, 0), (0, MLA_SLOT - QK_NOPE))
                                ).reshape(KV_LORA, H_A * MLA_SLOT).astype(BF16),
                "w_uv": w_uv[i].astype(BF16),
                "diff_lam": diff_lam[i],
                "diff_subln_w": diff_subln_w[i][None, :],
            }
            outs_p = _proj_even(hp, mod, l, False, nw[0], w, rope)
            outs_s = _proj_even(hs, mod, l, True, nw[0], w, rope)
            even_states.append(outs_p[6:])
            o_p = [_attn_even(outs_p[:6], False, lam_init, None, w)]
            caches = (cache_mla_ckv[:, i], cache_mla_kpe[:, i],
                      cache_diff_k[:, i].reshape(DEC_BATCH, PAST_LEN, 512),
                      cache_diff_v[:, i].reshape(DEC_BATCH, PAST_LEN, 512))
            o_s = [_attn_even(outs_s, True, lam_init, caches, w)]
            w_out = w_out_even[i].astype(BF16)
        else:
            q_w = jnp.tile(qk_norm_w[i, 0], H_C)[None, :]
            k_w = jnp.tile(qk_norm_w[i, 1], KV_C)[None, :]
            w = {"w_in": w_in_odd[i].astype(BF16), "q_w": q_w, "k_w": k_w,
                 "gq": _group_mean_matrix(512), "gk": _group_mean_matrix(128)}
            outs_p = _proj_odd(hp, mod, l, False, nw[0], w, rope)
            outs_s = _proj_odd(hs, mod, l, True, nw[0], w, rope)
            odd_states.append(outs_p[6:])
            o_p = [_attn_odd_prompt(outs_p[:6])]
            qc, kc, vc, qn, kn, vn = outs_s
            o_c = _gqa_sample(qc, kc, vc, cache_gqa_k[:, i].reshape(DEC_BATCH, PAST_LEN, 128),
                              cache_gqa_v[:, i].reshape(DEC_BATCH, PAST_LEN, 128))
            o_d = _na_sample(qn, kn, vn, cache_na_k[:, i].reshape(DEC_BATCH, PAST_LEN, 512),
                             cache_na_v[:, i].reshape(DEC_BATCH, PAST_LEN, 512), _na_bias_table(na_rpb[i]))
            o_s = [o_c, o_d]
            w_out = w_out_odd[i].astype(BF16)
        wf = {"w_up": w_up[l].astype(BF16), "conv_w": _pair_chunks(conv_w[l]),
              "conv_b": _pair_chunks(conv_b[l][None, :]), "w_down": w_down[l].astype(BF16)}
        h1p = _post_attn(o_p, hp, w_out, mod, l, False, nw[1])
        h1s = _post_attn(o_s, hs, w_out, mod, l, True, nw[1])
        hp = _ffn(h1p, wf, mod, l, False, nw[2], nw[3])
        hs = _ffn(h1s, wf, mod, l, True, nw[2], nw[3])

    def stack(states, k, shape):
        return jnp.stack([st[k].reshape((BATCH, SEQ) + shape) for st in states], axis=1)

    new_mla_ckv = stack(even_states, 0, (KV_LORA,))
    new_mla_kpe = stack(even_states, 1, (QK_ROPE,))
    new_diff_k = stack(even_states, 2, (H_B, 2 * DH_B))
    new_diff_v = stack(even_states, 3, (H_B, 2 * DH_B))
    new_gqa_k = stack(odd_states, 0, (KV_C, DH_C))
    new_gqa_v = stack(odd_states, 1, (KV_C, DH_C))
    new_na_k = stack(odd_states, 2, (H_D, DH_D))
    new_na_v = stack(odd_states, 3, (H_D, DH_D))
    return (hp.reshape(BATCH, SEQ, D_MODEL), hs.reshape(DEC_BATCH, DEC_SEQ, D_MODEL),
            new_mla_ckv, new_mla_kpe, new_diff_k, new_diff_v, new_gqa_k, new_gqa_v, new_na_k, new_na_v)
```

```python
import functools
import math

import numpy as np
import jax
import jax.numpy as jnp
from jax import lax
from jax.experimental import pallas as pl
from jax.experimental.pallas import tpu as pltpu

D_MODEL = 1024
BATCH = 32
SEQ = 256
DEPTH = 2
DEC_BATCH = 4
DEC_SEQ = 1024
PAST_LEN = 256
GRID_W = 64
HEAD_DIM = 64
H_A = 8
QK_NOPE = 64
QK_ROPE = 32
V_A = 64
Q_LORA = 256
KV_LORA = 128
H_B = 4
DH_B = HEAD_DIM
H_C = 8
KV_C = 2
DH_C = HEAD_DIM
H_D = 8
DH_D = HEAD_DIM
NA_WIN_ROWS = 8
NA_WIN_COLS = 16
D_FF = 2816
ROPE_THETA = 10000.0
EPS = 1e-6
NEG_INF = -1e30

LANES = 128
MOD_ROWS = 8
ROW_TILE = 512
FFN_ROW_TILE = 1024
FF_CHUNK = 256
Q_TILE = 256
LATENT_GROUP = 1
CONTEXT_GROUP = 4
VMEM_LIMIT = 48 * 1024 * 1024
FFN_VMEM_LIMIT = 56 * 1024 * 1024

F32 = jnp.float32
BF16 = jnp.bfloat16
LOG2E = math.log2(math.e)


def _cparams(*sem):
    return pltpu.CompilerParams(dimension_semantics=sem, vmem_limit_bytes=VMEM_LIMIT)


def _dot(a, b):
    return jnp.dot(a, b, preferred_element_type=F32)


def _dot_nt(a, b):
    return lax.dot_general(a, b, (((1,), (1,)), ((), ())), preferred_element_type=F32)


def _rms(x, w):
    return x * lax.rsqrt(jnp.mean(x * x, axis=-1, keepdims=True) + EPS) * w


def _group_rms(x, w, gmat):
    x2 = x * x
    hi = x2.astype(BF16)
    lo = (x2 - hi.astype(F32)).astype(BF16)
    ms = _dot(hi, gmat) + _dot(lo, gmat)
    return x * lax.rsqrt(ms + EPS) * w


def _rope(x, cos, sin_signed, half):
    outs = []
    for j in range(x.shape[1] // LANES):
        xc = x[:, j * LANES:(j + 1) * LANES]
        lane = lax.broadcasted_iota(jnp.int32, xc.shape, 1)
        first = (lane % (2 * half)) < half
        partner = jnp.where(first, pltpu.roll(xc, LANES - half, 1), pltpu.roll(xc, half, 1))
        outs.append(xc * cos + partner * sin_signed)
    return outs[0] if len(outs) == 1 else jnp.concatenate(outs, axis=1)


def _softmax_block_rows(n_keys):
    return max(16, min(64, (16 * 1280 // n_keys) // 16 * 16))


def _attention(maps, s_scr, p_scr, group, mxu_sum=False):
    slots = s_scr.shape[0]
    staged = {}

    def stage(i):
        s_ref = s_scr.at[i % slots]
        offs, off = [], 0
        for s in maps[i][0]():
            s_ref[:, off:off + s.shape[1]] = s
            offs.append(off)
            off += s.shape[1]
        staged[i] = (offs, off)

    def softmax(i):
        _, _, c, fix, _ = maps[i]
        n_keys = staged[i][1]
        s_ref, p_ref = s_scr.at[i % slots], p_scr.at[i % slots]
        rb = _softmax_block_rows(n_keys)
        sums = []
        for r0 in range(0, s_ref.shape[0], rb):
            s = s_ref[r0:r0 + rb, 0:n_keys]
            if fix is not None:
                s = fix(s, r0)
            m = jnp.max(s, axis=-1, keepdims=True)
            p = jnp.exp2((s - m) * c)
            if not mxu_sum:
                sums.append(jnp.sum(p, axis=-1, keepdims=True))
            p_ref[r0:r0 + rb, 0:n_keys] = p.astype(BF16)
        return None if mxu_sum else jnp.concatenate(sums, axis=0)

    def weighted_values(i, den):
        _, values, _, _, sink = maps[i]
        p_ref = p_scr.at[i % slots]
        offs, n_keys = staged.pop(i)
        acc = None
        for o, v in zip(offs, values()):
            part = _dot(p_ref[:, o:o + v.shape[0]], v)
            acc = part if acc is None else acc + part
        if mxu_sum:
            den = _dot(p_ref[:, 0:n_keys], jnp.ones((n_keys, LANES), BF16))
        sink(acc / den)

    groups = [range(g, min(g + group, len(maps))) for g in range(0, len(maps), group)]
    for i in groups[0]:
        stage(i)
    for gi, grp in enumerate(groups):
        if gi + 1 < len(groups):
            for i in groups[gi + 1]:
                stage(i)
        dens = [softmax(i) for i in grp]
        for i, den in zip(grp, dens):
            weighted_values(i, den)


def _attn_scratch(tq, n_keys, group):
    return [pltpu.VMEM((2 * group, tq, n_keys), F32), pltpu.VMEM((2 * group, tq, n_keys), BF16)]


def _upper_half(shape):
    return lax.broadcasted_iota(jnp.int32, shape, 1) >= HEAD_DIM


def _keep_half(x, half):
    upper = _upper_half(x.shape)
    return jnp.where(upper if half else ~upper, x, jnp.zeros_like(x))


def _swap_halves(x):
    return jnp.concatenate([x[:, HEAD_DIM:], x[:, :HEAD_DIM]], axis=1)


def _pair_store(o_ref, c0):
    got = {}

    def make(half):
        def sink(o):
            got[half] = o
            if len(got) == 2:
                o_ref[:, c0:c0 + LANES] = jnp.where(_upper_half(o.shape), got[1], got[0]).astype(BF16)
        return sink
    return make(0), make(1)


def _mod_kernel(c_ref, w_ref, b_ref, o_ref):
    cv = c_ref[...]
    act = cv / (1.0 + jnp.exp(-cv))
    o_ref[...] = _dot(act.astype(BF16), w_ref[...].astype(BF16)) + b_ref[...]


def _modulation(cvecs, w_mod, b_mod):
    tn = 1024
    n = 6 * D_MODEL
    return pl.pallas_call(
        _mod_kernel,
        grid=(DEPTH, n // tn),
        in_specs=[
            pl.BlockSpec((MOD_ROWS, D_MODEL), lambda l, j: (0, 0)),
            pl.BlockSpec((None, D_MODEL, tn), lambda l, j: (l, 0, j)),
            pl.BlockSpec((None, 1, tn), lambda l, j: (l, 0, j)),
        ],
        out_specs=pl.BlockSpec((None, MOD_ROWS, tn), lambda l, j: (l, 0, j)),
        out_shape=jax.ShapeDtypeStruct((DEPTH, MOD_ROWS, n), F32),
        compiler_params=_cparams("parallel", "parallel"),
        name="adaln_mod",
    )(cvecs, w_mod, b_mod.reshape(DEPTH, 1, n))


def _mod_spec(layer, sample, tm):
    if sample:
        per = DEC_SEQ // tm
        return pl.BlockSpec((None, None, 6, D_MODEL), lambda i, *_: (layer, 1 + i // per, 0, 0))
    return pl.BlockSpec((None, None, 6, D_MODEL), lambda i, *_: (layer, 0, 0, 0))


def _full(shape):
    nd = len(shape)
    return pl.BlockSpec(shape, lambda *_: (0,) * nd)


def _rows(tm, width):
    return pl.BlockSpec((tm, width), lambda i, *_: (i, 0))


def _rope_spec(tm):
    per = DEC_SEQ // tm
    return pl.BlockSpec((tm, LANES), lambda i, *_: (i % per, 0))


EVEN_IN = 2048
MLA_SLOT = 128


def _proj_even_kernel(sample, *refs):
    if sample:
        (h_ref, mod_ref, nw_ref, win_ref, qnw_ref, kvnw_ref, wuq_ref, wuk_ref, wuv_ref,
         ca_ref, sa_ref, ch_ref, sh_ref,
         qa_ref, ka_ref, va_ref, qd_ref, kd_ref, vd_ref) = refs
    else:
        (h_ref, mod_ref, nw_ref, win_ref, qnw_ref, kvnw_ref, wuq_ref, wuk_ref, wuv_ref,
         qa_ref, ka_ref, va_ref, qd_ref, kd_ref, vd_ref,
         ckv_st_ref, kpe_st_ref, kd_st_ref, vd_st_ref) = refs
    mod = mod_ref[...]
    u = _rms(h_ref[...], nw_ref[...]) * (1.0 + mod[1:2]) + mod[0:1]
    z = _dot(u.astype(BF16), win_ref[...])
    cq = _rms(z[:, 0:256], qnw_ref[...]).astype(BF16)
    qa = _dot(cq, wuq_ref[...])
    ckv = _rms(z[:, 256:384], kvnw_ref[...])
    ckv_b = ckv.astype(BF16)
    kn = _dot(ckv_b, wuk_ref[...])
    va_ref[...] = _dot(ckv_b, wuv_ref[...]).astype(BF16)
    kpe_slot = z[:, 384:512]
    qd = z[:, 512:1024]
    kd = z[:, 1024:1536]
    vd = z[:, 1536:2048]
    vd_ref[...] = vd.astype(BF16)
    if sample:
        ca, sa, ch, sh = ca_ref[...], sa_ref[...], ch_ref[...], sh_ref[...]
        qa = _rope(qa, ca, sa, QK_ROPE // 2)
        kpe_rot = _rope(kpe_slot, ca, sa, QK_ROPE // 2)
        qd_ref[...] = _rope(qd, ch, sh, DH_B // 2).astype(BF16)
        kd_ref[...] = _rope(kd, ch, sh, DH_B // 2).astype(BF16)
    else:
        kpe_rot = kpe_slot
        qd_ref[...] = qd.astype(BF16)
        kd_ref[...] = kd.astype(BF16)
        ckv_st_ref[...] = ckv
        kpe_st_ref[...] = kpe_slot[:, QK_NOPE:QK_NOPE + QK_ROPE]
        kd_st_ref[...] = kd
        vd_st_ref[...] = vd
    qa_ref[...] = qa.astype(BF16)
    for hd in range(H_A):
        sl = slice(hd * MLA_SLOT, (hd + 1) * MLA_SLOT)
        ka_ref[:, sl] = (kn[:, sl] + kpe_rot).astype(BF16)


def _proj_even(h, mod, layer, sample, nw, w, rope):
    n = h.shape[0]
    tm = ROW_TILE
    wide = H_A * MLA_SLOT
    ins = [h, mod, nw, w["w_in"], w["q_norm_w"], w["kv_norm_w"], w["w_uq"], w["w_uk"], w["w_uv"]]
    in_specs = [_rows(tm, D_MODEL), _mod_spec(layer, sample, tm), _full((1, D_MODEL)), _full((D_MODEL, EVEN_IN)),
                _full((1, Q_LORA)), _full((1, KV_LORA)), _full((Q_LORA, wide)), _full((KV_LORA, wide)),
                _full((KV_LORA, 512))]
    widths = [wide, wide, 512, 512, 512, 512]
    out_shape = [jax.ShapeDtypeStruct((n, wd), BF16) for wd in widths]
    out_specs = [_rows(tm, wd) for wd in widths]
    if sample:
        ins += [rope["ca"], rope["sa"], rope["ch"], rope["sh"]]
        in_specs += [_rope_spec(tm)] * 4
    else:
        for wd in (KV_LORA, QK_ROPE, 512, 512):
            out_shape.append(jax.ShapeDtypeStruct((n, wd), F32))
            out_specs.append(_rows(tm, wd))
    return pl.pallas_call(
        functools.partial(_proj_even_kernel, sample),
        grid=(n // tm,),
        in_specs=in_specs,
        out_specs=out_specs,
        out_shape=out_shape,
        compiler_params=_cparams("parallel"),
        name="proj_even_s" if sample else "proj_even_p",
    )(*ins)


def _attn_even_kernel(sample, lam_init, *refs):
    if sample:
        (qa_ref, ka_ref, va_ref, qd_ref, kd_ref, vd_ref,
         cckv_ref, ckpe_ref, cdk_ref, cdv_ref, wuk_ref, wuv_ref, lam_ref, sub_ref, o_ref, s_scr, p_scr) = refs
    else:
        (qa_ref, ka_ref, va_ref, qd_ref, kd_ref, vd_ref, lam_ref, sub_ref, o_ref, s_scr, p_scr) = refs
    lf = lam_ref[...]
    lam = (jnp.exp(jnp.sum(lf[0:1] * lf[1:2], axis=-1, keepdims=True))
           - jnp.exp(jnp.sum(lf[2:3] * lf[3:4], axis=-1, keepdims=True)) + lam_init)
    if sample:
        cckv = cckv_ref[...].astype(BF16)
        kn_ctx = _dot(cckv, wuk_ref[...])
        va_ctx = _dot(cckv, wuv_ref[...]).astype(BF16)
        n_ctx = cckv.shape[0]
        kpe_ctx = jnp.concatenate([jnp.zeros((n_ctx, QK_NOPE), F32), ckpe_ref[...],
                                   jnp.zeros((n_ctx, MLA_SLOT - QK_NOPE - QK_ROPE), F32)], axis=1)
        kd_ctx = cdk_ref[...].astype(BF16)
        vd_ctx = cdv_ref[...].astype(BF16)
    maps = []
    c_a = (QK_NOPE + QK_ROPE) ** -0.5 * LOG2E
    for hd in range(H_A):
        sl = slice(hd * MLA_SLOT, (hd + 1) * MLA_SLOT)
        if hd % 2 == 0:
            sinks = _pair_store(o_ref, hd * V_A)

        def scores(sl=sl):
            q = qa_ref[:, sl]
            out = [_dot_nt(q, ka_ref[:, sl])]
            if sample:
                out.append(_dot_nt(q, (kn_ctx[:, sl] + kpe_ctx).astype(BF16)))
            return out

        def values(vs=slice(hd // 2 * LANES, (hd // 2 + 1) * LANES)):
            return [va_ref[:, vs]] + ([va_ctx[:, vs]] if sample else [])

        maps.append((scores, values, c_a, None, sinks[hd % 2]))
    c_b = DH_B ** -0.5 * LOG2E
    base = H_A * V_A
    sub_w = sub_ref[...]
    for hd in range(H_B):
        hs = slice(hd * 2 * DH_B, (hd + 1) * 2 * DH_B)
        outs = []

        def sink(o, outs=outs, hs=hs):
            outs.append(o)
            if len(outs) == 2:
                ob = _rms(outs[0] - lam * outs[1], sub_w) * (1.0 - lam_init)
                o_ref[:, base + hs.start:base + hs.stop] = ob.astype(BF16)

        def values(hs=hs):
            return [vd_ref[:, hs]] + ([vd_ctx[:, hs]] if sample else [])

        for comp in range(2):
            def scores(hs=hs, comp=comp):
                q = _keep_half(qd_ref[:, hs], comp)
                out = [_dot_nt(q, kd_ref[:, hs])]
                if sample:
                    out.append(_dot_nt(q, kd_ctx[:, hs]))
                return out

            maps.append((scores, values, c_b, None, sink))
    _attention(maps, s_scr, p_scr, s_scr.shape[0] // 2, mxu_sum=not sample)


def _attn_even(p, sample, lam_init, caches, w):
    qa, ka, va, qd, kd, vd = p
    n = qa.shape[0]
    wide = H_A * MLA_SLOT
    if sample:
        tq, per = Q_TILE, DEC_SEQ // Q_TILE
        grid = (DEC_BATCH, per)
        qspec = lambda wd: pl.BlockSpec((tq, wd), lambda b, i: (b * per + i, 0))
        kspec = lambda wd: pl.BlockSpec((DEC_SEQ, wd), lambda b, i: (b, 0))
        cspec = lambda wd: pl.BlockSpec((None, PAST_LEN, wd), lambda b, i: (b, 0, 0))
        ins = [qa, ka, va, qd, kd, vd, *caches, w["w_uk"], w["w_uv"], w["diff_lam"], w["diff_subln_w"]]
        in_specs = [qspec(wide), kspec(wide), kspec(512), qspec(512), kspec(512), kspec(512),
                    cspec(KV_LORA), cspec(QK_ROPE), cspec(512), cspec(512),
                    _full((KV_LORA, wide)), _full((KV_LORA, 512)), _full((4, DH_B)), _full((1, 2 * DH_B))]
        out_spec = qspec(D_MODEL)
        sem = ("parallel", "parallel")
        n_keys, group = DEC_SEQ + PAST_LEN, LATENT_GROUP
    else:
        tq = SEQ
        grid = (BATCH,)
        spec = lambda wd: pl.BlockSpec((SEQ, wd), lambda b: (b, 0))
        ins = [qa, ka, va, qd, kd, vd, w["diff_lam"], w["diff_subln_w"]]
        in_specs = [spec(wide), spec(wide), spec(512), spec(512), spec(512), spec(512),
                    _full((4, DH_B)), _full((1, 2 * DH_B))]
        out_spec = spec(D_MODEL)
        sem = ("parallel",)
        n_keys, group = SEQ, CONTEXT_GROUP
    return pl.pallas_call(
        functools.partial(_attn_even_kernel, sample, lam_init),
        grid=grid,
        in_specs=in_specs,
        out_specs=out_spec,
        out_shape=jax.ShapeDtypeStruct((n, D_MODEL), BF16),
        scratch_shapes=_attn_scratch(tq, n_keys, group),
        compiler_params=_cparams(*sem),
        name="attn_even_s" if sample else "attn_even_p",
    )(*ins)


ODD_IN = 2304


def _proj_odd_kernel(sample, *refs):
    if sample:
        (h_ref, mod_ref, nw_ref, win_ref, qw_ref, kw_ref, gq_ref, gk_ref, ch_ref, sh_ref,
         qc_ref, kc_ref, vc_ref, qn_ref, kn_ref, vn_ref) = refs
    else:
        (h_ref, mod_ref, nw_ref, win_ref, qw_ref, kw_ref, gq_ref, gk_ref,
         qc_ref, kc_ref, vc_ref, qn_ref, kn_ref, vn_ref,
         kc_st_ref, vc_st_ref, kn_st_ref, vn_st_ref) = refs
    mod = mod_ref[...]
    u = _rms(h_ref[...], nw_ref[...]) * (1.0 + mod[1:2]) + mod[0:1]
    z = _dot(u.astype(BF16), win_ref[...])
    qc = _group_rms(z[:, 0:512], qw_ref[...], gq_ref[...])
    kc = _group_rms(z[:, 512:640], kw_ref[...], gk_ref[...])
    vc = z[:, 640:768]
    kn = z[:, 1280:1792]
    vn = z[:, 1792:2304]
    vc_ref[...] = vc.astype(BF16)
    qn_ref[...] = z[:, 768:1280].astype(BF16)
    kn_ref[...] = kn.astype(BF16)
    vn_ref[...] = vn.astype(BF16)
    if sample:
        ch, sh = ch_ref[...], sh_ref[...]
        qc_ref[...] = _rope(qc, ch, sh, DH_C // 2).astype(BF16)
        kc_ref[...] = _rope(kc, ch, sh, DH_C // 2).astype(BF16)
    else:
        qc_ref[...] = qc.astype(BF16)
        kc_ref[...] = kc.astype(BF16)
        kc_st_ref[...] = kc
        vc_st_ref[...] = vc
        kn_st_ref[...] = kn
        vn_st_ref[...] = vn


def _proj_odd(h, mod, layer, sample, nw, w, rope):
    n = h.shape[0]
    tm = ROW_TILE
    ins = [h, mod, nw, w["w_in"], w["q_w"], w["k_w"], w["gq"], w["gk"]]
    in_specs = [_rows(tm, D_MODEL), _mod_spec(layer, sample, tm), _full((1, D_MODEL)), _full((D_MODEL, ODD_IN)),
                _full((1, 512)), _full((1, 128)), _full((512, 512)), _full((128, 128))]
    widths = [512, 128, 128, 512, 512, 512]
    out_shape = [jax.ShapeDtypeStruct((n, wd), BF16) for wd in widths]
    out_specs = [_rows(tm, wd) for wd in widths]
    if sample:
        ins += [rope["ch"], rope["sh"]]
        in_specs += [_rope_spec(tm)] * 2
    else:
        for wd in (128, 128, 512, 512):
            out_shape.append(jax.ShapeDtypeStruct((n, wd), F32))
            out_specs.append(_rows(tm, wd))
    return pl.pallas_call(
        functools.partial(_proj_odd_kernel, sample),
        grid=(n // tm,),
        in_specs=in_specs,
        out_specs=out_specs,
        out_shape=out_shape,
        compiler_params=_cparams("parallel"),
        name="proj_odd_s" if sample else "proj_odd_p",
    )(*ins)


def _attn_odd_prompt_kernel(qc_ref, kc_ref, vc_ref, qn_ref, kn_ref, vn_ref, o_ref, s_scr, p_scr):
    c = DH_C ** -0.5 * LOG2E
    group = H_C // KV_C
    kc = (kc_ref[...], _swap_halves(kc_ref[...]))
    vc = (vc_ref[...], _swap_halves(vc_ref[...]))
    maps = []
    for hd in range(H_C):
        half, swap = hd % 2, (hd // group) != (hd % 2)
        ps = slice(hd // 2 * LANES, (hd // 2 + 1) * LANES)
        if half == 0:
            sinks = _pair_store(o_ref, ps.start)
        maps.append((lambda ps=ps, half=half, swap=swap: [_dot_nt(_keep_half(qc_ref[:, ps], half), kc[swap])],
                     lambda swap=swap: [vc[swap]], c, None, sinks[half]))
    base = H_C * DH_C
    c = DH_D ** -0.5 * LOG2E
    for hd in range(H_D):
        half = hd % 2
        ps = slice(hd // 2 * LANES, (hd // 2 + 1) * LANES)
        if half == 0:
            sinks = _pair_store(o_ref, base + ps.start)
        maps.append((lambda ps=ps, half=half: [_dot_nt(_keep_half(qn_ref[:, ps], half), kn_ref[:, ps])],
                     lambda ps=ps: [vn_ref[:, ps]], c, None, sinks[half]))
    _attention(maps, s_scr, p_scr, s_scr.shape[0] // 2, mxu_sum=True)


def _attn_odd_prompt(p):
    qc, kc, vc, qn, kn, vn = p
    spec = lambda wd: pl.BlockSpec((SEQ, wd), lambda b: (b, 0))
    return pl.pallas_call(
        _attn_odd_prompt_kernel,
        grid=(BATCH,),
        in_specs=[spec(512), spec(128), spec(128), spec(512), spec(512), spec(512)],
        out_specs=spec(D_MODEL),
        out_shape=jax.ShapeDtypeStruct((qc.shape[0], D_MODEL), BF16),
        scratch_shapes=_attn_scratch(SEQ, SEQ, CONTEXT_GROUP),
        compiler_params=_cparams("parallel"),
        name="attn_odd_p",
    )(qc, kc, vc, qn, kn, vn)


def _gqa_sample_kernel(q_ref, k_ref, v_ref, ck_ref, cv_ref, o_ref, s_scr, p_scr):
    c = DH_C ** -0.5 * LOG2E
    group = H_C // KV_C
    k_loc = (k_ref[...], _swap_halves(k_ref[...]))
    v_loc = (v_ref[...], _swap_halves(v_ref[...]))
    k_ctx = ck_ref[...].astype(BF16)
    v_ctx = cv_ref[...].astype(BF16)
    k_ctx = (k_ctx, _swap_halves(k_ctx))
    v_ctx = (v_ctx, _swap_halves(v_ctx))
    maps = []
    for hd in range(H_C):
        half, swap = hd % 2, (hd // group) != (hd % 2)
        ps = slice(hd // 2 * LANES, (hd // 2 + 1) * LANES)
        if half == 0:
            sinks = _pair_store(o_ref, ps.start)

        def scores(ps=ps, half=half, swap=swap):
            q = _keep_half(q_ref[:, ps], half)
            return [_dot_nt(q, k_loc[swap]), _dot_nt(q, k_ctx[swap])]

        maps.append((scores, lambda swap=swap: [v_loc[swap], v_ctx[swap]], c, None, sinks[half]))
    _attention(maps, s_scr, p_scr, s_scr.shape[0] // 2)


def _gqa_sample(qc, kc, vc, cache_k, cache_v):
    tq, per = Q_TILE, DEC_SEQ // Q_TILE
    return pl.pallas_call(
        _gqa_sample_kernel,
        grid=(DEC_BATCH, per),
        in_specs=[pl.BlockSpec((tq, 512), lambda b, i: (b * per + i, 0)),
                  pl.BlockSpec((DEC_SEQ, 128), lambda b, i: (b, 0)),
                  pl.BlockSpec((DEC_SEQ, 128), lambda b, i: (b, 0)),
                  pl.BlockSpec((None, PAST_LEN, 128), lambda b, i: (b, 0, 0)),
                  pl.BlockSpec((None, PAST_LEN, 128), lambda b, i: (b, 0, 0))],
        out_specs=pl.BlockSpec((tq, 512), lambda b, i: (b * per + i, 0)),
        out_shape=jax.ShapeDtypeStruct((qc.shape[0], 512), BF16),
        scratch_shapes=_attn_scratch(tq, DEC_SEQ + PAST_LEN, LATENT_GROUP),
        compiler_params=_cparams("parallel", "parallel"),
        name="gqa_s",
    )(qc, kc, vc, cache_k, cache_v)


NA_ROWS = DEC_SEQ // GRID_W
NA_KR = min(NA_WIN_ROWS, NA_ROWS)
NA_LOC = NA_KR * GRID_W


def _na_sample_kernel(q_ref, k_ref, v_ref, ck_ref, cv_ref, bias_ref, o_ref, s_scr, p_scr):
    r = pl.program_id(1)
    rs = jnp.clip(r - NA_KR // 2, 0, NA_ROWS - NA_KR)
    start = pl.multiple_of(rs * GRID_W, GRID_W)
    scale = DH_D ** -0.5
    k_loc = k_ref[pl.ds(start, NA_LOC), :]
    v_loc = v_ref[pl.ds(start, NA_LOC), :]
    k_ctx = ck_ref[...].astype(BF16)
    v_ctx = cv_ref[...].astype(BF16)
    rb = _softmax_block_rows(NA_LOC + PAST_LEN)
    col_ok = {}
    for r0 in range(0, GRID_W, rb):
        wq = lax.broadcasted_iota(jnp.int32, (rb, NA_LOC), 0) + r0
        wk = lax.broadcasted_iota(jnp.int32, (rb, NA_LOC), 1) % GRID_W
        cs = jnp.clip(wq - NA_WIN_COLS // 2, 0, GRID_W - NA_WIN_COLS)
        col_ok[r0] = (wk >= cs) & (wk < cs + NA_WIN_COLS)
    maps = []
    for hd in range(H_D):
        half = hd % 2
        ps = slice(hd // 2 * LANES, (hd // 2 + 1) * LANES)
        if half == 0:
            sinks = _pair_store(o_ref, ps.start)

        def fix(s, r0, hd=hd):
            loc = s[:, :NA_LOC] * scale + bias_ref[hd, r0:r0 + rb, :]
            return jnp.concatenate([jnp.where(col_ok[r0], loc, NEG_INF), s[:, NA_LOC:] * scale], axis=1)

        def scores(ps=ps, half=half):
            q = _keep_half(q_ref[:, ps], half)
            return [_dot_nt(q, k_loc[:, ps]), _dot_nt(q, k_ctx[:, ps])]

        maps.append((scores, lambda ps=ps: [v_loc[:, ps], v_ctx[:, ps]], LOG2E, fix, sinks[half]))
    _attention(maps, s_scr, p_scr, s_scr.shape[0] // 2)


def _na_sample(qn, kn, vn, cache_k, cache_v, bias):
    def bias_map(b, r):
        rs = jnp.clip(r - NA_KR // 2, 0, NA_ROWS - NA_KR)
        return (0, rs - r + NA_WIN_ROWS - 1, 0, 0)
    return pl.pallas_call(
        _na_sample_kernel,
        grid=(DEC_BATCH, NA_ROWS),
        in_specs=[pl.BlockSpec((GRID_W, 512), lambda b, r: (b * NA_ROWS + r, 0)),
                  pl.BlockSpec((DEC_SEQ, 512), lambda b, r: (b, 0)),
                  pl.BlockSpec((DEC_SEQ, 512), lambda b, r: (b, 0)),
                  pl.BlockSpec((None, PAST_LEN, 512), lambda b, r: (b, 0, 0)),
                  pl.BlockSpec((None, PAST_LEN, 512), lambda b, r: (b, 0, 0)),
                  pl.BlockSpec((H_D, None, GRID_W, NA_LOC), bias_map)],
        out_specs=pl.BlockSpec((GRID_W, 512), lambda b, r: (b * NA_ROWS + r, 0)),
        out_shape=jax.ShapeDtypeStruct((qn.shape[0], 512), BF16),
        scratch_shapes=_attn_scratch(GRID_W, NA_LOC + PAST_LEN, CONTEXT_GROUP),
        compiler_params=_cparams("parallel", "parallel"),
        name="na_s",
    )(qn, kn, vn, cache_k, cache_v, bias)


def _na_bias_table(rpb):
    edge = GRID_W - NA_WIN_COLS
    ext = jnp.pad(rpb.astype(F32), ((0, 0), (0, 0), (edge, edge)), mode="edge")
    toep = jnp.stack([ext[:, :, GRID_W - 1 - wq:2 * GRID_W - 1 - wq] for wq in range(GRID_W)], axis=1)
    flat = toep.reshape(H_D, GRID_W, (2 * NA_WIN_ROWS - 1) * GRID_W)
    return jnp.stack([flat[:, :, d0 * GRID_W:d0 * GRID_W + NA_LOC] for d0 in range(NA_WIN_ROWS)], axis=1)


def _post_attn_kernel(n_parts, *refs):
    o_refs = refs[:n_parts]
    h_ref, wout_ref, mod_ref, nw1_ref, h1_ref = refs[n_parts:]
    y = None
    off = 0
    for o_ref in o_refs:
        wd = o_ref.shape[1]
        part = _dot(o_ref[...], wout_ref[off:off + wd, :])
        y = part if y is None else y + part
        off += wd
    mod = mod_ref[...]
    h1_ref[...] = h_ref[...] + mod[2:3] * _rms(y, nw1_ref[...])


def _post_attn(o_parts, h, w_out, mod, layer, sample, nw1):
    n = h.shape[0]
    tm = ROW_TILE
    in_specs = [_rows(tm, o.shape[1]) for o in o_parts]
    in_specs += [_rows(tm, D_MODEL), _full((D_MODEL, D_MODEL)), _mod_spec(layer, sample, tm), _full((1, D_MODEL))]
    return pl.pallas_call(
        functools.partial(_post_attn_kernel, len(o_parts)),
        grid=(n // tm,),
        in_specs=in_specs,
        out_specs=_rows(tm, D_MODEL),
        out_shape=jax.ShapeDtypeStruct((n, D_MODEL), F32),
        compiler_params=_cparams("parallel"),
        name="post_attn_s" if sample else "post_attn_p",
    )(*o_parts, h, w_out, mod, nw1)


FF_PAIR = 2 * FF_CHUNK
N_FF = D_FF // FF_CHUNK
SUBLANES = 8
ACT_TILES = 4
ROW_BLOCK = 256
NORM_ROWS = 32


def _ffn_kernel(seq_len, h_ref, wup_ref, cw_ref, cb_ref, wd_ref, mod_ref, nw2_ref, nw3_ref, o_ref,
                u_ref, z0_ref, z1_ref, a_ref):
    tm = h_ref.shape[0]
    n_blocks = tm // ROW_BLOCK
    rows = ACT_TILES * SUBLANES
    mod = mod_ref[...]
    sub = lax.broadcasted_iota(jnp.int32, (SUBLANES, LANES), 0)
    zero_rows = jnp.zeros((SUBLANES, FF_PAIR), F32)
    for z_ref in (z0_ref, z1_ref):
        z_ref[0:SUBLANES, :] = zero_rows
        z_ref[SUBLANES + tm:2 * SUBLANES + tm, :] = zero_rows

    nw2 = nw2_ref[...]
    for c in range(tm // NORM_ROWS):
        x = h_ref[c * NORM_ROWS:(c + 1) * NORM_ROWS, :]
        u = _rms(x, nw2) * (1.0 + mod[4:5]) + mod[3:4]
        u_ref[c * NORM_ROWS:(c + 1) * NORM_ROWS, :] = u.astype(BF16)

    def up(j, z_ref, blk):
        r0 = blk * ROW_BLOCK
        u = u_ref[r0:r0 + ROW_BLOCK, :]
        rows_ = slice(SUBLANES + r0, SUBLANES + r0 + ROW_BLOCK)
        for half in range(2):
            c0 = half * D_FF + j * FF_CHUNK
            if not isinstance(c0, int):
                c0 = pl.multiple_of(c0, FF_CHUNK)
            z_ref[rows_, half * FF_CHUNK:(half + 1) * FF_CHUNK] = _dot(u, wup_ref[:, pl.ds(c0, FF_CHUNK)])

    def act(j, z_ref, col, blk):
        cw = cw_ref[j]
        cb = cb_ref[j]
        for lc in range(FF_CHUNK // LANES):
            taps = []
            for lane0 in (lc * LANES, FF_CHUNK + lc * LANES):
                lanes = slice(lane0, lane0 + LANES)
                taps.append([jnp.broadcast_to(cw[k:k + 1, lanes], (rows, LANES)) for k in range(3)]
                            + [jnp.broadcast_to(cb[:, lanes], (rows, LANES))])
            for c in range(ROW_BLOCK // rows):
                r = blk * ROW_BLOCK + c * rows
                first = r % seq_len == 0
                last = (r + rows) % seq_len == 0

                def conv(lane0, tap):
                    ext = z_ref[r:r + rows + 2 * SUBLANES, lane0:lane0 + LANES]
                    tiles = [ext[t * SUBLANES:(t + 1) * SUBLANES] for t in range(ACT_TILES + 2)]
                    down = [pltpu.roll(t, 1, 0) for t in tiles[:-1]]
                    up_ = [pltpu.roll(t, SUBLANES - 1, 0) for t in tiles[1:]]
                    prev, nxt = [], []
                    for t in range(ACT_TILES):
                        above = 0.0 if (first and t == 0) else down[t]
                        below = 0.0 if (last and t == ACT_TILES - 1) else up_[t + 1]
                        prev.append(jnp.where(sub == 0, above, down[t + 1]))
                        nxt.append(jnp.where(sub == SUBLANES - 1, below, up_[t]))
                    prev = jnp.concatenate(prev, axis=0)
                    nxt = jnp.concatenate(nxt, axis=0)
                    return prev * tap[0] + ext[SUBLANES:SUBLANES + rows] * tap[1] + nxt * tap[2] + tap[3]

                g = conv(lc * LANES, taps[0])
                v = conv(FF_CHUNK + lc * LANES, taps[1])
                a = (g / (1.0 + jnp.exp(-g))) * v
                lane = col + lc * LANES
                if not isinstance(lane, int):
                    lane = pl.multiple_of(lane, LANES)
                a_ref[r:r + rows, pl.ds(lane, LANES)] = a.astype(BF16)

    for blk in range(n_blocks):
        up(0, z0_ref, blk)

    def pair(i, carry):
        j = 2 * i
        col = pl.multiple_of(j * FF_CHUNK, FF_CHUNK)
        for blk in range(n_blocks):
            up(j + 1, z1_ref, blk)
            act(j, z0_ref, col, blk)
        for blk in range(n_blocks):
            up(j + 2, z0_ref, blk)
            act(j + 1, z1_ref, col + FF_CHUNK, blk)
        return carry

    lax.fori_loop(0, (N_FF - 1) // 2, pair, 0)
    for blk in range(n_blocks):
        act(N_FF - 1, z0_ref, (N_FF - 1) * FF_CHUNK, blk)
    nw3 = nw3_ref[...]
    for blk in range(n_blocks):
        r0 = blk * ROW_BLOCK
        y = _dot(a_ref[r0:r0 + ROW_BLOCK, :], wd_ref[...])
        for c in range(ROW_BLOCK // NORM_ROWS):
            rs = slice(r0 + c * NORM_ROWS, r0 + (c + 1) * NORM_ROWS)
            o_ref[rs, :] = h_ref[rs, :] + mod[5:6] * _rms(y[c * NORM_ROWS:(c + 1) * NORM_ROWS], nw3)


def _ffn(h1, w, mod, layer, sample, nw2, nw3):
    n = h1.shape[0]
    tm = FFN_ROW_TILE
    seq_len = DEC_SEQ if sample else SEQ
    once = pl.Buffered(1)
    in_specs = [
        _rows(tm, D_MODEL),
        pl.BlockSpec((D_MODEL, 2 * D_FF), lambda i: (0, 0), pipeline_mode=once),
        pl.BlockSpec((N_FF, 3, FF_PAIR), lambda i: (0, 0, 0), pipeline_mode=once),
        pl.BlockSpec((N_FF, 1, FF_PAIR), lambda i: (0, 0, 0), pipeline_mode=once),
        pl.BlockSpec((D_FF, D_MODEL), lambda i: (0, 0), pipeline_mode=once),
        _mod_spec(layer, sample, tm),
        _full((1, D_MODEL)),
        _full((1, D_MODEL)),
    ]
    return pl.pallas_call(
        functools.partial(_ffn_kernel, seq_len),
        grid=(n // tm,),
        in_specs=in_specs,
        out_specs=_rows(tm, D_MODEL),
        out_shape=jax.ShapeDtypeStruct((n, D_MODEL), F32),
        scratch_shapes=[pltpu.VMEM((tm, D_MODEL), BF16), pltpu.VMEM((tm + 2 * SUBLANES, FF_PAIR), F32),
                        pltpu.VMEM((tm + 2 * SUBLANES, FF_PAIR), F32), pltpu.VMEM((tm, D_FF), BF16)],
        compiler_params=pltpu.CompilerParams(dimension_semantics=("parallel",), vmem_limit_bytes=FFN_VMEM_LIMIT),
        name="ffn_s" if sample else "ffn_p",
    )(h1, w["w_up"], w["conv_w"], w["conv_b"], w["w_down"], mod, nw2, nw3)


def _pair_chunks(x):
    lead = x.shape[:-1]
    x = x.reshape(lead + (2, N_FF, FF_CHUNK))
    x = jnp.moveaxis(x, -2, 0)
    return x.reshape((N_FF,) + lead + (FF_PAIR,))


def _rope_tables():
    def table(rot_dim):
        t = np.arange(DEC_SEQ)
        n_freq = rot_dim // 4
        inv = 1.0 / (ROPE_THETA ** (np.arange(n_freq) / n_freq))
        ang = np.concatenate([(t // GRID_W)[:, None] * inv[None, :], (t % GRID_W)[:, None] * inv[None, :]], axis=-1)
        cos = np.cos(ang).astype(np.float32)
        sin = np.sin(ang).astype(np.float32)
        reps = LANES // rot_dim
        return (np.tile(np.concatenate([cos, cos], axis=-1), (1, reps)),
                np.tile(np.concatenate([-sin, sin], axis=-1), (1, reps)))
    ca, sa = table(QK_ROPE)
    ch, sh = table(HEAD_DIM)
    rope_lanes = (np.arange(MLA_SLOT) >= QK_NOPE) & (np.arange(MLA_SLOT) < QK_NOPE + QK_ROPE)
    ca = np.where(rope_lanes[None, :], ca, 1.0).astype(np.float32)
    sa = np.where(rope_lanes[None, :], sa, 0.0).astype(np.float32)
    return {"ca": jnp.asarray(ca), "sa": jnp.asarray(sa), "ch": jnp.asarray(ch), "sh": jnp.asarray(sh)}


def _group_mean_matrix(width):
    idx = np.arange(width) // HEAD_DIM
    return jnp.asarray((idx[:, None] == idx[None, :]).astype(np.float32) / HEAD_DIM, BF16)


def kernel(x_prompt, x_sample, c, cache_mla_ckv, cache_mla_kpe, cache_diff_k, cache_diff_v, cache_gqa_k, cache_gqa_v, cache_na_k, cache_na_v, c_ctx, norm_w, w_mod, b_mod, w_in_even, w_out_even, w_uq, q_norm_w, kv_norm_w, w_uk, w_uv, diff_lam, diff_subln_w, w_in_odd, w_out_odd, qk_norm_w, na_rpb, w_up, conv_w, conv_b, w_down):
    rope = _rope_tables()
    n_p = BATCH * SEQ
    n_s = DEC_BATCH * DEC_SEQ
    cvecs = jnp.concatenate([c_ctx[None, :], c, jnp.zeros((MOD_ROWS - 1 - DEC_BATCH, D_MODEL), F32)], axis=0)
    mod = _modulation(cvecs, w_mod, b_mod).reshape(DEPTH, MOD_ROWS, 6, D_MODEL)
    hp = x_prompt.reshape(n_p, D_MODEL)
    hs = x_sample.reshape(n_s, D_MODEL)
    even_states, odd_states = [], []
    for l in range(DEPTH):
        i = l // 2
        nw = [norm_w[l, k][None, :] for k in range(4)]
        if l % 2 == 0:
            lam_init = 0.8 - 0.6 * math.exp(-0.3 * l)
            wi = w_in_even[i]
            w_uq3 = w_uq[i].reshape(Q_LORA, H_A, QK_NOPE + QK_ROPE)
            w = {
                "w_in": jnp.concatenate([wi[:, :384], jnp.zeros((D_MODEL, QK_NOPE), F32), wi[:, 384:416],
                                         jnp.zeros((D_MODEL, MLA_SLOT - QK_NOPE - QK_ROPE), F32), wi[:, 416:]],
                                        axis=1).astype(BF16),
                "q_norm_w": q_norm_w[i][None, :],
                "kv_norm_w": kv_norm_w[i][None, :],
                "w_uq": jnp.pad(w_uq3, ((0, 0), (0, 0), (0, MLA_SLOT - QK_NOPE - QK_ROPE))
                                ).reshape(Q_LORA, H_A * MLA_SLOT).astype(BF16),
                "w_uk": jnp.pad(w_uk[i].reshape(KV_LORA, H_A, QK_NOPE), ((0, 0), (0, 0), (0, MLA_SLOT - QK_NOPE))
                                ).reshape(KV_LORA, H_A * MLA_SLOT).astype(BF16),
                "w_uv": w_uv[i].astype(BF16),
                "diff_lam": diff_lam[i],
                "diff_subln_w": diff_subln_w[i][None, :],
            }
            outs_p = _proj_even(hp, mod, l, False, nw[0], w, rope)
            outs_s = _proj_even(hs, mod, l, True, nw[0], w, rope)
            even_states.append(outs_p[6:])
            o_p = [_attn_even(outs_p[:6], False, lam_init, None, w)]
            caches = (cache_mla_ckv[:, i], cache_mla_kpe[:, i],
                      cache_diff_k[:, i].reshape(DEC_BATCH, PAST_LEN, 512),
                      cache_diff_v[:, i].reshape(DEC_BATCH, PAST_LEN, 512))
            o_s = [_attn_even(outs_s, True, lam_init, caches, w)]
            w_out = w_out_even[i].astype(BF16)
        else:
            q_w = jnp.tile(qk_norm_w[i, 0], H_C)[None, :]
            k_w = jnp.tile(qk_norm_w[i, 1], KV_C)[None, :]
            w = {"w_in": w_in_odd[i].astype(BF16), "q_w": q_w, "k_w": k_w,
                 "gq": _group_mean_matrix(512), "gk": _group_mean_matrix(128)}
            outs_p = _proj_odd(hp, mod, l, False, nw[0], w, rope)
            outs_s = _proj_odd(hs, mod, l, True, nw[0], w, rope)
            odd_states.append(outs_p[6:])
            o_p = [_attn_odd_prompt(outs_p[:6])]
            qc, kc, vc, qn, kn, vn = outs_s
            o_c = _gqa_sample(qc, kc, vc, cache_gqa_k[:, i].reshape(DEC_BATCH, PAST_LEN, 128),
                              cache_gqa_v[:, i].reshape(DEC_BATCH, PAST_LEN, 128))
            o_d = _na_sample(qn, kn, vn, cache_na_k[:, i].reshape(DEC_BATCH, PAST_LEN, 512),
                             cache_na_v[:, i].reshape(DEC_BATCH, PAST_LEN, 512), _na_bias_table(na_rpb[i]))
            o_s = [o_c, o_d]
            w_out = w_out_odd[i].astype(BF16)
        wf = {"w_up": w_up[l].astype(BF16), "conv_w": _pair_chunks(conv_w[l]),
              "conv_b": _pair_chunks(conv_b[l][None, :]), "w_down": w_down[l].astype(BF16)}
        h1p = _post_attn(o_p, hp, w_out, mod, l, False, nw[1])
        h1s = _post_attn(o_s, hs, w_out, mod, l, True, nw[1])
        hp = _ffn(h1p, wf, mod, l, False, nw[2], nw[3])
        hs = _ffn(h1s, wf, mod, l, True, nw[2], nw[3])

    def stack(states, k, shape):
        return jnp.stack([st[k].reshape((BATCH, SEQ) + shape) for st in states], axis=1)

    new_mla_ckv = stack(even_states, 0, (KV_LORA,))
    new_mla_kpe = stack(even_states, 1, (QK_ROPE,))
    new_diff_k = stack(even_states, 2, (H_B, 2 * DH_B))
    new_diff_v = stack(even_states, 3, (H_B, 2 * DH_B))
    new_gqa_k = stack(odd_states, 0, (KV_C, DH_C))
    new_gqa_v = stack(odd_states, 1, (KV_C, DH_C))
    new_na_k = stack(odd_states, 2, (H_D, DH_D))
    new_na_v = stack(odd_states, 3, (H_D, DH_D))
    return (hp.reshape(BATCH, SEQ, D_MODEL), hs.reshape(DEC_BATCH, DEC_SEQ, D_MODEL),
            new_mla_ckv, new_mla_kpe, new_diff_k, new_diff_v, new_gqa_k, new_gqa_v, new_na_k, new_na_v)
```

```python
import functools
import math

import numpy as np
import jax
import jax.numpy as jnp
from jax import lax
from jax.experimental import pallas as pl
from jax.experimental.pallas import tpu as pltpu

D_MODEL = 1024
BATCH = 32
SEQ = 256
DEPTH = 2
DEC_BATCH = 4
DEC_SEQ = 1024
PAST_LEN = 256
GRID_W = 64
HEAD_DIM = 64
H_A = 8
QK_NOPE = 64
QK_ROPE = 32
V_A = 64
Q_LORA = 256
KV_LORA = 128
H_B = 4
DH_B = HEAD_DIM
H_C = 8
KV_C = 2
DH_C = HEAD_DIM
H_D = 8
DH_D = HEAD_DIM
NA_WIN_ROWS = 8
NA_WIN_COLS = 16
D_FF = 2816
ROPE_THETA = 10000.0
EPS = 1e-6
NEG_INF = -1e30

LANES = 128
MOD_ROWS = 8
ROW_TILE = 512
FFN_ROW_TILE = 1024
FF_CHUNK = 256
Q_TILE = 256
LATENT_GROUP = 1
CONTEXT_GROUP = 4
VMEM_LIMIT = 48 * 1024 * 1024
FFN_VMEM_LIMIT = 56 * 1024 * 1024

F32 = jnp.float32
BF16 = jnp.bfloat16
LOG2E = math.log2(math.e)


def _cparams(*sem):
    return pltpu.CompilerParams(dimension_semantics=sem, vmem_limit_bytes=VMEM_LIMIT)


def _dot(a, b):
    return jnp.dot(a, b, preferred_element_type=F32)


def _dot_nt(a, b):
    return lax.dot_general(a, b, (((1,), (1,)), ((), ())), preferred_element_type=F32)


def _rms(x, w):
    return x * lax.rsqrt(jnp.mean(x * x, axis=-1, keepdims=True) + EPS) * w


def _group_rms(x, w, gmat):
    x2 = x * x
    hi = x2.astype(BF16)
    lo = (x2 - hi.astype(F32)).astype(BF16)
    ms = _dot(hi, gmat) + _dot(lo, gmat)
    return x * lax.rsqrt(ms + EPS) * w


def _rope(x, cos, sin_signed, half):
    outs = []
    for j in range(x.shape[1] // LANES):
        xc = x[:, j * LANES:(j + 1) * LANES]
        lane = lax.broadcasted_iota(jnp.int32, xc.shape, 1)
        first = (lane % (2 * half)) < half
        partner = jnp.where(first, pltpu.roll(xc, LANES - half, 1), pltpu.roll(xc, half, 1))
        outs.append(xc * cos + partner * sin_signed)
    return outs[0] if len(outs) == 1 else jnp.concatenate(outs, axis=1)


def _softmax_block_rows(n_keys):
    return max(16, min(64, (16 * 1280 // n_keys) // 16 * 16))


def _attention(maps, s_scr, p_scr, group, mxu_sum=False):
    slots = s_scr.shape[0]
    staged = {}

    def stage(i):
        s_ref = s_scr.at[i % slots]
        offs, off = [], 0
        for s in maps[i][0]():
            s_ref[:, off:off + s.shape[1]] = s
            offs.append(off)
            off += s.shape[1]
        staged[i] = (offs, off)

    def softmax(i):
        _, _, c, fix, _ = maps[i]
        n_keys = staged[i][1]
        s_ref, p_ref = s_scr.at[i % slots], p_scr.at[i % slots]
        rb = _softmax_block_rows(n_keys)
        sums = []
        for r0 in range(0, s_ref.shape[0], rb):
            s = s_ref[r0:r0 + rb, 0:n_keys]
            if fix is not None:
                s = fix(s, r0)
            m = jnp.max(s, axis=-1, keepdims=True)
            p = jnp.exp2((s - m) * c)
            if not mxu_sum:
                sums.append(jnp.sum(p, axis=-1, keepdims=True))
            p_ref[r0:r0 + rb, 0:n_keys] = p.astype(BF16)
        return None if mxu_sum else jnp.concatenate(sums, axis=0)

    def weighted_values(i, den):
        _, values, _, _, sink = maps[i]
        p_ref = p_scr.at[i % slots]
        offs, n_keys = staged.pop(i)
        acc = None
        for o, v in zip(offs, values()):
            part = _dot(p_ref[:, o:o + v.shape[0]], v)
            acc = part if acc is None else acc + part
        if mxu_sum:
            den = _dot(p_ref[:, 0:n_keys], jnp.ones((n_keys, LANES), BF16))
        sink(acc / den)

    groups = [range(g, min(g + group, len(maps))) for g in range(0, len(maps), group)]
    for i in groups[0]:
        stage(i)
    for gi, grp in enumerate(groups):
        if gi + 1 < len(groups):
            for i in groups[gi + 1]:
                stage(i)
        dens = [softmax(i) for i in grp]
        for i, den in zip(grp, dens):
            weighted_values(i, den)


def _attn_scratch(tq, n_keys, group):
    return [pltpu.VMEM((2 * group, tq, n_keys), F32), pltpu.VMEM((2 * group, tq, n_keys), BF16)]


def _upper_half(shape):
    return lax.broadcasted_iota(jnp.int32, shape, 1) >= HEAD_DIM


def _keep_half(x, half):
    upper = _upper_half(x.shape)
    return jnp.where(upper if half else ~upper, x, jnp.zeros_like(x))


def _swap_halves(x):
    return jnp.concatenate([x[:, HEAD_DIM:], x[:, :HEAD_DIM]], axis=1)


def _pair_store(o_ref, c0):
    got = {}

    def make(half):
        def sink(o):
            got[half] = o
            if len(got) == 2:
                o_ref[:, c0:c0 + LANES] = jnp.where(_upper_half(o.shape), got[1], got[0]).astype(BF16)
        return sink
    return make(0), make(1)


def _mod_kernel(c_ref, w_ref, b_ref, o_ref):
    cv = c_ref[...]
    act = cv / (1.0 + jnp.exp(-cv))
    o_ref[...] = _dot(act.astype(BF16), w_ref[...].astype(BF16)) + b_ref[...]


def _modulation(cvecs, w_mod, b_mod):
    tn = 1024
    n = 6 * D_MODEL
    return pl.pallas_call(
        _mod_kernel,
        grid=(DEPTH, n // tn),
        in_specs=[
            pl.BlockSpec((MOD_ROWS, D_MODEL), lambda l, j: (0, 0)),
            pl.BlockSpec((None, D_MODEL, tn), lambda l, j: (l, 0, j)),
            pl.BlockSpec((None, 1, tn), lambda l, j: (l, 0, j)),
        ],
        out_specs=pl.BlockSpec((None, MOD_ROWS, tn), lambda l, j: (l, 0, j)),
        out_shape=jax.ShapeDtypeStruct((DEPTH, MOD_ROWS, n), F32),
        compiler_params=_cparams("parallel", "parallel"),
        name="adaln_mod",
    )(cvecs, w_mod, b_mod.reshape(DEPTH, 1, n))


def _mod_spec(layer, sample, tm):
    if sample:
        per = DEC_SEQ // tm
        return pl.BlockSpec((None, None, 6, D_MODEL), lambda i, *_: (layer, 1 + i // per, 0, 0))
    return pl.BlockSpec((None, None, 6, D_MODEL), lambda i, *_: (layer, 0, 0, 0))


def _full(shape):
    nd = len(shape)
    return pl.BlockSpec(shape, lambda *_: (0,) * nd)


def _rows(tm, width):
    return pl.BlockSpec((tm, width), lambda i, *_: (i, 0))


def _rope_spec(tm):
    per = DEC_SEQ // tm
    return pl.BlockSpec((tm, LANES), lambda i, *_: (i % per, 0))


EVEN_IN = 2048
MLA_SLOT = 128


def _proj_even_kernel(sample, *refs):
    if sample:
        (h_ref, mod_ref, nw_ref, win_ref, qnw_ref, kvnw_ref, wuq_ref, wuk_ref, wuv_ref,
         ca_ref, sa_ref, ch_ref, sh_ref,
         qa_ref, ka_ref, va_ref, qd_ref, kd_ref, vd_ref) = refs
    else:
        (h_ref, mod_ref, nw_ref, win_ref, qnw_ref, kvnw_ref, wuq_ref, wuk_ref, wuv_ref,
         qa_ref, ka_ref, va_ref, qd_ref, kd_ref, vd_ref,
         ckv_st_ref, kpe_st_ref, kd_st_ref, vd_st_ref) = refs
    mod = mod_ref[...]
    u = _rms(h_ref[...], nw_ref[...]) * (1.0 + mod[1:2]) + mod[0:1]
    z = _dot(u.astype(BF16), win_ref[...])
    cq = _rms(z[:, 0:256], qnw_ref[...]).astype(BF16)
    qa = _dot(cq, wuq_ref[...])
    ckv = _rms(z[:, 256:384], kvnw_ref[...])
    ckv_b = ckv.astype(BF16)
    kn = _dot(ckv_b, wuk_ref[...])
    va_ref[...] = _dot(ckv_b, wuv_ref[...]).astype(BF16)
    kpe_slot = z[:, 384:512]
    qd = z[:, 512:1024]
    kd = z[:, 1024:1536]
    vd = z[:, 1536:2048]
    vd_ref[...] = vd.astype(BF16)
    if sample:
        ca, sa, ch, sh = ca_ref[...], sa_ref[...], ch_ref[...], sh_ref[...]
        qa = _rope(qa, ca, sa, QK_ROPE // 2)
        kpe_rot = _rope(kpe_slot, ca, sa, QK_ROPE // 2)
        qd_ref[...] = _rope(qd, ch, sh, DH_B // 2).astype(BF16)
        kd_ref[...] = _rope(kd, ch, sh, DH_B // 2).astype(BF16)
    else:
        kpe_rot = kpe_slot
        qd_ref[...] = qd.astype(BF16)
        kd_ref[...] = kd.astype(BF16)
        ckv_st_ref[...] = ckv
        kpe_st_ref[...] = kpe_slot[:, QK_NOPE:QK_NOPE + QK_ROPE]
        kd_st_ref[...] = kd
        vd_st_ref[...] = vd
    qa_ref[...] = qa.astype(BF16)
    for hd in range(H_A):
        sl = slice(hd * MLA_SLOT, (hd + 1) * MLA_SLOT)
        ka_ref[:, sl] = (kn[:, sl] + kpe_rot).astype(BF16)


def _proj_even(h, mod, layer, sample, nw, w, rope):
    n = h.shape[0]
    tm = ROW_TILE
    wide = H_A * MLA_SLOT
    ins = [h, mod, nw, w["w_in"], w["q_norm_w"], w["kv_norm_w"], w["w_uq"], w["w_uk"], w["w_uv"]]
    in_specs = [_rows(tm, D_MODEL), _mod_spec(layer, sample, tm), _full((1, D_MODEL)), _full((D_MODEL, EVEN_IN)),
                _full((1, Q_LORA)), _full((1, KV_LORA)), _full((Q_LORA, wide)), _full((KV_LORA, wide)),
                _full((KV_LORA, 512))]
    widths = [wide, wide, 512, 512, 512, 512]
    out_shape = [jax.ShapeDtypeStruct((n, wd), BF16) for wd in widths]
    out_specs = [_rows(tm, wd) for wd in widths]
    if sample:
        ins += [rope["ca"], rope["sa"], rope["ch"], rope["sh"]]
        in_specs += [_rope_spec(tm)] * 4
    else:
        for wd in (KV_LORA, QK_ROPE, 512, 512):
            out_shape.append(jax.ShapeDtypeStruct((n, wd), F32))
            out_specs.append(_rows(tm, wd))
    return pl.pallas_call(
        functools.partial(_proj_even_kernel, sample),
        grid=(n // tm,),
        in_specs=in_specs,
        out_specs=out_specs,
        out_shape=out_shape,
        compiler_params=_cparams("parallel"),
        name="proj_even_s" if sample else "proj_even_p",
    )(*ins)


def _attn_even_kernel(sample, lam_init, *refs):
    if sample:
        (qa_ref, ka_ref, va_ref, qd_ref, kd_ref, vd_ref,
         cckv_ref, ckpe_ref, cdk_ref, cdv_ref, wuk_ref, wuv_ref, lam_ref, sub_ref, o_ref, s_scr, p_scr) = refs
    else:
        (qa_ref, ka_ref, va_ref, qd_ref, kd_ref, vd_ref, lam_ref, sub_ref, o_ref, s_scr, p_scr) = refs
    lf = lam_ref[...]
    lam = (jnp.exp(jnp.sum(lf[0:1] * lf[1:2], axis=-1, keepdims=True))
           - jnp.exp(jnp.sum(lf[2:3] * lf[3:4], axis=-1, keepdims=True)) + lam_init)
    if sample:
        cckv = cckv_ref[...].astype(BF16)
        kn_ctx = _dot(cckv, wuk_ref[...])
        va_ctx = _dot(cckv, wuv_ref[...]).astype(BF16)
        n_ctx = cckv.shape[0]
        kpe_ctx = jnp.concatenate([jnp.zeros((n_ctx, QK_NOPE), F32), ckpe_ref[...],
                                   jnp.zeros((n_ctx, MLA_SLOT - QK_NOPE - QK_ROPE), F32)], axis=1)
        kd_ctx = cdk_ref[...].astype(BF16)
        vd_ctx = cdv_ref[...].astype(BF16)
    maps = []
    c_a = (QK_NOPE + QK_ROPE) ** -0.5 * LOG2E
    for hd in range(H_A):
        sl = slice(hd * MLA_SLOT, (hd + 1) * MLA_SLOT)
        if hd % 2 == 0:
            sinks = _pair_store(o_ref, hd * V_A)

        def scores(sl=sl):
            q = qa_ref[:, sl]
            out = [_dot_nt(q, ka_ref[:, sl])]
            if sample:
                out.append(_dot_nt(q, (kn_ctx[:, sl] + kpe_ctx).astype(BF16)))
            return out

        def values(vs=slice(hd // 2 * LANES, (hd // 2 + 1) * LANES)):
            return [va_ref[:, vs]] + ([va_ctx[:, vs]] if sample else [])

        maps.append((scores, values, c_a, None, sinks[hd % 2]))
    c_b = DH_B ** -0.5 * LOG2E
    base = H_A * V_A
    sub_w = sub_ref[...]
    for hd in range(H_B):
        hs = slice(hd * 2 * DH_B, (hd + 1) * 2 * DH_B)
        outs = []

        def sink(o, outs=outs, hs=hs):
            outs.append(o)
            if len(outs) == 2:
                ob = _rms(outs[0] - lam * outs[1], sub_w) * (1.0 - lam_init)
                o_ref[:, base + hs.start:base + hs.stop] = ob.astype(BF16)

        def values(hs=hs):
            return [vd_ref[:, hs]] + ([vd_ctx[:, hs]] if sample else [])

        for comp in range(2):
            def scores(hs=hs, comp=comp):
                q = _keep_half(qd_ref[:, hs], comp)
                out = [_dot_nt(q, kd_ref[:, hs])]
                if sample:
                    out.append(_dot_nt(q, kd_ctx[:, hs]))
                return out

            maps.append((scores, values, c_b, None, sink))
    _attention(maps, s_scr, p_scr, s_scr.shape[0] // 2, mxu_sum=not sample)


def _attn_even(p, sample, lam_init, caches, w):
    qa, ka, va, qd, kd, vd = p
    n = qa.shape[0]
    wide = H_A * MLA_SLOT
    if sample:
        tq, per = Q_TILE, DEC_SEQ // Q_TILE
        grid = (DEC_BATCH, per)
        qspec = lambda wd: pl.BlockSpec((tq, wd), lambda b, i: (b * per + i, 0))
        kspec = lambda wd: pl.BlockSpec((DEC_SEQ, wd), lambda b, i: (b, 0))
        cspec = lambda wd: pl.BlockSpec((None, PAST_LEN, wd), lambda b, i: (b, 0, 0))
        ins = [qa, ka, va, qd, kd, vd, *caches, w["w_uk"], w["w_uv"], w["diff_lam"], w["diff_subln_w"]]
        in_specs = [qspec(wide), kspec(wide), kspec(512), qspec(512), kspec(512), kspec(512),
                    cspec(KV_LORA), cspec(QK_ROPE), cspec(512), cspec(512),
                    _full((KV_LORA, wide)), _full((KV_LORA, 512)), _full((4, DH_B)), _full((1, 2 * DH_B))]
        out_spec = qspec(D_MODEL)
        sem = ("parallel", "parallel")
        n_keys, group = DEC_SEQ + PAST_LEN, LATENT_GROUP
    else:
        tq = SEQ
        grid = (BATCH,)
        spec = lambda wd: pl.BlockSpec((SEQ, wd), lambda b: (b, 0))
        ins = [qa, ka, va, qd, kd, vd, w["diff_lam"], w["diff_subln_w"]]
        in_specs = [spec(wide), spec(wide), spec(512), spec(512), spec(512), spec(512),
                    _full((4, DH_B)), _full((1, 2 * DH_B))]
        out_spec = spec(D_MODEL)
        sem = ("parallel",)
        n_keys, group = SEQ, CONTEXT_GROUP
    return pl.pallas_call(
        functools.partial(_attn_even_kernel, sample, lam_init),
        grid=grid,
        in_specs=in_specs,
        out_specs=out_spec,
        out_shape=jax.ShapeDtypeStruct((n, D_MODEL), BF16),
        scratch_shapes=_attn_scratch(tq, n_keys, group),
        compiler_params=_cparams(*sem),
        name="attn_even_s" if sample else "attn_even_p",
    )(*ins)


ODD_IN = 2304


def _proj_odd_kernel(sample, *refs):
    if sample:
        (h_ref, mod_ref, nw_ref, win_ref, qw_ref, kw_ref, gq_ref, gk_ref, ch_ref, sh_ref,
         qc_ref, kc_ref, vc_ref, qn_ref, kn_ref, vn_ref) = refs
    else:
        (h_ref, mod_ref, nw_ref, win_ref, qw_ref, kw_ref, gq_ref, gk_ref,
         qc_ref, kc_ref, vc_ref, qn_ref, kn_ref, vn_ref,
         kc_st_ref, vc_st_ref, kn_st_ref, vn_st_ref) = refs
    mod = mod_ref[...]
    u = _rms(h_ref[...], nw_ref[...]) * (1.0 + mod[1:2]) + mod[0:1]
    z = _dot(u.astype(BF16), win_ref[...])
    qc = _group_rms(z[:, 0:512], qw_ref[...], gq_ref[...])
    kc = _group_rms(z[:, 512:640], kw_ref[...], gk_ref[...])
    vc = z[:, 640:768]
    kn = z[:, 1280:1792]
    vn = z[:, 1792:2304]
    vc_ref[...] = vc.astype(BF16)
    qn_ref[...] = z[:, 768:1280].astype(BF16)
    kn_ref[...] = kn.astype(BF16)
    vn_ref[...] = vn.astype(BF16)
    if sample:
        ch, sh = ch_ref[...], sh_ref[...]
        qc_ref[...] = _rope(qc, ch, sh, DH_C // 2).astype(BF16)
        kc_ref[...] = _rope(kc, ch, sh, DH_C // 2).astype(BF16)
    else:
        qc_ref[...] = qc.astype(BF16)
        kc_ref[...] = kc.astype(BF16)
        kc_st_ref[...] = kc
        vc_st_ref[...] = vc
        kn_st_ref[...] = kn
        vn_st_ref[...] = vn


def _proj_odd(h, mod, layer, sample, nw, w, rope):
    n = h.shape[0]
    tm = ROW_TILE
    ins = [h, mod, nw, w["w_in"], w["q_w"], w["k_w"], w["gq"], w["gk"]]
    in_specs = [_rows(tm, D_MODEL), _mod_spec(layer, sample, tm), _full((1, D_MODEL)), _full((D_MODEL, ODD_IN)),
                _full((1, 512)), _full((1, 128)), _full((512, 512)), _full((128, 128))]
    widths = [512, 128, 128, 512, 512, 512]
    out_shape = [jax.ShapeDtypeStruct((n, wd), BF16) for wd in widths]
    out_specs = [_rows(tm, wd) for wd in widths]
    if sample:
        ins += [rope["ch"], rope["sh"]]
        in_specs += [_rope_spec(tm)] * 2
    else:
        for wd in (128, 128, 512, 512):
            out_shape.append(jax.ShapeDtypeStruct((n, wd), F32))
            out_specs.append(_rows(tm, wd))
    return pl.pallas_call(
        functools.partial(_proj_odd_kernel, sample),
        grid=(n // tm,),
        in_specs=in_specs,
        out_specs=out_specs,
        out_shape=out_shape,
        compiler_params=_cparams("parallel"),
        name="proj_odd_s" if sample else "proj_odd_p",
    )(*ins)


def _attn_odd_prompt_kernel(qc_ref, kc_ref, vc_ref, qn_ref, kn_ref, vn_ref, o_ref, s_scr, p_scr):
    c = DH_C ** -0.5 * LOG2E
    group = H_C // KV_C
    kc = (kc_ref[...], _swap_halves(kc_ref[...]))
    vc = (vc_ref[...], _swap_halves(vc_ref[...]))
    maps = []
    for hd in range(H_C):
        half, swap = hd % 2, (hd // group) != (hd % 2)
        ps = slice(hd // 2 * LANES, (hd // 2 + 1) * LANES)
        if half == 0:
            sinks = _pair_store(o_ref, ps.start)
        maps.append((lambda ps=ps, half=half, swap=swap: [_dot_nt(_keep_half(qc_ref[:, ps], half), kc[swap])],
                     lambda swap=swap: [vc[swap]], c, None, sinks[half]))
    base = H_C * DH_C
    c = DH_D ** -0.5 * LOG2E
    for hd in range(H_D):
        half = hd % 2
        ps = slice(hd // 2 * LANES, (hd // 2 + 1) * LANES)
        if half == 0:
            sinks = _pair_store(o_ref, base + ps.start)
        maps.append((lambda ps=ps, half=half: [_dot_nt(_keep_half(qn_ref[:, ps], half), kn_ref[:, ps])],
                     lambda ps=ps: [vn_ref[:, ps]], c, None, sinks[half]))
    _attention(maps, s_scr, p_scr, s_scr.shape[0] // 2, mxu_sum=True)


def _attn_odd_prompt(p):
    qc, kc, vc, qn, kn, vn = p
    spec = lambda wd: pl.BlockSpec((SEQ, wd), lambda b: (b, 0))
    return pl.pallas_call(
        _attn_odd_prompt_kernel,
        grid=(BATCH,),
        in_specs=[spec(512), spec(128), spec(128), spec(512), spec(512), spec(512)],
        out_specs=spec(D_MODEL),
        out_shape=jax.ShapeDtypeStruct((qc.shape[0], D_MODEL), BF16),
        scratch_shapes=_attn_scratch(SEQ, SEQ, CONTEXT_GROUP),
        compiler_params=_cparams("parallel"),
        name="attn_odd_p",
    )(qc, kc, vc, qn, kn, vn)


def _gqa_sample_kernel(q_ref, k_ref, v_ref, ck_ref, cv_ref, o_ref, s_scr, p_scr):
    c = DH_C ** -0.5 * LOG2E
    group = H_C // KV_C
    k_loc = (k_ref[...], _swap_halves(k_ref[...]))
    v_loc = (v_ref[...], _swap_halves(v_ref[...]))
    k_ctx = ck_ref[...].astype(BF16)
    v_ctx = cv_ref[...].astype(BF16)
    k_ctx = (k_ctx, _swap_halves(k_ctx))
    v_ctx = (v_ctx, _swap_halves(v_ctx))
    maps = []
    for hd in range(H_C):
        half, swap = hd % 2, (hd // group) != (hd % 2)
        ps = slice(hd // 2 * LANES, (hd // 2 + 1) * LANES)
        if half == 0:
            sinks = _pair_store(o_ref, ps.start)

        def scores(ps=ps, half=half, swap=swap):
            q = _keep_half(q_ref[:, ps], half)
            return [_dot_nt(q, k_loc[swap]), _dot_nt(q, k_ctx[swap])]

        maps.append((scores, lambda swap=swap: [v_loc[swap], v_ctx[swap]], c, None, sinks[half]))
    _attention(maps, s_scr, p_scr, s_scr.shape[0] // 2)


def _gqa_sample(qc, kc, vc, cache_k, cache_v):
    tq, per = Q_TILE, DEC_SEQ // Q_TILE
    return pl.pallas_call(
        _gqa_sample_kernel,
        grid=(DEC_BATCH, per),
        in_specs=[pl.BlockSpec((tq, 512), lambda b, i: (b * per + i, 0)),
                  pl.BlockSpec((DEC_SEQ, 128), lambda b, i: (b, 0)),
                  pl.BlockSpec((DEC_SEQ, 128), lambda b, i: (b, 0)),
                  pl.BlockSpec((None, PAST_LEN, 128), lambda b, i: (b, 0, 0)),
                  pl.BlockSpec((None, PAST_LEN, 128), lambda b, i: (b, 0, 0))],
        out_specs=pl.BlockSpec((tq, 512), lambda b, i: (b * per + i, 0)),
        out_shape=jax.ShapeDtypeStruct((qc.shape[0], 512), BF16),
        scratch_shapes=_attn_scratch(tq, DEC_SEQ + PAST_LEN, LATENT_GROUP),
        compiler_params=_cparams("parallel", "parallel"),
        name="gqa_s",
    )(qc, kc, vc, cache_k, cache_v)


NA_ROWS = DEC_SEQ // GRID_W
NA_KR = min(NA_WIN_ROWS, NA_ROWS)
NA_LOC = NA_KR * GRID_W


def _na_sample_kernel(q_ref, k_ref, v_ref, ck_ref, cv_ref, bias_ref, o_ref, s_scr, p_scr):
    r = pl.program_id(1)
    rs = jnp.clip(r - NA_KR // 2, 0, NA_ROWS - NA_KR)
    start = pl.multiple_of(rs * GRID_W, GRID_W)
    scale = DH_D ** -0.5
    k_loc = k_ref[pl.ds(start, NA_LOC), :]
    v_loc = v_ref[pl.ds(start, NA_LOC), :]
    k_ctx = ck_ref[...].astype(BF16)
    v_ctx = cv_ref[...].astype(BF16)
    rb = _softmax_block_rows(NA_LOC + PAST_LEN)
    col_ok = {}
    for r0 in range(0, GRID_W, rb):
        wq = lax.broadcasted_iota(jnp.int32, (rb, NA_LOC), 0) + r0
        wk = lax.broadcasted_iota(jnp.int32, (rb, NA_LOC), 1) % GRID_W
        cs = jnp.clip(wq - NA_WIN_COLS // 2, 0, GRID_W - NA_WIN_COLS)
        col_ok[r0] = (wk >= cs) & (wk < cs + NA_WIN_COLS)
    maps = []
    for hd in range(H_D):
        half = hd % 2
        ps = slice(hd // 2 * LANES, (hd // 2 + 1) * LANES)
        if half == 0:
            sinks = _pair_store(o_ref, ps.start)

        def fix(s, r0, hd=hd):
            loc = s[:, :NA_LOC] * scale + bias_ref[hd, r0:r0 + rb, :]
            return jnp.concatenate([jnp.where(col_ok[r0], loc, NEG_INF), s[:, NA_LOC:] * scale], axis=1)

        def scores(ps=ps, half=half):
            q = _keep_half(q_ref[:, ps], half)
            return [_dot_nt(q, k_loc[:, ps]), _dot_nt(q, k_ctx[:, ps])]

        maps.append((scores, lambda ps=ps: [v_loc[:, ps], v_ctx[:, ps]], LOG2E, fix, sinks[half]))
    _attention(maps, s_scr, p_scr, s_scr.shape[0] // 2)


def _na_sample(qn, kn, vn, cache_k, cache_v, bias):
    def bias_map(b, r):
        rs = jnp.clip(r - NA_KR // 2, 0, NA_ROWS - NA_KR)
        return (0, rs - r + NA_WIN_ROWS - 1, 0, 0)
    return pl.pallas_call(
        _na_sample_kernel,
        grid=(DEC_BATCH, NA_ROWS),
        in_specs=[pl.BlockSpec((GRID_W, 512), lambda b, r: (b * NA_ROWS + r, 0)),
                  pl.BlockSpec((DEC_SEQ, 512), lambda b, r: (b, 0)),
                  pl.BlockSpec((DEC_SEQ, 512), lambda b, r: (b, 0)),
                  pl.BlockSpec((None, PAST_LEN, 512), lambda b, r: (b, 0, 0)),
                  pl.BlockSpec((None, PAST_LEN, 512), lambda b, r: (b, 0, 0)),
                  pl.BlockSpec((H_D, None, GRID_W, NA_LOC), bias_map)],
        out_specs=pl.BlockSpec((GRID_W, 512), lambda b, r: (b * NA_ROWS + r, 0)),
        out_shape=jax.ShapeDtypeStruct((qn.shape[0], 512), BF16),
        scratch_shapes=_attn_scratch(GRID_W, NA_LOC + PAST_LEN, CONTEXT_GROUP),
        compiler_params=_cparams("parallel", "parallel"),
        name="na_s",
    )(qn, kn, vn, cache_k, cache_v, bias)


def _na_bias_table(rpb):
    edge = GRID_W - NA_WIN_COLS
    ext = jnp.pad(rpb.astype(F32), ((0, 0), (0, 0), (edge, edge)), mode="edge")
    toep = jnp.stack([ext[:, :, GRID_W - 1 - wq:2 * GRID_W - 1 - wq] for wq in range(GRID_W)], axis=1)
    flat = toep.reshape(H_D, GRID_W, (2 * NA_WIN_ROWS - 1) * GRID_W)
    return jnp.stack([flat[:, :, d0 * GRID_W:d0 * GRID_W + NA_LOC] for d0 in range(NA_WIN_ROWS)], axis=1)


def _post_attn_kernel(n_parts, *refs):
    o_refs = refs[:n_parts]
    h_ref, wout_ref, mod_ref, nw1_ref, h1_ref = refs[n_parts:]
    y = None
    off = 0
    for o_ref in o_refs:
        wd = o_ref.shape[1]
        part = _dot(o_ref[...], wout_ref[off:off + wd, :])
        y = part if y is None else y + part
        off += wd
    mod = mod_ref[...]
    h1_ref[...] = h_ref[...] + mod[2:3] * _rms(y, nw1_ref[...])


def _post_attn(o_parts, h, w_out, mod, layer, sample, nw1):
    n = h.shape[0]
    tm = ROW_TILE
    in_specs = [_rows(tm, o.shape[1]) for o in o_parts]
    in_specs += [_rows(tm, D_MODEL), _full((D_MODEL, D_MODEL)), _mod_spec(layer, sample, tm), _full((1, D_MODEL))]
    return pl.pallas_call(
        functools.partial(_post_attn_kernel, len(o_parts)),
        grid=(n // tm,),
        in_specs=in_specs,
        out_specs=_rows(tm, D_MODEL),
        out_shape=jax.ShapeDtypeStruct((n, D_MODEL), F32),
        compiler_params=_cparams("parallel"),
        name="post_attn_s" if sample else "post_attn_p",
    )(*o_parts, h, w_out, mod, nw1)


FF_PAIR = 2 * FF_CHUNK
N_FF = D_FF // FF_CHUNK
SUBLANES = 8
ACT_TILES = 4
ROW_BLOCK = 512
NORM_ROWS = 32


def _ffn_kernel(seq_len, h_ref, wup_ref, cw_ref, cb_ref, wd_ref, mod_ref, nw2_ref, nw3_ref, o_ref,
                u_ref, z0_ref, z1_ref, a_ref):
    tm = h_ref.shape[0]
    n_blocks = tm // ROW_BLOCK
    rows = ACT_TILES * SUBLANES
    mod = mod_ref[...]
    sub = lax.broadcasted_iota(jnp.int32, (SUBLANES, LANES), 0)
    zero_rows = jnp.zeros((SUBLANES, FF_PAIR), F32)
    for z_ref in (z0_ref, z1_ref):
        z_ref[0:SUBLANES, :] = zero_rows
        z_ref[SUBLANES + tm:2 * SUBLANES + tm, :] = zero_rows

    nw2 = nw2_ref[...]
    for c in range(tm // NORM_ROWS):
        x = h_ref[c * NORM_ROWS:(c + 1) * NORM_ROWS, :]
        u = _rms(x, nw2) * (1.0 + mod[4:5]) + mod[3:4]
        u_ref[c * NORM_ROWS:(c + 1) * NORM_ROWS, :] = u.astype(BF16)

    def up(j, z_ref, blk):
        r0 = blk * ROW_BLOCK
        u = u_ref[r0:r0 + ROW_BLOCK, :]
        rows_ = slice(SUBLANES + r0, SUBLANES + r0 + ROW_BLOCK)
        for half in range(2):
            c0 = half * D_FF + j * FF_CHUNK
            if not isinstance(c0, int):
                c0 = pl.multiple_of(c0, FF_CHUNK)
            z_ref[rows_, half * FF_CHUNK:(half + 1) * FF_CHUNK] = _dot(u, wup_ref[:, pl.ds(c0, FF_CHUNK)])

    def act(j, z_ref, col, blk):
        cw = cw_ref[j]
        cb = cb_ref[j]
        for lc in range(FF_CHUNK // LANES):
            taps = []
            for lane0 in (lc * LANES, FF_CHUNK + lc * LANES):
                lanes = slice(lane0, lane0 + LANES)
                taps.append([jnp.broadcast_to(cw[k:k + 1, lanes], (rows, LANES)) for k in range(3)]
                            + [jnp.broadcast_to(cb[:, lanes], (rows, LANES))])
            for c in range(ROW_BLOCK // rows):
                r = blk * ROW_BLOCK + c * rows
                first = r % seq_len == 0
                last = (r + rows) % seq_len == 0

                def conv(lane0, tap):
                    ext = z_ref[r:r + rows + 2 * SUBLANES, lane0:lane0 + LANES]
                    tiles = [ext[t * SUBLANES:(t + 1) * SUBLANES] for t in range(ACT_TILES + 2)]
                    down = [pltpu.roll(t, 1, 0) for t in tiles[:-1]]
                    up_ = [pltpu.roll(t, SUBLANES - 1, 0) for t in tiles[1:]]
                    prev, nxt = [], []
                    for t in range(ACT_TILES):
                        above = 0.0 if (first and t == 0) else down[t]
                        below = 0.0 if (last and t == ACT_TILES - 1) else up_[t + 1]
                        prev.append(jnp.where(sub == 0, above, down[t + 1]))
                        nxt.append(jnp.where(sub == SUBLANES - 1, below, up_[t]))
                    prev = jnp.concatenate(prev, axis=0)
                    nxt = jnp.concatenate(nxt, axis=0)
                    return prev * tap[0] + ext[SUBLANES:SUBLANES + rows] * tap[1] + nxt * tap[2] + tap[3]

                g = conv(lc * LANES, taps[0])
                v = conv(FF_CHUNK + lc * LANES, taps[1])
                a = (g / (1.0 + jnp.exp(-g))) * v
                lane = col + lc * LANES
                if not isinstance(lane, int):
                    lane = pl.multiple_of(lane, LANES)
                a_ref[r:r + rows, pl.ds(lane, LANES)] = a.astype(BF16)

    for blk in range(n_blocks):
        up(0, z0_ref, blk)

    def pair(i, carry):
        j = 2 * i
        col = pl.multiple_of(j * FF_CHUNK, FF_CHUNK)
        for blk in range(n_blocks):
            up(j + 1, z1_ref, blk)
            act(j, z0_ref, col, blk)
        for blk in range(n_blocks):
            up(j + 2, z0_ref, blk)
            act(j + 1, z1_ref, col + FF_CHUNK, blk)
        return carry

    lax.fori_loop(0, (N_FF - 1) // 2, pair, 0)
    for blk in range(n_blocks):
        act(N_FF - 1, z0_ref, (N_FF - 1) * FF_CHUNK, blk)
    nw3 = nw3_ref[...]
    for blk in range(n_blocks):
        r0 = blk * ROW_BLOCK
        y = _dot(a_ref[r0:r0 + ROW_BLOCK, :], wd_ref[...])
        for c in range(ROW_BLOCK // NORM_ROWS):
            rs = slice(r0 + c * NORM_ROWS, r0 + (c + 1) * NORM_ROWS)
            o_ref[rs, :] = h_ref[rs, :] + mod[5:6] * _rms(y[c * NORM_ROWS:(c + 1) * NORM_ROWS], nw3)


def _ffn(h1, w, mod, layer, sample, nw2, nw3):
    n = h1.shape[0]
    tm = FFN_ROW_TILE
    seq_len = DEC_SEQ if sample else SEQ
    once = pl.Buffered(1)
    in_specs = [
        _rows(tm, D_MODEL),
        pl.BlockSpec((D_MODEL, 2 * D_FF), lambda i: (0, 0), pipeline_mode=once),
        pl.BlockSpec((N_FF, 3, FF_PAIR), lambda i: (0, 0, 0), pipeline_mode=once),
        pl.BlockSpec((N_FF, 1, FF_PAIR), lambda i: (0, 0, 0), pipeline_mode=once),
        pl.BlockSpec((D_FF, D_MODEL), lambda i: (0, 0), pipeline_mode=once),
        _mod_spec(layer, sample, tm),
        _full((1, D_MODEL)),
        _full((1, D_MODEL)),
    ]
    return pl.pallas_call(
        functools.partial(_ffn_kernel, seq_len),
        grid=(n // tm,),
        in_specs=in_specs,
        out_specs=_rows(tm, D_MODEL),
        out_shape=jax.ShapeDtypeStruct((n, D_MODEL), F32),
        scratch_shapes=[pltpu.VMEM((tm, D_MODEL), BF16), pltpu.VMEM((tm + 2 * SUBLANES, FF_PAIR), F32),
                        pltpu.VMEM((tm + 2 * SUBLANES, FF_PAIR), F32), pltpu.VMEM((tm, D_FF), BF16)],
        compiler_params=pltpu.CompilerParams(dimension_semantics=("parallel",), vmem_limit_bytes=FFN_VMEM_LIMIT),
        name="ffn_s" if sample else "ffn_p",
    )(h1, w["w_up"], w["conv_w"], w["conv_b"], w["w_down"], mod, nw2, nw3)


def _pair_chunks(x):
    lead = x.shape[:-1]
    x = x.reshape(lead + (2, N_FF, FF_CHUNK))
    x = jnp.moveaxis(x, -2, 0)
    return x.reshape((N_FF,) + lead + (FF_PAIR,))


def _rope_tables():
    def table(rot_dim):
        t = np.arange(DEC_SEQ)
        n_freq = rot_dim // 4
        inv = 1.0 / (ROPE_THETA ** (np.arange(n_freq) / n_freq))
        ang = np.concatenate([(t // GRID_W)[:, None] * inv[None, :], (t % GRID_W)[:, None] * inv[None, :]], axis=-1)
        cos = np.cos(ang).astype(np.float32)
        sin = np.sin(ang).astype(np.float32)
        reps = LANES // rot_dim
        return (np.tile(np.concatenate([cos, cos], axis=-1), (1, reps)),
                np.tile(np.concatenate([-sin, sin], axis=-1), (1, reps)))
    ca, sa = table(QK_ROPE)
    ch, sh = table(HEAD_DIM)
    rope_lanes = (np.arange(MLA_SLOT) >= QK_NOPE) & (np.arange(MLA_SLOT) < QK_NOPE + QK_ROPE)
    ca = np.where(rope_lanes[None, :], ca, 1.0).astype(np.float32)
    sa = np.where(rope_lanes[None, :], sa, 0.0).astype(np.float32)
    return {"ca": jnp.asarray(ca), "sa": jnp.asarray(sa), "ch": jnp.asarray(ch), "sh": jnp.asarray(sh)}


def _group_mean_matrix(width):
    idx = np.arange(width) // HEAD_DIM
    return jnp.asarray((idx[:, None] == idx[None, :]).astype(np.float32) / HEAD_DIM, BF16)


def kernel(x_prompt, x_sample, c, cache_mla_ckv, cache_mla_kpe, cache_diff_k, cache_diff_v, cache_gqa_k, cache_gqa_v, cache_na_k, cache_na_v, c_ctx, norm_w, w_mod, b_mod, w_in_even, w_out_even, w_uq, q_norm_w, kv_norm_w, w_uk, w_uv, diff_lam, diff_subln_w, w_in_odd, w_out_odd, qk_norm_w, na_rpb, w_up, conv_w, conv_b, w_down):
    rope = _rope_tables()
    n_p = BATCH * SEQ
    n_s = DEC_BATCH * DEC_SEQ
    cvecs = jnp.concatenate([c_ctx[None, :], c, jnp.zeros((MOD_ROWS - 1 - DEC_BATCH, D_MODEL), F32)], axis=0)
    mod = _modulation(cvecs, w_mod, b_mod).reshape(DEPTH, MOD_ROWS, 6, D_MODEL)
    hp = x_prompt.reshape(n_p, D_MODEL)
    hs = x_sample.reshape(n_s, D_MODEL)
    even_states, odd_states = [], []
    for l in range(DEPTH):
        i = l // 2
        nw = [norm_w[l, k][None, :] for k in range(4)]
        if l % 2 == 0:
            lam_init = 0.8 - 0.6 * math.exp(-0.3 * l)
            wi = w_in_even[i]
            w_uq3 = w_uq[i].reshape(Q_LORA, H_A, QK_NOPE + QK_ROPE)
            w = {
                "w_in": jnp.concatenate([wi[:, :384], jnp.zeros((D_MODEL, QK_NOPE), F32), wi[:, 384:416],
                                         jnp.zeros((D_MODEL, MLA_SLOT - QK_NOPE - QK_ROPE), F32), wi[:, 416:]],
                                        axis=1).astype(BF16),
                "q_norm_w": q_norm_w[i][None, :],
                "kv_norm_w": kv_norm_w[i][None, :],
                "w_uq": jnp.pad(w_uq3, ((0, 0), (0, 0), (0, MLA_SLOT - QK_NOPE - QK_ROPE))
                                ).reshape(Q_LORA, H_A * MLA_SLOT).astype(BF16),
                "w_uk": jnp.pad(w_uk[i].reshape(KV_LORA, H_A, QK_NOPE), ((0, 0), (0, 0), (0, MLA_SLOT - QK_NOPE))
                                ).reshape(KV_LORA, H_A * MLA_SLOT).astype(BF16),
                "w_uv": w_uv[i].astype(BF16),
                "diff_lam": diff_lam[i],
                "diff_subln_w": diff_subln_w[i][None, :],
            }
            outs_p = _proj_even(hp, mod, l, False, nw[0], w, rope)
            outs_s = _proj_even(hs, mod, l, True, nw[0], w, rope)
            even_states.append(outs_p[6:])
            o_p = [_attn_even(outs_p[:6], False, lam_init, None, w)]
            caches = (cache_mla_ckv[:, i], cache_mla_kpe[:, i],
                      cache_diff_k[:, i].reshape(DEC_BATCH, PAST_LEN, 512),
                      cache_diff_v[:, i].reshape(DEC_BATCH, PAST_LEN, 512))
            o_s = [_attn_even(outs_s, True, lam_init, caches, w)]
            w_out = w_out_even[i].astype(BF16)
        else:
            q_w = jnp.tile(qk_norm_w[i, 0], H_C)[None, :]
            k_w = jnp.tile(qk_norm_w[i, 1], KV_C)[None, :]
            w = {"w_in": w_in_odd[i].astype(BF16), "q_w": q_w, "k_w": k_w,
                 "gq": _group_mean_matrix(512), "gk": _group_mean_matrix(128)}
            outs_p = _proj_odd(hp, mod, l, False, nw[0], w, rope)
            outs_s = _proj_odd(hs, mod, l, True, nw[0], w, rope)
            odd_states.append(outs_p[6:])
            o_p = [_attn_odd_prompt(outs_p[:6])]
            qc, kc, vc, qn, kn, vn = outs_s
            o_c = _gqa_sample(qc, kc, vc, cache_gqa_k[:, i].reshape(DEC_BATCH, PAST_LEN, 128),
                              cache_gqa_v[:, i].reshape(DEC_BATCH, PAST_LEN, 128))
            o_d = _na_sample(qn, kn, vn, cache_na_k[:, i].reshape(DEC_BATCH, PAST_LEN, 512),
                             cache_na_v[:, i].reshape(DEC_BATCH, PAST_LEN, 512), _na_bias_table(na_rpb[i]))
            o_s = [o_c, o_d]
            w_out = w_out_odd[i].astype(BF16)
        wf = {"w_up": w_up[l].astype(BF16), "conv_w": _pair_chunks(conv_w[l]),
              "conv_b": _pair_chunks(conv_b[l][None, :]), "w_down": w_down[l].astype(BF16)}
        h1p = _post_attn(o_p, hp, w_out, mod, l, False, nw[1])
        h1s = _post_attn(o_s, hs, w_out, mod, l, True, nw[1])
        hp = _ffn(h1p, wf, mod, l, False, nw[2], nw[3])
        hs = _ffn(h1s, wf, mod, l, True, nw[2], nw[3])

    def stack(states, k, shape):
        return jnp.stack([st[k].reshape((BATCH, SEQ) + shape) for st in states], axis=1)

    new_mla_ckv = stack(even_states, 0, (KV_LORA,))
    new_mla_kpe = stack(even_states, 1, (QK_ROPE,))
    new_diff_k = stack(even_states, 2, (H_B, 2 * DH_B))
    new_diff_v = stack(even_states, 3, (H_B, 2 * DH_B))
    new_gqa_k = stack(odd_states, 0, (KV_C, DH_C))
    new_gqa_v = stack(odd_states, 1, (KV_C, DH_C))
    new_na_k = stack(odd_states, 2, (H_D, DH_D))
    new_na_v = stack(odd_states, 3, (H_D, DH_D))
    return (hp.reshape(BATCH, SEQ, D_MODEL), hs.reshape(DEC_BATCH, DEC_SEQ, D_MODEL),
            new_mla_ckv, new_mla_kpe, new_diff_k, new_diff_v, new_gqa_k, new_gqa_v, new_na_k, new_na_v)
```

```python
import functools
import math

import numpy as np
import jax
import jax.numpy as jnp
from jax import lax
from jax.experimental import pallas as pl
from jax.experimental.pallas import tpu as pltpu

D_MODEL = 1024
BATCH = 32
SEQ = 256
DEPTH = 2
DEC_BATCH = 4
DEC_SEQ = 1024
PAST_LEN = 256
GRID_W = 64
HEAD_DIM = 64
H_A = 8
QK_NOPE = 64
QK_ROPE = 32
V_A = 64
Q_LORA = 256
KV_LORA = 128
H_B = 4
DH_B = HEAD_DIM
H_C = 8
KV_C = 2
DH_C = HEAD_DIM
H_D = 8
DH_D = HEAD_DIM
NA_WIN_ROWS = 8
NA_WIN_COLS = 16
D_FF = 2816
ROPE_THETA = 10000.0
EPS = 1e-6
NEG_INF = -1e30

LANES = 128
MOD_ROWS = 8
ROW_TILE = 512
FFN_ROW_TILE = 1024
FF_CHUNK = 256
Q_TILE = 256
LATENT_GROUP = 1
CONTEXT_GROUP = 4
VMEM_LIMIT = 48 * 1024 * 1024
FFN_VMEM_LIMIT = 56 * 1024 * 1024

F32 = jnp.float32
BF16 = jnp.bfloat16
LOG2E = math.log2(math.e)


def _cparams(*sem):
    return pltpu.CompilerParams(dimension_semantics=sem, vmem_limit_bytes=VMEM_LIMIT)


def _dot(a, b):
    return jnp.dot(a, b, preferred_element_type=F32)


def _dot_nt(a, b):
    return lax.dot_general(a, b, (((1,), (1,)), ((), ())), preferred_element_type=F32)


def _rms(x, w):
    return x * lax.rsqrt(jnp.mean(x * x, axis=-1, keepdims=True) + EPS) * w


def _group_rms(x, w, gmat):
    x2 = x * x
    hi = x2.astype(BF16)
    lo = (x2 - hi.astype(F32)).astype(BF16)
    ms = _dot(hi, gmat) + _dot(lo, gmat)
    return x * lax.rsqrt(ms + EPS) * w


def _rope(x, cos, sin_signed, half):
    outs = []
    for j in range(x.shape[1] // LANES):
        xc = x[:, j * LANES:(j + 1) * LANES]
        lane = lax.broadcasted_iota(jnp.int32, xc.shape, 1)
        first = (lane % (2 * half)) < half
        partner = jnp.where(first, pltpu.roll(xc, LANES - half, 1), pltpu.roll(xc, half, 1))
        outs.append(xc * cos + partner * sin_signed)
    return outs[0] if len(outs) == 1 else jnp.concatenate(outs, axis=1)


def _softmax_block_rows(n_keys):
    return max(16, min(64, (16 * 1280 // n_keys) // 16 * 16))


def _attention(maps, s_scr, p_scr, group, mxu_sum=False):
    slots = s_scr.shape[0]
    staged = {}

    def stage(i):
        s_ref = s_scr.at[i % slots]
        offs, off = [], 0
        for s in maps[i][0]():
            s_ref[:, off:off + s.shape[1]] = s
            offs.append(off)
            off += s.shape[1]
        staged[i] = (offs, off)

    def softmax(i):
        _, _, c, fix, _ = maps[i]
        n_keys = staged[i][1]
        s_ref, p_ref = s_scr.at[i % slots], p_scr.at[i % slots]
        rb = _softmax_block_rows(n_keys)
        sums = []
        for r0 in range(0, s_ref.shape[0], rb):
            s = s_ref[r0:r0 + rb, 0:n_keys]
            if fix is not None:
                s = fix(s, r0)
            m = jnp.max(s, axis=-1, keepdims=True)
            p = jnp.exp2((s - m) * c)
            if not mxu_sum:
                sums.append(jnp.sum(p, axis=-1, keepdims=True))
            p_ref[r0:r0 + rb, 0:n_keys] = p.astype(BF16)
        return None if mxu_sum else jnp.concatenate(sums, axis=0)

    def weighted_values(i, den):
        _, values, _, _, sink = maps[i]
        p_ref = p_scr.at[i % slots]
        offs, n_keys = staged.pop(i)
        acc = None
        for o, v in zip(offs, values()):
            part = _dot(p_ref[:, o:o + v.shape[0]], v)
            acc = part if acc is None else acc + part
        if mxu_sum:
            den = _dot(p_ref[:, 0:n_keys], jnp.ones((n_keys, LANES), BF16))
        sink(acc / den)

    groups = [range(g, min(g + group, len(maps))) for g in range(0, len(maps), group)]
    for i in groups[0]:
        stage(i)
    for gi, grp in enumerate(groups):
        if gi + 1 < len(groups):
            for i in groups[gi + 1]:
                stage(i)
        dens = [softmax(i) for i in grp]
        for i, den in zip(grp, dens):
            weighted_values(i, den)


def _attn_scratch(tq, n_keys, group):
    return [pltpu.VMEM((2 * group, tq, n_keys), F32), pltpu.VMEM((2 * group, tq, n_keys), BF16)]


def _upper_half(shape):
    return lax.broadcasted_iota(jnp.int32, shape, 1) >= HEAD_DIM


def _keep_half(x, half):
    upper = _upper_half(x.shape)
    return jnp.where(upper if half else ~upper, x, jnp.zeros_like(x))


def _swap_halves(x):
    return jnp.concatenate([x[:, HEAD_DIM:], x[:, :HEAD_DIM]], axis=1)


def _pair_store(o_ref, c0):
    got = {}

    def make(half):
        def sink(o):
            got[half] = o
            if len(got) == 2:
                o_ref[:, c0:c0 + LANES] = jnp.where(_upper_half(o.shape), got[1], got[0]).astype(BF16)
        return sink
    return make(0), make(1)


def _mod_kernel(c_ref, w_ref, b_ref, o_ref):
    cv = c_ref[...]
    act = cv / (1.0 + jnp.exp(-cv))
    o_ref[...] = _dot(act.astype(BF16), w_ref[...].astype(BF16)) + b_ref[...]


def _modulation(cvecs, w_mod, b_mod):
    tn = 1024
    n = 6 * D_MODEL
    return pl.pallas_call(
        _mod_kernel,
        grid=(DEPTH, n // tn),
        in_specs=[
            pl.BlockSpec((MOD_ROWS, D_MODEL), lambda l, j: (0, 0)),
            pl.BlockSpec((None, D_MODEL, tn), lambda l, j: (l, 0, j)),
            pl.BlockSpec((None, 1, tn), lambda l, j: (l, 0, j)),
        ],
        out_specs=pl.BlockSpec((None, MOD_ROWS, tn), lambda l, j: (l, 0, j)),
        out_shape=jax.ShapeDtypeStruct((DEPTH, MOD_ROWS, n), F32),
        compiler_params=_cparams("parallel", "parallel"),
        name="adaln_mod",
    )(cvecs, w_mod, b_mod.reshape(DEPTH, 1, n))


def _mod_spec(layer, sample, tm):
    if sample:
        per = DEC_SEQ // tm
        return pl.BlockSpec((None, None, 6, D_MODEL), lambda i, *_: (layer, 1 + i // per, 0, 0))
    return pl.BlockSpec((None, None, 6, D_MODEL), lambda i, *_: (layer, 0, 0, 0))


def _full(shape):
    nd = len(shape)
    return pl.BlockSpec(shape, lambda *_: (0,) * nd)


def _rows(tm, width):
    return pl.BlockSpec((tm, width), lambda i, *_: (i, 0))


def _rope_spec(tm):
    per = DEC_SEQ // tm
    return pl.BlockSpec((tm, LANES), lambda i, *_: (i % per, 0))


EVEN_IN = 2048
MLA_SLOT = 128


def _proj_even_kernel(sample, *refs):
    if sample:
        (h_ref, mod_ref, nw_ref, win_ref, qnw_ref, kvnw_ref, wuq_ref, wuk_ref, wuv_ref,
         ca_ref, sa_ref, ch_ref, sh_ref,
         qa_ref, ka_ref, va_ref, qd_ref, kd_ref, vd_ref) = refs
    else:
        (h_ref, mod_ref, nw_ref, win_ref, qnw_ref, kvnw_ref, wuq_ref, wuk_ref, wuv_ref,
         qa_ref, ka_ref, va_ref, qd_ref, kd_ref, vd_ref,
         ckv_st_ref, kpe_st_ref, kd_st_ref, vd_st_ref) = refs
    mod = mod_ref[...]
    u = _rms(h_ref[...], nw_ref[...]) * (1.0 + mod[1:2]) + mod[0:1]
    z = _dot(u.astype(BF16), win_ref[...])
    cq = _rms(z[:, 0:256], qnw_ref[...]).astype(BF16)
    qa = _dot(cq, wuq_ref[...])
    ckv = _rms(z[:, 256:384], kvnw_ref[...])
    ckv_b = ckv.astype(BF16)
    kn = _dot(ckv_b, wuk_ref[...])
    va_ref[...] = _dot(ckv_b, wuv_ref[...]).astype(BF16)
    kpe_slot = z[:, 384:512]
    qd = z[:, 512:1024]
    kd = z[:, 1024:1536]
    vd = z[:, 1536:2048]
    vd_ref[...] = vd.astype(BF16)
    if sample:
        ca, sa, ch, sh = ca_ref[...], sa_ref[...], ch_ref[...], sh_ref[...]
        qa = _rope(qa, ca, sa, QK_ROPE // 2)
        kpe_rot = _rope(kpe_slot, ca, sa, QK_ROPE // 2)
        qd_ref[...] = _rope(qd, ch, sh, DH_B // 2).astype(BF16)
        kd_ref[...] = _rope(kd, ch, sh, DH_B // 2).astype(BF16)
    else:
        kpe_rot = kpe_slot
        qd_ref[...] = qd.astype(BF16)
        kd_ref[...] = kd.astype(BF16)
        ckv_st_ref[...] = ckv
        kpe_st_ref[...] = kpe_slot[:, QK_NOPE:QK_NOPE + QK_ROPE]
        kd_st_ref[...] = kd
        vd_st_ref[...] = vd
    qa_ref[...] = qa.astype(BF16)
    for hd in range(H_A):
        sl = slice(hd * MLA_SLOT, (hd + 1) * MLA_SLOT)
        ka_ref[:, sl] = (kn[:, sl] + kpe_rot).astype(BF16)


def _proj_even(h, mod, layer, sample, nw, w, rope):
    n = h.shape[0]
    tm = ROW_TILE
    wide = H_A * MLA_SLOT
    ins = [h, mod, nw, w["w_in"], w["q_norm_w"], w["kv_norm_w"], w["w_uq"], w["w_uk"], w["w_uv"]]
    in_specs = [_rows(tm, D_MODEL), _mod_spec(layer, sample, tm), _full((1, D_MODEL)), _full((D_MODEL, EVEN_IN)),
                _full((1, Q_LORA)), _full((1, KV_LORA)), _full((Q_LORA, wide)), _full((KV_LORA, wide)),
                _full((KV_LORA, 512))]
    widths = [wide, wide, 512, 512, 512, 512]
    out_shape = [jax.ShapeDtypeStruct((n, wd), BF16) for wd in widths]
    out_specs = [_rows(tm, wd) for wd in widths]
    if sample:
        ins += [rope["ca"], rope["sa"], rope["ch"], rope["sh"]]
        in_specs += [_rope_spec(tm)] * 4
    else:
        for wd in (KV_LORA, QK_ROPE, 512, 512):
            out_shape.append(jax.ShapeDtypeStruct((n, wd), F32))
            out_specs.append(_rows(tm, wd))
    return pl.pallas_call(
        functools.partial(_proj_even_kernel, sample),
        grid=(n // tm,),
        in_specs=in_specs,
        out_specs=out_specs,
        out_shape=out_shape,
        compiler_params=_cparams("parallel"),
        name="proj_even_s" if sample else "proj_even_p",
    )(*ins)


def _attn_even_kernel(sample, lam_init, *refs):
    if sample:
        (qa_ref, ka_ref, va_ref, qd_ref, kd_ref, vd_ref,
         cckv_ref, ckpe_ref, cdk_ref, cdv_ref, wuk_ref, wuv_ref, lam_ref, sub_ref, o_ref, s_scr, p_scr) = refs
    else:
        (qa_ref, ka_ref, va_ref, qd_ref, kd_ref, vd_ref, lam_ref, sub_ref, o_ref, s_scr, p_scr) = refs
    lf = lam_ref[...]
    lam = (jnp.exp(jnp.sum(lf[0:1] * lf[1:2], axis=-1, keepdims=True))
           - jnp.exp(jnp.sum(lf[2:3] * lf[3:4], axis=-1, keepdims=True)) + lam_init)
    if sample:
        cckv = cckv_ref[...].astype(BF16)
        kn_ctx = _dot(cckv, wuk_ref[...])
        va_ctx = _dot(cckv, wuv_ref[...]).astype(BF16)
        n_ctx = cckv.shape[0]
        kpe_ctx = jnp.concatenate([jnp.zeros((n_ctx, QK_NOPE), F32), ckpe_ref[...],
                                   jnp.zeros((n_ctx, MLA_SLOT - QK_NOPE - QK_ROPE), F32)], axis=1)
        kd_ctx = cdk_ref[...].astype(BF16)
        vd_ctx = cdv_ref[...].astype(BF16)
    maps = []
    c_a = (QK_NOPE + QK_ROPE) ** -0.5 * LOG2E
    for hd in range(H_A):
        sl = slice(hd * MLA_SLOT, (hd + 1) * MLA_SLOT)
        if hd % 2 == 0:
            sinks = _pair_store(o_ref, hd * V_A)

        def scores(sl=sl):
            q = qa_ref[:, sl]
            out = [_dot_nt(q, ka_ref[:, sl])]
            if sample:
                out.append(_dot_nt(q, (kn_ctx[:, sl] + kpe_ctx).astype(BF16)))
            return out

        def values(vs=slice(hd // 2 * LANES, (hd // 2 + 1) * LANES)):
            return [va_ref[:, vs]] + ([va_ctx[:, vs]] if sample else [])

        maps.append((scores, values, c_a, None, sinks[hd % 2]))
    c_b = DH_B ** -0.5 * LOG2E
    base = H_A * V_A
    sub_w = sub_ref[...]
    for hd in range(H_B):
        hs = slice(hd * 2 * DH_B, (hd + 1) * 2 * DH_B)
        outs = []

        def sink(o, outs=outs, hs=hs):
            outs.append(o)
            if len(outs) == 2:
                ob = _rms(outs[0] - lam * outs[1], sub_w) * (1.0 - lam_init)
                o_ref[:, base + hs.start:base + hs.stop] = ob.astype(BF16)

        def values(hs=hs):
            return [vd_ref[:, hs]] + ([vd_ctx[:, hs]] if sample else [])

        for comp in range(2):
            def scores(hs=hs, comp=comp):
                q = _keep_half(qd_ref[:, hs], comp)
                out = [_dot_nt(q, kd_ref[:, hs])]
                if sample:
                    out.append(_dot_nt(q, kd_ctx[:, hs]))
                return out

            maps.append((scores, values, c_b, None, sink))
    _attention(maps, s_scr, p_scr, s_scr.shape[0] // 2, mxu_sum=not sample)


def _attn_even(p, sample, lam_init, caches, w):
    qa, ka, va, qd, kd, vd = p
    n = qa.shape[0]
    wide = H_A * MLA_SLOT
    if sample:
        tq, per = Q_TILE, DEC_SEQ // Q_TILE
        grid = (DEC_BATCH, per)
        qspec = lambda wd: pl.BlockSpec((tq, wd), lambda b, i: (b * per + i, 0))
        kspec = lambda wd: pl.BlockSpec((DEC_SEQ, wd), lambda b, i: (b, 0))
        cspec = lambda wd: pl.BlockSpec((None, PAST_LEN, wd), lambda b, i: (b, 0, 0))
        ins = [qa, ka, va, qd, kd, vd, *caches, w["w_uk"], w["w_uv"], w["diff_lam"], w["diff_subln_w"]]
        in_specs = [qspec(wide), kspec(wide), kspec(512), qspec(512), kspec(512), kspec(512),
                    cspec(KV_LORA), cspec(QK_ROPE), cspec(512), cspec(512),
                    _full((KV_LORA, wide)), _full((KV_LORA, 512)), _full((4, DH_B)), _full((1, 2 * DH_B))]
        out_spec = qspec(D_MODEL)
        sem = ("parallel", "parallel")
        n_keys, group = DEC_SEQ + PAST_LEN, LATENT_GROUP
    else:
        tq = SEQ
        grid = (BATCH,)
        spec = lambda wd: pl.BlockSpec((SEQ, wd), lambda b: (b, 0))
        ins = [qa, ka, va, qd, kd, vd, w["diff_lam"], w["diff_subln_w"]]
        in_specs = [spec(wide), spec(wide), spec(512), spec(512), spec(512), spec(512),
                    _full((4, DH_B)), _full((1, 2 * DH_B))]
        out_spec = spec(D_MODEL)
        sem = ("parallel",)
        n_keys, group = SEQ, CONTEXT_GROUP
    return pl.pallas_call(
        functools.partial(_attn_even_kernel, sample, lam_init),
        grid=grid,
        in_specs=in_specs,
        out_specs=out_spec,
        out_shape=jax.ShapeDtypeStruct((n, D_MODEL), BF16),
        scratch_shapes=_attn_scratch(tq, n_keys, group),
        compiler_params=_cparams(*sem),
        name="attn_even_s" if sample else "attn_even_p",
    )(*ins)


ODD_IN = 2304


def _proj_odd_kernel(sample, *refs):
    if sample:
        (h_ref, mod_ref, nw_ref, win_ref, qw_ref, kw_ref, gq_ref, gk_ref, ch_ref, sh_ref,
         qc_ref, kc_ref, vc_ref, qn_ref, kn_ref, vn_ref) = refs
    else:
        (h_ref, mod_ref, nw_ref, win_ref, qw_ref, kw_ref, gq_ref, gk_ref,
         qc_ref, kc_ref, vc_ref, qn_ref, kn_ref, vn_ref,
         kc_st_ref, vc_st_ref, kn_st_ref, vn_st_ref) = refs
    mod = mod_ref[...]
    u = _rms(h_ref[...], nw_ref[...]) * (1.0 + mod[1:2]) + mod[0:1]
    z = _dot(u.astype(BF16), win_ref[...])
    qc = _group_rms(z[:, 0:512], qw_ref[...], gq_ref[...])
    kc = _group_rms(z[:, 512:640], kw_ref[...], gk_ref[...])
    vc = z[:, 640:768]
    kn = z[:, 1280:1792]
    vn = z[:, 1792:2304]
    vc_ref[...] = vc.astype(BF16)
    qn_ref[...] = z[:, 768:1280].astype(BF16)
    kn_ref[...] = kn.astype(BF16)
    vn_ref[...] = vn.astype(BF16)
    if sample:
        ch, sh = ch_ref[...], sh_ref[...]
        qc_ref[...] = _rope(qc, ch, sh, DH_C // 2).astype(BF16)
        kc_ref[...] = _rope(kc, ch, sh, DH_C // 2).astype(BF16)
    else:
        qc_ref[...] = qc.astype(BF16)
        kc_ref[...] = kc.astype(BF16)
        kc_st_ref[...] = kc
        vc_st_ref[...] = vc
        kn_st_ref[...] = kn
        vn_st_ref[...] = vn


def _proj_odd(h, mod, layer, sample, nw, w, rope):
    n = h.shape[0]
    tm = ROW_TILE
    ins = [h, mod, nw, w["w_in"], w["q_w"], w["k_w"], w["gq"], w["gk"]]
    in_specs = [_rows(tm, D_MODEL), _mod_spec(layer, sample, tm), _full((1, D_MODEL)), _full((D_MODEL, ODD_IN)),
                _full((1, 512)), _full((1, 128)), _full((512, 512)), _full((128, 128))]
    widths = [512, 128, 128, 512, 512, 512]
    out_shape = [jax.ShapeDtypeStruct((n, wd), BF16) for wd in widths]
    out_specs = [_rows(tm, wd) for wd in widths]
    if sample:
        ins += [rope["ch"], rope["sh"]]
        in_specs += [_rope_spec(tm)] * 2
    else:
        for wd in (128, 128, 512, 512):
            out_shape.append(jax.ShapeDtypeStruct((n, wd), F32))
            out_specs.append(_rows(tm, wd))
    return pl.pallas_call(
        functools.partial(_proj_odd_kernel, sample),
        grid=(n // tm,),
        in_specs=in_specs,
        out_specs=out_specs,
        out_shape=out_shape,
        compiler_params=_cparams("parallel"),
        name="proj_odd_s" if sample else "proj_odd_p",
    )(*ins)


def _attn_odd_prompt_kernel(qc_ref, kc_ref, vc_ref, qn_ref, kn_ref, vn_ref, o_ref, s_scr, p_scr):
    c = DH_C ** -0.5 * LOG2E
    group = H_C // KV_C
    kc = (kc_ref[...], _swap_halves(kc_ref[...]))
    vc = (vc_ref[...], _swap_halves(vc_ref[...]))
    maps = []
    for hd in range(H_C):
        half, swap = hd % 2, (hd // group) != (hd % 2)
        ps = slice(hd // 2 * LANES, (hd // 2 + 1) * LANES)
        if half == 0:
            sinks = _pair_store(o_ref, ps.start)
        maps.append((lambda ps=ps, half=half, swap=swap: [_dot_nt(_keep_half(qc_ref[:, ps], half), kc[swap])],
                     lambda swap=swap: [vc[swap]], c, None, sinks[half]))
    base = H_C * DH_C
    c = DH_D ** -0.5 * LOG2E
    for hd in range(H_D):
        half = hd % 2
        ps = slice(hd // 2 * LANES, (hd // 2 + 1) * LANES)
        if half == 0:
            sinks = _pair_store(o_ref, base + ps.start)
        maps.append((lambda ps=ps, half=half: [_dot_nt(_keep_half(qn_ref[:, ps], half), kn_ref[:, ps])],
                     lambda ps=ps: [vn_ref[:, ps]], c, None, sinks[half]))
    _attention(maps, s_scr, p_scr, s_scr.shape[0] // 2, mxu_sum=True)


def _attn_odd_prompt(p):
    qc, kc, vc, qn, kn, vn = p
    spec = lambda wd: pl.BlockSpec((SEQ, wd), lambda b: (b, 0))
    return pl.pallas_call(
        _attn_odd_prompt_kernel,
        grid=(BATCH,),
        in_specs=[spec(512), spec(128), spec(128), spec(512), spec(512), spec(512)],
        out_specs=spec(D_MODEL),
        out_shape=jax.ShapeDtypeStruct((qc.shape[0], D_MODEL), BF16),
        scratch_shapes=_attn_scratch(SEQ, SEQ, CONTEXT_GROUP),
        compiler_params=_cparams("parallel"),
        name="attn_odd_p",
    )(qc, kc, vc, qn, kn, vn)


def _gqa_sample_kernel(q_ref, k_ref, v_ref, ck_ref, cv_ref, o_ref, s_scr, p_scr):
    c = DH_C ** -0.5 * LOG2E
    group = H_C // KV_C
    k_loc = (k_ref[...], _swap_halves(k_ref[...]))
    v_loc = (v_ref[...], _swap_halves(v_ref[...]))
    k_ctx = ck_ref[...].astype(BF16)
    v_ctx = cv_ref[...].astype(BF16)
    k_ctx = (k_ctx, _swap_halves(k_ctx))
    v_ctx = (v_ctx, _swap_halves(v_ctx))
    maps = []
    for hd in range(H_C):
        half, swap = hd % 2, (hd // group) != (hd % 2)
        ps = slice(hd // 2 * LANES, (hd // 2 + 1) * LANES)
        if half == 0:
            sinks = _pair_store(o_ref, ps.start)

        def scores(ps=ps, half=half, swap=swap):
            q = _keep_half(q_ref[:, ps], half)
            return [_dot_nt(q, k_loc[swap]), _dot_nt(q, k_ctx[swap])]

        maps.append((scores, lambda swap=swap: [v_loc[swap], v_ctx[swap]], c, None, sinks[half]))
    _attention(maps, s_scr, p_scr, s_scr.shape[0] // 2)


def _gqa_sample(qc, kc, vc, cache_k, cache_v):
    tq, per = Q_TILE, DEC_SEQ // Q_TILE
    return pl.pallas_call(
        _gqa_sample_kernel,
        grid=(DEC_BATCH, per),
        in_specs=[pl.BlockSpec((tq, 512), lambda b, i: (b * per + i, 0)),
                  pl.BlockSpec((DEC_SEQ, 128), lambda b, i: (b, 0)),
                  pl.BlockSpec((DEC_SEQ, 128), lambda b, i: (b, 0)),
                  pl.BlockSpec((None, PAST_LEN, 128), lambda b, i: (b, 0, 0)),
                  pl.BlockSpec((None, PAST_LEN, 128), lambda b, i: (b, 0, 0))],
        out_specs=pl.BlockSpec((tq, 512), lambda b, i: (b * per + i, 0)),
        out_shape=jax.ShapeDtypeStruct((qc.shape[0], 512), BF16),
        scratch_shapes=_attn_scratch(tq, DEC_SEQ + PAST_LEN, LATENT_GROUP),
        compiler_params=_cparams("parallel", "parallel"),
        name="gqa_s",
    )(qc, kc, vc, cache_k, cache_v)


NA_ROWS = DEC_SEQ // GRID_W
NA_KR = min(NA_WIN_ROWS, NA_ROWS)
NA_LOC = NA_KR * GRID_W


def _na_sample_kernel(q_ref, k_ref, v_ref, ck_ref, cv_ref, bias_ref, o_ref, s_scr, p_scr):
    r = pl.program_id(1)
    rs = jnp.clip(r - NA_KR // 2, 0, NA_ROWS - NA_KR)
    start = pl.multiple_of(rs * GRID_W, GRID_W)
    scale = DH_D ** -0.5
    k_loc = k_ref[pl.ds(start, NA_LOC), :]
    v_loc = v_ref[pl.ds(start, NA_LOC), :]
    k_ctx = ck_ref[...].astype(BF16)
    v_ctx = cv_ref[...].astype(BF16)
    rb = _softmax_block_rows(NA_LOC + PAST_LEN)
    col_ok = {}
    for r0 in range(0, GRID_W, rb):
        wq = lax.broadcasted_iota(jnp.int32, (rb, NA_LOC), 0) + r0
        wk = lax.broadcasted_iota(jnp.int32, (rb, NA_LOC), 1) % GRID_W
        cs = jnp.clip(wq - NA_WIN_COLS // 2, 0, GRID_W - NA_WIN_COLS)
        col_ok[r0] = (wk >= cs) & (wk < cs + NA_WIN_COLS)
    maps = []
    for hd in range(H_D):
        half = hd % 2
        ps = slice(hd // 2 * LANES, (hd // 2 + 1) * LANES)
        if half == 0:
            sinks = _pair_store(o_ref, ps.start)

        def fix(s, r0, hd=hd):
            loc = s[:, :NA_LOC] * scale + bias_ref[hd, r0:r0 + rb, :]
            return jnp.concatenate([jnp.where(col_ok[r0], loc, NEG_INF), s[:, NA_LOC:] * scale], axis=1)

        def scores(ps=ps, half=half):
            q = _keep_half(q_ref[:, ps], half)
            return [_dot_nt(q, k_loc[:, ps]), _dot_nt(q, k_ctx[:, ps])]

        maps.append((scores, lambda ps=ps: [v_loc[:, ps], v_ctx[:, ps]], LOG2E, fix, sinks[half]))
    _attention(maps, s_scr, p_scr, s_scr.shape[0] // 2)


def _na_sample(qn, kn, vn, cache_k, cache_v, bias):
    def bias_map(b, r):
        rs = jnp.clip(r - NA_KR // 2, 0, NA_ROWS - NA_KR)
        return (0, rs - r + NA_WIN_ROWS - 1, 0, 0)
    return pl.pallas_call(
        _na_sample_kernel,
        grid=(DEC_BATCH, NA_ROWS),
        in_specs=[pl.BlockSpec((GRID_W, 512), lambda b, r: (b * NA_ROWS + r, 0)),
                  pl.BlockSpec((DEC_SEQ, 512), lambda b, r: (b, 0)),
                  pl.BlockSpec((DEC_SEQ, 512), lambda b, r: (b, 0)),
                  pl.BlockSpec((None, PAST_LEN, 512), lambda b, r: (b, 0, 0)),
                  pl.BlockSpec((None, PAST_LEN, 512), lambda b, r: (b, 0, 0)),
                  pl.BlockSpec((H_D, None, GRID_W, NA_LOC), bias_map)],
        out_specs=pl.BlockSpec((GRID_W, 512), lambda b, r: (b * NA_ROWS + r, 0)),
        out_shape=jax.ShapeDtypeStruct((qn.shape[0], 512), BF16),
        scratch_shapes=_attn_scratch(GRID_W, NA_LOC + PAST_LEN, CONTEXT_GROUP),
        compiler_params=_cparams("parallel", "parallel"),
        name="na_s",
    )(qn, kn, vn, cache_k, cache_v, bias)


def _na_bias_table(rpb):
    edge = GRID_W - NA_WIN_COLS
    ext = jnp.pad(rpb.astype(F32), ((0, 0), (0, 0), (edge, edge)), mode="edge")
    toep = jnp.stack([ext[:, :, GRID_W - 1 - wq:2 * GRID_W - 1 - wq] for wq in range(GRID_W)], axis=1)
    flat = toep.reshape(H_D, GRID_W, (2 * NA_WIN_ROWS - 1) * GRID_W)
    return jnp.stack([flat[:, :, d0 * GRID_W:d0 * GRID_W + NA_LOC] for d0 in range(NA_WIN_ROWS)], axis=1)


def _post_attn_kernel(n_parts, *refs):
    o_refs = refs[:n_parts]
    h_ref, wout_ref, mod_ref, nw1_ref, h1_ref = refs[n_parts:]
    y = None
    off = 0
    for o_ref in o_refs:
        wd = o_ref.shape[1]
        part = _dot(o_ref[...], wout_ref[off:off + wd, :])
        y = part if y is None else y + part
        off += wd
    mod = mod_ref[...]
    h1_ref[...] = h_ref[...] + mod[2:3] * _rms(y, nw1_ref[...])


def _post_attn(o_parts, h, w_out, mod, layer, sample, nw1):
    n = h.shape[0]
    tm = ROW_TILE
    in_specs = [_rows(tm, o.shape[1]) for o in o_parts]
    in_specs += [_rows(tm, D_MODEL), _full((D_MODEL, D_MODEL)), _mod_spec(layer, sample, tm), _full((1, D_MODEL))]
    return pl.pallas_call(
        functools.partial(_post_attn_kernel, len(o_parts)),
        grid=(n // tm,),
        in_specs=in_specs,
        out_specs=_rows(tm, D_MODEL),
        out_shape=jax.ShapeDtypeStruct((n, D_MODEL), F32),
        compiler_params=_cparams("parallel"),
        name="post_attn_s" if sample else "post_attn_p",
    )(*o_parts, h, w_out, mod, nw1)


FF_PAIR = 2 * FF_CHUNK
N_FF = D_FF // FF_CHUNK
SUBLANES = 8
ACT_TILES = 4
ROW_BLOCK = 512
NORM_ROWS = 32


def _ffn_kernel(seq_len, h_ref, wup_ref, cw_ref, cb_ref, wd_ref, mod_ref, nw2_ref, nw3_ref, o_ref,
                u_ref, z0_ref, z1_ref, a_ref, a_last_ref):
    tm = h_ref.shape[0]
    n_blocks = tm // ROW_BLOCK
    rows = ACT_TILES * SUBLANES
    mod = mod_ref[...]
    sub = lax.broadcasted_iota(jnp.int32, (SUBLANES, LANES), 0)
    zero_rows = jnp.zeros((SUBLANES, FF_PAIR), F32)
    for z_ref in (z0_ref, z1_ref):
        z_ref[0:SUBLANES, :] = zero_rows
        z_ref[SUBLANES + tm:2 * SUBLANES + tm, :] = zero_rows

    nw2 = nw2_ref[...]

    def pre_norm(blk):
        for c in range(ROW_BLOCK // NORM_ROWS):
            rs = slice(blk * ROW_BLOCK + c * NORM_ROWS, blk * ROW_BLOCK + (c + 1) * NORM_ROWS)
            u = _rms(h_ref[rs, :], nw2) * (1.0 + mod[4:5]) + mod[3:4]
            u_ref[rs, :] = u.astype(BF16)

    def up(j, z_ref, blk):
        r0 = blk * ROW_BLOCK
        u = u_ref[r0:r0 + ROW_BLOCK, :]
        rows_ = slice(SUBLANES + r0, SUBLANES + r0 + ROW_BLOCK)
        for half in range(2):
            c0 = half * D_FF + j * FF_CHUNK
            if not isinstance(c0, int):
                c0 = pl.multiple_of(c0, FF_CHUNK)
            z_ref[rows_, half * FF_CHUNK:(half + 1) * FF_CHUNK] = _dot(u, wup_ref[:, pl.ds(c0, FF_CHUNK)])

    def act(j, z_ref, col, blk, dst_ref=a_ref):
        cw = cw_ref[j]
        cb = cb_ref[j]
        for lc in range(FF_CHUNK // LANES):
            taps = []
            for lane0 in (lc * LANES, FF_CHUNK + lc * LANES):
                lanes = slice(lane0, lane0 + LANES)
                taps.append([jnp.broadcast_to(cw[k:k + 1, lanes], (SUBLANES, LANES)) for k in range(3)]
                            + [jnp.broadcast_to(cb[:, lanes], (SUBLANES, LANES))])
            for c in range(ROW_BLOCK // rows):
                r = blk * ROW_BLOCK + c * rows
                first = r % seq_len == 0
                last = (r + rows) % seq_len == 0

                def conv(lane0, tap):
                    ext = z_ref[r:r + rows + 2 * SUBLANES, lane0:lane0 + LANES]
                    tiles = [ext[t * SUBLANES:(t + 1) * SUBLANES] for t in range(ACT_TILES + 2)]
                    down = [pltpu.roll(t, 1, 0) for t in tiles[:-1]]
                    up_ = [pltpu.roll(t, SUBLANES - 1, 0) for t in tiles[1:]]
                    out = []
                    for t in range(ACT_TILES):
                        above = 0.0 if (first and t == 0) else down[t]
                        below = 0.0 if (last and t == ACT_TILES - 1) else up_[t + 1]
                        prev = jnp.where(sub == 0, above, down[t + 1])
                        nxt = jnp.where(sub == SUBLANES - 1, below, up_[t])
                        out.append(prev * tap[0] + tiles[t + 1] * tap[1] + nxt * tap[2] + tap[3])
                    return jnp.concatenate(out, axis=0)

                g = conv(lc * LANES, taps[0])
                v = conv(FF_CHUNK + lc * LANES, taps[1])
                a = (g / (1.0 + jnp.exp2(g * -LOG2E))) * v
                lane = col + lc * LANES
                if not isinstance(lane, int):
                    lane = pl.multiple_of(lane, LANES)
                dst_ref[r:r + rows, pl.ds(lane, LANES)] = a.astype(BF16)

    for blk in range(n_blocks):
        pre_norm(blk)
        up(0, z0_ref, blk)

    def pair(i, carry):
        j = 2 * i
        col = pl.multiple_of(j * FF_CHUNK, FF_CHUNK)
        for blk in range(n_blocks):
            up(j + 1, z1_ref, blk)
            act(j, z0_ref, col, blk)
        for blk in range(n_blocks):
            up(j + 2, z0_ref, blk)
            act(j + 1, z1_ref, col + FF_CHUNK, blk)
        return carry

    lax.fori_loop(0, (N_FF - 1) // 2, pair, 0)
    nw3 = nw3_ref[...]
    k_main = (N_FF - 1) * FF_CHUNK
    for blk in range(n_blocks):
        r0 = blk * ROW_BLOCK
        y = _dot(a_ref[r0:r0 + ROW_BLOCK, 0:k_main], wd_ref[0:k_main, :])
        act(N_FF - 1, z0_ref, 0, blk, a_last_ref)
        y = y + _dot(a_last_ref[r0:r0 + ROW_BLOCK, :], wd_ref[k_main:D_FF, :])
        for c in range(ROW_BLOCK // NORM_ROWS):
            rs = slice(r0 + c * NORM_ROWS, r0 + (c + 1) * NORM_ROWS)
            o_ref[rs, :] = h_ref[rs, :] + mod[5:6] * _rms(y[c * NORM_ROWS:(c + 1) * NORM_ROWS], nw3)


def _ffn(h1, w, mod, layer, sample, nw2, nw3):
    n = h1.shape[0]
    tm = FFN_ROW_TILE
    seq_len = DEC_SEQ if sample else SEQ
    once = pl.Buffered(1)
    in_specs = [
        _rows(tm, D_MODEL),
        pl.BlockSpec((None, D_MODEL, 2 * D_FF), lambda i: (layer, 0, 0), pipeline_mode=once),
        pl.BlockSpec((N_FF, 3, FF_PAIR), lambda i: (0, 0, 0), pipeline_mode=once),
        pl.BlockSpec((N_FF, 1, FF_PAIR), lambda i: (0, 0, 0), pipeline_mode=once),
        pl.BlockSpec((None, D_FF, D_MODEL), lambda i: (layer, 0, 0), pipeline_mode=once),
        _mod_spec(layer, sample, tm),
        _full((1, D_MODEL)),
        _full((1, D_MODEL)),
    ]
    return pl.pallas_call(
        functools.partial(_ffn_kernel, seq_len),
        grid=(n // tm,),
        in_specs=in_specs,
        out_specs=_rows(tm, D_MODEL),
        out_shape=jax.ShapeDtypeStruct((n, D_MODEL), F32),
        scratch_shapes=[pltpu.VMEM((tm, D_MODEL), BF16), pltpu.VMEM((tm + 2 * SUBLANES, FF_PAIR), F32),
                        pltpu.VMEM((tm + 2 * SUBLANES, FF_PAIR), F32), pltpu.VMEM((tm, D_FF - FF_CHUNK), BF16),
                        pltpu.VMEM((tm, FF_CHUNK), BF16)],
        compiler_params=pltpu.CompilerParams(dimension_semantics=("parallel",), vmem_limit_bytes=FFN_VMEM_LIMIT),
        name="ffn_s" if sample else "ffn_p",
    )(h1, w["w_up"], w["conv_w"], w["conv_b"], w["w_down"], mod, nw2, nw3)


def _pair_chunks(x):
    lead = x.shape[:-1]
    x = x.reshape(lead + (2, N_FF, FF_CHUNK))
    x = jnp.moveaxis(x, -2, 0)
    return x.reshape((N_FF,) + lead + (FF_PAIR,))


def _rope_tables():
    def table(rot_dim):
        t = np.arange(DEC_SEQ)
        n_freq = rot_dim // 4
        inv = 1.0 / (ROPE_THETA ** (np.arange(n_freq) / n_freq))
        ang = np.concatenate([(t // GRID_W)[:, None] * inv[None, :], (t % GRID_W)[:, None] * inv[None, :]], axis=-1)
        cos = np.cos(ang).astype(np.float32)
        sin = np.sin(ang).astype(np.float32)
        reps = LANES // rot_dim
        return (np.tile(np.concatenate([cos, cos], axis=-1), (1, reps)),
                np.tile(np.concatenate([-sin, sin], axis=-1), (1, reps)))
    ca, sa = table(QK_ROPE)
    ch, sh = table(HEAD_DIM)
    rope_lanes = (np.arange(MLA_SLOT) >= QK_NOPE) & (np.arange(MLA_SLOT) < QK_NOPE + QK_ROPE)
    ca = np.where(rope_lanes[None, :], ca, 1.0).astype(np.float32)
    sa = np.where(rope_lanes[None, :], sa, 0.0).astype(np.float32)
    return {"ca": jnp.asarray(ca), "sa": jnp.asarray(sa), "ch": jnp.asarray(ch), "sh": jnp.asarray(sh)}


def _group_mean_matrix(width):
    idx = np.arange(width) // HEAD_DIM
    return jnp.asarray((idx[:, None] == idx[None, :]).astype(np.float32) / HEAD_DIM, BF16)


def kernel(x_prompt, x_sample, c, cache_mla_ckv, cache_mla_kpe, cache_diff_k, cache_diff_v, cache_gqa_k, cache_gqa_v, cache_na_k, cache_na_v, c_ctx, norm_w, w_mod, b_mod, w_in_even, w_out_even, w_uq, q_norm_w, kv_norm_w, w_uk, w_uv, diff_lam, diff_subln_w, w_in_odd, w_out_odd, qk_norm_w, na_rpb, w_up, conv_w, conv_b, w_down):
    rope = _rope_tables()
    n_p = BATCH * SEQ
    n_s = DEC_BATCH * DEC_SEQ
    cvecs = jnp.concatenate([c_ctx[None, :], c, jnp.zeros((MOD_ROWS - 1 - DEC_BATCH, D_MODEL), F32)], axis=0)
    mod = _modulation(cvecs, w_mod, b_mod).reshape(DEPTH, MOD_ROWS, 6, D_MODEL)
    hp = x_prompt.reshape(n_p, D_MODEL)
    hs = x_sample.reshape(n_s, D_MODEL)
    even_states, odd_states = [], []
    w_up_b = w_up.astype(BF16)
    w_down_b = w_down.astype(BF16)
    for l in range(DEPTH):
        i = l // 2
        nw = [norm_w[l, k][None, :] for k in range(4)]
        if l % 2 == 0:
            lam_init = 0.8 - 0.6 * math.exp(-0.3 * l)
            wi = w_in_even[i]
            w_uq3 = w_uq[i].reshape(Q_LORA, H_A, QK_NOPE + QK_ROPE)
            w = {
                "w_in": jnp.concatenate([wi[:, :384], jnp.zeros((D_MODEL, QK_NOPE), F32), wi[:, 384:416],
                                         jnp.zeros((D_MODEL, MLA_SLOT - QK_NOPE - QK_ROPE), F32), wi[:, 416:]],
                                        axis=1).astype(BF16),
                "q_norm_w": q_norm_w[i][None, :],
                "kv_norm_w": kv_norm_w[i][None, :],
                "w_uq": jnp.pad(w_uq3, ((0, 0), (0, 0), (0, MLA_SLOT - QK_NOPE - QK_ROPE))
                                ).reshape(Q_LORA, H_A * MLA_SLOT).astype(BF16),
                "w_uk": jnp.pad(w_uk[i].reshape(KV_LORA, H_A, QK_NOPE), ((0, 0), (0, 0), (0, MLA_SLOT - QK_NOPE))
                                ).reshape(KV_LORA, H_A * MLA_SLOT).astype(BF16),
                "w_uv": w_uv[i].astype(BF16),
                "diff_lam": diff_lam[i],
                "diff_subln_w": diff_subln_w[i][None, :],
            }
            outs_p = _proj_even(hp, mod, l, False, nw[0], w, rope)
            outs_s = _proj_even(hs, mod, l, True, nw[0], w, rope)
            even_states.append(outs_p[6:])
            o_p = [_attn_even(outs_p[:6], False, lam_init, None, w)]
            caches = (cache_mla_ckv[:, i], cache_mla_kpe[:, i],
                      cache_diff_k[:, i].reshape(DEC_BATCH, PAST_LEN, 512),
                      cache_diff_v[:, i].reshape(DEC_BATCH, PAST_LEN, 512))
            o_s = [_attn_even(outs_s, True, lam_init, caches, w)]
            w_out = w_out_even[i].astype(BF16)
        else:
            q_w = jnp.tile(qk_norm_w[i, 0], H_C)[None, :]
            k_w = jnp.tile(qk_norm_w[i, 1], KV_C)[None, :]
            w = {"w_in": w_in_odd[i].astype(BF16), "q_w": q_w, "k_w": k_w,
                 "gq": _group_mean_matrix(512), "gk": _group_mean_matrix(128)}
            outs_p = _proj_odd(hp, mod, l, False, nw[0], w, rope)
            outs_s = _proj_odd(hs, mod, l, True, nw[0], w, rope)
            odd_states.append(outs_p[6:])
            o_p = [_attn_odd_prompt(outs_p[:6])]
            qc, kc, vc, qn, kn, vn = outs_s
            o_c = _gqa_sample(qc, kc, vc, cache_gqa_k[:, i].reshape(DEC_BATCH, PAST_LEN, 128),
                              cache_gqa_v[:, i].reshape(DEC_BATCH, PAST_LEN, 128))
            o_d = _na_sample(qn, kn, vn, cache_na_k[:, i].reshape(DEC_BATCH, PAST_LEN, 512),
                             cache_na_v[:, i].reshape(DEC_BATCH, PAST_LEN, 512), _na_bias_table(na_rpb[i]))
            o_s = [o_c, o_d]
            w_out = w_out_odd[i].astype(BF16)
        wf = {"w_up": w_up_b, "conv_w": _pair_chunks(conv_w[l]),
              "conv_b": _pair_chunks(conv_b[l][None, :]), "w_down": w_down_b}
        h1p = _post_attn(o_p, hp, w_out, mod, l, False, nw[1])
        h1s = _post_attn(o_s, hs, w_out, mod, l, True, nw[1])
        hp = _ffn(h1p, wf, mod, l, False, nw[2], nw[3])
        hs = _ffn(h1s, wf, mod, l, True, nw[2], nw[3])

    def stack(states, k, shape):
        return jnp.stack([st[k].reshape((BATCH, SEQ) + shape) for st in states], axis=1)

    new_mla_ckv = stack(even_states, 0, (KV_LORA,))
    new_mla_kpe = stack(even_states, 1, (QK_ROPE,))
    new_diff_k = stack(even_states, 2, (H_B, 2 * DH_B))
    new_diff_v = stack(even_states, 3, (H_B, 2 * DH_B))
    new_gqa_k = stack(odd_states, 0, (KV_C, DH_C))
    new_gqa_v = stack(odd_states, 1, (KV_C, DH_C))
    new_na_k = stack(odd_states, 2, (H_D, DH_D))
    new_na_v = stack(odd_states, 3, (H_D, DH_D))
    return (hp.reshape(BATCH, SEQ, D_MODEL), hs.reshape(DEC_BATCH, DEC_SEQ, D_MODEL),
            new_mla_ckv, new_mla_kpe, new_diff_k, new_diff_v, new_gqa_k, new_gqa_v, new_na_k, new_na_v)
```

```python
import functools
import math

import numpy as np
import jax
import jax.numpy as jnp
from jax import lax
from jax.experimental import pallas as pl
from jax.experimental.pallas import tpu as pltpu

D_MODEL = 1024
BATCH = 32
SEQ = 256
DEPTH = 2
DEC_BATCH = 4
DEC_SEQ = 1024
PAST_LEN = 256
GRID_W = 64
HEAD_DIM = 64
H_A = 8
QK_NOPE = 64
QK_ROPE = 32
V_A = 64
Q_LORA = 256
KV_LORA = 128
H_B = 4
DH_B = HEAD_DIM
H_C = 8
KV_C = 2
DH_C = HEAD_DIM
H_D = 8
DH_D = HEAD_DIM
NA_WIN_ROWS = 8
NA_WIN_COLS = 16
D_FF = 2816
ROPE_THETA = 10000.0
EPS = 1e-6
NEG_INF = -1e30

LANES = 128
MOD_ROWS = 8
ROW_TILE = 512
FFN_ROW_TILE = 1024
FF_CHUNK = 256
Q_TILE = 256
LATENT_GROUP = 1
CONTEXT_GROUP = 4
VMEM_LIMIT = 48 * 1024 * 1024
FFN_VMEM_LIMIT = 56 * 1024 * 1024

F32 = jnp.float32
BF16 = jnp.bfloat16
LOG2E = math.log2(math.e)


def _cparams(*sem):
    return pltpu.CompilerParams(dimension_semantics=sem, vmem_limit_bytes=VMEM_LIMIT)


def _dot(a, b):
    return jnp.dot(a, b, preferred_element_type=F32)


def _dot_nt(a, b):
    return lax.dot_general(a, b, (((1,), (1,)), ((), ())), preferred_element_type=F32)


def _rms(x, w):
    return x * lax.rsqrt(jnp.mean(x * x, axis=-1, keepdims=True) + EPS) * w


def _group_rms(x, w, gmat):
    x2 = x * x
    hi = x2.astype(BF16)
    lo = (x2 - hi.astype(F32)).astype(BF16)
    ms = _dot(hi, gmat) + _dot(lo, gmat)
    return x * lax.rsqrt(ms + EPS) * w


def _rope(x, cos, sin_signed, half):
    outs = []
    for j in range(x.shape[1] // LANES):
        xc = x[:, j * LANES:(j + 1) * LANES]
        lane = lax.broadcasted_iota(jnp.int32, xc.shape, 1)
        first = (lane % (2 * half)) < half
        partner = jnp.where(first, pltpu.roll(xc, LANES - half, 1), pltpu.roll(xc, half, 1))
        outs.append(xc * cos + partner * sin_signed)
    return outs[0] if len(outs) == 1 else jnp.concatenate(outs, axis=1)


def _softmax_block_rows(n_keys):
    return max(16, min(64, (16 * 1280 // n_keys) // 16 * 16))


def _attention(maps, s_scr, p_scr, group, mxu_sum=False):
    slots = s_scr.shape[0]
    staged = {}

    def stage(i):
        s_ref = s_scr.at[i % slots]
        offs, off = [], 0
        for s in maps[i][0]():
            s_ref[:, off:off + s.shape[1]] = s
            offs.append(off)
            off += s.shape[1]
        staged[i] = (offs, off)

    def softmax(i):
        _, _, c, fix, _ = maps[i]
        n_keys = staged[i][1]
        s_ref, p_ref = s_scr.at[i % slots], p_scr.at[i % slots]
        rb = _softmax_block_rows(n_keys)
        sums = []
        for r0 in range(0, s_ref.shape[0], rb):
            s = s_ref[r0:r0 + rb, 0:n_keys]
            if fix is not None:
                s = fix(s, r0)
            m = jnp.max(s, axis=-1, keepdims=True)
            p = jnp.exp2((s - m) * c)
            if not mxu_sum:
                sums.append(jnp.sum(p, axis=-1, keepdims=True))
            p_ref[r0:r0 + rb, 0:n_keys] = p.astype(BF16)
        return None if mxu_sum else jnp.concatenate(sums, axis=0)

    def weighted_values(i, den):
        _, values, _, _, sink = maps[i]
        p_ref = p_scr.at[i % slots]
        offs, n_keys = staged.pop(i)
        acc = None
        for o, v in zip(offs, values()):
            part = _dot(p_ref[:, o:o + v.shape[0]], v)
            acc = part if acc is None else acc + part
        if mxu_sum:
            den = _dot(p_ref[:, 0:n_keys], jnp.ones((n_keys, LANES), BF16))
        sink(acc / den)

    groups = [range(g, min(g + group, len(maps))) for g in range(0, len(maps), group)]
    for i in groups[0]:
        stage(i)
    for gi, grp in enumerate(groups):
        if gi + 1 < len(groups):
            for i in groups[gi + 1]:
                stage(i)
        dens = [softmax(i) for i in grp]
        for i, den in zip(grp, dens):
            weighted_values(i, den)


def _attn_scratch(tq, n_keys, group):
    return [pltpu.VMEM((2 * group, tq, n_keys), F32), pltpu.VMEM((2 * group, tq, n_keys), BF16)]


def _upper_half(shape):
    return lax.broadcasted_iota(jnp.int32, shape, 1) >= HEAD_DIM


def _keep_half(x, half):
    upper = _upper_half(x.shape)
    return jnp.where(upper if half else ~upper, x, jnp.zeros_like(x))


def _swap_halves(x):
    return jnp.concatenate([x[:, HEAD_DIM:], x[:, :HEAD_DIM]], axis=1)


def _pair_store(o_ref, c0):
    got = {}

    def make(half):
        def sink(o):
            got[half] = o
            if len(got) == 2:
                o_ref[:, c0:c0 + LANES] = jnp.where(_upper_half(o.shape), got[1], got[0]).astype(BF16)
        return sink
    return make(0), make(1)


def _out_proj_residual(o_scr, h_ref, wout_ref, mod_ref, nw1_ref, h1_ref):
    y = _dot(o_scr[...], wout_ref[...])
    gate = mod_ref[2:3, :]
    nw1 = nw1_ref[...]
    for c in range(y.shape[0] // NORM_ROWS):
        rs = slice(c * NORM_ROWS, (c + 1) * NORM_ROWS)
        h1_ref[rs, :] = h_ref[rs, :] + gate * _rms(y[rs], nw1)


def _mod_kernel(c_ref, w_ref, b_ref, o_ref):
    cv = c_ref[...]
    act = cv / (1.0 + jnp.exp(-cv))
    o_ref[...] = _dot(act.astype(BF16), w_ref[...].astype(BF16)) + b_ref[...]


def _modulation(cvecs, w_mod, b_mod):
    tn = 1024
    n = 6 * D_MODEL
    return pl.pallas_call(
        _mod_kernel,
        grid=(DEPTH, n // tn),
        in_specs=[
            pl.BlockSpec((MOD_ROWS, D_MODEL), lambda l, j: (0, 0)),
            pl.BlockSpec((None, D_MODEL, tn), lambda l, j: (l, 0, j)),
            pl.BlockSpec((None, 1, tn), lambda l, j: (l, 0, j)),
        ],
        out_specs=pl.BlockSpec((None, MOD_ROWS, tn), lambda l, j: (l, 0, j)),
        out_shape=jax.ShapeDtypeStruct((DEPTH, MOD_ROWS, n), F32),
        compiler_params=_cparams("parallel", "parallel"),
        name="adaln_mod",
    )(cvecs, w_mod, b_mod.reshape(DEPTH, 1, n))


def _mod_spec(layer, sample, tm):
    if sample:
        per = DEC_SEQ // tm
        return pl.BlockSpec((None, None, 6, D_MODEL), lambda i, *_: (layer, 1 + i // per, 0, 0))
    return pl.BlockSpec((None, None, 6, D_MODEL), lambda i, *_: (layer, 0, 0, 0))


def _full(shape):
    nd = len(shape)
    return pl.BlockSpec(shape, lambda *_: (0,) * nd)


def _rows(tm, width):
    return pl.BlockSpec((tm, width), lambda i, *_: (i, 0))


def _rope_spec(tm):
    per = DEC_SEQ // tm
    return pl.BlockSpec((tm, LANES), lambda i, *_: (i % per, 0))


EVEN_IN = 2048
MLA_SLOT = 128


def _proj_even_kernel(sample, *refs):
    if sample:
        (h_ref, mod_ref, nw_ref, win_ref, qnw_ref, kvnw_ref, wuq_ref, wuk_ref, wuv_ref,
         ca_ref, sa_ref, ch_ref, sh_ref,
         qa_ref, ka_ref, va_ref, qd_ref, kd_ref, vd_ref) = refs
    else:
        (h_ref, mod_ref, nw_ref, win_ref, qnw_ref, kvnw_ref, wuq_ref, wuk_ref, wuv_ref,
         qa_ref, ka_ref, va_ref, qd_ref, kd_ref, vd_ref,
         ckv_st_ref, kpe_st_ref, kd_st_ref, vd_st_ref) = refs
    mod = mod_ref[...]
    u = _rms(h_ref[...], nw_ref[...]) * (1.0 + mod[1:2]) + mod[0:1]
    z = _dot(u.astype(BF16), win_ref[...])
    cq = _rms(z[:, 0:256], qnw_ref[...]).astype(BF16)
    qa = _dot(cq, wuq_ref[...])
    ckv = _rms(z[:, 256:384], kvnw_ref[...])
    ckv_b = ckv.astype(BF16)
    kn = _dot(ckv_b, wuk_ref[...])
    va_ref[...] = _dot(ckv_b, wuv_ref[...]).astype(BF16)
    kpe_slot = z[:, 384:512]
    qd = z[:, 512:1024]
    kd = z[:, 1024:1536]
    vd = z[:, 1536:2048]
    vd_ref[...] = vd.astype(BF16)
    if sample:
        ca, sa, ch, sh = ca_ref[...], sa_ref[...], ch_ref[...], sh_ref[...]
        qa = _rope(qa, ca, sa, QK_ROPE // 2)
        kpe_rot = _rope(kpe_slot, ca, sa, QK_ROPE // 2)
        qd_ref[...] = _rope(qd, ch, sh, DH_B // 2).astype(BF16)
        kd_ref[...] = _rope(kd, ch, sh, DH_B // 2).astype(BF16)
    else:
        kpe_rot = kpe_slot
        qd_ref[...] = qd.astype(BF16)
        kd_ref[...] = kd.astype(BF16)
        ckv_st_ref[...] = ckv
        kpe_st_ref[...] = kpe_slot[:, QK_NOPE:QK_NOPE + QK_ROPE]
        kd_st_ref[...] = kd
        vd_st_ref[...] = vd
    qa_ref[...] = qa.astype(BF16)
    for hd in range(H_A):
        sl = slice(hd * MLA_SLOT, (hd + 1) * MLA_SLOT)
        ka_ref[:, sl] = (kn[:, sl] + kpe_rot).astype(BF16)


def _proj_even(h, mod, layer, sample, nw, w, rope):
    n = h.shape[0]
    tm = ROW_TILE
    wide = H_A * MLA_SLOT
    ins = [h, mod, nw, w["w_in"], w["q_norm_w"], w["kv_norm_w"], w["w_uq"], w["w_uk"], w["w_uv"]]
    in_specs = [_rows(tm, D_MODEL), _mod_spec(layer, sample, tm), _full((1, D_MODEL)), _full((D_MODEL, EVEN_IN)),
                _full((1, Q_LORA)), _full((1, KV_LORA)), _full((Q_LORA, wide)), _full((KV_LORA, wide)),
                _full((KV_LORA, 512))]
    widths = [wide, wide, 512, 512, 512, 512]
    out_shape = [jax.ShapeDtypeStruct((n, wd), BF16) for wd in widths]
    out_specs = [_rows(tm, wd) for wd in widths]
    if sample:
        ins += [rope["ca"], rope["sa"], rope["ch"], rope["sh"]]
        in_specs += [_rope_spec(tm)] * 4
    else:
        for wd in (KV_LORA, QK_ROPE, 512, 512):
            out_shape.append(jax.ShapeDtypeStruct((n, wd), F32))
            out_specs.append(_rows(tm, wd))
    return pl.pallas_call(
        functools.partial(_proj_even_kernel, sample),
        grid=(n // tm,),
        in_specs=in_specs,
        out_specs=out_specs,
        out_shape=out_shape,
        compiler_params=_cparams("parallel"),
        name="proj_even_s" if sample else "proj_even_p",
    )(*ins)


def _attn_even_kernel(sample, lam_init, *refs):
    if sample:
        (qa_ref, ka_ref, va_ref, qd_ref, kd_ref, vd_ref,
         cckv_ref, ckpe_ref, cdk_ref, cdv_ref, wuk_ref, wuv_ref, lam_ref, sub_ref,
         h_ref, wout_ref, mod_ref, nw1_ref, h1_ref, s_scr, p_scr, o_ref) = refs
    else:
        (qa_ref, ka_ref, va_ref, qd_ref, kd_ref, vd_ref, lam_ref, sub_ref,
         h_ref, wout_ref, mod_ref, nw1_ref, h1_ref, s_scr, p_scr, o_ref) = refs
    lf = lam_ref[...]
    lam = (jnp.exp(jnp.sum(lf[0:1] * lf[1:2], axis=-1, keepdims=True))
           - jnp.exp(jnp.sum(lf[2:3] * lf[3:4], axis=-1, keepdims=True)) + lam_init)
    if sample:
        cckv = cckv_ref[...].astype(BF16)
        kn_ctx = _dot(cckv, wuk_ref[...])
        va_ctx = _dot(cckv, wuv_ref[...]).astype(BF16)
        n_ctx = cckv.shape[0]
        kpe_ctx = jnp.concatenate([jnp.zeros((n_ctx, QK_NOPE), F32), ckpe_ref[...],
                                   jnp.zeros((n_ctx, MLA_SLOT - QK_NOPE - QK_ROPE), F32)], axis=1)
        kd_ctx = cdk_ref[...].astype(BF16)
        vd_ctx = cdv_ref[...].astype(BF16)
    maps = []
    c_a = (QK_NOPE + QK_ROPE) ** -0.5 * LOG2E
    for hd in range(H_A):
        sl = slice(hd * MLA_SLOT, (hd + 1) * MLA_SLOT)
        if hd % 2 == 0:
            sinks = _pair_store(o_ref, hd * V_A)

        def scores(sl=sl):
            q = qa_ref[:, sl]
            out = [_dot_nt(q, ka_ref[:, sl])]
            if sample:
                out.append(_dot_nt(q, (kn_ctx[:, sl] + kpe_ctx).astype(BF16)))
            return out

        def values(vs=slice(hd // 2 * LANES, (hd // 2 + 1) * LANES)):
            return [va_ref[:, vs]] + ([va_ctx[:, vs]] if sample else [])

        maps.append((scores, values, c_a, None, sinks[hd % 2]))
    c_b = DH_B ** -0.5 * LOG2E
    base = H_A * V_A
    sub_w = sub_ref[...]
    for hd in range(H_B):
        hs = slice(hd * 2 * DH_B, (hd + 1) * 2 * DH_B)
        outs = []

        def sink(o, outs=outs, hs=hs):
            outs.append(o)
            if len(outs) == 2:
                ob = _rms(outs[0] - lam * outs[1], sub_w) * (1.0 - lam_init)
                o_ref[:, base + hs.start:base + hs.stop] = ob.astype(BF16)

        def values(hs=hs):
            return [vd_ref[:, hs]] + ([vd_ctx[:, hs]] if sample else [])

        for comp in range(2):
            def scores(hs=hs, comp=comp):
                q = _keep_half(qd_ref[:, hs], comp)
                out = [_dot_nt(q, kd_ref[:, hs])]
                if sample:
                    out.append(_dot_nt(q, kd_ctx[:, hs]))
                return out

            maps.append((scores, values, c_b, None, sink))
    _attention(maps, s_scr, p_scr, s_scr.shape[0] // 2, mxu_sum=not sample)
    _out_proj_residual(o_ref, h_ref, wout_ref, mod_ref, nw1_ref, h1_ref)


def _attn_even(p, sample, lam_init, caches, w, h, w_out, mod, layer, nw1):
    qa, ka, va, qd, kd, vd = p
    n = qa.shape[0]
    wide = H_A * MLA_SLOT
    if sample:
        tq, per = Q_TILE, DEC_SEQ // Q_TILE
        grid = (DEC_BATCH, per)
        qspec = lambda wd: pl.BlockSpec((tq, wd), lambda b, i: (b * per + i, 0))
        kspec = lambda wd: pl.BlockSpec((DEC_SEQ, wd), lambda b, i: (b, 0))
        cspec = lambda wd: pl.BlockSpec((None, PAST_LEN, wd), lambda b, i: (b, 0, 0))
        ins = [qa, ka, va, qd, kd, vd, *caches, w["w_uk"], w["w_uv"], w["diff_lam"], w["diff_subln_w"]]
        in_specs = [qspec(wide), kspec(wide), kspec(512), qspec(512), kspec(512), kspec(512),
                    cspec(KV_LORA), cspec(QK_ROPE), cspec(512), cspec(512),
                    _full((KV_LORA, wide)), _full((KV_LORA, 512)), _full((4, DH_B)), _full((1, 2 * DH_B))]
        out_spec = qspec(D_MODEL)
        mod_spec = pl.BlockSpec((None, None, 6, D_MODEL), lambda b, i: (layer, 1 + b, 0, 0))
        sem = ("parallel", "parallel")
        n_keys, group = DEC_SEQ + PAST_LEN, LATENT_GROUP
    else:
        tq = SEQ
        grid = (BATCH,)
        spec = lambda wd: pl.BlockSpec((SEQ, wd), lambda b: (b, 0))
        ins = [qa, ka, va, qd, kd, vd, w["diff_lam"], w["diff_subln_w"]]
        in_specs = [spec(wide), spec(wide), spec(512), spec(512), spec(512), spec(512),
                    _full((4, DH_B)), _full((1, 2 * DH_B))]
        out_spec = spec(D_MODEL)
        mod_spec = pl.BlockSpec((None, None, 6, D_MODEL), lambda b: (layer, 0, 0, 0))
        sem = ("parallel",)
        n_keys, group = SEQ, CONTEXT_GROUP
    ins += [h, w_out, mod, nw1]
    in_specs += [out_spec, _full((D_MODEL, D_MODEL)), mod_spec, _full((1, D_MODEL))]
    return pl.pallas_call(
        functools.partial(_attn_even_kernel, sample, lam_init),
        grid=grid,
        in_specs=in_specs,
        out_specs=out_spec,
        out_shape=jax.ShapeDtypeStruct((n, D_MODEL), F32),
        scratch_shapes=_attn_scratch(tq, n_keys, group) + [pltpu.VMEM((tq, D_MODEL), BF16)],
        compiler_params=_cparams(*sem),
        name="attn_even_s" if sample else "attn_even_p",
    )(*ins)


ODD_IN = 2304


def _proj_odd_kernel(sample, *refs):
    if sample:
        (h_ref, mod_ref, nw_ref, win_ref, qw_ref, kw_ref, gq_ref, gk_ref, ch_ref, sh_ref,
         qc_ref, kc_ref, vc_ref, qn_ref, kn_ref, vn_ref) = refs
    else:
        (h_ref, mod_ref, nw_ref, win_ref, qw_ref, kw_ref, gq_ref, gk_ref,
         qc_ref, kc_ref, vc_ref, qn_ref, kn_ref, vn_ref,
         kc_st_ref, vc_st_ref, kn_st_ref, vn_st_ref) = refs
    mod = mod_ref[...]
    u = _rms(h_ref[...], nw_ref[...]) * (1.0 + mod[1:2]) + mod[0:1]
    z = _dot(u.astype(BF16), win_ref[...])
    qc = _group_rms(z[:, 0:512], qw_ref[...], gq_ref[...])
    kc = _group_rms(z[:, 512:640], kw_ref[...], gk_ref[...])
    vc = z[:, 640:768]
    kn = z[:, 1280:1792]
    vn = z[:, 1792:2304]
    vc_ref[...] = vc.astype(BF16)
    qn_ref[...] = z[:, 768:1280].astype(BF16)
    kn_ref[...] = kn.astype(BF16)
    vn_ref[...] = vn.astype(BF16)
    if sample:
        ch, sh = ch_ref[...], sh_ref[...]
        qc_ref[...] = _rope(qc, ch, sh, DH_C // 2).astype(BF16)
        kc_ref[...] = _rope(kc, ch, sh, DH_C // 2).astype(BF16)
    else:
        qc_ref[...] = qc.astype(BF16)
        kc_ref[...] = kc.astype(BF16)
        kc_st_ref[...] = kc
        vc_st_ref[...] = vc
        kn_st_ref[...] = kn
        vn_st_ref[...] = vn


def _proj_odd(h, mod, layer, sample, nw, w, rope):
    n = h.shape[0]
    tm = ROW_TILE
    ins = [h, mod, nw, w["w_in"], w["q_w"], w["k_w"], w["gq"], w["gk"]]
    in_specs = [_rows(tm, D_MODEL), _mod_spec(layer, sample, tm), _full((1, D_MODEL)), _full((D_MODEL, ODD_IN)),
                _full((1, 512)), _full((1, 128)), _full((512, 512)), _full((128, 128))]
    widths = [512, 128, 128, 512, 512, 512]
    out_shape = [jax.ShapeDtypeStruct((n, wd), BF16) for wd in widths]
    out_specs = [_rows(tm, wd) for wd in widths]
    if sample:
        ins += [rope["ch"], rope["sh"]]
        in_specs += [_rope_spec(tm)] * 2
    else:
        for wd in (128, 128, 512, 512):
            out_shape.append(jax.ShapeDtypeStruct((n, wd), F32))
            out_specs.append(_rows(tm, wd))
    return pl.pallas_call(
        functools.partial(_proj_odd_kernel, sample),
        grid=(n // tm,),
        in_specs=in_specs,
        out_specs=out_specs,
        out_shape=out_shape,
        compiler_params=_cparams("parallel"),
        name="proj_odd_s" if sample else "proj_odd_p",
    )(*ins)


def _attn_odd_prompt_kernel(qc_ref, kc_ref, vc_ref, qn_ref, kn_ref, vn_ref, h_ref, wout_ref, mod_ref, nw1_ref,
                            h1_ref, s_scr, p_scr, o_ref):
    c = DH_C ** -0.5 * LOG2E
    group = H_C // KV_C
    kc = (kc_ref[...], _swap_halves(kc_ref[...]))
    vc = (vc_ref[...], _swap_halves(vc_ref[...]))
    maps = []
    for hd in range(H_C):
        half, swap = hd % 2, (hd // group) != (hd % 2)
        ps = slice(hd // 2 * LANES, (hd // 2 + 1) * LANES)
        if half == 0:
            sinks = _pair_store(o_ref, ps.start)
        maps.append((lambda ps=ps, half=half, swap=swap: [_dot_nt(_keep_half(qc_ref[:, ps], half), kc[swap])],
                     lambda swap=swap: [vc[swap]], c, None, sinks[half]))
    base = H_C * DH_C
    c = DH_D ** -0.5 * LOG2E
    for hd in range(H_D):
        half = hd % 2
        ps = slice(hd // 2 * LANES, (hd // 2 + 1) * LANES)
        if half == 0:
            sinks = _pair_store(o_ref, base + ps.start)
        maps.append((lambda ps=ps, half=half: [_dot_nt(_keep_half(qn_ref[:, ps], half), kn_ref[:, ps])],
                     lambda ps=ps: [vn_ref[:, ps]], c, None, sinks[half]))
    _attention(maps, s_scr, p_scr, s_scr.shape[0] // 2, mxu_sum=True)
    _out_proj_residual(o_ref, h_ref, wout_ref, mod_ref, nw1_ref, h1_ref)


def _attn_odd_prompt(p, h, w_out, mod, layer, nw1):
    qc, kc, vc, qn, kn, vn = p
    spec = lambda wd: pl.BlockSpec((SEQ, wd), lambda b: (b, 0))
    return pl.pallas_call(
        _attn_odd_prompt_kernel,
        grid=(BATCH,),
        in_specs=[spec(512), spec(128), spec(128), spec(512), spec(512), spec(512),
                  spec(D_MODEL), _full((D_MODEL, D_MODEL)),
                  pl.BlockSpec((None, None, 6, D_MODEL), lambda b: (layer, 0, 0, 0)), _full((1, D_MODEL))],
        out_specs=spec(D_MODEL),
        out_shape=jax.ShapeDtypeStruct((qc.shape[0], D_MODEL), F32),
        scratch_shapes=_attn_scratch(SEQ, SEQ, CONTEXT_GROUP) + [pltpu.VMEM((SEQ, D_MODEL), BF16)],
        compiler_params=_cparams("parallel"),
        name="attn_odd_p",
    )(qc, kc, vc, qn, kn, vn, h, w_out, mod, nw1)


def _gqa_sample_kernel(q_ref, k_ref, v_ref, ck_ref, cv_ref, o_ref, s_scr, p_scr):
    c = DH_C ** -0.5 * LOG2E
    group = H_C // KV_C
    k_loc = (k_ref[...], _swap_halves(k_ref[...]))
    v_loc = (v_ref[...], _swap_halves(v_ref[...]))
    k_ctx = ck_ref[...].astype(BF16)
    v_ctx = cv_ref[...].astype(BF16)
    k_ctx = (k_ctx, _swap_halves(k_ctx))
    v_ctx = (v_ctx, _swap_halves(v_ctx))
    maps = []
    for hd in range(H_C):
        half, swap = hd % 2, (hd // group) != (hd % 2)
        ps = slice(hd // 2 * LANES, (hd // 2 + 1) * LANES)
        if half == 0:
            sinks = _pair_store(o_ref, ps.start)

        def scores(ps=ps, half=half, swap=swap):
            q = _keep_half(q_ref[:, ps], half)
            return [_dot_nt(q, k_loc[swap]), _dot_nt(q, k_ctx[swap])]

        maps.append((scores, lambda swap=swap: [v_loc[swap], v_ctx[swap]], c, None, sinks[half]))
    _attention(maps, s_scr, p_scr, s_scr.shape[0] // 2)


def _gqa_sample(qc, kc, vc, cache_k, cache_v):
    tq, per = Q_TILE, DEC_SEQ // Q_TILE
    return pl.pallas_call(
        _gqa_sample_kernel,
        grid=(DEC_BATCH, per),
        in_specs=[pl.BlockSpec((tq, 512), lambda b, i: (b * per + i, 0)),
                  pl.BlockSpec((DEC_SEQ, 128), lambda b, i: (b, 0)),
                  pl.BlockSpec((DEC_SEQ, 128), lambda b, i: (b, 0)),
                  pl.BlockSpec((None, PAST_LEN, 128), lambda b, i: (b, 0, 0)),
                  pl.BlockSpec((None, PAST_LEN, 128), lambda b, i: (b, 0, 0))],
        out_specs=pl.BlockSpec((tq, 512), lambda b, i: (b * per + i, 0)),
        out_shape=jax.ShapeDtypeStruct((qc.shape[0], 512), BF16),
        scratch_shapes=_attn_scratch(tq, DEC_SEQ + PAST_LEN, LATENT_GROUP),
        compiler_params=_cparams("parallel", "parallel"),
        name="gqa_s",
    )(qc, kc, vc, cache_k, cache_v)


NA_ROWS = DEC_SEQ // GRID_W
NA_KR = min(NA_WIN_ROWS, NA_ROWS)
NA_LOC = NA_KR * GRID_W


def _na_sample_kernel(q_ref, k_ref, v_ref, ck_ref, cv_ref, bias_ref, o_ref, s_scr, p_scr):
    r = pl.program_id(1)
    rs = jnp.clip(r - NA_KR // 2, 0, NA_ROWS - NA_KR)
    start = pl.multiple_of(rs * GRID_W, GRID_W)
    scale = DH_D ** -0.5
    k_loc = k_ref[pl.ds(start, NA_LOC), :]
    v_loc = v_ref[pl.ds(start, NA_LOC), :]
    k_ctx = ck_ref[...].astype(BF16)
    v_ctx = cv_ref[...].astype(BF16)
    rb = _softmax_block_rows(NA_LOC + PAST_LEN)
    col_ok = {}
    for r0 in range(0, GRID_W, rb):
        wq = lax.broadcasted_iota(jnp.int32, (rb, NA_LOC), 0) + r0
        wk = lax.broadcasted_iota(jnp.int32, (rb, NA_LOC), 1) % GRID_W
        cs = jnp.clip(wq - NA_WIN_COLS // 2, 0, GRID_W - NA_WIN_COLS)
        col_ok[r0] = (wk >= cs) & (wk < cs + NA_WIN_COLS)
    maps = []
    for hd in range(H_D):
        half = hd % 2
        ps = slice(hd // 2 * LANES, (hd // 2 + 1) * LANES)
        if half == 0:
            sinks = _pair_store(o_ref, ps.start)

        def fix(s, r0, hd=hd):
            loc = s[:, :NA_LOC] * scale + bias_ref[hd, r0:r0 + rb, :]
            return jnp.concatenate([jnp.where(col_ok[r0], loc, NEG_INF), s[:, NA_LOC:] * scale], axis=1)

        def scores(ps=ps, half=half):
            q = _keep_half(q_ref[:, ps], half)
            return [_dot_nt(q, k_loc[:, ps]), _dot_nt(q, k_ctx[:, ps])]

        maps.append((scores, lambda ps=ps: [v_loc[:, ps], v_ctx[:, ps]], LOG2E, fix, sinks[half]))
    _attention(maps, s_scr, p_scr, s_scr.shape[0] // 2)


def _na_sample(qn, kn, vn, cache_k, cache_v, bias):
    def bias_map(b, r):
        rs = jnp.clip(r - NA_KR // 2, 0, NA_ROWS - NA_KR)
        return (0, rs - r + NA_WIN_ROWS - 1, 0, 0)
    return pl.pallas_call(
        _na_sample_kernel,
        grid=(DEC_BATCH, NA_ROWS),
        in_specs=[pl.BlockSpec((GRID_W, 512), lambda b, r: (b * NA_ROWS + r, 0)),
                  pl.BlockSpec((DEC_SEQ, 512), lambda b, r: (b, 0)),
                  pl.BlockSpec((DEC_SEQ, 512), lambda b, r: (b, 0)),
                  pl.BlockSpec((None, PAST_LEN, 512), lambda b, r: (b, 0, 0)),
                  pl.BlockSpec((None, PAST_LEN, 512), lambda b, r: (b, 0, 0)),
                  pl.BlockSpec((H_D, None, GRID_W, NA_LOC), bias_map)],
        out_specs=pl.BlockSpec((GRID_W, 512), lambda b, r: (b * NA_ROWS + r, 0)),
        out_shape=jax.ShapeDtypeStruct((qn.shape[0], 512), BF16),
        scratch_shapes=_attn_scratch(GRID_W, NA_LOC + PAST_LEN, CONTEXT_GROUP),
        compiler_params=_cparams("parallel", "parallel"),
        name="na_s",
    )(qn, kn, vn, cache_k, cache_v, bias)


def _na_bias_table(rpb):
    edge = GRID_W - NA_WIN_COLS
    ext = jnp.pad(rpb.astype(F32), ((0, 0), (0, 0), (edge, edge)), mode="edge")
    toep = jnp.stack([ext[:, :, GRID_W - 1 - wq:2 * GRID_W - 1 - wq] for wq in range(GRID_W)], axis=1)
    flat = toep.reshape(H_D, GRID_W, (2 * NA_WIN_ROWS - 1) * GRID_W)
    return jnp.stack([flat[:, :, d0 * GRID_W:d0 * GRID_W + NA_LOC] for d0 in range(NA_WIN_ROWS)], axis=1)


def _post_attn_kernel(n_parts, *refs):
    o_refs = refs[:n_parts]
    h_ref, wout_ref, mod_ref, nw1_ref, h1_ref = refs[n_parts:]
    y = None
    off = 0
    for o_ref in o_refs:
        wd = o_ref.shape[1]
        part = _dot(o_ref[...], wout_ref[off:off + wd, :])
        y = part if y is None else y + part
        off += wd
    mod = mod_ref[...]
    h1_ref[...] = h_ref[...] + mod[2:3] * _rms(y, nw1_ref[...])


def _post_attn(o_parts, h, w_out, mod, layer, sample, nw1):
    n = h.shape[0]
    tm = ROW_TILE
    in_specs = [_rows(tm, o.shape[1]) for o in o_parts]
    in_specs += [_rows(tm, D_MODEL), _full((D_MODEL, D_MODEL)), _mod_spec(layer, sample, tm), _full((1, D_MODEL))]
    return pl.pallas_call(
        functools.partial(_post_attn_kernel, len(o_parts)),
        grid=(n // tm,),
        in_specs=in_specs,
        out_specs=_rows(tm, D_MODEL),
        out_shape=jax.ShapeDtypeStruct((n, D_MODEL), F32),
        compiler_params=_cparams("parallel"),
        name="post_attn_s" if sample else "post_attn_p",
    )(*o_parts, h, w_out, mod, nw1)


FF_PAIR = 2 * FF_CHUNK
N_FF = D_FF // FF_CHUNK
SUBLANES = 8
ACT_TILES = 4
ROW_BLOCK = 512
NORM_ROWS = 32


def _ffn_kernel(seq_len, h_ref, wup_ref, cw_ref, cb_ref, wd_ref, mod_ref, nw2_ref, nw3_ref, o_ref,
                u_ref, z0_ref, z1_ref, a_ref, a_last_ref):
    tm = h_ref.shape[0]
    n_blocks = tm // ROW_BLOCK
    rows = ACT_TILES * SUBLANES
    mod = mod_ref[...]
    sub = lax.broadcasted_iota(jnp.int32, (SUBLANES, LANES), 0)
    zero_rows = jnp.zeros((SUBLANES, FF_PAIR), F32)
    for z_ref in (z0_ref, z1_ref):
        z_ref[0:SUBLANES, :] = zero_rows
        z_ref[SUBLANES + tm:2 * SUBLANES + tm, :] = zero_rows

    nw2 = nw2_ref[...]

    def pre_norm(blk):
        for c in range(ROW_BLOCK // NORM_ROWS):
            rs = slice(blk * ROW_BLOCK + c * NORM_ROWS, blk * ROW_BLOCK + (c + 1) * NORM_ROWS)
            u = _rms(h_ref[rs, :], nw2) * (1.0 + mod[4:5]) + mod[3:4]
            u_ref[rs, :] = u.astype(BF16)

    def up(j, z_ref, blk):
        r0 = blk * ROW_BLOCK
        u = u_ref[r0:r0 + ROW_BLOCK, :]
        rows_ = slice(SUBLANES + r0, SUBLANES + r0 + ROW_BLOCK)
        for half in range(2):
            c0 = half * D_FF + j * FF_CHUNK
            if not isinstance(c0, int):
                c0 = pl.multiple_of(c0, FF_CHUNK)
            z_ref[rows_, half * FF_CHUNK:(half + 1) * FF_CHUNK] = _dot(u, wup_ref[:, pl.ds(c0, FF_CHUNK)])

    def act(j, z_ref, col, blk, dst_ref=a_ref):
        cw = cw_ref[j]
        cb = cb_ref[j]
        for lc in range(FF_CHUNK // LANES):
            taps = []
            for lane0 in (lc * LANES, FF_CHUNK + lc * LANES):
                lanes = slice(lane0, lane0 + LANES)
                taps.append([jnp.broadcast_to(cw[k:k + 1, lanes], (SUBLANES, LANES)) for k in range(3)]
                            + [jnp.broadcast_to(cb[:, lanes], (SUBLANES, LANES))])
            for c in range(ROW_BLOCK // rows):
                r = blk * ROW_BLOCK + c * rows
                first = r % seq_len == 0
                last = (r + rows) % seq_len == 0

                def conv(lane0, tap):
                    ext = z_ref[r:r + rows + 2 * SUBLANES, lane0:lane0 + LANES]
                    tiles = [ext[t * SUBLANES:(t + 1) * SUBLANES] for t in range(ACT_TILES + 2)]
                    down = [pltpu.roll(t, 1, 0) for t in tiles[:-1]]
                    up_ = [pltpu.roll(t, SUBLANES - 1, 0) for t in tiles[1:]]
                    out = []
                    for t in range(ACT_TILES):
                        above = 0.0 if (first and t == 0) else down[t]
                        below = 0.0 if (last and t == ACT_TILES - 1) else up_[t + 1]
                        prev = jnp.where(sub == 0, above, down[t + 1])
                        nxt = jnp.where(sub == SUBLANES - 1, below, up_[t])
                        out.append(prev * tap[0] + tiles[t + 1] * tap[1] + nxt * tap[2] + tap[3])
                    return jnp.concatenate(out, axis=0)

                g = conv(lc * LANES, taps[0])
                v = conv(FF_CHUNK + lc * LANES, taps[1])
                a = (g / (1.0 + jnp.exp2(g * -LOG2E))) * v
                lane = col + lc * LANES
                if not isinstance(lane, int):
                    lane = pl.multiple_of(lane, LANES)
                dst_ref[r:r + rows, pl.ds(lane, LANES)] = a.astype(BF16)

    for blk in range(n_blocks):
        pre_norm(blk)
        up(0, z0_ref, blk)

    def pair(i, carry):
        j = 2 * i
        col = pl.multiple_of(j * FF_CHUNK, FF_CHUNK)
        for blk in range(n_blocks):
            up(j + 1, z1_ref, blk)
            act(j, z0_ref, col, blk)
        for blk in range(n_blocks):
            up(j + 2, z0_ref, blk)
            act(j + 1, z1_ref, col + FF_CHUNK, blk)
        return carry

    lax.fori_loop(0, (N_FF - 1) // 2, pair, 0)
    nw3 = nw3_ref[...]
    k_main = (N_FF - 1) * FF_CHUNK
    for blk in range(n_blocks):
        r0 = blk * ROW_BLOCK
        y = _dot(a_ref[r0:r0 + ROW_BLOCK, 0:k_main], wd_ref[0:k_main, :])
        act(N_FF - 1, z0_ref, 0, blk, a_last_ref)
        y = y + _dot(a_last_ref[r0:r0 + ROW_BLOCK, :], wd_ref[k_main:D_FF, :])
        for c in range(ROW_BLOCK // NORM_ROWS):
            rs = slice(r0 + c * NORM_ROWS, r0 + (c + 1) * NORM_ROWS)
            o_ref[rs, :] = h_ref[rs, :] + mod[5:6] * _rms(y[c * NORM_ROWS:(c + 1) * NORM_ROWS], nw3)


def _ffn(h1, w, mod, layer, sample, nw2, nw3):
    n = h1.shape[0]
    tm = FFN_ROW_TILE
    seq_len = DEC_SEQ if sample else SEQ
    once = pl.Buffered(1)
    in_specs = [
        _rows(tm, D_MODEL),
        pl.BlockSpec((None, D_MODEL, 2 * D_FF), lambda i: (layer, 0, 0), pipeline_mode=once),
        pl.BlockSpec((N_FF, 3, FF_PAIR), lambda i: (0, 0, 0), pipeline_mode=once),
        pl.BlockSpec((N_FF, 1, FF_PAIR), lambda i: (0, 0, 0), pipeline_mode=once),
        pl.BlockSpec((None, D_FF, D_MODEL), lambda i: (layer, 0, 0), pipeline_mode=once),
        _mod_spec(layer, sample, tm),
        _full((1, D_MODEL)),
        _full((1, D_MODEL)),
    ]
    return pl.pallas_call(
        functools.partial(_ffn_kernel, seq_len),
        grid=(n // tm,),
        in_specs=in_specs,
        out_specs=_rows(tm, D_MODEL),
        out_shape=jax.ShapeDtypeStruct((n, D_MODEL), F32),
        scratch_shapes=[pltpu.VMEM((tm, D_MODEL), BF16), pltpu.VMEM((tm + 2 * SUBLANES, FF_PAIR), F32),
                        pltpu.VMEM((tm + 2 * SUBLANES, FF_PAIR), F32), pltpu.VMEM((tm, D_FF - FF_CHUNK), BF16),
                        pltpu.VMEM((tm, FF_CHUNK), BF16)],
        compiler_params=pltpu.CompilerParams(dimension_semantics=("parallel",), vmem_limit_bytes=FFN_VMEM_LIMIT),
        name="ffn_s" if sample else "ffn_p",
    )(h1, w["w_up"], w["conv_w"], w["conv_b"], w["w_down"], mod, nw2, nw3)


def _pair_chunks(x):
    lead = x.shape[:-1]
    x = x.reshape(lead + (2, N_FF, FF_CHUNK))
    x = jnp.moveaxis(x, -2, 0)
    return x.reshape((N_FF,) + lead + (FF_PAIR,))


def _rope_tables():
    def table(rot_dim):
        t = np.arange(DEC_SEQ)
        n_freq = rot_dim // 4
        inv = 1.0 / (ROPE_THETA ** (np.arange(n_freq) / n_freq))
        ang = np.concatenate([(t // GRID_W)[:, None] * inv[None, :], (t % GRID_W)[:, None] * inv[None, :]], axis=-1)
        cos = np.cos(ang).astype(np.float32)
        sin = np.sin(ang).astype(np.float32)
        reps = LANES // rot_dim
        return (np.tile(np.concatenate([cos, cos], axis=-1), (1, reps)),
                np.tile(np.concatenate([-sin, sin], axis=-1), (1, reps)))
    ca, sa = table(QK_ROPE)
    ch, sh = table(HEAD_DIM)
    rope_lanes = (np.arange(MLA_SLOT) >= QK_NOPE) & (np.arange(MLA_SLOT) < QK_NOPE + QK_ROPE)
    ca = np.where(rope_lanes[None, :], ca, 1.0).astype(np.float32)
    sa = np.where(rope_lanes[None, :], sa, 0.0).astype(np.float32)
    return {"ca": jnp.asarray(ca), "sa": jnp.asarray(sa), "ch": jnp.asarray(ch), "sh": jnp.asarray(sh)}


def _group_mean_matrix(width):
    idx = np.arange(width) // HEAD_DIM
    return jnp.asarray((idx[:, None] == idx[None, :]).astype(np.float32) / HEAD_DIM, BF16)


def kernel(x_prompt, x_sample, c, cache_mla_ckv, cache_mla_kpe, cache_diff_k, cache_diff_v, cache_gqa_k, cache_gqa_v, cache_na_k, cache_na_v, c_ctx, norm_w, w_mod, b_mod, w_in_even, w_out_even, w_uq, q_norm_w, kv_norm_w, w_uk, w_uv, diff_lam, diff_subln_w, w_in_odd, w_out_odd, qk_norm_w, na_rpb, w_up, conv_w, conv_b, w_down):
    rope = _rope_tables()
    n_p = BATCH * SEQ
    n_s = DEC_BATCH * DEC_SEQ
    cvecs = jnp.concatenate([c_ctx[None, :], c, jnp.zeros((MOD_ROWS - 1 - DEC_BATCH, D_MODEL), F32)], axis=0)
    mod = _modulation(cvecs, w_mod, b_mod).reshape(DEPTH, MOD_ROWS, 6, D_MODEL)
    hp = x_prompt.reshape(n_p, D_MODEL)
    hs = x_sample.reshape(n_s, D_MODEL)
    even_states, odd_states = [], []
    w_up_b = w_up.astype(BF16)
    w_down_b = w_down.astype(BF16)
    for l in range(DEPTH):
        i = l // 2
        nw = [norm_w[l, k][None, :] for k in range(4)]
        if l % 2 == 0:
            lam_init = 0.8 - 0.6 * math.exp(-0.3 * l)
            wi = w_in_even[i]
            w_uq3 = w_uq[i].reshape(Q_LORA, H_A, QK_NOPE + QK_ROPE)
            w = {
                "w_in": jnp.concatenate([wi[:, :384], jnp.zeros((D_MODEL, QK_NOPE), F32), wi[:, 384:416],
                                         jnp.zeros((D_MODEL, MLA_SLOT - QK_NOPE - QK_ROPE), F32), wi[:, 416:]],
                                        axis=1).astype(BF16),
                "q_norm_w": q_norm_w[i][None, :],
                "kv_norm_w": kv_norm_w[i][None, :],
                "w_uq": jnp.pad(w_uq3, ((0, 0), (0, 0), (0, MLA_SLOT - QK_NOPE - QK_ROPE))
                                ).reshape(Q_LORA, H_A * MLA_SLOT).astype(BF16),
                "w_uk": jnp.pad(w_uk[i].reshape(KV_LORA, H_A, QK_NOPE), ((0, 0), (0, 0), (0, MLA_SLOT - QK_NOPE))
                                ).reshape(KV_LORA, H_A * MLA_SLOT).astype(BF16),
                "w_uv": w_uv[i].astype(BF16),
                "diff_lam": diff_lam[i],
                "diff_subln_w": diff_subln_w[i][None, :],
            }
            outs_p = _proj_even(hp, mod, l, False, nw[0], w, rope)
            outs_s = _proj_even(hs, mod, l, True, nw[0], w, rope)
            even_states.append(outs_p[6:])
            w_out = w_out_even[i].astype(BF16)
            h1p = _attn_even(outs_p[:6], False, lam_init, None, w, hp, w_out, mod, l, nw[1])
            caches = (cache_mla_ckv[:, i], cache_mla_kpe[:, i],
                      cache_diff_k[:, i].reshape(DEC_BATCH, PAST_LEN, 512),
                      cache_diff_v[:, i].reshape(DEC_BATCH, PAST_LEN, 512))
            h1s = _attn_even(outs_s, True, lam_init, caches, w, hs, w_out, mod, l, nw[1])
        else:
            q_w = jnp.tile(qk_norm_w[i, 0], H_C)[None, :]
            k_w = jnp.tile(qk_norm_w[i, 1], KV_C)[None, :]
            w = {"w_in": w_in_odd[i].astype(BF16), "q_w": q_w, "k_w": k_w,
                 "gq": _group_mean_matrix(512), "gk": _group_mean_matrix(128)}
            outs_p = _proj_odd(hp, mod, l, False, nw[0], w, rope)
            outs_s = _proj_odd(hs, mod, l, True, nw[0], w, rope)
            odd_states.append(outs_p[6:])
            w_out = w_out_odd[i].astype(BF16)
            h1p = _attn_odd_prompt(outs_p[:6], hp, w_out, mod, l, nw[1])
            qc, kc, vc, qn, kn, vn = outs_s
            o_c = _gqa_sample(qc, kc, vc, cache_gqa_k[:, i].reshape(DEC_BATCH, PAST_LEN, 128),
                              cache_gqa_v[:, i].reshape(DEC_BATCH, PAST_LEN, 128))
            o_d = _na_sample(qn, kn, vn, cache_na_k[:, i].reshape(DEC_BATCH, PAST_LEN, 512),
                             cache_na_v[:, i].reshape(DEC_BATCH, PAST_LEN, 512), _na_bias_table(na_rpb[i]))
            h1s = _post_attn([o_c, o_d], hs, w_out, mod, l, True, nw[1])
        wf = {"w_up": w_up_b, "conv_w": _pair_chunks(conv_w[l]),
              "conv_b": _pair_chunks(conv_b[l][None, :]), "w_down": w_down_b}
        hp = _ffn(h1p, wf, mod, l, False, nw[2], nw[3])
        hs = _ffn(h1s, wf, mod, l, True, nw[2], nw[3])

    def stack(states, k, shape):
        return jnp.stack([st[k].reshape((BATCH, SEQ) + shape) for st in states], axis=1)

    new_mla_ckv = stack(even_states, 0, (KV_LORA,))
    new_mla_kpe = stack(even_states, 1, (QK_ROPE,))
    new_diff_k = stack(even_states, 2, (H_B, 2 * DH_B))
    new_diff_v = stack(even_states, 3, (H_B, 2 * DH_B))
    new_gqa_k = stack(odd_states, 0, (KV_C, DH_C))
    new_gqa_v = stack(odd_states, 1, (KV_C, DH_C))
    new_na_k = stack(odd_states, 2, (H_D, DH_D))
    new_na_v = stack(odd_states, 3, (H_D, DH_D))
    return (hp.reshape(BATCH, SEQ, D_MODEL), hs.reshape(DEC_BATCH, DEC_SEQ, D_MODEL),
            new_mla_ckv, new_mla_kpe, new_diff_k, new_diff_v, new_gqa_k, new_gqa_v, new_na_k, new_na_v)
```

```python
import functools
import math

import numpy as np
import jax
import jax.numpy as jnp
from jax import lax
from jax.experimental import pallas as pl
from jax.experimental.pallas import tpu as pltpu

D_MODEL = 1024
BATCH = 32
SEQ = 256
DEPTH = 2
DEC_BATCH = 4
DEC_SEQ = 1024
PAST_LEN = 256
GRID_W = 64
HEAD_DIM = 64
H_A = 8
QK_NOPE = 64
QK_ROPE = 32
V_A = 64
Q_LORA = 256
KV_LORA = 128
H_B = 4
DH_B = HEAD_DIM
H_C = 8
KV_C = 2
DH_C = HEAD_DIM
H_D = 8
DH_D = HEAD_DIM
NA_WIN_ROWS = 8
NA_WIN_COLS = 16
D_FF = 2816
ROPE_THETA = 10000.0
EPS = 1e-6
NEG_INF = -1e30

LANES = 128
MOD_ROWS = 8
ROW_TILE = 512
FFN_ROW_TILE = 1024
FF_CHUNK = 256
Q_TILE = 256
LATENT_GROUP = 1
CONTEXT_GROUP = 4
VMEM_LIMIT = 48 * 1024 * 1024
FFN_VMEM_LIMIT = 56 * 1024 * 1024

F32 = jnp.float32
BF16 = jnp.bfloat16
LOG2E = math.log2(math.e)


def _cparams(*sem):
    return pltpu.CompilerParams(dimension_semantics=sem, vmem_limit_bytes=VMEM_LIMIT)


def _dot(a, b):
    return jnp.dot(a, b, preferred_element_type=F32)


def _dot_nt(a, b):
    return lax.dot_general(a, b, (((1,), (1,)), ((), ())), preferred_element_type=F32)


def _rms(x, w):
    return x * lax.rsqrt(jnp.mean(x * x, axis=-1, keepdims=True) + EPS) * w


def _group_rms(x, w, gmat):
    x2 = x * x
    hi = x2.astype(BF16)
    lo = (x2 - hi.astype(F32)).astype(BF16)
    ms = _dot(hi, gmat) + _dot(lo, gmat)
    return x * lax.rsqrt(ms + EPS) * w


def _rope(x, cos, sin_signed, half):
    outs = []
    for j in range(x.shape[1] // LANES):
        xc = x[:, j * LANES:(j + 1) * LANES]
        lane = lax.broadcasted_iota(jnp.int32, xc.shape, 1)
        first = (lane % (2 * half)) < half
        partner = jnp.where(first, pltpu.roll(xc, LANES - half, 1), pltpu.roll(xc, half, 1))
        outs.append(xc * cos + partner * sin_signed)
    return outs[0] if len(outs) == 1 else jnp.concatenate(outs, axis=1)


def _softmax_block_rows(n_keys):
    return max(16, min(64, (16 * 1280 // n_keys) // 16 * 16))


def _attention(maps, s_scr, p_scr, group, mxu_sum=False):
    slots = s_scr.shape[0]
    staged = {}

    def stage(i):
        s_ref = s_scr.at[i % slots]
        offs, off = [], 0
        for s in maps[i][0]():
            s_ref[:, off:off + s.shape[1]] = s
            offs.append(off)
            off += s.shape[1]
        staged[i] = (offs, off)

    def softmax(i):
        _, _, c, fix, _ = maps[i]
        n_keys = staged[i][1]
        s_ref, p_ref = s_scr.at[i % slots], p_scr.at[i % slots]
        rb = _softmax_block_rows(n_keys)
        sums = []
        for r0 in range(0, s_ref.shape[0], rb):
            s = s_ref[r0:r0 + rb, 0:n_keys]
            if fix is not None:
                s = fix(s, r0)
            m = jnp.max(s, axis=-1, keepdims=True)
            p = jnp.exp2((s - m) * c)
            if not mxu_sum:
                sums.append(jnp.sum(p, axis=-1, keepdims=True))
            p_ref[r0:r0 + rb, 0:n_keys] = p.astype(BF16)
        return None if mxu_sum else jnp.concatenate(sums, axis=0)

    def weighted_values(i, den):
        _, values, _, _, sink = maps[i]
        p_ref = p_scr.at[i % slots]
        offs, n_keys = staged.pop(i)
        acc = None
        for o, v in zip(offs, values()):
            part = _dot(p_ref[:, o:o + v.shape[0]], v)
            acc = part if acc is None else acc + part
        if mxu_sum:
            den = _dot(p_ref[:, 0:n_keys], jnp.ones((n_keys, LANES), BF16))
        sink(acc / den)

    groups = [range(g, min(g + group, len(maps))) for g in range(0, len(maps), group)]
    for i in groups[0]:
        stage(i)
    for gi, grp in enumerate(groups):
        if gi + 1 < len(groups):
            for i in groups[gi + 1]:
                stage(i)
        dens = [softmax(i) for i in grp]
        for i, den in zip(grp, dens):
            weighted_values(i, den)


def _attn_scratch(tq, n_keys, group):
    return [pltpu.VMEM((2 * group, tq, n_keys), F32), pltpu.VMEM((2 * group, tq, n_keys), BF16)]


def _upper_half(shape):
    return lax.broadcasted_iota(jnp.int32, shape, 1) >= HEAD_DIM


def _keep_half(x, half):
    upper = _upper_half(x.shape)
    return jnp.where(upper if half else ~upper, x, jnp.zeros_like(x))


def _swap_halves(x):
    return jnp.concatenate([x[:, HEAD_DIM:], x[:, :HEAD_DIM]], axis=1)


def _pair_store(o_ref, c0):
    got = {}

    def make(half):
        def sink(o):
            got[half] = o
            if len(got) == 2:
                o_ref[:, c0:c0 + LANES] = jnp.where(_upper_half(o.shape), got[1], got[0]).astype(BF16)
        return sink
    return make(0), make(1)


def _out_proj_residual(o_scr, h_ref, wout_ref, mod_ref, nw1_ref, h1_ref):
    y = _dot(o_scr[...], wout_ref[...])
    gate = mod_ref[2:3, :]
    nw1 = nw1_ref[...]
    for c in range(y.shape[0] // NORM_ROWS):
        rs = slice(c * NORM_ROWS, (c + 1) * NORM_ROWS)
        h1_ref[rs, :] = h_ref[rs, :] + gate * _rms(y[rs], nw1)


def _mod_kernel(c_ref, w_ref, b_ref, o_ref):
    cv = c_ref[...]
    act = cv / (1.0 + jnp.exp(-cv))
    o_ref[...] = _dot(act.astype(BF16), w_ref[...].astype(BF16)) + b_ref[...]


def _modulation(cvecs, w_mod, b_mod):
    tn = 1024
    n = 6 * D_MODEL
    return pl.pallas_call(
        _mod_kernel,
        grid=(DEPTH, n // tn),
        in_specs=[
            pl.BlockSpec((MOD_ROWS, D_MODEL), lambda l, j: (0, 0)),
            pl.BlockSpec((None, D_MODEL, tn), lambda l, j: (l, 0, j)),
            pl.BlockSpec((None, 1, tn), lambda l, j: (l, 0, j)),
        ],
        out_specs=pl.BlockSpec((None, MOD_ROWS, tn), lambda l, j: (l, 0, j)),
        out_shape=jax.ShapeDtypeStruct((DEPTH, MOD_ROWS, n), F32),
        compiler_params=_cparams("parallel", "parallel"),
        name="adaln_mod",
    )(cvecs, w_mod, b_mod.reshape(DEPTH, 1, n))


def _mod_spec(layer, sample, tm):
    if sample:
        per = DEC_SEQ // tm
        return pl.BlockSpec((None, None, 6, D_MODEL), lambda i, *_: (layer, 1 + i // per, 0, 0))
    return pl.BlockSpec((None, None, 6, D_MODEL), lambda i, *_: (layer, 0, 0, 0))


def _full(shape):
    nd = len(shape)
    return pl.BlockSpec(shape, lambda *_: (0,) * nd)


def _rows(tm, width):
    return pl.BlockSpec((tm, width), lambda i, *_: (i, 0))


def _rope_spec(tm):
    per = DEC_SEQ // tm
    return pl.BlockSpec((tm, LANES), lambda i, *_: (i % per, 0))


EVEN_IN = 2048
MLA_SLOT = 128


def _proj_even_kernel(sample, *refs):
    if sample:
        (h_ref, mod_ref, nw_ref, win_ref, qnw_ref, kvnw_ref, wuq_ref, wuk_ref, wuv_ref,
         ca_ref, sa_ref, ch_ref, sh_ref,
         qa_ref, ka_ref, va_ref, qd_ref, kd_ref, vd_ref) = refs
    else:
        (h_ref, mod_ref, nw_ref, win_ref, qnw_ref, kvnw_ref, wuq_ref, wuk_ref, wuv_ref,
         qa_ref, ka_ref, va_ref, qd_ref, kd_ref, vd_ref,
         ckv_st_ref, kpe_st_ref, kd_st_ref, vd_st_ref) = refs
    mod = mod_ref[...]
    u = _rms(h_ref[...], nw_ref[...]) * (1.0 + mod[1:2]) + mod[0:1]
    z = _dot(u.astype(BF16), win_ref[...])
    cq = _rms(z[:, 0:256], qnw_ref[...]).astype(BF16)
    qa = _dot(cq, wuq_ref[...])
    ckv = _rms(z[:, 256:384], kvnw_ref[...])
    ckv_b = ckv.astype(BF16)
    kn = _dot(ckv_b, wuk_ref[...])
    va_ref[...] = _dot(ckv_b, wuv_ref[...]).astype(BF16)
    kpe_slot = z[:, 384:512]
    qd = z[:, 512:1024]
    kd = z[:, 1024:1536]
    vd = z[:, 1536:2048]
    vd_ref[...] = vd.astype(BF16)
    if sample:
        ca, sa, ch, sh = ca_ref[...], sa_ref[...], ch_ref[...], sh_ref[...]
        qa = _rope(qa, ca, sa, QK_ROPE // 2)
        kpe_rot = _rope(kpe_slot, ca, sa, QK_ROPE // 2)
        qd_ref[...] = _rope(qd, ch, sh, DH_B // 2).astype(BF16)
        kd_ref[...] = _rope(kd, ch, sh, DH_B // 2).astype(BF16)
    else:
        kpe_rot = kpe_slot
        qd_ref[...] = qd.astype(BF16)
        kd_ref[...] = kd.astype(BF16)
        ckv_st_ref[...] = ckv
        kpe_st_ref[...] = kpe_slot[:, QK_NOPE:QK_NOPE + QK_ROPE]
        kd_st_ref[...] = kd
        vd_st_ref[...] = vd
    qa_ref[...] = qa.astype(BF16)
    for hd in range(H_A):
        sl = slice(hd * MLA_SLOT, (hd + 1) * MLA_SLOT)
        ka_ref[:, sl] = (kn[:, sl] + kpe_rot).astype(BF16)


def _proj_even(h, mod, layer, sample, nw, w, rope):
    n = h.shape[0]
    tm = ROW_TILE
    wide = H_A * MLA_SLOT
    ins = [h, mod, nw, w["w_in"], w["q_norm_w"], w["kv_norm_w"], w["w_uq"], w["w_uk"], w["w_uv"]]
    in_specs = [_rows(tm, D_MODEL), _mod_spec(layer, sample, tm), _full((1, D_MODEL)), _full((D_MODEL, EVEN_IN)),
                _full((1, Q_LORA)), _full((1, KV_LORA)), _full((Q_LORA, wide)), _full((KV_LORA, wide)),
                _full((KV_LORA, 512))]
    widths = [wide, wide, 512, 512, 512, 512]
    out_shape = [jax.ShapeDtypeStruct((n, wd), BF16) for wd in widths]
    out_specs = [_rows(tm, wd) for wd in widths]
    if sample:
        ins += [rope["ca"], rope["sa"], rope["ch"], rope["sh"]]
        in_specs += [_rope_spec(tm)] * 4
    else:
        for wd in (KV_LORA, QK_ROPE, 512, 512):
            out_shape.append(jax.ShapeDtypeStruct((n, wd), F32))
            out_specs.append(_rows(tm, wd))
    return pl.pallas_call(
        functools.partial(_proj_even_kernel, sample),
        grid=(n // tm,),
        in_specs=in_specs,
        out_specs=out_specs,
        out_shape=out_shape,
        compiler_params=_cparams("parallel"),
        name="proj_even_s" if sample else "proj_even_p",
    )(*ins)


def _attn_even_kernel(sample, lam_init, *refs):
    if sample:
        (qa_ref, ka_ref, va_ref, qd_ref, kd_ref, vd_ref,
         cckv_ref, ckpe_ref, cdk_ref, cdv_ref, wuk_ref, wuv_ref, lam_ref, sub_ref,
         h_ref, wout_ref, mod_ref, nw1_ref, h1_ref, s_scr, p_scr, o_ref) = refs
    else:
        (qa_ref, ka_ref, va_ref, qd_ref, kd_ref, vd_ref, lam_ref, sub_ref,
         h_ref, wout_ref, mod_ref, nw1_ref, h1_ref, s_scr, p_scr, o_ref) = refs
    lf = lam_ref[...]
    lam = (jnp.exp(jnp.sum(lf[0:1] * lf[1:2], axis=-1, keepdims=True))
           - jnp.exp(jnp.sum(lf[2:3] * lf[3:4], axis=-1, keepdims=True)) + lam_init)
    if sample:
        cckv = cckv_ref[...].astype(BF16)
        kn_ctx = _dot(cckv, wuk_ref[...])
        va_ctx = _dot(cckv, wuv_ref[...]).astype(BF16)
        n_ctx = cckv.shape[0]
        kpe_ctx = jnp.concatenate([jnp.zeros((n_ctx, QK_NOPE), F32), ckpe_ref[...],
                                   jnp.zeros((n_ctx, MLA_SLOT - QK_NOPE - QK_ROPE), F32)], axis=1)
        kd_ctx = cdk_ref[...].astype(BF16)
        vd_ctx = cdv_ref[...].astype(BF16)
    maps = []
    c_a = (QK_NOPE + QK_ROPE) ** -0.5 * LOG2E
    for hd in range(H_A):
        sl = slice(hd * MLA_SLOT, (hd + 1) * MLA_SLOT)
        if hd % 2 == 0:
            sinks = _pair_store(o_ref, hd * V_A)

        def scores(sl=sl):
            q = qa_ref[:, sl]
            out = [_dot_nt(q, ka_ref[:, sl])]
            if sample:
                out.append(_dot_nt(q, (kn_ctx[:, sl] + kpe_ctx).astype(BF16)))
            return out

        def values(vs=slice(hd // 2 * LANES, (hd // 2 + 1) * LANES)):
            return [va_ref[:, vs]] + ([va_ctx[:, vs]] if sample else [])

        maps.append((scores, values, c_a, None, sinks[hd % 2]))
    c_b = DH_B ** -0.5 * LOG2E
    base = H_A * V_A
    sub_w = sub_ref[...]
    for hd in range(H_B):
        hs = slice(hd * 2 * DH_B, (hd + 1) * 2 * DH_B)
        outs = []

        def sink(o, outs=outs, hs=hs):
            outs.append(o)
            if len(outs) == 2:
                ob = _rms(outs[0] - lam * outs[1], sub_w) * (1.0 - lam_init)
                o_ref[:, base + hs.start:base + hs.stop] = ob.astype(BF16)

        def values(hs=hs):
            return [vd_ref[:, hs]] + ([vd_ctx[:, hs]] if sample else [])

        for comp in range(2):
            def scores(hs=hs, comp=comp):
                q = _keep_half(qd_ref[:, hs], comp)
                out = [_dot_nt(q, kd_ref[:, hs])]
                if sample:
                    out.append(_dot_nt(q, kd_ctx[:, hs]))
                return out

            maps.append((scores, values, c_b, None, sink))
    _attention(maps, s_scr, p_scr, s_scr.shape[0] // 2, mxu_sum=not sample)
    _out_proj_residual(o_ref, h_ref, wout_ref, mod_ref, nw1_ref, h1_ref)


def _attn_even(p, sample, lam_init, caches, w, h, w_out, mod, layer, nw1):
    qa, ka, va, qd, kd, vd = p
    n = qa.shape[0]
    wide = H_A * MLA_SLOT
    if sample:
        tq, per = Q_TILE, DEC_SEQ // Q_TILE
        grid = (DEC_BATCH, per)
        qspec = lambda wd: pl.BlockSpec((tq, wd), lambda b, i: (b * per + i, 0))
        kspec = lambda wd: pl.BlockSpec((DEC_SEQ, wd), lambda b, i: (b, 0))
        cspec = lambda wd: pl.BlockSpec((None, PAST_LEN, wd), lambda b, i: (b, 0, 0))
        ins = [qa, ka, va, qd, kd, vd, *caches, w["w_uk"], w["w_uv"], w["diff_lam"], w["diff_subln_w"]]
        in_specs = [qspec(wide), kspec(wide), kspec(512), qspec(512), kspec(512), kspec(512),
                    cspec(KV_LORA), cspec(QK_ROPE), cspec(512), cspec(512),
                    _full((KV_LORA, wide)), _full((KV_LORA, 512)), _full((4, DH_B)), _full((1, 2 * DH_B))]
        out_spec = qspec(D_MODEL)
        mod_spec = pl.BlockSpec((None, None, 6, D_MODEL), lambda b, i: (layer, 1 + b, 0, 0))
        sem = ("parallel", "parallel")
        n_keys, group = DEC_SEQ + PAST_LEN, LATENT_GROUP
    else:
        tq = SEQ
        grid = (BATCH,)
        spec = lambda wd: pl.BlockSpec((SEQ, wd), lambda b: (b, 0))
        ins = [qa, ka, va, qd, kd, vd, w["diff_lam"], w["diff_subln_w"]]
        in_specs = [spec(wide), spec(wide), spec(512), spec(512), spec(512), spec(512),
                    _full((4, DH_B)), _full((1, 2 * DH_B))]
        out_spec = spec(D_MODEL)
        mod_spec = pl.BlockSpec((None, None, 6, D_MODEL), lambda b: (layer, 0, 0, 0))
        sem = ("parallel",)
        n_keys, group = SEQ, CONTEXT_GROUP
    ins += [h, w_out, mod, nw1]
    in_specs += [out_spec, _full((D_MODEL, D_MODEL)), mod_spec, _full((1, D_MODEL))]
    return pl.pallas_call(
        functools.partial(_attn_even_kernel, sample, lam_init),
        grid=grid,
        in_specs=in_specs,
        out_specs=out_spec,
        out_shape=jax.ShapeDtypeStruct((n, D_MODEL), F32),
        scratch_shapes=_attn_scratch(tq, n_keys, group) + [pltpu.VMEM((tq, D_MODEL), BF16)],
        compiler_params=_cparams(*sem),
        name="attn_even_s" if sample else "attn_even_p",
    )(*ins)


ODD_IN = 2304


def _proj_odd_kernel(sample, *refs):
    if sample:
        (h_ref, mod_ref, nw_ref, win_ref, qw_ref, kw_ref, gq_ref, gk_ref, ch_ref, sh_ref,
         qc_ref, kc_ref, vc_ref, qn_ref, kn_ref, vn_ref) = refs
    else:
        (h_ref, mod_ref, nw_ref, win_ref, qw_ref, kw_ref, gq_ref, gk_ref,
         qc_ref, kc_ref, vc_ref, qn_ref, kn_ref, vn_ref,
         kc_st_ref, vc_st_ref, kn_st_ref, vn_st_ref) = refs
    mod = mod_ref[...]
    u = _rms(h_ref[...], nw_ref[...]) * (1.0 + mod[1:2]) + mod[0:1]
    z = _dot(u.astype(BF16), win_ref[...])
    qc = _group_rms(z[:, 0:512], qw_ref[...], gq_ref[...])
    kc = _group_rms(z[:, 512:640], kw_ref[...], gk_ref[...])
    vc = z[:, 640:768]
    kn = z[:, 1280:1792]
    vn = z[:, 1792:2304]
    vc_ref[...] = vc.astype(BF16)
    qn_ref[...] = z[:, 768:1280].astype(BF16)
    kn_ref[...] = kn.astype(BF16)
    vn_ref[...] = vn.astype(BF16)
    if sample:
        ch, sh = ch_ref[...], sh_ref[...]
        qc_ref[...] = _rope(qc, ch, sh, DH_C // 2).astype(BF16)
        kc_ref[...] = _rope(kc, ch, sh, DH_C // 2).astype(BF16)
    else:
        qc_ref[...] = qc.astype(BF16)
        kc_ref[...] = kc.astype(BF16)
        kc_st_ref[...] = kc
        vc_st_ref[...] = vc
        kn_st_ref[...] = kn
        vn_st_ref[...] = vn


def _proj_odd(h, mod, layer, sample, nw, w, rope):
    n = h.shape[0]
    tm = ROW_TILE
    ins = [h, mod, nw, w["w_in"], w["q_w"], w["k_w"], w["gq"], w["gk"]]
    in_specs = [_rows(tm, D_MODEL), _mod_spec(layer, sample, tm), _full((1, D_MODEL)), _full((D_MODEL, ODD_IN)),
                _full((1, 512)), _full((1, 128)), _full((512, 512)), _full((128, 128))]
    widths = [512, 128, 128, 512, 512, 512]
    out_shape = [jax.ShapeDtypeStruct((n, wd), BF16) for wd in widths]
    out_specs = [_rows(tm, wd) for wd in widths]
    if sample:
        ins += [rope["ch"], rope["sh"]]
        in_specs += [_rope_spec(tm)] * 2
    else:
        for wd in (128, 128, 512, 512):
            out_shape.append(jax.ShapeDtypeStruct((n, wd), F32))
            out_specs.append(_rows(tm, wd))
    return pl.pallas_call(
        functools.partial(_proj_odd_kernel, sample),
        grid=(n // tm,),
        in_specs=in_specs,
        out_specs=out_specs,
        out_shape=out_shape,
        compiler_params=_cparams("parallel"),
        name="proj_odd_s" if sample else "proj_odd_p",
    )(*ins)


def _attn_odd_prompt_kernel(qc_ref, kc_ref, vc_ref, qn_ref, kn_ref, vn_ref, h_ref, wout_ref, mod_ref, nw1_ref,
                            h1_ref, s_scr, p_scr, o_ref):
    c = DH_C ** -0.5 * LOG2E
    group = H_C // KV_C
    kc = (kc_ref[...], _swap_halves(kc_ref[...]))
    vc = (vc_ref[...], _swap_halves(vc_ref[...]))
    maps = []
    for hd in range(H_C):
        half, swap = hd % 2, (hd // group) != (hd % 2)
        ps = slice(hd // 2 * LANES, (hd // 2 + 1) * LANES)
        if half == 0:
            sinks = _pair_store(o_ref, ps.start)
        maps.append((lambda ps=ps, half=half, swap=swap: [_dot_nt(_keep_half(qc_ref[:, ps], half), kc[swap])],
                     lambda swap=swap: [vc[swap]], c, None, sinks[half]))
    base = H_C * DH_C
    c = DH_D ** -0.5 * LOG2E
    for hd in range(H_D):
        half = hd % 2
        ps = slice(hd // 2 * LANES, (hd // 2 + 1) * LANES)
        if half == 0:
            sinks = _pair_store(o_ref, base + ps.start)
        maps.append((lambda ps=ps, half=half: [_dot_nt(_keep_half(qn_ref[:, ps], half), kn_ref[:, ps])],
                     lambda ps=ps: [vn_ref[:, ps]], c, None, sinks[half]))
    _attention(maps, s_scr, p_scr, s_scr.shape[0] // 2, mxu_sum=True)
    _out_proj_residual(o_ref, h_ref, wout_ref, mod_ref, nw1_ref, h1_ref)


def _attn_odd_prompt(p, h, w_out, mod, layer, nw1):
    qc, kc, vc, qn, kn, vn = p
    spec = lambda wd: pl.BlockSpec((SEQ, wd), lambda b: (b, 0))
    return pl.pallas_call(
        _attn_odd_prompt_kernel,
        grid=(BATCH,),
        in_specs=[spec(512), spec(128), spec(128), spec(512), spec(512), spec(512),
                  spec(D_MODEL), _full((D_MODEL, D_MODEL)),
                  pl.BlockSpec((None, None, 6, D_MODEL), lambda b: (layer, 0, 0, 0)), _full((1, D_MODEL))],
        out_specs=spec(D_MODEL),
        out_shape=jax.ShapeDtypeStruct((qc.shape[0], D_MODEL), F32),
        scratch_shapes=_attn_scratch(SEQ, SEQ, CONTEXT_GROUP) + [pltpu.VMEM((SEQ, D_MODEL), BF16)],
        compiler_params=_cparams("parallel"),
        name="attn_odd_p",
    )(qc, kc, vc, qn, kn, vn, h, w_out, mod, nw1)


def _gqa_sample_kernel(q_ref, k_ref, v_ref, ck_ref, cv_ref, o_ref, s_scr, p_scr):
    c = DH_C ** -0.5 * LOG2E
    group = H_C // KV_C
    k_loc = (k_ref[...], _swap_halves(k_ref[...]))
    v_loc = (v_ref[...], _swap_halves(v_ref[...]))
    k_ctx = ck_ref[...].astype(BF16)
    v_ctx = cv_ref[...].astype(BF16)
    k_ctx = (k_ctx, _swap_halves(k_ctx))
    v_ctx = (v_ctx, _swap_halves(v_ctx))
    maps = []
    for hd in range(H_C):
        half, swap = hd % 2, (hd // group) != (hd % 2)
        ps = slice(hd // 2 * LANES, (hd // 2 + 1) * LANES)
        if half == 0:
            sinks = _pair_store(o_ref, ps.start)

        def scores(ps=ps, half=half, swap=swap):
            q = _keep_half(q_ref[:, ps], half)
            return [_dot_nt(q, k_loc[swap]), _dot_nt(q, k_ctx[swap])]

        maps.append((scores, lambda swap=swap: [v_loc[swap], v_ctx[swap]], c, None, sinks[half]))
    _attention(maps, s_scr, p_scr, s_scr.shape[0] // 2)


def _gqa_sample(qc, kc, vc, cache_k, cache_v):
    tq, per = Q_TILE, DEC_SEQ // Q_TILE
    return pl.pallas_call(
        _gqa_sample_kernel,
        grid=(DEC_BATCH, per),
        in_specs=[pl.BlockSpec((tq, 512), lambda b, i: (b * per + i, 0)),
                  pl.BlockSpec((DEC_SEQ, 128), lambda b, i: (b, 0)),
                  pl.BlockSpec((DEC_SEQ, 128), lambda b, i: (b, 0)),
                  pl.BlockSpec((None, PAST_LEN, 128), lambda b, i: (b, 0, 0)),
                  pl.BlockSpec((None, PAST_LEN, 128), lambda b, i: (b, 0, 0))],
        out_specs=pl.BlockSpec((tq, 512), lambda b, i: (b * per + i, 0)),
        out_shape=jax.ShapeDtypeStruct((qc.shape[0], 512), BF16),
        scratch_shapes=_attn_scratch(tq, DEC_SEQ + PAST_LEN, LATENT_GROUP),
        compiler_params=_cparams("parallel", "parallel"),
        name="gqa_s",
    )(qc, kc, vc, cache_k, cache_v)


NA_ROWS = DEC_SEQ // GRID_W
NA_KR = min(NA_WIN_ROWS, NA_ROWS)
NA_LOC = NA_KR * GRID_W


def _na_sample_kernel(q_ref, k_ref, v_ref, ck_ref, cv_ref, bias_ref, o_ref, s_scr, p_scr):
    r = pl.program_id(1)
    rs = jnp.clip(r - NA_KR // 2, 0, NA_ROWS - NA_KR)
    start = pl.multiple_of(rs * GRID_W, GRID_W)
    scale = DH_D ** -0.5
    k_loc = k_ref[pl.ds(start, NA_LOC), :]
    v_loc = v_ref[pl.ds(start, NA_LOC), :]
    k_ctx = ck_ref[...].astype(BF16)
    v_ctx = cv_ref[...].astype(BF16)
    rb = _softmax_block_rows(NA_LOC + PAST_LEN)
    col_ok = {}
    for r0 in range(0, GRID_W, rb):
        wq = lax.broadcasted_iota(jnp.int32, (rb, NA_LOC), 0) + r0
        wk = lax.broadcasted_iota(jnp.int32, (rb, NA_LOC), 1) % GRID_W
        cs = jnp.clip(wq - NA_WIN_COLS // 2, 0, GRID_W - NA_WIN_COLS)
        col_ok[r0] = (wk >= cs) & (wk < cs + NA_WIN_COLS)
    maps = []
    for hd in range(H_D):
        half = hd % 2
        ps = slice(hd // 2 * LANES, (hd // 2 + 1) * LANES)
        if half == 0:
            sinks = _pair_store(o_ref, ps.start)

        def fix(s, r0, hd=hd):
            loc = s[:, :NA_LOC] * scale + bias_ref[hd, r0:r0 + rb, :]
            return jnp.concatenate([jnp.where(col_ok[r0], loc, NEG_INF), s[:, NA_LOC:] * scale], axis=1)

        def scores(ps=ps, half=half):
            q = _keep_half(q_ref[:, ps], half)
            return [_dot_nt(q, k_loc[:, ps]), _dot_nt(q, k_ctx[:, ps])]

        maps.append((scores, lambda ps=ps: [v_loc[:, ps], v_ctx[:, ps]], LOG2E, fix, sinks[half]))
    _attention(maps, s_scr, p_scr, s_scr.shape[0] // 2)


def _na_sample(qn, kn, vn, cache_k, cache_v, bias):
    def bias_map(b, r):
        rs = jnp.clip(r - NA_KR // 2, 0, NA_ROWS - NA_KR)
        return (0, rs - r + NA_WIN_ROWS - 1, 0, 0)
    return pl.pallas_call(
        _na_sample_kernel,
        grid=(DEC_BATCH, NA_ROWS),
        in_specs=[pl.BlockSpec((GRID_W, 512), lambda b, r: (b * NA_ROWS + r, 0)),
                  pl.BlockSpec((DEC_SEQ, 512), lambda b, r: (b, 0)),
                  pl.BlockSpec((DEC_SEQ, 512), lambda b, r: (b, 0)),
                  pl.BlockSpec((None, PAST_LEN, 512), lambda b, r: (b, 0, 0)),
                  pl.BlockSpec((None, PAST_LEN, 512), lambda b, r: (b, 0, 0)),
                  pl.BlockSpec((H_D, None, GRID_W, NA_LOC), bias_map)],
        out_specs=pl.BlockSpec((GRID_W, 512), lambda b, r: (b * NA_ROWS + r, 0)),
        out_shape=jax.ShapeDtypeStruct((qn.shape[0], 512), BF16),
        scratch_shapes=_attn_scratch(GRID_W, NA_LOC + PAST_LEN, CONTEXT_GROUP),
        compiler_params=_cparams("parallel", "parallel"),
        name="na_s",
    )(qn, kn, vn, cache_k, cache_v, bias)


def _na_bias_table(rpb):
    edge = GRID_W - NA_WIN_COLS
    ext = jnp.pad(rpb.astype(F32), ((0, 0), (0, 0), (edge, edge)), mode="edge")
    toep = jnp.stack([ext[:, :, GRID_W - 1 - wq:2 * GRID_W - 1 - wq] for wq in range(GRID_W)], axis=1)
    flat = toep.reshape(H_D, GRID_W, (2 * NA_WIN_ROWS - 1) * GRID_W)
    return jnp.stack([flat[:, :, d0 * GRID_W:d0 * GRID_W + NA_LOC] for d0 in range(NA_WIN_ROWS)], axis=1)


def _post_attn_kernel(n_parts, *refs):
    o_refs = refs[:n_parts]
    h_ref, wout_ref, mod_ref, nw1_ref, h1_ref = refs[n_parts:]
    y = None
    off = 0
    for o_ref in o_refs:
        wd = o_ref.shape[1]
        part = _dot(o_ref[...], wout_ref[off:off + wd, :])
        y = part if y is None else y + part
        off += wd
    mod = mod_ref[...]
    h1_ref[...] = h_ref[...] + mod[2:3] * _rms(y, nw1_ref[...])


def _post_attn(o_parts, h, w_out, mod, layer, sample, nw1):
    n = h.shape[0]
    tm = ROW_TILE
    in_specs = [_rows(tm, o.shape[1]) for o in o_parts]
    in_specs += [_rows(tm, D_MODEL), _full((D_MODEL, D_MODEL)), _mod_spec(layer, sample, tm), _full((1, D_MODEL))]
    return pl.pallas_call(
        functools.partial(_post_attn_kernel, len(o_parts)),
        grid=(n // tm,),
        in_specs=in_specs,
        out_specs=_rows(tm, D_MODEL),
        out_shape=jax.ShapeDtypeStruct((n, D_MODEL), F32),
        compiler_params=_cparams("parallel"),
        name="post_attn_s" if sample else "post_attn_p",
    )(*o_parts, h, w_out, mod, nw1)


FF_PAIR = 2 * FF_CHUNK
N_FF = D_FF // FF_CHUNK
SUBLANES = 8
ACT_TILES = 4
ROW_BLOCK = 1024
NORM_ROWS = 32


def _ffn_kernel(seq_len, h_ref, wup_ref, cw_ref, cb_ref, wd_ref, mod_ref, nw2_ref, nw3_ref, o_ref,
                u_ref, z0_ref, z1_ref, a_ref, a_last_ref):
    tm = h_ref.shape[0]
    n_blocks = tm // ROW_BLOCK
    rows = ACT_TILES * SUBLANES
    mod = mod_ref[...]
    sub = lax.broadcasted_iota(jnp.int32, (SUBLANES, LANES), 0)
    zero_rows = jnp.zeros((SUBLANES, FF_PAIR), F32)
    for z_ref in (z0_ref, z1_ref):
        z_ref[0:SUBLANES, :] = zero_rows
        z_ref[SUBLANES + tm:2 * SUBLANES + tm, :] = zero_rows

    nw2 = nw2_ref[...]

    def pre_norm(blk):
        for c in range(ROW_BLOCK // NORM_ROWS):
            rs = slice(blk * ROW_BLOCK + c * NORM_ROWS, blk * ROW_BLOCK + (c + 1) * NORM_ROWS)
            u = _rms(h_ref[rs, :], nw2) * (1.0 + mod[4:5]) + mod[3:4]
            u_ref[rs, :] = u.astype(BF16)

    def up(j, z_ref, blk):
        r0 = blk * ROW_BLOCK
        u = u_ref[r0:r0 + ROW_BLOCK, :]
        rows_ = slice(SUBLANES + r0, SUBLANES + r0 + ROW_BLOCK)
        for half in range(2):
            c0 = half * D_FF + j * FF_CHUNK
            if not isinstance(c0, int):
                c0 = pl.multiple_of(c0, FF_CHUNK)
            z_ref[rows_, half * FF_CHUNK:(half + 1) * FF_CHUNK] = _dot(u, wup_ref[:, pl.ds(c0, FF_CHUNK)])

    def act(j, z_ref, col, blk, dst_ref=a_ref):
        cw = cw_ref[j]
        cb = cb_ref[j]
        for lc in range(FF_CHUNK // LANES):
            taps = []
            for lane0 in (lc * LANES, FF_CHUNK + lc * LANES):
                lanes = slice(lane0, lane0 + LANES)
                taps.append([jnp.broadcast_to(cw[k:k + 1, lanes], (SUBLANES, LANES)) for k in range(3)]
                            + [jnp.broadcast_to(cb[:, lanes], (SUBLANES, LANES))])
            for c in range(ROW_BLOCK // rows):
                r = blk * ROW_BLOCK + c * rows
                first = r % seq_len == 0
                last = (r + rows) % seq_len == 0

                def conv(lane0, tap):
                    ext = z_ref[r:r + rows + 2 * SUBLANES, lane0:lane0 + LANES]
                    tiles = [ext[t * SUBLANES:(t + 1) * SUBLANES] for t in range(ACT_TILES + 2)]
                    down = [pltpu.roll(t, 1, 0) for t in tiles[:-1]]
                    up_ = [pltpu.roll(t, SUBLANES - 1, 0) for t in tiles[1:]]
                    out = []
                    for t in range(ACT_TILES):
                        above = 0.0 if (first and t == 0) else down[t]
                        below = 0.0 if (last and t == ACT_TILES - 1) else up_[t + 1]
                        prev = jnp.where(sub == 0, above, down[t + 1])
                        nxt = jnp.where(sub == SUBLANES - 1, below, up_[t])
                        out.append(prev * tap[0] + tiles[t + 1] * tap[1] + nxt * tap[2] + tap[3])
                    return jnp.concatenate(out, axis=0)

                g = conv(lc * LANES, taps[0])
                v = conv(FF_CHUNK + lc * LANES, taps[1])
                a = (g / (1.0 + jnp.exp2(g * -LOG2E))) * v
                lane = col + lc * LANES
                if not isinstance(lane, int):
                    lane = pl.multiple_of(lane, LANES)
                dst_ref[r:r + rows, pl.ds(lane, LANES)] = a.astype(BF16)

    for blk in range(n_blocks):
        pre_norm(blk)
        up(0, z0_ref, blk)

    def pair(i, carry):
        j = 2 * i
        col = pl.multiple_of(j * FF_CHUNK, FF_CHUNK)
        for blk in range(n_blocks):
            up(j + 1, z1_ref, blk)
            act(j, z0_ref, col, blk)
        for blk in range(n_blocks):
            up(j + 2, z0_ref, blk)
            act(j + 1, z1_ref, col + FF_CHUNK, blk)
        return carry

    lax.fori_loop(0, (N_FF - 1) // 2, pair, 0)
    nw3 = nw3_ref[...]
    k_main = (N_FF - 1) * FF_CHUNK
    for blk in range(n_blocks):
        r0 = blk * ROW_BLOCK
        y = _dot(a_ref[r0:r0 + ROW_BLOCK, 0:k_main], wd_ref[0:k_main, :])
        act(N_FF - 1, z0_ref, 0, blk, a_last_ref)
        y = y + _dot(a_last_ref[r0:r0 + ROW_BLOCK, :], wd_ref[k_main:D_FF, :])
        for c in range(ROW_BLOCK // NORM_ROWS):
            rs = slice(r0 + c * NORM_ROWS, r0 + (c + 1) * NORM_ROWS)
            o_ref[rs, :] = h_ref[rs, :] + mod[5:6] * _rms(y[c * NORM_ROWS:(c + 1) * NORM_ROWS], nw3)


def _ffn(h1, w, mod, layer, sample, nw2, nw3):
    n = h1.shape[0]
    tm = FFN_ROW_TILE
    seq_len = DEC_SEQ if sample else SEQ
    once = pl.Buffered(1)
    in_specs = [
        _rows(tm, D_MODEL),
        pl.BlockSpec((None, D_MODEL, 2 * D_FF), lambda i: (layer, 0, 0), pipeline_mode=once),
        pl.BlockSpec((N_FF, 3, FF_PAIR), lambda i: (0, 0, 0), pipeline_mode=once),
        pl.BlockSpec((N_FF, 1, FF_PAIR), lambda i: (0, 0, 0), pipeline_mode=once),
        pl.BlockSpec((None, D_FF, D_MODEL), lambda i: (layer, 0, 0), pipeline_mode=once),
        _mod_spec(layer, sample, tm),
        _full((1, D_MODEL)),
        _full((1, D_MODEL)),
    ]
    return pl.pallas_call(
        functools.partial(_ffn_kernel, seq_len),
        grid=(n // tm,),
        in_specs=in_specs,
        out_specs=_rows(tm, D_MODEL),
        out_shape=jax.ShapeDtypeStruct((n, D_MODEL), F32),
        scratch_shapes=[pltpu.VMEM((tm, D_MODEL), BF16), pltpu.VMEM((tm + 2 * SUBLANES, FF_PAIR), F32),
                        pltpu.VMEM((tm + 2 * SUBLANES, FF_PAIR), F32), pltpu.VMEM((tm, D_FF - FF_CHUNK), BF16),
                        pltpu.VMEM((tm, FF_CHUNK), BF16)],
        compiler_params=pltpu.CompilerParams(dimension_semantics=("parallel",), vmem_limit_bytes=FFN_VMEM_LIMIT),
        name="ffn_s" if sample else "ffn_p",
    )(h1, w["w_up"], w["conv_w"], w["conv_b"], w["w_down"], mod, nw2, nw3)


def _pair_chunks(x):
    lead = x.shape[:-1]
    x = x.reshape(lead + (2, N_FF, FF_CHUNK))
    x = jnp.moveaxis(x, -2, 0)
    return x.reshape((N_FF,) + lead + (FF_PAIR,))


def _rope_tables():
    def table(rot_dim):
        t = np.arange(DEC_SEQ)
        n_freq = rot_dim // 4
        inv = 1.0 / (ROPE_THETA ** (np.arange(n_freq) / n_freq))
        ang = np.concatenate([(t // GRID_W)[:, None] * inv[None, :], (t % GRID_W)[:, None] * inv[None, :]], axis=-1)
        cos = np.cos(ang).astype(np.float32)
        sin = np.sin(ang).astype(np.float32)
        reps = LANES // rot_dim
        return (np.tile(np.concatenate([cos, cos], axis=-1), (1, reps)),
                np.tile(np.concatenate([-sin, sin], axis=-1), (1, reps)))
    ca, sa = table(QK_ROPE)
    ch, sh = table(HEAD_DIM)
    rope_lanes = (np.arange(MLA_SLOT) >= QK_NOPE) & (np.arange(MLA_SLOT) < QK_NOPE + QK_ROPE)
    ca = np.where(rope_lanes[None, :], ca, 1.0).astype(np.float32)
    sa = np.where(rope_lanes[None, :], sa, 0.0).astype(np.float32)
    return {"ca": jnp.asarray(ca), "sa": jnp.asarray(sa), "ch": jnp.asarray(ch), "sh": jnp.asarray(sh)}


def _group_mean_matrix(width):
    idx = np.arange(width) // HEAD_DIM
    return jnp.asarray((idx[:, None] == idx[None, :]).astype(np.float32) / HEAD_DIM, BF16)


def kernel(x_prompt, x_sample, c, cache_mla_ckv, cache_mla_kpe, cache_diff_k, cache_diff_v, cache_gqa_k, cache_gqa_v, cache_na_k, cache_na_v, c_ctx, norm_w, w_mod, b_mod, w_in_even, w_out_even, w_uq, q_norm_w, kv_norm_w, w_uk, w_uv, diff_lam, diff_subln_w, w_in_odd, w_out_odd, qk_norm_w, na_rpb, w_up, conv_w, conv_b, w_down):
    rope = _rope_tables()
    n_p = BATCH * SEQ
    n_s = DEC_BATCH * DEC_SEQ
    cvecs = jnp.concatenate([c_ctx[None, :], c, jnp.zeros((MOD_ROWS - 1 - DEC_BATCH, D_MODEL), F32)], axis=0)
    mod = _modulation(cvecs, w_mod, b_mod).reshape(DEPTH, MOD_ROWS, 6, D_MODEL)
    hp = x_prompt.reshape(n_p, D_MODEL)
    hs = x_sample.reshape(n_s, D_MODEL)
    even_states, odd_states = [], []
    w_up_b = w_up.astype(BF16)
    w_down_b = w_down.astype(BF16)
    for l in range(DEPTH):
        i = l // 2
        nw = [norm_w[l, k][None, :] for k in range(4)]
        if l % 2 == 0:
            lam_init = 0.8 - 0.6 * math.exp(-0.3 * l)
            wi = w_in_even[i]
            w_uq3 = w_uq[i].reshape(Q_LORA, H_A, QK_NOPE + QK_ROPE)
            w = {
                "w_in": jnp.concatenate([wi[:, :384], jnp.zeros((D_MODEL, QK_NOPE), F32), wi[:, 384:416],
                                         jnp.zeros((D_MODEL, MLA_SLOT - QK_NOPE - QK_ROPE), F32), wi[:, 416:]],
                                        axis=1).astype(BF16),
                "q_norm_w": q_norm_w[i][None, :],
                "kv_norm_w": kv_norm_w[i][None, :],
                "w_uq": jnp.pad(w_uq3, ((0, 0), (0, 0), (0, MLA_SLOT - QK_NOPE - QK_ROPE))
                                ).reshape(Q_LORA, H_A * MLA_SLOT).astype(BF16),
                "w_uk": jnp.pad(w_uk[i].reshape(KV_LORA, H_A, QK_NOPE), ((0, 0), (0, 0), (0, MLA_SLOT - QK_NOPE))
                                ).reshape(KV_LORA, H_A * MLA_SLOT).astype(BF16),
                "w_uv": w_uv[i].astype(BF16),
                "diff_lam": diff_lam[i],
                "diff_subln_w": diff_subln_w[i][None, :],
            }
            outs_p = _proj_even(hp, mod, l, False, nw[0], w, rope)
            outs_s = _proj_even(hs, mod, l, True, nw[0], w, rope)
            even_states.append(outs_p[6:])
            w_out = w_out_even[i].astype(BF16)
            h1p = _attn_even(outs_p[:6], False, lam_init, None, w, hp, w_out, mod, l, nw[1])
            caches = (cache_mla_ckv[:, i], cache_mla_kpe[:, i],
                      cache_diff_k[:, i].reshape(DEC_BATCH, PAST_LEN, 512),
                      cache_diff_v[:, i].reshape(DEC_BATCH, PAST_LEN, 512))
            h1s = _attn_even(outs_s, True, lam_init, caches, w, hs, w_out, mod, l, nw[1])
        else:
            q_w = jnp.tile(qk_norm_w[i, 0], H_C)[None, :]
            k_w = jnp.tile(qk_norm_w[i, 1], KV_C)[None, :]
            w = {"w_in": w_in_odd[i].astype(BF16), "q_w": q_w, "k_w": k_w,
                 "gq": _group_mean_matrix(512), "gk": _group_mean_matrix(128)}
            outs_p = _proj_odd(hp, mod, l, False, nw[0], w, rope)
            outs_s = _proj_odd(hs, mod, l, True, nw[0], w, rope)
            odd_states.append(outs_p[6:])
            w_out = w_out_odd[i].astype(BF16)
            h1p = _attn_odd_prompt(outs_p[:6], hp, w_out, mod, l, nw[1])
            qc, kc, vc, qn, kn, vn = outs_s
            o_c = _gqa_sample(qc, kc, vc, cache_gqa_k[:, i].reshape(DEC_BATCH, PAST_LEN, 128),
                              cache_gqa_v[:, i].reshape(DEC_BATCH, PAST_LEN, 128))
            o_d = _na_sample(qn, kn, vn, cache_na_k[:, i].reshape(DEC_BATCH, PAST_LEN, 512),
                             cache_na_v[:, i].reshape(DEC_BATCH, PAST_LEN, 512), _na_bias_table(na_rpb[i]))
            h1s = _post_attn([o_c, o_d], hs, w_out, mod, l, True, nw[1])
        wf = {"w_up": w_up_b, "conv_w": _pair_chunks(conv_w[l]),
              "conv_b": _pair_chunks(conv_b[l][None, :]), "w_down": w_down_b}
        hp = _ffn(h1p, wf, mod, l, False, nw[2], nw[3])
        hs = _ffn(h1s, wf, mod, l, True, nw[2], nw[3])

    def stack(states, k, shape):
        return jnp.stack([st[k].reshape((BATCH, SEQ) + shape) for st in states], axis=1)

    new_mla_ckv = stack(even_states, 0, (KV_LORA,))
    new_mla_kpe = stack(even_states, 1, (QK_ROPE,))
    new_diff_k = stack(even_states, 2, (H_B, 2 * DH_B))
    new_diff_v = stack(even_states, 3, (H_B, 2 * DH_B))
    new_gqa_k = stack(odd_states, 0, (KV_C, DH_C))
    new_gqa_v = stack(odd_states, 1, (KV_C, DH_C))
    new_na_k = stack(odd_states, 2, (H_D, DH_D))
    new_na_v = stack(odd_states, 3, (H_D, DH_D))
    return (hp.reshape(BATCH, SEQ, D_MODEL), hs.reshape(DEC_BATCH, DEC_SEQ, D_MODEL),
            new_mla_ckv, new_mla_kpe, new_diff_k, new_diff_v, new_gqa_k, new_gqa_v, new_na_k, new_na_v)
```

```python
import functools
import math

import numpy as np
import jax
import jax.numpy as jnp
from jax import lax
from jax.experimental import pallas as pl
from jax.experimental.pallas import tpu as pltpu

D_MODEL = 1024
BATCH = 32
SEQ = 256
DEPTH = 2
DEC_BATCH = 4
DEC_SEQ = 1024
PAST_LEN = 256
GRID_W = 64
HEAD_DIM = 64
H_A = 8
QK_NOPE = 64
QK_ROPE = 32
V_A = 64
Q_LORA = 256
KV_LORA = 128
H_B = 4
DH_B = HEAD_DIM
H_C = 8
KV_C = 2
DH_C = HEAD_DIM
H_D = 8
DH_D = HEAD_DIM
NA_WIN_ROWS = 8
NA_WIN_COLS = 16
D_FF = 2816
ROPE_THETA = 10000.0
EPS = 1e-6
NEG_INF = -1e30

LANES = 128
MOD_ROWS = 8
ROW_TILE = 1024
FFN_ROW_TILE = 1024
FF_CHUNK = 256
Q_TILE = 256
LATENT_GROUP = 1
CONTEXT_GROUP = 4
VMEM_LIMIT = 48 * 1024 * 1024
FFN_VMEM_LIMIT = 56 * 1024 * 1024

F32 = jnp.float32
BF16 = jnp.bfloat16
LOG2E = math.log2(math.e)


def _cparams(*sem):
    return pltpu.CompilerParams(dimension_semantics=sem, vmem_limit_bytes=VMEM_LIMIT)


def _dot(a, b):
    return jnp.dot(a, b, preferred_element_type=F32)


def _dot_nt(a, b):
    return lax.dot_general(a, b, (((1,), (1,)), ((), ())), preferred_element_type=F32)


def _rms(x, w):
    return x * lax.rsqrt(jnp.mean(x * x, axis=-1, keepdims=True) + EPS) * w


def _group_rms(x, w, gmat):
    x2 = x * x
    hi = x2.astype(BF16)
    lo = (x2 - hi.astype(F32)).astype(BF16)
    ms = _dot(hi, gmat) + _dot(lo, gmat)
    return x * lax.rsqrt(ms + EPS) * w


def _rope(x, cos, sin_signed, half):
    outs = []
    for j in range(x.shape[1] // LANES):
        xc = x[:, j * LANES:(j + 1) * LANES]
        lane = lax.broadcasted_iota(jnp.int32, xc.shape, 1)
        first = (lane % (2 * half)) < half
        partner = jnp.where(first, pltpu.roll(xc, LANES - half, 1), pltpu.roll(xc, half, 1))
        outs.append(xc * cos + partner * sin_signed)
    return outs[0] if len(outs) == 1 else jnp.concatenate(outs, axis=1)


def _softmax_block_rows(n_keys):
    return max(16, min(64, (16 * 1280 // n_keys) // 16 * 16))


def _attention(maps, s_scr, p_scr, group, mxu_sum=False):
    slots = s_scr.shape[0]
    staged = {}

    def stage(i):
        s_ref = s_scr.at[i % slots]
        offs, off = [], 0
        for s in maps[i][0]():
            s_ref[:, off:off + s.shape[1]] = s
            offs.append(off)
            off += s.shape[1]
        staged[i] = (offs, off)

    def softmax(i):
        _, _, c, fix, _ = maps[i]
        n_keys = staged[i][1]
        s_ref, p_ref = s_scr.at[i % slots], p_scr.at[i % slots]
        rb = _softmax_block_rows(n_keys)
        sums = []
        for r0 in range(0, s_ref.shape[0], rb):
            s = s_ref[r0:r0 + rb, 0:n_keys]
            if fix is not None:
                s = fix(s, r0)
            m = jnp.max(s, axis=-1, keepdims=True)
            p = jnp.exp2((s - m) * c)
            if not mxu_sum:
                sums.append(jnp.sum(p, axis=-1, keepdims=True))
            p_ref[r0:r0 + rb, 0:n_keys] = p.astype(BF16)
        return None if mxu_sum else jnp.concatenate(sums, axis=0)

    def weighted_values(i, den):
        _, values, _, _, sink = maps[i]
        p_ref = p_scr.at[i % slots]
        offs, n_keys = staged.pop(i)
        acc = None
        for o, v in zip(offs, values()):
            part = _dot(p_ref[:, o:o + v.shape[0]], v)
            acc = part if acc is None else acc + part
        if mxu_sum:
            den = _dot(p_ref[:, 0:n_keys], jnp.ones((n_keys, LANES), BF16))
        sink(acc / den)

    groups = [range(g, min(g + group, len(maps))) for g in range(0, len(maps), group)]
    for i in groups[0]:
        stage(i)
    for gi, grp in enumerate(groups):
        if gi + 1 < len(groups):
            for i in groups[gi + 1]:
                stage(i)
        dens = [softmax(i) for i in grp]
        for i, den in zip(grp, dens):
            weighted_values(i, den)


def _attn_scratch(tq, n_keys, group):
    return [pltpu.VMEM((2 * group, tq, n_keys), F32), pltpu.VMEM((2 * group, tq, n_keys), BF16)]


def _upper_half(shape):
    return lax.broadcasted_iota(jnp.int32, shape, 1) >= HEAD_DIM


def _keep_half(x, half):
    upper = _upper_half(x.shape)
    return jnp.where(upper if half else ~upper, x, jnp.zeros_like(x))


def _swap_halves(x):
    return jnp.concatenate([x[:, HEAD_DIM:], x[:, :HEAD_DIM]], axis=1)


def _pair_store(o_ref, c0):
    got = {}

    def make(half):
        def sink(o):
            got[half] = o
            if len(got) == 2:
                o_ref[:, c0:c0 + LANES] = jnp.where(_upper_half(o.shape), got[1], got[0]).astype(BF16)
        return sink
    return make(0), make(1)


def _out_proj_residual(o_scr, h_ref, wout_ref, mod_ref, nw1_ref, h1_ref):
    y = _dot(o_scr[...], wout_ref[...])
    gate = mod_ref[2:3, :]
    nw1 = nw1_ref[...]
    for c in range(y.shape[0] // NORM_ROWS):
        rs = slice(c * NORM_ROWS, (c + 1) * NORM_ROWS)
        h1_ref[rs, :] = h_ref[rs, :] + gate * _rms(y[rs], nw1)


def _mod_kernel(c_ref, w_ref, b_ref, o_ref):
    cv = c_ref[...]
    act = cv / (1.0 + jnp.exp(-cv))
    o_ref[...] = _dot(act.astype(BF16), w_ref[...].astype(BF16)) + b_ref[...]


def _modulation(cvecs, w_mod, b_mod):
    tn = 1024
    n = 6 * D_MODEL
    return pl.pallas_call(
        _mod_kernel,
        grid=(DEPTH, n // tn),
        in_specs=[
            pl.BlockSpec((MOD_ROWS, D_MODEL), lambda l, j: (0, 0)),
            pl.BlockSpec((None, D_MODEL, tn), lambda l, j: (l, 0, j)),
            pl.BlockSpec((None, 1, tn), lambda l, j: (l, 0, j)),
        ],
        out_specs=pl.BlockSpec((None, MOD_ROWS, tn), lambda l, j: (l, 0, j)),
        out_shape=jax.ShapeDtypeStruct((DEPTH, MOD_ROWS, n), F32),
        compiler_params=_cparams("parallel", "parallel"),
        name="adaln_mod",
    )(cvecs, w_mod, b_mod.reshape(DEPTH, 1, n))


def _mod_spec(layer, sample, tm):
    if sample:
        per = DEC_SEQ // tm
        return pl.BlockSpec((None, None, 6, D_MODEL), lambda i, *_: (layer, 1 + i // per, 0, 0))
    return pl.BlockSpec((None, None, 6, D_MODEL), lambda i, *_: (layer, 0, 0, 0))


def _full(shape):
    nd = len(shape)
    return pl.BlockSpec(shape, lambda *_: (0,) * nd)


def _rows(tm, width):
    return pl.BlockSpec((tm, width), lambda i, *_: (i, 0))


def _rope_spec(tm):
    per = DEC_SEQ // tm
    return pl.BlockSpec((tm, LANES), lambda i, *_: (i % per, 0))


EVEN_IN = 2048
MLA_SLOT = 128


def _proj_even_kernel(sample, *refs):
    if sample:
        (h_ref, mod_ref, nw_ref, win_ref, qnw_ref, kvnw_ref, wuq_ref, wuk_ref, wuv_ref,
         ca_ref, sa_ref, ch_ref, sh_ref,
         qa_ref, ka_ref, va_ref, qd_ref, kd_ref, vd_ref) = refs
    else:
        (h_ref, mod_ref, nw_ref, win_ref, qnw_ref, kvnw_ref, wuq_ref, wuk_ref, wuv_ref,
         qa_ref, ka_ref, va_ref, qd_ref, kd_ref, vd_ref,
         ckv_st_ref, kpe_st_ref, kd_st_ref, vd_st_ref) = refs
    mod = mod_ref[...]
    u = _rms(h_ref[...], nw_ref[...]) * (1.0 + mod[1:2]) + mod[0:1]
    z = _dot(u.astype(BF16), win_ref[...])
    cq = _rms(z[:, 0:256], qnw_ref[...]).astype(BF16)
    qa = _dot(cq, wuq_ref[...])
    ckv = _rms(z[:, 256:384], kvnw_ref[...])
    ckv_b = ckv.astype(BF16)
    kn = _dot(ckv_b, wuk_ref[...])
    va_ref[...] = _dot(ckv_b, wuv_ref[...]).astype(BF16)
    kpe_slot = z[:, 384:512]
    qd = z[:, 512:1024]
    kd = z[:, 1024:1536]
    vd = z[:, 1536:2048]
    vd_ref[...] = vd.astype(BF16)
    if sample:
        ca, sa, ch, sh = ca_ref[...], sa_ref[...], ch_ref[...], sh_ref[...]
        qa = _rope(qa, ca, sa, QK_ROPE // 2)
        kpe_rot = _rope(kpe_slot, ca, sa, QK_ROPE // 2)
        qd_ref[...] = _rope(qd, ch, sh, DH_B // 2).astype(BF16)
        kd_ref[...] = _rope(kd, ch, sh, DH_B // 2).astype(BF16)
    else:
        kpe_rot = kpe_slot
        qd_ref[...] = qd.astype(BF16)
        kd_ref[...] = kd.astype(BF16)
        ckv_st_ref[...] = ckv
        kpe_st_ref[...] = kpe_slot[:, QK_NOPE:QK_NOPE + QK_ROPE]
        kd_st_ref[...] = kd
        vd_st_ref[...] = vd
    qa_ref[...] = qa.astype(BF16)
    for hd in range(H_A):
        sl = slice(hd * MLA_SLOT, (hd + 1) * MLA_SLOT)
        ka_ref[:, sl] = (kn[:, sl] + kpe_rot).astype(BF16)


def _proj_even(h, mod, layer, sample, nw, w, rope):
    n = h.shape[0]
    tm = ROW_TILE
    wide = H_A * MLA_SLOT
    ins = [h, mod, nw, w["w_in"], w["q_norm_w"], w["kv_norm_w"], w["w_uq"], w["w_uk"], w["w_uv"]]
    in_specs = [_rows(tm, D_MODEL), _mod_spec(layer, sample, tm), _full((1, D_MODEL)), _full((D_MODEL, EVEN_IN)),
                _full((1, Q_LORA)), _full((1, KV_LORA)), _full((Q_LORA, wide)), _full((KV_LORA, wide)),
                _full((KV_LORA, 512))]
    widths = [wide, wide, 512, 512, 512, 512]
    out_shape = [jax.ShapeDtypeStruct((n, wd), BF16) for wd in widths]
    out_specs = [_rows(tm, wd) for wd in widths]
    if sample:
        ins += [rope["ca"], rope["sa"], rope["ch"], rope["sh"]]
        in_specs += [_rope_spec(tm)] * 4
    else:
        for wd in (KV_LORA, QK_ROPE, 512, 512):
            out_shape.append(jax.ShapeDtypeStruct((n, wd), F32))
            out_specs.append(_rows(tm, wd))
    return pl.pallas_call(
        functools.partial(_proj_even_kernel, sample),
        grid=(n // tm,),
        in_specs=in_specs,
        out_specs=out_specs,
        out_shape=out_shape,
        compiler_params=_cparams("parallel"),
        name="proj_even_s" if sample else "proj_even_p",
    )(*ins)


def _attn_even_kernel(sample, lam_init, *refs):
    if sample:
        (qa_ref, ka_ref, va_ref, qd_ref, kd_ref, vd_ref,
         cckv_ref, ckpe_ref, cdk_ref, cdv_ref, wuk_ref, wuv_ref, lam_ref, sub_ref,
         h_ref, wout_ref, mod_ref, nw1_ref, h1_ref, s_scr, p_scr, o_ref) = refs
    else:
        (qa_ref, ka_ref, va_ref, qd_ref, kd_ref, vd_ref, lam_ref, sub_ref,
         h_ref, wout_ref, mod_ref, nw1_ref, h1_ref, s_scr, p_scr, o_ref) = refs
    lf = lam_ref[...]
    lam = (jnp.exp(jnp.sum(lf[0:1] * lf[1:2], axis=-1, keepdims=True))
           - jnp.exp(jnp.sum(lf[2:3] * lf[3:4], axis=-1, keepdims=True)) + lam_init)
    if sample:
        cckv = cckv_ref[...].astype(BF16)
        kn_ctx = _dot(cckv, wuk_ref[...])
        va_ctx = _dot(cckv, wuv_ref[...]).astype(BF16)
        n_ctx = cckv.shape[0]
        kpe_ctx = jnp.concatenate([jnp.zeros((n_ctx, QK_NOPE), F32), ckpe_ref[...],
                                   jnp.zeros((n_ctx, MLA_SLOT - QK_NOPE - QK_ROPE), F32)], axis=1)
        kd_ctx = cdk_ref[...].astype(BF16)
        vd_ctx = cdv_ref[...].astype(BF16)
    maps = []
    c_a = (QK_NOPE + QK_ROPE) ** -0.5 * LOG2E
    for hd in range(H_A):
        sl = slice(hd * MLA_SLOT, (hd + 1) * MLA_SLOT)
        if hd % 2 == 0:
            sinks = _pair_store(o_ref, hd * V_A)

        def scores(sl=sl):
            q = qa_ref[:, sl]
            out = [_dot_nt(q, ka_ref[:, sl])]
            if sample:
                out.append(_dot_nt(q, (kn_ctx[:, sl] + kpe_ctx).astype(BF16)))
            return out

        def values(vs=slice(hd // 2 * LANES, (hd // 2 + 1) * LANES)):
            return [va_ref[:, vs]] + ([va_ctx[:, vs]] if sample else [])

        maps.append((scores, values, c_a, None, sinks[hd % 2]))
    c_b = DH_B ** -0.5 * LOG2E
    base = H_A * V_A
    sub_w = sub_ref[...]
    for hd in range(H_B):
        hs = slice(hd * 2 * DH_B, (hd + 1) * 2 * DH_B)
        outs = []

        def sink(o, outs=outs, hs=hs):
            outs.append(o)
            if len(outs) == 2:
                ob = _rms(outs[0] - lam * outs[1], sub_w) * (1.0 - lam_init)
                o_ref[:, base + hs.start:base + hs.stop] = ob.astype(BF16)

        def values(hs=hs):
            return [vd_ref[:, hs]] + ([vd_ctx[:, hs]] if sample else [])

        for comp in range(2):
            def scores(hs=hs, comp=comp):
                q = _keep_half(qd_ref[:, hs], comp)
                out = [_dot_nt(q, kd_ref[:, hs])]
                if sample:
                    out.append(_dot_nt(q, kd_ctx[:, hs]))
                return out

            maps.append((scores, values, c_b, None, sink))
    _attention(maps, s_scr, p_scr, s_scr.shape[0] // 2, mxu_sum=not sample)
    _out_proj_residual(o_ref, h_ref, wout_ref, mod_ref, nw1_ref, h1_ref)


def _attn_even(p, sample, lam_init, caches, w, h, w_out, mod, layer, nw1):
    qa, ka, va, qd, kd, vd = p
    n = qa.shape[0]
    wide = H_A * MLA_SLOT
    if sample:
        tq, per = Q_TILE, DEC_SEQ // Q_TILE
        grid = (DEC_BATCH, per)
        qspec = lambda wd: pl.BlockSpec((tq, wd), lambda b, i: (b * per + i, 0))
        kspec = lambda wd: pl.BlockSpec((DEC_SEQ, wd), lambda b, i: (b, 0))
        cspec = lambda wd: pl.BlockSpec((None, PAST_LEN, wd), lambda b, i: (b, 0, 0))
        ins = [qa, ka, va, qd, kd, vd, *caches, w["w_uk"], w["w_uv"], w["diff_lam"], w["diff_subln_w"]]
        in_specs = [qspec(wide), kspec(wide), kspec(512), qspec(512), kspec(512), kspec(512),
                    cspec(KV_LORA), cspec(QK_ROPE), cspec(512), cspec(512),
                    _full((KV_LORA, wide)), _full((KV_LORA, 512)), _full((4, DH_B)), _full((1, 2 * DH_B))]
        out_spec = qspec(D_MODEL)
        mod_spec = pl.BlockSpec((None, None, 6, D_MODEL), lambda b, i: (layer, 1 + b, 0, 0))
        sem = ("parallel", "parallel")
        n_keys, group = DEC_SEQ + PAST_LEN, LATENT_GROUP
    else:
        tq = SEQ
        grid = (BATCH,)
        spec = lambda wd: pl.BlockSpec((SEQ, wd), lambda b: (b, 0))
        ins = [qa, ka, va, qd, kd, vd, w["diff_lam"], w["diff_subln_w"]]
        in_specs = [spec(wide), spec(wide), spec(512), spec(512), spec(512), spec(512),
                    _full((4, DH_B)), _full((1, 2 * DH_B))]
        out_spec = spec(D_MODEL)
        mod_spec = pl.BlockSpec((None, None, 6, D_MODEL), lambda b: (layer, 0, 0, 0))
        sem = ("parallel",)
        n_keys, group = SEQ, CONTEXT_GROUP
    ins += [h, w_out, mod, nw1]
    in_specs += [out_spec, _full((D_MODEL, D_MODEL)), mod_spec, _full((1, D_MODEL))]
    return pl.pallas_call(
        functools.partial(_attn_even_kernel, sample, lam_init),
        grid=grid,
        in_specs=in_specs,
        out_specs=out_spec,
        out_shape=jax.ShapeDtypeStruct((n, D_MODEL), F32),
        scratch_shapes=_attn_scratch(tq, n_keys, group) + [pltpu.VMEM((tq, D_MODEL), BF16)],
        compiler_params=_cparams(*sem),
        name="attn_even_s" if sample else "attn_even_p",
    )(*ins)


ODD_IN = 2304


def _proj_odd_kernel(sample, *refs):
    if sample:
        (h_ref, mod_ref, nw_ref, win_ref, qw_ref, kw_ref, gq_ref, gk_ref, ch_ref, sh_ref,
         qc_ref, kc_ref, vc_ref, qn_ref, kn_ref, vn_ref) = refs
    else:
        (h_ref, mod_ref, nw_ref, win_ref, qw_ref, kw_ref, gq_ref, gk_ref,
         qc_ref, kc_ref, vc_ref, qn_ref, kn_ref, vn_ref,
         kc_st_ref, vc_st_ref, kn_st_ref, vn_st_ref) = refs
    mod = mod_ref[...]
    u = _rms(h_ref[...], nw_ref[...]) * (1.0 + mod[1:2]) + mod[0:1]
    z = _dot(u.astype(BF16), win_ref[...])
    qc = _group_rms(z[:, 0:512], qw_ref[...], gq_ref[...])
    kc = _group_rms(z[:, 512:640], kw_ref[...], gk_ref[...])
    vc = z[:, 640:768]
    kn = z[:, 1280:1792]
    vn = z[:, 1792:2304]
    vc_ref[...] = vc.astype(BF16)
    qn_ref[...] = z[:, 768:1280].astype(BF16)
    kn_ref[...] = kn.astype(BF16)
    vn_ref[...] = vn.astype(BF16)
    if sample:
        ch, sh = ch_ref[...], sh_ref[...]
        qc_ref[...] = _rope(qc, ch, sh, DH_C // 2).astype(BF16)
        kc_ref[...] = _rope(kc, ch, sh, DH_C // 2).astype(BF16)
    else:
        qc_ref[...] = qc.astype(BF16)
        kc_ref[...] = kc.astype(BF16)
        kc_st_ref[...] = kc
        vc_st_ref[...] = vc
        kn_st_ref[...] = kn
        vn_st_ref[...] = vn


def _proj_odd(h, mod, layer, sample, nw, w, rope):
    n = h.shape[0]
    tm = ROW_TILE
    ins = [h, mod, nw, w["w_in"], w["q_w"], w["k_w"], w["gq"], w["gk"]]
    in_specs = [_rows(tm, D_MODEL), _mod_spec(layer, sample, tm), _full((1, D_MODEL)), _full((D_MODEL, ODD_IN)),
                _full((1, 512)), _full((1, 128)), _full((512, 512)), _full((128, 128))]
    widths = [512, 128, 128, 512, 512, 512]
    out_shape = [jax.ShapeDtypeStruct((n, wd), BF16) for wd in widths]
    out_specs = [_rows(tm, wd) for wd in widths]
    if sample:
        ins += [rope["ch"], rope["sh"]]
        in_specs += [_rope_spec(tm)] * 2
    else:
        for wd in (128, 128, 512, 512):
            out_shape.append(jax.ShapeDtypeStruct((n, wd), F32))
            out_specs.append(_rows(tm, wd))
    return pl.pallas_call(
        functools.partial(_proj_odd_kernel, sample),
        grid=(n // tm,),
        in_specs=in_specs,
        out_specs=out_specs,
        out_shape=out_shape,
        compiler_params=_cparams("parallel"),
        name="proj_odd_s" if sample else "proj_odd_p",
    )(*ins)


def _attn_odd_prompt_kernel(qc_ref, kc_ref, vc_ref, qn_ref, kn_ref, vn_ref, h_ref, wout_ref, mod_ref, nw1_ref,
                            h1_ref, s_scr, p_scr, o_ref):
    c = DH_C ** -0.5 * LOG2E
    group = H_C // KV_C
    kc = (kc_ref[...], _swap_halves(kc_ref[...]))
    vc = (vc_ref[...], _swap_halves(vc_ref[...]))
    maps = []
    for hd in range(H_C):
        half, swap = hd % 2, (hd // group) != (hd % 2)
        ps = slice(hd // 2 * LANES, (hd // 2 + 1) * LANES)
        if half == 0:
            sinks = _pair_store(o_ref, ps.start)
        maps.append((lambda ps=ps, half=half, swap=swap: [_dot_nt(_keep_half(qc_ref[:, ps], half), kc[swap])],
                     lambda swap=swap: [vc[swap]], c, None, sinks[half]))
    base = H_C * DH_C
    c = DH_D ** -0.5 * LOG2E
    for hd in range(H_D):
        half = hd % 2
        ps = slice(hd // 2 * LANES, (hd // 2 + 1) * LANES)
        if half == 0:
            sinks = _pair_store(o_ref, base + ps.start)
        maps.append((lambda ps=ps, half=half: [_dot_nt(_keep_half(qn_ref[:, ps], half), kn_ref[:, ps])],
                     lambda ps=ps: [vn_ref[:, ps]], c, None, sinks[half]))
    _attention(maps, s_scr, p_scr, s_scr.shape[0] // 2, mxu_sum=True)
    _out_proj_residual(o_ref, h_ref, wout_ref, mod_ref, nw1_ref, h1_ref)


def _attn_odd_prompt(p, h, w_out, mod, layer, nw1):
    qc, kc, vc, qn, kn, vn = p
    spec = lambda wd: pl.BlockSpec((SEQ, wd), lambda b: (b, 0))
    return pl.pallas_call(
        _attn_odd_prompt_kernel,
        grid=(BATCH,),
        in_specs=[spec(512), spec(128), spec(128), spec(512), spec(512), spec(512),
                  spec(D_MODEL), _full((D_MODEL, D_MODEL)),
                  pl.BlockSpec((None, None, 6, D_MODEL), lambda b: (layer, 0, 0, 0)), _full((1, D_MODEL))],
        out_specs=spec(D_MODEL),
        out_shape=jax.ShapeDtypeStruct((qc.shape[0], D_MODEL), F32),
        scratch_shapes=_attn_scratch(SEQ, SEQ, CONTEXT_GROUP) + [pltpu.VMEM((SEQ, D_MODEL), BF16)],
        compiler_params=_cparams("parallel"),
        name="attn_odd_p",
    )(qc, kc, vc, qn, kn, vn, h, w_out, mod, nw1)


def _gqa_sample_kernel(q_ref, k_ref, v_ref, ck_ref, cv_ref, o_ref, s_scr, p_scr):
    c = DH_C ** -0.5 * LOG2E
    group = H_C // KV_C
    k_loc = (k_ref[...], _swap_halves(k_ref[...]))
    v_loc = (v_ref[...], _swap_halves(v_ref[...]))
    k_ctx = ck_ref[...].astype(BF16)
    v_ctx = cv_ref[...].astype(BF16)
    k_ctx = (k_ctx, _swap_halves(k_ctx))
    v_ctx = (v_ctx, _swap_halves(v_ctx))
    maps = []
    for hd in range(H_C):
        half, swap = hd % 2, (hd // group) != (hd % 2)
        ps = slice(hd // 2 * LANES, (hd // 2 + 1) * LANES)
        if half == 0:
            sinks = _pair_store(o_ref, ps.start)

        def scores(ps=ps, half=half, swap=swap):
            q = _keep_half(q_ref[:, ps], half)
            return [_dot_nt(q, k_loc[swap]), _dot_nt(q, k_ctx[swap])]

        maps.append((scores, lambda swap=swap: [v_loc[swap], v_ctx[swap]], c, None, sinks[half]))
    _attention(maps, s_scr, p_scr, s_scr.shape[0] // 2)


def _gqa_sample(qc, kc, vc, cache_k, cache_v):
    tq, per = Q_TILE, DEC_SEQ // Q_TILE
    return pl.pallas_call(
        _gqa_sample_kernel,
        grid=(DEC_BATCH, per),
        in_specs=[pl.BlockSpec((tq, 512), lambda b, i: (b * per + i, 0)),
                  pl.BlockSpec((DEC_SEQ, 128), lambda b, i: (b, 0)),
                  pl.BlockSpec((DEC_SEQ, 128), lambda b, i: (b, 0)),
                  pl.BlockSpec((None, PAST_LEN, 128), lambda b, i: (b, 0, 0)),
                  pl.BlockSpec((None, PAST_LEN, 128), lambda b, i: (b, 0, 0))],
        out_specs=pl.BlockSpec((tq, 512), lambda b, i: (b * per + i, 0)),
        out_shape=jax.ShapeDtypeStruct((qc.shape[0], 512), BF16),
        scratch_shapes=_attn_scratch(tq, DEC_SEQ + PAST_LEN, LATENT_GROUP),
        compiler_params=_cparams("parallel", "parallel"),
        name="gqa_s",
    )(qc, kc, vc, cache_k, cache_v)


NA_ROWS = DEC_SEQ // GRID_W
NA_KR = min(NA_WIN_ROWS, NA_ROWS)
NA_LOC = NA_KR * GRID_W


def _na_sample_kernel(q_ref, k_ref, v_ref, ck_ref, cv_ref, bias_ref, o_ref, s_scr, p_scr):
    r = pl.program_id(1)
    rs = jnp.clip(r - NA_KR // 2, 0, NA_ROWS - NA_KR)
    start = pl.multiple_of(rs * GRID_W, GRID_W)
    scale = DH_D ** -0.5
    k_loc = k_ref[pl.ds(start, NA_LOC), :]
    v_loc = v_ref[pl.ds(start, NA_LOC), :]
    k_ctx = ck_ref[...].astype(BF16)
    v_ctx = cv_ref[...].astype(BF16)
    rb = _softmax_block_rows(NA_LOC + PAST_LEN)
    col_ok = {}
    for r0 in range(0, GRID_W, rb):
        wq = lax.broadcasted_iota(jnp.int32, (rb, NA_LOC), 0) + r0
        wk = lax.broadcasted_iota(jnp.int32, (rb, NA_LOC), 1) % GRID_W
        cs = jnp.clip(wq - NA_WIN_COLS // 2, 0, GRID_W - NA_WIN_COLS)
        col_ok[r0] = (wk >= cs) & (wk < cs + NA_WIN_COLS)
    maps = []
    for hd in range(H_D):
        half = hd % 2
        ps = slice(hd // 2 * LANES, (hd // 2 + 1) * LANES)
        if half == 0:
            sinks = _pair_store(o_ref, ps.start)

        def fix(s, r0, hd=hd):
            loc = s[:, :NA_LOC] * scale + bias_ref[hd, r0:r0 + rb, :]
            return jnp.concatenate([jnp.where(col_ok[r0], loc, NEG_INF), s[:, NA_LOC:] * scale], axis=1)

        def scores(ps=ps, half=half):
            q = _keep_half(q_ref[:, ps], half)
            return [_dot_nt(q, k_loc[:, ps]), _dot_nt(q, k_ctx[:, ps])]

        maps.append((scores, lambda ps=ps: [v_loc[:, ps], v_ctx[:, ps]], LOG2E, fix, sinks[half]))
    _attention(maps, s_scr, p_scr, s_scr.shape[0] // 2)


def _na_sample(qn, kn, vn, cache_k, cache_v, bias):
    def bias_map(b, r):
        rs = jnp.clip(r - NA_KR // 2, 0, NA_ROWS - NA_KR)
        return (0, rs - r + NA_WIN_ROWS - 1, 0, 0)
    return pl.pallas_call(
        _na_sample_kernel,
        grid=(DEC_BATCH, NA_ROWS),
        in_specs=[pl.BlockSpec((GRID_W, 512), lambda b, r: (b * NA_ROWS + r, 0)),
                  pl.BlockSpec((DEC_SEQ, 512), lambda b, r: (b, 0)),
                  pl.BlockSpec((DEC_SEQ, 512), lambda b, r: (b, 0)),
                  pl.BlockSpec((None, PAST_LEN, 512), lambda b, r: (b, 0, 0)),
                  pl.BlockSpec((None, PAST_LEN, 512), lambda b, r: (b, 0, 0)),
                  pl.BlockSpec((H_D, None, GRID_W, NA_LOC), bias_map)],
        out_specs=pl.BlockSpec((GRID_W, 512), lambda b, r: (b * NA_ROWS + r, 0)),
        out_shape=jax.ShapeDtypeStruct((qn.shape[0], 512), BF16),
        scratch_shapes=_attn_scratch(GRID_W, NA_LOC + PAST_LEN, CONTEXT_GROUP),
        compiler_params=_cparams("parallel", "parallel"),
        name="na_s",
    )(qn, kn, vn, cache_k, cache_v, bias)


def _na_bias_table(rpb):
    edge = GRID_W - NA_WIN_COLS
    n_dr = 2 * NA_WIN_ROWS - 1
    v = jnp.pad(rpb.astype(F32), ((0, 0), (0, 0), (edge, edge + 1)), mode="edge")
    skew = jnp.tile(v, (1, 1, GRID_W))[:, :, :GRID_W * (2 * GRID_W - 1)]
    toep = skew.reshape(H_D, n_dr, GRID_W, 2 * GRID_W - 1)[:, :, :, GRID_W - 1:]
    flat = toep.transpose(0, 2, 1, 3).reshape(H_D, GRID_W, n_dr * GRID_W)
    return jnp.stack([flat[:, :, d0 * GRID_W:d0 * GRID_W + NA_LOC] for d0 in range(NA_WIN_ROWS)], axis=1)


def _post_attn_kernel(n_parts, *refs):
    o_refs = refs[:n_parts]
    h_ref, wout_ref, mod_ref, nw1_ref, h1_ref = refs[n_parts:]
    y = None
    off = 0
    for o_ref in o_refs:
        wd = o_ref.shape[1]
        part = _dot(o_ref[...], wout_ref[off:off + wd, :])
        y = part if y is None else y + part
        off += wd
    mod = mod_ref[...]
    h1_ref[...] = h_ref[...] + mod[2:3] * _rms(y, nw1_ref[...])


def _post_attn(o_parts, h, w_out, mod, layer, sample, nw1):
    n = h.shape[0]
    tm = ROW_TILE
    in_specs = [_rows(tm, o.shape[1]) for o in o_parts]
    in_specs += [_rows(tm, D_MODEL), _full((D_MODEL, D_MODEL)), _mod_spec(layer, sample, tm), _full((1, D_MODEL))]
    return pl.pallas_call(
        functools.partial(_post_attn_kernel, len(o_parts)),
        grid=(n // tm,),
        in_specs=in_specs,
        out_specs=_rows(tm, D_MODEL),
        out_shape=jax.ShapeDtypeStruct((n, D_MODEL), F32),
        compiler_params=_cparams("parallel"),
        name="post_attn_s" if sample else "post_attn_p",
    )(*o_parts, h, w_out, mod, nw1)


FF_PAIR = 2 * FF_CHUNK
N_FF = D_FF // FF_CHUNK
SUBLANES = 8
ACT_TILES = 4
ROW_BLOCK = 1024
NORM_ROWS = 32


def _ffn_kernel(seq_len, h_ref, wup_ref, cw_ref, cb_ref, wd_ref, mod_ref, nw2_ref, nw3_ref, o_ref,
                u_ref, z0_ref, z1_ref, a_ref, a_last_ref):
    tm = h_ref.shape[0]
    n_blocks = tm // ROW_BLOCK
    rows = ACT_TILES * SUBLANES
    mod = mod_ref[...]
    sub = lax.broadcasted_iota(jnp.int32, (SUBLANES, LANES), 0)
    zero_rows = jnp.zeros((SUBLANES, FF_PAIR), F32)
    for z_ref in (z0_ref, z1_ref):
        z_ref[0:SUBLANES, :] = zero_rows
        z_ref[SUBLANES + tm:2 * SUBLANES + tm, :] = zero_rows

    nw2 = nw2_ref[...]

    def pre_norm(blk):
        for c in range(ROW_BLOCK // NORM_ROWS):
            rs = slice(blk * ROW_BLOCK + c * NORM_ROWS, blk * ROW_BLOCK + (c + 1) * NORM_ROWS)
            u = _rms(h_ref[rs, :], nw2) * (1.0 + mod[4:5]) + mod[3:4]
            u_ref[rs, :] = u.astype(BF16)

    def up(j, z_ref, blk):
        r0 = blk * ROW_BLOCK
        u = u_ref[r0:r0 + ROW_BLOCK, :]
        rows_ = slice(SUBLANES + r0, SUBLANES + r0 + ROW_BLOCK)
        for half in range(2):
            c0 = half * D_FF + j * FF_CHUNK
            if not isinstance(c0, int):
                c0 = pl.multiple_of(c0, FF_CHUNK)
            z_ref[rows_, half * FF_CHUNK:(half + 1) * FF_CHUNK] = _dot(u, wup_ref[:, pl.ds(c0, FF_CHUNK)])

    def act(j, z_ref, col, blk, dst_ref=a_ref):
        cw = cw_ref[j]
        cb = cb_ref[j]
        for lc in range(FF_CHUNK // LANES):
            taps = []
            for lane0 in (lc * LANES, FF_CHUNK + lc * LANES):
                lanes = slice(lane0, lane0 + LANES)
                taps.append([jnp.broadcast_to(cw[k:k + 1, lanes], (SUBLANES, LANES)) for k in range(3)]
                            + [jnp.broadcast_to(cb[:, lanes], (SUBLANES, LANES))])
            for c in range(ROW_BLOCK // rows):
                r = blk * ROW_BLOCK + c * rows
                first = r % seq_len == 0
                last = (r + rows) % seq_len == 0

                def conv(lane0, tap):
                    ext = z_ref[r:r + rows + 2 * SUBLANES, lane0:lane0 + LANES]
                    tiles = [ext[t * SUBLANES:(t + 1) * SUBLANES] for t in range(ACT_TILES + 2)]
                    down = [pltpu.roll(t, 1, 0) for t in tiles[:-1]]
                    up_ = [pltpu.roll(t, SUBLANES - 1, 0) for t in tiles[1:]]
                    out = []
                    for t in range(ACT_TILES):
                        above = 0.0 if (first and t == 0) else down[t]
                        below = 0.0 if (last and t == ACT_TILES - 1) else up_[t + 1]
                        prev = jnp.where(sub == 0, above, down[t + 1])
                        nxt = jnp.where(sub == SUBLANES - 1, below, up_[t])
                        out.append(prev * tap[0] + tiles[t + 1] * tap[1] + nxt * tap[2] + tap[3])
                    return jnp.concatenate(out, axis=0)

                g = conv(lc * LANES, taps[0])
                v = conv(FF_CHUNK + lc * LANES, taps[1])
                a = (g / (1.0 + jnp.exp2(g * -LOG2E))) * v
                lane = col + lc * LANES
                if not isinstance(lane, int):
                    lane = pl.multiple_of(lane, LANES)
                dst_ref[r:r + rows, pl.ds(lane, LANES)] = a.astype(BF16)

    for blk in range(n_blocks):
        pre_norm(blk)
        up(0, z0_ref, blk)

    def pair(i, carry):
        j = 2 * i
        col = pl.multiple_of(j * FF_CHUNK, FF_CHUNK)
        for blk in range(n_blocks):
            up(j + 1, z1_ref, blk)
            act(j, z0_ref, col, blk)
        for blk in range(n_blocks):
            up(j + 2, z0_ref, blk)
            act(j + 1, z1_ref, col + FF_CHUNK, blk)
        return carry

    lax.fori_loop(0, (N_FF - 1) // 2, pair, 0)
    nw3 = nw3_ref[...]
    k_main = (N_FF - 1) * FF_CHUNK
    for blk in range(n_blocks):
        r0 = blk * ROW_BLOCK
        y = _dot(a_ref[r0:r0 + ROW_BLOCK, 0:k_main], wd_ref[0:k_main, :])
        act(N_FF - 1, z0_ref, 0, blk, a_last_ref)
        y = y + _dot(a_last_ref[r0:r0 + ROW_BLOCK, :], wd_ref[k_main:D_FF, :])
        for c in range(ROW_BLOCK // NORM_ROWS):
            rs = slice(r0 + c * NORM_ROWS, r0 + (c + 1) * NORM_ROWS)
            o_ref[rs, :] = h_ref[rs, :] + mod[5:6] * _rms(y[c * NORM_ROWS:(c + 1) * NORM_ROWS], nw3)


def _ffn(h1, w, mod, layer, sample, nw2, nw3):
    n = h1.shape[0]
    tm = FFN_ROW_TILE
    seq_len = DEC_SEQ if sample else SEQ
    once = pl.Buffered(1)
    in_specs = [
        _rows(tm, D_MODEL),
        pl.BlockSpec((None, D_MODEL, 2 * D_FF), lambda i: (layer, 0, 0), pipeline_mode=once),
        pl.BlockSpec((N_FF, 3, FF_PAIR), lambda i: (0, 0, 0), pipeline_mode=once),
        pl.BlockSpec((N_FF, 1, FF_PAIR), lambda i: (0, 0, 0), pipeline_mode=once),
        pl.BlockSpec((None, D_FF, D_MODEL), lambda i: (layer, 0, 0), pipeline_mode=once),
        _mod_spec(layer, sample, tm),
        _full((1, D_MODEL)),
        _full((1, D_MODEL)),
    ]
    return pl.pallas_call(
        functools.partial(_ffn_kernel, seq_len),
        grid=(n // tm,),
        in_specs=in_specs,
        out_specs=_rows(tm, D_MODEL),
        out_shape=jax.ShapeDtypeStruct((n, D_MODEL), F32),
        scratch_shapes=[pltpu.VMEM((tm, D_MODEL), BF16), pltpu.VMEM((tm + 2 * SUBLANES, FF_PAIR), F32),
                        pltpu.VMEM((tm + 2 * SUBLANES, FF_PAIR), F32), pltpu.VMEM((tm, D_FF - FF_CHUNK), BF16),
                        pltpu.VMEM((tm, FF_CHUNK), BF16)],
        compiler_params=pltpu.CompilerParams(dimension_semantics=("parallel",), vmem_limit_bytes=FFN_VMEM_LIMIT),
        name="ffn_s" if sample else "ffn_p",
    )(h1, w["w_up"], w["conv_w"], w["conv_b"], w["w_down"], mod, nw2, nw3)


def _pair_chunks(x):
    lead = x.shape[:-1]
    x = x.reshape(lead + (2, N_FF, FF_CHUNK))
    x = jnp.moveaxis(x, -2, 0)
    return x.reshape((N_FF,) + lead + (FF_PAIR,))


def _rope_tables():
    def table(rot_dim):
        t = np.arange(DEC_SEQ)
        n_freq = rot_dim // 4
        inv = 1.0 / (ROPE_THETA ** (np.arange(n_freq) / n_freq))
        ang = np.concatenate([(t // GRID_W)[:, None] * inv[None, :], (t % GRID_W)[:, None] * inv[None, :]], axis=-1)
        cos = np.cos(ang).astype(np.float32)
        sin = np.sin(ang).astype(np.float32)
        reps = LANES // rot_dim
        return (np.tile(np.concatenate([cos, cos], axis=-1), (1, reps)),
                np.tile(np.concatenate([-sin, sin], axis=-1), (1, reps)))
    ca, sa = table(QK_ROPE)
    ch, sh = table(HEAD_DIM)
    rope_lanes = (np.arange(MLA_SLOT) >= QK_NOPE) & (np.arange(MLA_SLOT) < QK_NOPE + QK_ROPE)
    ca = np.where(rope_lanes[None, :], ca, 1.0).astype(np.float32)
    sa = np.where(rope_lanes[None, :], sa, 0.0).astype(np.float32)
    return {"ca": jnp.asarray(ca), "sa": jnp.asarray(sa), "ch": jnp.asarray(ch), "sh": jnp.asarray(sh)}


def _group_mean_matrix(width):
    idx = np.arange(width) // HEAD_DIM
    return jnp.asarray((idx[:, None] == idx[None, :]).astype(np.float32) / HEAD_DIM, BF16)


def kernel(x_prompt, x_sample, c, cache_mla_ckv, cache_mla_kpe, cache_diff_k, cache_diff_v, cache_gqa_k, cache_gqa_v, cache_na_k, cache_na_v, c_ctx, norm_w, w_mod, b_mod, w_in_even, w_out_even, w_uq, q_norm_w, kv_norm_w, w_uk, w_uv, diff_lam, diff_subln_w, w_in_odd, w_out_odd, qk_norm_w, na_rpb, w_up, conv_w, conv_b, w_down):
    rope = _rope_tables()
    n_p = BATCH * SEQ
    n_s = DEC_BATCH * DEC_SEQ
    cvecs = jnp.concatenate([c_ctx[None, :], c, jnp.zeros((MOD_ROWS - 1 - DEC_BATCH, D_MODEL), F32)], axis=0)
    mod = _modulation(cvecs, w_mod, b_mod).reshape(DEPTH, MOD_ROWS, 6, D_MODEL)
    hp = x_prompt.reshape(n_p, D_MODEL)
    hs = x_sample.reshape(n_s, D_MODEL)
    even_states, odd_states = [], []
    w_up_b = w_up.astype(BF16)
    w_down_b = w_down.astype(BF16)
    for l in range(DEPTH):
        i = l // 2
        nw = [norm_w[l, k][None, :] for k in range(4)]
        if l % 2 == 0:
            lam_init = 0.8 - 0.6 * math.exp(-0.3 * l)
            wi = w_in_even[i]
            w_uq3 = w_uq[i].reshape(Q_LORA, H_A, QK_NOPE + QK_ROPE)
            w = {
                "w_in": jnp.concatenate([wi[:, :384], jnp.zeros((D_MODEL, QK_NOPE), F32), wi[:, 384:416],
                                         jnp.zeros((D_MODEL, MLA_SLOT - QK_NOPE - QK_ROPE), F32), wi[:, 416:]],
                                        axis=1).astype(BF16),
                "q_norm_w": q_norm_w[i][None, :],
                "kv_norm_w": kv_norm_w[i][None, :],
                "w_uq": jnp.pad(w_uq3, ((0, 0), (0, 0), (0, MLA_SLOT - QK_NOPE - QK_ROPE))
                                ).reshape(Q_LORA, H_A * MLA_SLOT).astype(BF16),
                "w_uk": jnp.pad(w_uk[i].reshape(KV_LORA, H_A, QK_NOPE), ((0, 0), (0, 0), (0, MLA_SLOT - QK_NOPE))
                                ).reshape(KV_LORA, H_A * MLA_SLOT).astype(BF16),
                "w_uv": w_uv[i].astype(BF16),
                "diff_lam": diff_lam[i],
                "diff_subln_w": diff_subln_w[i][None, :],
            }
            outs_p = _proj_even(hp, mod, l, False, nw[0], w, rope)
            outs_s = _proj_even(hs, mod, l, True, nw[0], w, rope)
            even_states.append(outs_p[6:])
            w_out = w_out_even[i].astype(BF16)
            h1p = _attn_even(outs_p[:6], False, lam_init, None, w, hp, w_out, mod, l, nw[1])
            caches = (cache_mla_ckv[:, i], cache_mla_kpe[:, i],
                      cache_diff_k[:, i].reshape(DEC_BATCH, PAST_LEN, 512),
                      cache_diff_v[:, i].reshape(DEC_BATCH, PAST_LEN, 512))
            h1s = _attn_even(outs_s, True, lam_init, caches, w, hs, w_out, mod, l, nw[1])
        else:
            q_w = jnp.tile(qk_norm_w[i, 0], H_C)[None, :]
            k_w = jnp.tile(qk_norm_w[i, 1], KV_C)[None, :]
            w = {"w_in": w_in_odd[i].astype(BF16), "q_w": q_w, "k_w": k_w,
                 "gq": _group_mean_matrix(512), "gk": _group_mean_matrix(128)}
            outs_p = _proj_odd(hp, mod, l, False, nw[0], w, rope)
            outs_s = _proj_odd(hs, mod, l, True, nw[0], w, rope)
            odd_states.append(outs_p[6:])
            w_out = w_out_odd[i].astype(BF16)
            h1p = _attn_odd_prompt(outs_p[:6], hp, w_out, mod, l, nw[1])
            qc, kc, vc, qn, kn, vn = outs_s
            o_c = _gqa_sample(qc, kc, vc, cache_gqa_k[:, i].reshape(DEC_BATCH, PAST_LEN, 128),
                              cache_gqa_v[:, i].reshape(DEC_BATCH, PAST_LEN, 128))
            o_d = _na_sample(qn, kn, vn, cache_na_k[:, i].reshape(DEC_BATCH, PAST_LEN, 512),
                             cache_na_v[:, i].reshape(DEC_BATCH, PAST_LEN, 512), _na_bias_table(na_rpb[i]))
            h1s = _post_attn([o_c, o_d], hs, w_out, mod, l, True, nw[1])
        wf = {"w_up": w_up_b, "conv_w": _pair_chunks(conv_w[l]),
              "conv_b": _pair_chunks(conv_b[l][None, :]), "w_down": w_down_b}
        hp = _ffn(h1p, wf, mod, l, False, nw[2], nw[3])
        hs = _ffn(h1s, wf, mod, l, True, nw[2], nw[3])

    def stack(states, k, shape):
        return jnp.stack([st[k].reshape((BATCH, SEQ) + shape) for st in states], axis=1)

    new_mla_ckv = stack(even_states, 0, (KV_LORA,))
    new_mla_kpe = stack(even_states, 1, (QK_ROPE,))
    new_diff_k = stack(even_states, 2, (H_B, 2 * DH_B))
    new_diff_v = stack(even_states, 3, (H_B, 2 * DH_B))
    new_gqa_k = stack(odd_states, 0, (KV_C, DH_C))
    new_gqa_v = stack(odd_states, 1, (KV_C, DH_C))
    new_na_k = stack(odd_states, 2, (H_D, DH_D))
    new_na_v = stack(odd_states, 3, (H_D, DH_D))
    return (hp.reshape(BATCH, SEQ, D_MODEL), hs.reshape(DEC_BATCH, DEC_SEQ, D_MODEL),
            new_mla_ckv, new_mla_kpe, new_diff_k, new_diff_v, new_gqa_k, new_gqa_v, new_na_k, new_na_v)
```

```python
import functools
import math

import numpy as np
import jax
import jax.numpy as jnp
from jax import lax
from jax.experimental import pallas as pl
from jax.experimental.pallas import tpu as pltpu

D_MODEL = 1024
BATCH = 32
SEQ = 256
DEPTH = 2
DEC_BATCH = 4
DEC_SEQ = 1024
PAST_LEN = 256
GRID_W = 64
HEAD_DIM = 64
H_A = 8
QK_NOPE = 64
QK_ROPE = 32
V_A = 64
Q_LORA = 256
KV_LORA = 128
H_B = 4
DH_B = HEAD_DIM
H_C = 8
KV_C = 2
DH_C = HEAD_DIM
H_D = 8
DH_D = HEAD_DIM
NA_WIN_ROWS = 8
NA_WIN_COLS = 16
D_FF = 2816
ROPE_THETA = 10000.0
EPS = 1e-6
NEG_INF = -1e30

LANES = 128
MOD_ROWS = 8
ROW_TILE = 512
FFN_ROW_TILE = 1024
FF_CHUNK = 256
Q_TILE = 256
LATENT_GROUP = 1
CONTEXT_GROUP = 4
VMEM_LIMIT = 48 * 1024 * 1024
FFN_VMEM_LIMIT = 56 * 1024 * 1024

F32 = jnp.float32
BF16 = jnp.bfloat16
LOG2E = math.log2(math.e)


def _cparams(*sem):
    return pltpu.CompilerParams(dimension_semantics=sem, vmem_limit_bytes=VMEM_LIMIT)


def _dot(a, b):
    return jnp.dot(a, b, preferred_element_type=F32)


def _dot_nt(a, b):
    return lax.dot_general(a, b, (((1,), (1,)), ((), ())), preferred_element_type=F32)


def _rms(x, w):
    return x * lax.rsqrt(jnp.mean(x * x, axis=-1, keepdims=True) + EPS) * w


def _group_rms(x, w, gmat):
    x2 = x * x
    hi = x2.astype(BF16)
    lo = (x2 - hi.astype(F32)).astype(BF16)
    ms = _dot(hi, gmat) + _dot(lo, gmat)
    return x * lax.rsqrt(ms + EPS) * w


def _rope(x, cos, sin_signed, half):
    outs = []
    for j in range(x.shape[1] // LANES):
        xc = x[:, j * LANES:(j + 1) * LANES]
        lane = lax.broadcasted_iota(jnp.int32, xc.shape, 1)
        first = (lane % (2 * half)) < half
        partner = jnp.where(first, pltpu.roll(xc, LANES - half, 1), pltpu.roll(xc, half, 1))
        outs.append(xc * cos + partner * sin_signed)
    return outs[0] if len(outs) == 1 else jnp.concatenate(outs, axis=1)


def _softmax_block_rows(n_keys):
    return max(16, min(64, (16 * 1280 // n_keys) // 16 * 16))


def _attention(maps, s_scr, p_scr, group, mxu_sum=False):
    slots = s_scr.shape[0]
    staged = {}

    def stage(i):
        s_ref = s_scr.at[i % slots]
        offs, off = [], 0
        for s in maps[i][0]():
            s_ref[:, off:off + s.shape[1]] = s
            offs.append(off)
            off += s.shape[1]
        staged[i] = (offs, off)

    def softmax(i):
        _, _, c, fix, _ = maps[i]
        n_keys = staged[i][1]
        s_ref, p_ref = s_scr.at[i % slots], p_scr.at[i % slots]
        rb = _softmax_block_rows(n_keys)
        sums = []
        for r0 in range(0, s_ref.shape[0], rb):
            s = s_ref[r0:r0 + rb, 0:n_keys]
            if fix is not None:
                s = fix(s, r0)
            m = jnp.max(s, axis=-1, keepdims=True)
            p = jnp.exp2((s - m) * c)
            if not mxu_sum:
                sums.append(jnp.sum(p, axis=-1, keepdims=True))
            p_ref[r0:r0 + rb, 0:n_keys] = p.astype(BF16)
        return None if mxu_sum else jnp.concatenate(sums, axis=0)

    def weighted_values(i, den):
        _, values, _, _, sink = maps[i]
        p_ref = p_scr.at[i % slots]
        offs, n_keys = staged.pop(i)
        acc = None
        for o, v in zip(offs, values()):
            part = _dot(p_ref[:, o:o + v.shape[0]], v)
            acc = part if acc is None else acc + part
        if mxu_sum:
            den = _dot(p_ref[:, 0:n_keys], jnp.ones((n_keys, LANES), BF16))
        sink(acc / den)

    groups = [range(g, min(g + group, len(maps))) for g in range(0, len(maps), group)]
    for i in groups[0]:
        stage(i)
    for gi, grp in enumerate(groups):
        if gi + 1 < len(groups):
            for i in groups[gi + 1]:
                stage(i)
        dens = [softmax(i) for i in grp]
        for i, den in zip(grp, dens):
            weighted_values(i, den)


def _attn_scratch(tq, n_keys, group):
    return [pltpu.VMEM((2 * group, tq, n_keys), F32), pltpu.VMEM((2 * group, tq, n_keys), BF16)]


def _upper_half(shape):
    return lax.broadcasted_iota(jnp.int32, shape, 1) >= HEAD_DIM


def _keep_half(x, half):
    upper = _upper_half(x.shape)
    return jnp.where(upper if half else ~upper, x, jnp.zeros_like(x))


def _swap_halves(x):
    return jnp.concatenate([x[:, HEAD_DIM:], x[:, :HEAD_DIM]], axis=1)


def _pair_store(o_ref, c0):
    got = {}

    def make(half):
        def sink(o):
            got[half] = o
            if len(got) == 2:
                o_ref[:, c0:c0 + LANES] = jnp.where(_upper_half(o.shape), got[1], got[0]).astype(BF16)
        return sink
    return make(0), make(1)


def _out_proj_residual(o_scr, h_ref, wout_ref, mod_ref, nw1_ref, h1_ref):
    y = _dot(o_scr[...], wout_ref[...])
    gate = mod_ref[2:3, :]
    nw1 = nw1_ref[...]
    for c in range(y.shape[0] // NORM_ROWS):
        rs = slice(c * NORM_ROWS, (c + 1) * NORM_ROWS)
        h1_ref[rs, :] = h_ref[rs, :] + gate * _rms(y[rs], nw1)


def _mod_kernel(c_ref, w_ref, b_ref, o_ref):
    cv = c_ref[...]
    act = cv / (1.0 + jnp.exp(-cv))
    o_ref[...] = _dot(act.astype(BF16), w_ref[...].astype(BF16)) + b_ref[...]


def _modulation(cvecs, w_mod, b_mod):
    tn = 1024
    n = 6 * D_MODEL
    return pl.pallas_call(
        _mod_kernel,
        grid=(DEPTH, n // tn),
        in_specs=[
            pl.BlockSpec((MOD_ROWS, D_MODEL), lambda l, j: (0, 0)),
            pl.BlockSpec((None, D_MODEL, tn), lambda l, j: (l, 0, j)),
            pl.BlockSpec((None, 1, tn), lambda l, j: (l, 0, j)),
        ],
        out_specs=pl.BlockSpec((None, MOD_ROWS, tn), lambda l, j: (l, 0, j)),
        out_shape=jax.ShapeDtypeStruct((DEPTH, MOD_ROWS, n), F32),
        compiler_params=_cparams("parallel", "parallel"),
        name="adaln_mod",
    )(cvecs, w_mod, b_mod.reshape(DEPTH, 1, n))


def _mod_spec(layer, sample, tm):
    if sample:
        per = DEC_SEQ // tm
        return pl.BlockSpec((None, None, 6, D_MODEL), lambda i, *_: (layer, 1 + i // per, 0, 0))
    return pl.BlockSpec((None, None, 6, D_MODEL), lambda i, *_: (layer, 0, 0, 0))


def _full(shape):
    nd = len(shape)
    return pl.BlockSpec(shape, lambda *_: (0,) * nd)


def _rows(tm, width):
    return pl.BlockSpec((tm, width), lambda i, *_: (i, 0))


def _rope_spec(tm):
    per = DEC_SEQ // tm
    return pl.BlockSpec((tm, LANES), lambda i, *_: (i % per, 0))


EVEN_IN = 2048
MLA_SLOT = 128


def _proj_even_kernel(sample, *refs):
    if sample:
        (h_ref, mod_ref, nw_ref, win_ref, qnw_ref, kvnw_ref, wuq_ref, wuk_ref, wuv_ref,
         ca_ref, sa_ref, ch_ref, sh_ref,
         qa_ref, ka_ref, va_ref, qd_ref, kd_ref, vd_ref) = refs
    else:
        (h_ref, mod_ref, nw_ref, win_ref, qnw_ref, kvnw_ref, wuq_ref, wuk_ref, wuv_ref,
         qa_ref, ka_ref, va_ref, qd_ref, kd_ref, vd_ref,
         ckv_st_ref, kpe_st_ref, kd_st_ref, vd_st_ref) = refs
    mod = mod_ref[...]
    u = _rms(h_ref[...], nw_ref[...]) * (1.0 + mod[1:2]) + mod[0:1]
    z = _dot(u.astype(BF16), win_ref[...])
    cq = _rms(z[:, 0:256], qnw_ref[...]).astype(BF16)
    qa = _dot(cq, wuq_ref[...])
    ckv = _rms(z[:, 256:384], kvnw_ref[...])
    ckv_b = ckv.astype(BF16)
    kn = _dot(ckv_b, wuk_ref[...])
    va_ref[...] = _dot(ckv_b, wuv_ref[...]).astype(BF16)
    kpe_slot = z[:, 384:512]
    qd = z[:, 512:1024]
    kd = z[:, 1024:1536]
    vd = z[:, 1536:2048]
    vd_ref[...] = vd.astype(BF16)
    if sample:
        ca, sa, ch, sh = ca_ref[...], sa_ref[...], ch_ref[...], sh_ref[...]
        qa = _rope(qa, ca, sa, QK_ROPE // 2)
        kpe_rot = _rope(kpe_slot, ca, sa, QK_ROPE // 2)
        qd_ref[...] = _rope(qd, ch, sh, DH_B // 2).astype(BF16)
        kd_ref[...] = _rope(kd, ch, sh, DH_B // 2).astype(BF16)
    else:
        kpe_rot = kpe_slot
        qd_ref[...] = qd.astype(BF16)
        kd_ref[...] = kd.astype(BF16)
        ckv_st_ref[...] = ckv
        kpe_st_ref[...] = kpe_slot[:, QK_NOPE:QK_NOPE + QK_ROPE]
        kd_st_ref[...] = kd
        vd_st_ref[...] = vd
    qa_ref[...] = qa.astype(BF16)
    for hd in range(H_A):
        sl = slice(hd * MLA_SLOT, (hd + 1) * MLA_SLOT)
        ka_ref[:, sl] = (kn[:, sl] + kpe_rot).astype(BF16)


def _proj_even(h, mod, layer, sample, nw, w, rope):
    n = h.shape[0]
    tm = ROW_TILE
    wide = H_A * MLA_SLOT
    ins = [h, mod, nw, w["w_in"], w["q_norm_w"], w["kv_norm_w"], w["w_uq"], w["w_uk"], w["w_uv"]]
    in_specs = [_rows(tm, D_MODEL), _mod_spec(layer, sample, tm), _full((1, D_MODEL)), _full((D_MODEL, EVEN_IN)),
                _full((1, Q_LORA)), _full((1, KV_LORA)), _full((Q_LORA, wide)), _full((KV_LORA, wide)),
                _full((KV_LORA, 512))]
    widths = [wide, wide, 512, 512, 512, 512]
    out_shape = [jax.ShapeDtypeStruct((n, wd), BF16) for wd in widths]
    out_specs = [_rows(tm, wd) for wd in widths]
    if sample:
        ins += [rope["ca"], rope["sa"], rope["ch"], rope["sh"]]
        in_specs += [_rope_spec(tm)] * 4
    else:
        for wd in (KV_LORA, QK_ROPE, 512, 512):
            out_shape.append(jax.ShapeDtypeStruct((n, wd), F32))
            out_specs.append(_rows(tm, wd))
    return pl.pallas_call(
        functools.partial(_proj_even_kernel, sample),
        grid=(n // tm,),
        in_specs=in_specs,
        out_specs=out_specs,
        out_shape=out_shape,
        compiler_params=_cparams("parallel"),
        name="proj_even_s" if sample else "proj_even_p",
    )(*ins)


def _attn_even_kernel(sample, lam_init, *refs):
    if sample:
        (qa_ref, ka_ref, va_ref, qd_ref, kd_ref, vd_ref,
         cckv_ref, ckpe_ref, cdk_ref, cdv_ref, wuk_ref, wuv_ref, lam_ref, sub_ref,
         h_ref, wout_ref, mod_ref, nw1_ref, h1_ref, s_scr, p_scr, o_ref) = refs
    else:
        (qa_ref, ka_ref, va_ref, qd_ref, kd_ref, vd_ref, lam_ref, sub_ref,
         h_ref, wout_ref, mod_ref, nw1_ref, h1_ref, s_scr, p_scr, o_ref) = refs
    lf = lam_ref[...]
    lam = (jnp.exp(jnp.sum(lf[0:1] * lf[1:2], axis=-1, keepdims=True))
           - jnp.exp(jnp.sum(lf[2:3] * lf[3:4], axis=-1, keepdims=True)) + lam_init)
    if sample:
        cckv = cckv_ref[...].astype(BF16)
        kn_ctx = _dot(cckv, wuk_ref[...])
        va_ctx = _dot(cckv, wuv_ref[...]).astype(BF16)
        n_ctx = cckv.shape[0]
        kpe_ctx = jnp.concatenate([jnp.zeros((n_ctx, QK_NOPE), F32), ckpe_ref[...],
                                   jnp.zeros((n_ctx, MLA_SLOT - QK_NOPE - QK_ROPE), F32)], axis=1)
        kd_ctx = cdk_ref[...].astype(BF16)
        vd_ctx = cdv_ref[...].astype(BF16)
    maps = []
    c_a = (QK_NOPE + QK_ROPE) ** -0.5 * LOG2E
    for hd in range(H_A):
        sl = slice(hd * MLA_SLOT, (hd + 1) * MLA_SLOT)
        if hd % 2 == 0:
            sinks = _pair_store(o_ref, hd * V_A)

        def scores(sl=sl):
            q = qa_ref[:, sl]
            out = [_dot_nt(q, ka_ref[:, sl])]
            if sample:
                out.append(_dot_nt(q, (kn_ctx[:, sl] + kpe_ctx).astype(BF16)))
            return out

        def values(vs=slice(hd // 2 * LANES, (hd // 2 + 1) * LANES)):
            return [va_ref[:, vs]] + ([va_ctx[:, vs]] if sample else [])

        maps.append((scores, values, c_a, None, sinks[hd % 2]))
    c_b = DH_B ** -0.5 * LOG2E
    base = H_A * V_A
    sub_w = sub_ref[...]
    for hd in range(H_B):
        hs = slice(hd * 2 * DH_B, (hd + 1) * 2 * DH_B)
        outs = []

        def sink(o, outs=outs, hs=hs):
            outs.append(o)
            if len(outs) == 2:
                ob = _rms(outs[0] - lam * outs[1], sub_w) * (1.0 - lam_init)
                o_ref[:, base + hs.start:base + hs.stop] = ob.astype(BF16)

        def values(hs=hs):
            return [vd_ref[:, hs]] + ([vd_ctx[:, hs]] if sample else [])

        for comp in range(2):
            def scores(hs=hs, comp=comp):
                q = _keep_half(qd_ref[:, hs], comp)
                out = [_dot_nt(q, kd_ref[:, hs])]
                if sample:
                    out.append(_dot_nt(q, kd_ctx[:, hs]))
                return out

            maps.append((scores, values, c_b, None, sink))
    _attention(maps, s_scr, p_scr, s_scr.shape[0] // 2, mxu_sum=not sample)
    _out_proj_residual(o_ref, h_ref, wout_ref, mod_ref, nw1_ref, h1_ref)


def _attn_even(p, sample, lam_init, caches, w, h, w_out, mod, layer, nw1):
    qa, ka, va, qd, kd, vd = p
    n = qa.shape[0]
    wide = H_A * MLA_SLOT
    if sample:
        tq, per = Q_TILE, DEC_SEQ // Q_TILE
        grid = (DEC_BATCH, per)
        qspec = lambda wd: pl.BlockSpec((tq, wd), lambda b, i: (b * per + i, 0))
        kspec = lambda wd: pl.BlockSpec((DEC_SEQ, wd), lambda b, i: (b, 0))
        cspec = lambda wd: pl.BlockSpec((None, PAST_LEN, wd), lambda b, i: (b, 0, 0))
        ins = [qa, ka, va, qd, kd, vd, *caches, w["w_uk"], w["w_uv"], w["diff_lam"], w["diff_subln_w"]]
        in_specs = [qspec(wide), kspec(wide), kspec(512), qspec(512), kspec(512), kspec(512),
                    cspec(KV_LORA), cspec(QK_ROPE), cspec(512), cspec(512),
                    _full((KV_LORA, wide)), _full((KV_LORA, 512)), _full((4, DH_B)), _full((1, 2 * DH_B))]
        out_spec = qspec(D_MODEL)
        mod_spec = pl.BlockSpec((None, None, 6, D_MODEL), lambda b, i: (layer, 1 + b, 0, 0))
        sem = ("parallel", "parallel")
        n_keys, group = DEC_SEQ + PAST_LEN, LATENT_GROUP
    else:
        tq = SEQ
        grid = (BATCH,)
        spec = lambda wd: pl.BlockSpec((SEQ, wd), lambda b: (b, 0))
        ins = [qa, ka, va, qd, kd, vd, w["diff_lam"], w["diff_subln_w"]]
        in_specs = [spec(wide), spec(wide), spec(512), spec(512), spec(512), spec(512),
                    _full((4, DH_B)), _full((1, 2 * DH_B))]
        out_spec = spec(D_MODEL)
        mod_spec = pl.BlockSpec((None, None, 6, D_MODEL), lambda b: (layer, 0, 0, 0))
        sem = ("parallel",)
        n_keys, group = SEQ, CONTEXT_GROUP
    ins += [h, w_out, mod, nw1]
    in_specs += [out_spec, _full((D_MODEL, D_MODEL)), mod_spec, _full((1, D_MODEL))]
    return pl.pallas_call(
        functools.partial(_attn_even_kernel, sample, lam_init),
        grid=grid,
        in_specs=in_specs,
        out_specs=out_spec,
        out_shape=jax.ShapeDtypeStruct((n, D_MODEL), F32),
        scratch_shapes=_attn_scratch(tq, n_keys, group) + [pltpu.VMEM((tq, D_MODEL), BF16)],
        compiler_params=_cparams(*sem),
        name="attn_even_s" if sample else "attn_even_p",
    )(*ins)


ODD_IN = 2304


def _proj_odd_kernel(sample, *refs):
    if sample:
        (h_ref, mod_ref, nw_ref, win_ref, qw_ref, kw_ref, gq_ref, gk_ref, ch_ref, sh_ref,
         qc_ref, kc_ref, vc_ref, qn_ref, kn_ref, vn_ref) = refs
    else:
        (h_ref, mod_ref, nw_ref, win_ref, qw_ref, kw_ref, gq_ref, gk_ref,
         qc_ref, kc_ref, vc_ref, qn_ref, kn_ref, vn_ref,
         kc_st_ref, vc_st_ref, kn_st_ref, vn_st_ref) = refs
    mod = mod_ref[...]
    u = _rms(h_ref[...], nw_ref[...]) * (1.0 + mod[1:2]) + mod[0:1]
    z = _dot(u.astype(BF16), win_ref[...])
    qc = _group_rms(z[:, 0:512], qw_ref[...], gq_ref[...])
    kc = _group_rms(z[:, 512:640], kw_ref[...], gk_ref[...])
    vc = z[:, 640:768]
    kn = z[:, 1280:1792]
    vn = z[:, 1792:2304]
    vc_ref[...] = vc.astype(BF16)
    qn_ref[...] = z[:, 768:1280].astype(BF16)
    kn_ref[...] = kn.astype(BF16)
    vn_ref[...] = vn.astype(BF16)
    if sample:
        ch, sh = ch_ref[...], sh_ref[...]
        qc_ref[...] = _rope(qc, ch, sh, DH_C // 2).astype(BF16)
        kc_ref[...] = _rope(kc, ch, sh, DH_C // 2).astype(BF16)
    else:
        qc_ref[...] = qc.astype(BF16)
        kc_ref[...] = kc.astype(BF16)
        kc_st_ref[...] = kc
        vc_st_ref[...] = vc
        kn_st_ref[...] = kn.reshape(kn.shape[0], H_D, DH_D)
        vn_st_ref[...] = vn.reshape(vn.shape[0], H_D, DH_D)


def _proj_odd(h, mod, layer, sample, nw, w, rope):
    n = h.shape[0]
    tm = ROW_TILE
    ins = [h, mod, nw, w["w_in"], w["q_w"], w["k_w"], w["gq"], w["gk"]]
    in_specs = [_rows(tm, D_MODEL), _mod_spec(layer, sample, tm), _full((1, D_MODEL)), _full((D_MODEL, ODD_IN)),
                _full((1, 512)), _full((1, 128)), _full((512, 512)), _full((128, 128))]
    widths = [512, 128, 128, 512, 512, 512]
    out_shape = [jax.ShapeDtypeStruct((n, wd), BF16) for wd in widths]
    out_specs = [_rows(tm, wd) for wd in widths]
    if sample:
        ins += [rope["ch"], rope["sh"]]
        in_specs += [_rope_spec(tm)] * 2
    else:
        for wd in (128, 128):
            out_shape.append(jax.ShapeDtypeStruct((n, wd), F32))
            out_specs.append(_rows(tm, wd))
        for _ in range(2):
            out_shape.append(jax.ShapeDtypeStruct((n, H_D, DH_D), F32))
            out_specs.append(pl.BlockSpec((tm, H_D, DH_D), lambda i: (i, 0, 0)))
    return pl.pallas_call(
        functools.partial(_proj_odd_kernel, sample),
        grid=(n // tm,),
        in_specs=in_specs,
        out_specs=out_specs,
        out_shape=out_shape,
        compiler_params=_cparams("parallel"),
        name="proj_odd_s" if sample else "proj_odd_p",
    )(*ins)


def _attn_odd_prompt_kernel(qc_ref, kc_ref, vc_ref, qn_ref, kn_ref, vn_ref, h_ref, wout_ref, mod_ref, nw1_ref,
                            h1_ref, s_scr, p_scr, o_ref):
    c = DH_C ** -0.5 * LOG2E
    group = H_C // KV_C
    kc = (kc_ref[...], _swap_halves(kc_ref[...]))
    vc = (vc_ref[...], _swap_halves(vc_ref[...]))
    maps = []
    for hd in range(H_C):
        half, swap = hd % 2, (hd // group) != (hd % 2)
        ps = slice(hd // 2 * LANES, (hd // 2 + 1) * LANES)
        if half == 0:
            sinks = _pair_store(o_ref, ps.start)
        maps.append((lambda ps=ps, half=half, swap=swap: [_dot_nt(_keep_half(qc_ref[:, ps], half), kc[swap])],
                     lambda swap=swap: [vc[swap]], c, None, sinks[half]))
    base = H_C * DH_C
    c = DH_D ** -0.5 * LOG2E
    for hd in range(H_D):
        half = hd % 2
        ps = slice(hd // 2 * LANES, (hd // 2 + 1) * LANES)
        if half == 0:
            sinks = _pair_store(o_ref, base + ps.start)
        maps.append((lambda ps=ps, half=half: [_dot_nt(_keep_half(qn_ref[:, ps], half), kn_ref[:, ps])],
                     lambda ps=ps: [vn_ref[:, ps]], c, None, sinks[half]))
    _attention(maps, s_scr, p_scr, s_scr.shape[0] // 2, mxu_sum=True)
    _out_proj_residual(o_ref, h_ref, wout_ref, mod_ref, nw1_ref, h1_ref)


def _attn_odd_prompt(p, h, w_out, mod, layer, nw1):
    qc, kc, vc, qn, kn, vn = p
    spec = lambda wd: pl.BlockSpec((SEQ, wd), lambda b: (b, 0))
    return pl.pallas_call(
        _attn_odd_prompt_kernel,
        grid=(BATCH,),
        in_specs=[spec(512), spec(128), spec(128), spec(512), spec(512), spec(512),
                  spec(D_MODEL), _full((D_MODEL, D_MODEL)),
                  pl.BlockSpec((None, None, 6, D_MODEL), lambda b: (layer, 0, 0, 0)), _full((1, D_MODEL))],
        out_specs=spec(D_MODEL),
        out_shape=jax.ShapeDtypeStruct((qc.shape[0], D_MODEL), F32),
        scratch_shapes=_attn_scratch(SEQ, SEQ, CONTEXT_GROUP) + [pltpu.VMEM((SEQ, D_MODEL), BF16)],
        compiler_params=_cparams("parallel"),
        name="attn_odd_p",
    )(qc, kc, vc, qn, kn, vn, h, w_out, mod, nw1)


def _gqa_sample_kernel(q_ref, k_ref, v_ref, ck_ref, cv_ref, o_ref, s_scr, p_scr):
    c = DH_C ** -0.5 * LOG2E
    group = H_C // KV_C
    k_loc = (k_ref[...], _swap_halves(k_ref[...]))
    v_loc = (v_ref[...], _swap_halves(v_ref[...]))
    k_ctx = ck_ref[...].astype(BF16)
    v_ctx = cv_ref[...].astype(BF16)
    k_ctx = (k_ctx, _swap_halves(k_ctx))
    v_ctx = (v_ctx, _swap_halves(v_ctx))
    maps = []
    for hd in range(H_C):
        half, swap = hd % 2, (hd // group) != (hd % 2)
        ps = slice(hd // 2 * LANES, (hd // 2 + 1) * LANES)
        if half == 0:
            sinks = _pair_store(o_ref, ps.start)

        def scores(ps=ps, half=half, swap=swap):
            q = _keep_half(q_ref[:, ps], half)
            return [_dot_nt(q, k_loc[swap]), _dot_nt(q, k_ctx[swap])]

        maps.append((scores, lambda swap=swap: [v_loc[swap], v_ctx[swap]], c, None, sinks[half]))
    _attention(maps, s_scr, p_scr, s_scr.shape[0] // 2)


def _gqa_sample(qc, kc, vc, cache_k, cache_v):
    tq, per = Q_TILE, DEC_SEQ // Q_TILE
    return pl.pallas_call(
        _gqa_sample_kernel,
        grid=(DEC_BATCH, per),
        in_specs=[pl.BlockSpec((tq, 512), lambda b, i: (b * per + i, 0)),
                  pl.BlockSpec((DEC_SEQ, 128), lambda b, i: (b, 0)),
                  pl.BlockSpec((DEC_SEQ, 128), lambda b, i: (b, 0)),
                  pl.BlockSpec((None, PAST_LEN, 128), lambda b, i: (b, 0, 0)),
                  pl.BlockSpec((None, PAST_LEN, 128), lambda b, i: (b, 0, 0))],
        out_specs=pl.BlockSpec((tq, 512), lambda b, i: (b * per + i, 0)),
        out_shape=jax.ShapeDtypeStruct((qc.shape[0], 512), BF16),
        scratch_shapes=_attn_scratch(tq, DEC_SEQ + PAST_LEN, LATENT_GROUP),
        compiler_params=_cparams("parallel", "parallel"),
        name="gqa_s",
    )(qc, kc, vc, cache_k, cache_v)


NA_ROWS = DEC_SEQ // GRID_W
NA_KR = min(NA_WIN_ROWS, NA_ROWS)
NA_LOC = NA_KR * GRID_W


def _na_sample_kernel(q_ref, k_ref, v_ref, ck_ref, cv_ref, bias_ref, o_ref, s_scr, p_scr):
    r = pl.program_id(1)
    rs = jnp.clip(r - NA_KR // 2, 0, NA_ROWS - NA_KR)
    start = pl.multiple_of(rs * GRID_W, GRID_W)
    scale = DH_D ** -0.5
    k_loc = k_ref[pl.ds(start, NA_LOC), :]
    v_loc = v_ref[pl.ds(start, NA_LOC), :]
    k_ctx = ck_ref[...].astype(BF16)
    v_ctx = cv_ref[...].astype(BF16)
    rb = _softmax_block_rows(NA_LOC + PAST_LEN)
    col_ok = {}
    for r0 in range(0, GRID_W, rb):
        wq = lax.broadcasted_iota(jnp.int32, (rb, NA_LOC), 0) + r0
        wk = lax.broadcasted_iota(jnp.int32, (rb, NA_LOC), 1) % GRID_W
        cs = jnp.clip(wq - NA_WIN_COLS // 2, 0, GRID_W - NA_WIN_COLS)
        col_ok[r0] = (wk >= cs) & (wk < cs + NA_WIN_COLS)
    maps = []
    for hd in range(H_D):
        half = hd % 2
        ps = slice(hd // 2 * LANES, (hd // 2 + 1) * LANES)
        if half == 0:
            sinks = _pair_store(o_ref, ps.start)

        def fix(s, r0, hd=hd):
            loc = s[:, :NA_LOC] * scale + bias_ref[hd, r0:r0 + rb, :]
            return jnp.concatenate([jnp.where(col_ok[r0], loc, NEG_INF), s[:, NA_LOC:] * scale], axis=1)

        def scores(ps=ps, half=half):
            q = _keep_half(q_ref[:, ps], half)
            return [_dot_nt(q, k_loc[:, ps]), _dot_nt(q, k_ctx[:, ps])]

        maps.append((scores, lambda ps=ps: [v_loc[:, ps], v_ctx[:, ps]], LOG2E, fix, sinks[half]))
    _attention(maps, s_scr, p_scr, s_scr.shape[0] // 2)


def _na_sample(qn, kn, vn, cache_k, cache_v, bias):
    def bias_map(b, r):
        rs = jnp.clip(r - NA_KR // 2, 0, NA_ROWS - NA_KR)
        return (0, rs - r + NA_WIN_ROWS - 1, 0, 0)
    return pl.pallas_call(
        _na_sample_kernel,
        grid=(DEC_BATCH, NA_ROWS),
        in_specs=[pl.BlockSpec((GRID_W, 512), lambda b, r: (b * NA_ROWS + r, 0)),
                  pl.BlockSpec((DEC_SEQ, 512), lambda b, r: (b, 0)),
                  pl.BlockSpec((DEC_SEQ, 512), lambda b, r: (b, 0)),
                  pl.BlockSpec((None, PAST_LEN, 512), lambda b, r: (b, 0, 0)),
                  pl.BlockSpec((None, PAST_LEN, 512), lambda b, r: (b, 0, 0)),
                  pl.BlockSpec((H_D, None, GRID_W, NA_LOC), bias_map)],
        out_specs=pl.BlockSpec((GRID_W, 512), lambda b, r: (b * NA_ROWS + r, 0)),
        out_shape=jax.ShapeDtypeStruct((qn.shape[0], 512), BF16),
        scratch_shapes=_attn_scratch(GRID_W, NA_LOC + PAST_LEN, CONTEXT_GROUP),
        compiler_params=_cparams("parallel", "parallel"),
        name="na_s",
    )(qn, kn, vn, cache_k, cache_v, bias)


def _na_bias_table(rpb):
    edge = GRID_W - NA_WIN_COLS
    n_dr = 2 * NA_WIN_ROWS - 1
    v = jnp.pad(rpb.astype(F32), ((0, 0), (0, 0), (edge, edge + 1)), mode="edge")
    skew = jnp.tile(v, (1, 1, GRID_W))[:, :, :GRID_W * (2 * GRID_W - 1)]
    toep = skew.reshape(H_D, n_dr, GRID_W, 2 * GRID_W - 1)[:, :, :, GRID_W - 1:]
    flat = toep.transpose(0, 2, 1, 3).reshape(H_D, GRID_W, n_dr * GRID_W)
    return jnp.stack([flat[:, :, d0 * GRID_W:d0 * GRID_W + NA_LOC] for d0 in range(NA_WIN_ROWS)], axis=1)


def _post_attn_kernel(n_parts, *refs):
    o_refs = refs[:n_parts]
    h_ref, wout_ref, mod_ref, nw1_ref, h1_ref = refs[n_parts:]
    y = None
    off = 0
    for o_ref in o_refs:
        wd = o_ref.shape[1]
        part = _dot(o_ref[...], wout_ref[off:off + wd, :])
        y = part if y is None else y + part
        off += wd
    mod = mod_ref[...]
    h1_ref[...] = h_ref[...] + mod[2:3] * _rms(y, nw1_ref[...])


def _post_attn(o_parts, h, w_out, mod, layer, sample, nw1):
    n = h.shape[0]
    tm = ROW_TILE
    in_specs = [_rows(tm, o.shape[1]) for o in o_parts]
    in_specs += [_rows(tm, D_MODEL), _full((D_MODEL, D_MODEL)), _mod_spec(layer, sample, tm), _full((1, D_MODEL))]
    return pl.pallas_call(
        functools.partial(_post_attn_kernel, len(o_parts)),
        grid=(n // tm,),
        in_specs=in_specs,
        out_specs=_rows(tm, D_MODEL),
        out_shape=jax.ShapeDtypeStruct((n, D_MODEL), F32),
        compiler_params=_cparams("parallel"),
        name="post_attn_s" if sample else "post_attn_p",
    )(*o_parts, h, w_out, mod, nw1)


FF_PAIR = 2 * FF_CHUNK
N_FF = D_FF // FF_CHUNK
SUBLANES = 8
ACT_TILES = 4
ROW_BLOCK = 1024
NORM_ROWS = 32


def _ffn_kernel(seq_len, h_ref, wup_ref, cw_ref, cb_ref, wd_ref, mod_ref, nw2_ref, nw3_ref, o_ref,
                u_ref, z0_ref, z1_ref, a_ref, a_last_ref):
    tm = h_ref.shape[0]
    n_blocks = tm // ROW_BLOCK
    rows = ACT_TILES * SUBLANES
    mod = mod_ref[...]
    sub = lax.broadcasted_iota(jnp.int32, (SUBLANES, LANES), 0)
    zero_rows = jnp.zeros((SUBLANES, FF_PAIR), F32)
    for z_ref in (z0_ref, z1_ref):
        z_ref[0:SUBLANES, :] = zero_rows
        z_ref[SUBLANES + tm:2 * SUBLANES + tm, :] = zero_rows

    nw2 = nw2_ref[...]

    def pre_norm(blk):
        for c in range(ROW_BLOCK // NORM_ROWS):
            rs = slice(blk * ROW_BLOCK + c * NORM_ROWS, blk * ROW_BLOCK + (c + 1) * NORM_ROWS)
            u = _rms(h_ref[rs, :], nw2) * (1.0 + mod[4:5]) + mod[3:4]
            u_ref[rs, :] = u.astype(BF16)

    def up(j, z_ref, blk):
        r0 = blk * ROW_BLOCK
        u = u_ref[r0:r0 + ROW_BLOCK, :]
        rows_ = slice(SUBLANES + r0, SUBLANES + r0 + ROW_BLOCK)
        for half in range(2):
            c0 = half * D_FF + j * FF_CHUNK
            if not isinstance(c0, int):
                c0 = pl.multiple_of(c0, FF_CHUNK)
            z_ref[rows_, half * FF_CHUNK:(half + 1) * FF_CHUNK] = _dot(u, wup_ref[:, pl.ds(c0, FF_CHUNK)])

    def act(j, z_ref, col, blk, dst_ref=a_ref):
        cw = cw_ref[j]
        cb = cb_ref[j]
        for lc in range(FF_CHUNK // LANES):
            taps = []
            for lane0 in (lc * LANES, FF_CHUNK + lc * LANES):
                lanes = slice(lane0, lane0 + LANES)
                taps.append([jnp.broadcast_to(cw[k:k + 1, lanes], (SUBLANES, LANES)) for k in range(3)]
                            + [jnp.broadcast_to(cb[:, lanes], (SUBLANES, LANES))])
            for c in range(ROW_BLOCK // rows):
                r = blk * ROW_BLOCK + c * rows
                first = r % seq_len == 0
                last = (r + rows) % seq_len == 0

                def conv(lane0, tap):
                    ext = z_ref[r:r + rows + 2 * SUBLANES, lane0:lane0 + LANES]
                    tiles = [ext[t * SUBLANES:(t + 1) * SUBLANES] for t in range(ACT_TILES + 2)]
                    down = [pltpu.roll(t, 1, 0) for t in tiles[:-1]]
                    up_ = [pltpu.roll(t, SUBLANES - 1, 0) for t in tiles[1:]]
                    out = []
                    for t in range(ACT_TILES):
                        above = 0.0 if (first and t == 0) else down[t]
                        below = 0.0 if (last and t == ACT_TILES - 1) else up_[t + 1]
                        prev = jnp.where(sub == 0, above, down[t + 1])
                        nxt = jnp.where(sub == SUBLANES - 1, below, up_[t])
                        out.append(prev * tap[0] + tiles[t + 1] * tap[1] + nxt * tap[2] + tap[3])
                    return jnp.concatenate(out, axis=0)

                g = conv(lc * LANES, taps[0])
                v = conv(FF_CHUNK + lc * LANES, taps[1])
                a = (g / (1.0 + jnp.exp2(g * -LOG2E))) * v
                lane = col + lc * LANES
                if not isinstance(lane, int):
                    lane = pl.multiple_of(lane, LANES)
                dst_ref[r:r + rows, pl.ds(lane, LANES)] = a.astype(BF16)

    for blk in range(n_blocks):
        pre_norm(blk)
        up(0, z0_ref, blk)

    def pair(i, carry):
        j = 2 * i
        col = pl.multiple_of(j * FF_CHUNK, FF_CHUNK)
        for blk in range(n_blocks):
            up(j + 1, z1_ref, blk)
            act(j, z0_ref, col, blk)
        for blk in range(n_blocks):
            up(j + 2, z0_ref, blk)
            act(j + 1, z1_ref, col + FF_CHUNK, blk)
        return carry

    lax.fori_loop(0, (N_FF - 1) // 2, pair, 0)
    nw3 = nw3_ref[...]
    k_main = (N_FF - 1) * FF_CHUNK
    for blk in range(n_blocks):
        r0 = blk * ROW_BLOCK
        y = _dot(a_ref[r0:r0 + ROW_BLOCK, 0:k_main], wd_ref[0:k_main, :])
        act(N_FF - 1, z0_ref, 0, blk, a_last_ref)
        y = y + _dot(a_last_ref[r0:r0 + ROW_BLOCK, :], wd_ref[k_main:D_FF, :])
        for c in range(ROW_BLOCK // NORM_ROWS):
            rs = slice(r0 + c * NORM_ROWS, r0 + (c + 1) * NORM_ROWS)
            o_ref[rs, :] = h_ref[rs, :] + mod[5:6] * _rms(y[c * NORM_ROWS:(c + 1) * NORM_ROWS], nw3)


def _ffn(h1, w, mod, layer, sample, nw2, nw3):
    n = h1.shape[0]
    tm = FFN_ROW_TILE
    seq_len = DEC_SEQ if sample else SEQ
    once = pl.Buffered(1)
    in_specs = [
        _rows(tm, D_MODEL),
        pl.BlockSpec((None, D_MODEL, 2 * D_FF), lambda i: (layer, 0, 0), pipeline_mode=once),
        pl.BlockSpec((N_FF, 3, FF_PAIR), lambda i: (0, 0, 0), pipeline_mode=once),
        pl.BlockSpec((N_FF, 1, FF_PAIR), lambda i: (0, 0, 0), pipeline_mode=once),
        pl.BlockSpec((None, D_FF, D_MODEL), lambda i: (layer, 0, 0), pipeline_mode=once),
        _mod_spec(layer, sample, tm),
        _full((1, D_MODEL)),
        _full((1, D_MODEL)),
    ]
    return pl.pallas_call(
        functools.partial(_ffn_kernel, seq_len),
        grid=(n // tm,),
        in_specs=in_specs,
        out_specs=_rows(tm, D_MODEL),
        out_shape=jax.ShapeDtypeStruct((n, D_MODEL), F32),
        scratch_shapes=[pltpu.VMEM((tm, D_MODEL), BF16), pltpu.VMEM((tm + 2 * SUBLANES, FF_PAIR), F32),
                        pltpu.VMEM((tm + 2 * SUBLANES, FF_PAIR), F32), pltpu.VMEM((tm, D_FF - FF_CHUNK), BF16),
                        pltpu.VMEM((tm, FF_CHUNK), BF16)],
        compiler_params=pltpu.CompilerParams(dimension_semantics=("parallel",), vmem_limit_bytes=FFN_VMEM_LIMIT),
        name="ffn_s" if sample else "ffn_p",
    )(h1, w["w_up"], w["conv_w"], w["conv_b"], w["w_down"], mod, nw2, nw3)


def _pair_chunks(x):
    lead = x.shape[:-1]
    x = x.reshape(lead + (2, N_FF, FF_CHUNK))
    x = jnp.moveaxis(x, -2, 0)
    return x.reshape((N_FF,) + lead + (FF_PAIR,))


def _rope_tables():
    def table(rot_dim):
        t = np.arange(DEC_SEQ)
        n_freq = rot_dim // 4
        inv = 1.0 / (ROPE_THETA ** (np.arange(n_freq) / n_freq))
        ang = np.concatenate([(t // GRID_W)[:, None] * inv[None, :], (t % GRID_W)[:, None] * inv[None, :]], axis=-1)
        cos = np.cos(ang).astype(np.float32)
        sin = np.sin(ang).astype(np.float32)
        reps = LANES // rot_dim
        return (np.tile(np.concatenate([cos, cos], axis=-1), (1, reps)),
                np.tile(np.concatenate([-sin, sin], axis=-1), (1, reps)))
    ca, sa = table(QK_ROPE)
    ch, sh = table(HEAD_DIM)
    rope_lanes = (np.arange(MLA_SLOT) >= QK_NOPE) & (np.arange(MLA_SLOT) < QK_NOPE + QK_ROPE)
    ca = np.where(rope_lanes[None, :], ca, 1.0).astype(np.float32)
    sa = np.where(rope_lanes[None, :], sa, 0.0).astype(np.float32)
    return {"ca": jnp.asarray(ca), "sa": jnp.asarray(sa), "ch": jnp.asarray(ch), "sh": jnp.asarray(sh)}


def _group_mean_matrix(width):
    idx = np.arange(width) // HEAD_DIM
    return jnp.asarray((idx[:, None] == idx[None, :]).astype(np.float32) / HEAD_DIM, BF16)


def kernel(x_prompt, x_sample, c, cache_mla_ckv, cache_mla_kpe, cache_diff_k, cache_diff_v, cache_gqa_k, cache_gqa_v, cache_na_k, cache_na_v, c_ctx, norm_w, w_mod, b_mod, w_in_even, w_out_even, w_uq, q_norm_w, kv_norm_w, w_uk, w_uv, diff_lam, diff_subln_w, w_in_odd, w_out_odd, qk_norm_w, na_rpb, w_up, conv_w, conv_b, w_down):
    rope = _rope_tables()
    n_p = BATCH * SEQ
    n_s = DEC_BATCH * DEC_SEQ
    cvecs = jnp.concatenate([c_ctx[None, :], c, jnp.zeros((MOD_ROWS - 1 - DEC_BATCH, D_MODEL), F32)], axis=0)
    mod = _modulation(cvecs, w_mod, b_mod).reshape(DEPTH, MOD_ROWS, 6, D_MODEL)
    hp = x_prompt.reshape(n_p, D_MODEL)
    hs = x_sample.reshape(n_s, D_MODEL)
    even_states, odd_states = [], []
    w_up_b = w_up.astype(BF16)
    w_down_b = w_down.astype(BF16)
    for l in range(DEPTH):
        i = l // 2
        nw = [norm_w[l, k][None, :] for k in range(4)]
        if l % 2 == 0:
            lam_init = 0.8 - 0.6 * math.exp(-0.3 * l)
            wi = w_in_even[i]
            w_uq3 = w_uq[i].reshape(Q_LORA, H_A, QK_NOPE + QK_ROPE)
            w = {
                "w_in": jnp.concatenate([wi[:, :384], jnp.zeros((D_MODEL, QK_NOPE), F32), wi[:, 384:416],
                                         jnp.zeros((D_MODEL, MLA_SLOT - QK_NOPE - QK_ROPE), F32), wi[:, 416:]],
                                        axis=1).astype(BF16),
                "q_norm_w": q_norm_w[i][None, :],
                "kv_norm_w": kv_norm_w[i][None, :],
                "w_uq": jnp.pad(w_uq3, ((0, 0), (0, 0), (0, MLA_SLOT - QK_NOPE - QK_ROPE))
                                ).reshape(Q_LORA, H_A * MLA_SLOT).astype(BF16),
                "w_uk": jnp.pad(w_uk[i].reshape(KV_LORA, H_A, QK_NOPE), ((0, 0), (0, 0), (0, MLA_SLOT - QK_NOPE))
                                ).reshape(KV_LORA, H_A * MLA_SLOT).astype(BF16),
                "w_uv": w_uv[i].astype(BF16),
                "diff_lam": diff_lam[i],
                "diff_subln_w": diff_subln_w[i][None, :],
            }
            outs_p = _proj_even(hp, mod, l, False, nw[0], w, rope)
            outs_s = _proj_even(hs, mod, l, True, nw[0], w, rope)
            even_states.append(outs_p[6:])
            w_out = w_out_even[i].astype(BF16)
            h1p = _attn_even(outs_p[:6], False, lam_init, None, w, hp, w_out, mod, l, nw[1])
            caches = (cache_mla_ckv[:, i], cache_mla_kpe[:, i],
                      cache_diff_k[:, i].reshape(DEC_BATCH, PAST_LEN, 512),
                      cache_diff_v[:, i].reshape(DEC_BATCH, PAST_LEN, 512))
            h1s = _attn_even(outs_s, True, lam_init, caches, w, hs, w_out, mod, l, nw[1])
        else:
            q_w = jnp.tile(qk_norm_w[i, 0], H_C)[None, :]
            k_w = jnp.tile(qk_norm_w[i, 1], KV_C)[None, :]
            w = {"w_in": w_in_odd[i].astype(BF16), "q_w": q_w, "k_w": k_w,
                 "gq": _group_mean_matrix(512), "gk": _group_mean_matrix(128)}
            outs_p = _proj_odd(hp, mod, l, False, nw[0], w, rope)
            outs_s = _proj_odd(hs, mod, l, True, nw[0], w, rope)
            odd_states.append(outs_p[6:])
            w_out = w_out_odd[i].astype(BF16)
            h1p = _attn_odd_prompt(outs_p[:6], hp, w_out, mod, l, nw[1])
            qc, kc, vc, qn, kn, vn = outs_s
            o_c = _gqa_sample(qc, kc, vc, cache_gqa_k[:, i].reshape(DEC_BATCH, PAST_LEN, 128),
                              cache_gqa_v[:, i].reshape(DEC_BATCH, PAST_LEN, 128))
            o_d = _na_sample(qn, kn, vn, cache_na_k[:, i].reshape(DEC_BATCH, PAST_LEN, 512),
                             cache_na_v[:, i].reshape(DEC_BATCH, PAST_LEN, 512), _na_bias_table(na_rpb[i]))
            h1s = _post_attn([o_c, o_d], hs, w_out, mod, l, True, nw[1])
        wf = {"w_up": w_up_b, "conv_w": _pair_chunks(conv_w[l]),
              "conv_b": _pair_chunks(conv_b[l][None, :]), "w_down": w_down_b}
        hp = _ffn(h1p, wf, mod, l, False, nw[2], nw[3])
        hs = _ffn(h1s, wf, mod, l, True, nw[2], nw[3])

    def stack(states, k, shape):
        return jnp.stack([st[k].reshape((BATCH, SEQ) + shape) for st in states], axis=1)

    new_mla_ckv = stack(even_states, 0, (KV_LORA,))
    new_mla_kpe = stack(even_states, 1, (QK_ROPE,))
    new_diff_k = stack(even_states, 2, (H_B, 2 * DH_B))
    new_diff_v = stack(even_states, 3, (H_B, 2 * DH_B))
    new_gqa_k = stack(odd_states, 0, (KV_C, DH_C))
    new_gqa_v = stack(odd_states, 1, (KV_C, DH_C))
    new_na_k = stack(odd_states, 2, (H_D, DH_D))
    new_na_v = stack(odd_states, 3, (H_D, DH_D))
    return (hp.reshape(BATCH, SEQ, D_MODEL), hs.reshape(DEC_BATCH, DEC_SEQ, D_MODEL),
            new_mla_ckv, new_mla_kpe, new_diff_k, new_diff_v, new_gqa_k, new_gqa_v, new_na_k, new_na_v)
```

```python
import functools
import math

import numpy as np
import jax
import jax.numpy as jnp
from jax import lax
from jax.experimental import pallas as pl
from jax.experimental.pallas import tpu as pltpu

D_MODEL = 1024
BATCH = 32
SEQ = 256
DEPTH = 2
DEC_BATCH = 4
DEC_SEQ = 1024
PAST_LEN = 256
GRID_W = 64
HEAD_DIM = 64
H_A = 8
QK_NOPE = 64
QK_ROPE = 32
V_A = 64
Q_LORA = 256
KV_LORA = 128
H_B = 4
DH_B = HEAD_DIM
H_C = 8
KV_C = 2
DH_C = HEAD_DIM
H_D = 8
DH_D = HEAD_DIM
NA_WIN_ROWS = 8
NA_WIN_COLS = 16
D_FF = 2816
ROPE_THETA = 10000.0
EPS = 1e-6
NEG_INF = -1e30

LANES = 128
MOD_ROWS = 8
ROW_TILE = 512
FFN_ROW_TILE = 1024
FF_CHUNK = 256
Q_TILE = 256
LATENT_GROUP = 1
CONTEXT_GROUP = 4
VMEM_LIMIT = 48 * 1024 * 1024
FFN_VMEM_LIMIT = 56 * 1024 * 1024

F32 = jnp.float32
BF16 = jnp.bfloat16
LOG2E = math.log2(math.e)


def _cparams(*sem):
    return pltpu.CompilerParams(dimension_semantics=sem, vmem_limit_bytes=VMEM_LIMIT)


def _dot(a, b):
    return jnp.dot(a, b, preferred_element_type=F32)


def _dot_nt(a, b):
    return lax.dot_general(a, b, (((1,), (1,)), ((), ())), preferred_element_type=F32)


def _rms(x, w):
    return x * lax.rsqrt(jnp.mean(x * x, axis=-1, keepdims=True) + EPS) * w


def _group_rms(x, w, gmat):
    x2 = x * x
    hi = x2.astype(BF16)
    lo = (x2 - hi.astype(F32)).astype(BF16)
    ms = _dot(hi, gmat) + _dot(lo, gmat)
    return x * lax.rsqrt(ms + EPS) * w


def _rope(x, cos, sin_signed, half):
    outs = []
    for j in range(x.shape[1] // LANES):
        xc = x[:, j * LANES:(j + 1) * LANES]
        lane = lax.broadcasted_iota(jnp.int32, xc.shape, 1)
        first = (lane % (2 * half)) < half
        partner = jnp.where(first, pltpu.roll(xc, LANES - half, 1), pltpu.roll(xc, half, 1))
        outs.append(xc * cos + partner * sin_signed)
    return outs[0] if len(outs) == 1 else jnp.concatenate(outs, axis=1)


def _softmax_block_rows(n_keys):
    return max(16, min(64, (16 * 1280 // n_keys) // 16 * 16))


def _attention(maps, s_scr, p_scr, group, mxu_sum=False):
    slots = s_scr.shape[0]
    staged = {}

    def stage(i):
        s_ref = s_scr.at[i % slots]
        offs, off = [], 0
        for s in maps[i][0]():
            s_ref[:, off:off + s.shape[1]] = s
            offs.append(off)
            off += s.shape[1]
        staged[i] = (offs, off)

    def softmax(i):
        _, _, c, fix, _ = maps[i]
        n_keys = staged[i][1]
        s_ref, p_ref = s_scr.at[i % slots], p_scr.at[i % slots]
        rb = _softmax_block_rows(n_keys)
        sums = []
        for r0 in range(0, s_ref.shape[0], rb):
            s = s_ref[r0:r0 + rb, 0:n_keys]
            if fix is not None:
                s = fix(s, r0)
            m = jnp.max(s, axis=-1, keepdims=True)
            p = jnp.exp2((s - m) * c)
            if not mxu_sum:
                sums.append(jnp.sum(p, axis=-1, keepdims=True))
            p_ref[r0:r0 + rb, 0:n_keys] = p.astype(BF16)
        return None if mxu_sum else jnp.concatenate(sums, axis=0)

    def weighted_values(i, den):
        _, values, _, _, sink = maps[i]
        p_ref = p_scr.at[i % slots]
        offs, n_keys = staged.pop(i)
        acc = None
        for o, v in zip(offs, values()):
            part = _dot(p_ref[:, o:o + v.shape[0]], v)
            acc = part if acc is None else acc + part
        if mxu_sum:
            den = _dot(p_ref[:, 0:n_keys], jnp.ones((n_keys, LANES), BF16))
        sink(acc / den)

    groups = [range(g, min(g + group, len(maps))) for g in range(0, len(maps), group)]
    for i in groups[0]:
        stage(i)
    for gi, grp in enumerate(groups):
        if gi + 1 < len(groups):
            for i in groups[gi + 1]:
                stage(i)
        dens = [softmax(i) for i in grp]
        for i, den in zip(grp, dens):
            weighted_values(i, den)


def _attn_scratch(tq, n_keys, group):
    return [pltpu.VMEM((2 * group, tq, n_keys), F32), pltpu.VMEM((2 * group, tq, n_keys), BF16)]


def _upper_half(shape):
    return lax.broadcasted_iota(jnp.int32, shape, 1) >= HEAD_DIM


def _keep_half(x, half):
    upper = _upper_half(x.shape)
    return jnp.where(upper if half else ~upper, x, jnp.zeros_like(x))


def _swap_halves(x):
    return jnp.concatenate([x[:, HEAD_DIM:], x[:, :HEAD_DIM]], axis=1)


def _pair_store(o_ref, c0):
    got = {}

    def make(half):
        def sink(o):
            got[half] = o
            if len(got) == 2:
                o_ref[:, c0:c0 + LANES] = jnp.where(_upper_half(o.shape), got[1], got[0]).astype(BF16)
        return sink
    return make(0), make(1)


def _out_proj_residual(o_scr, h_ref, wout_ref, mod_ref, nw1_ref, h1_ref):
    y = _dot(o_scr[...], wout_ref[...])
    gate = mod_ref[2:3, :]
    nw1 = nw1_ref[...]
    for c in range(y.shape[0] // NORM_ROWS):
        rs = slice(c * NORM_ROWS, (c + 1) * NORM_ROWS)
        h1_ref[rs, :] = h_ref[rs, :] + gate * _rms(y[rs], nw1)


def _mod_kernel(c_ref, w_ref, b_ref, o_ref):
    cv = c_ref[...]
    act = cv / (1.0 + jnp.exp(-cv))
    o_ref[...] = _dot(act.astype(BF16), w_ref[...].astype(BF16)) + b_ref[...]


def _modulation(cvecs, w_mod, b_mod):
    tn = 1024
    n = 6 * D_MODEL
    return pl.pallas_call(
        _mod_kernel,
        grid=(DEPTH, n // tn),
        in_specs=[
            pl.BlockSpec((MOD_ROWS, D_MODEL), lambda l, j: (0, 0)),
            pl.BlockSpec((None, D_MODEL, tn), lambda l, j: (l, 0, j)),
            pl.BlockSpec((None, 1, tn), lambda l, j: (l, 0, j)),
        ],
        out_specs=pl.BlockSpec((None, MOD_ROWS, tn), lambda l, j: (l, 0, j)),
        out_shape=jax.ShapeDtypeStruct((DEPTH, MOD_ROWS, n), F32),
        compiler_params=_cparams("parallel", "parallel"),
        name="adaln_mod",
    )(cvecs, w_mod, b_mod.reshape(DEPTH, 1, n))


def _mod_spec(layer, sample, tm):
    if sample:
        per = DEC_SEQ // tm
        return pl.BlockSpec((None, None, 6, D_MODEL), lambda i, *_: (layer, 1 + i // per, 0, 0))
    return pl.BlockSpec((None, None, 6, D_MODEL), lambda i, *_: (layer, 0, 0, 0))


def _full(shape):
    nd = len(shape)
    return pl.BlockSpec(shape, lambda *_: (0,) * nd)


def _rows(tm, width):
    return pl.BlockSpec((tm, width), lambda i, *_: (i, 0))


def _rope_spec(tm):
    per = DEC_SEQ // tm
    return pl.BlockSpec((tm, LANES), lambda i, *_: (i % per, 0))


EVEN_IN = 2048
MLA_SLOT = 128


def _proj_even_kernel(sample, *refs):
    if sample:
        (h_ref, mod_ref, nw_ref, win_ref, qnw_ref, kvnw_ref, wuq_ref, wuk_ref, wuv_ref,
         ca_ref, sa_ref, ch_ref, sh_ref,
         qa_ref, ka_ref, va_ref, qd_ref, kd_ref, vd_ref) = refs
    else:
        (h_ref, mod_ref, nw_ref, win_ref, qnw_ref, kvnw_ref, wuq_ref, wuk_ref, wuv_ref,
         qa_ref, ka_ref, va_ref, qd_ref, kd_ref, vd_ref,
         ckv_st_ref, kpe_st_ref, kd_st_ref, vd_st_ref) = refs
    mod = mod_ref[...]
    u = _rms(h_ref[...], nw_ref[...]) * (1.0 + mod[1:2]) + mod[0:1]
    z = _dot(u.astype(BF16), win_ref[...])
    cq = _rms(z[:, 0:256], qnw_ref[...]).astype(BF16)
    qa = _dot(cq, wuq_ref[...])
    ckv = _rms(z[:, 256:384], kvnw_ref[...])
    ckv_b = ckv.astype(BF16)
    kn = _dot(ckv_b, wuk_ref[...])
    va_ref[...] = _dot(ckv_b, wuv_ref[...]).astype(BF16)
    kpe_slot = z[:, 384:512]
    qd = z[:, 512:1024]
    kd = z[:, 1024:1536]
    vd = z[:, 1536:2048]
    vd_ref[...] = vd.astype(BF16)
    if sample:
        ca, sa, ch, sh = ca_ref[...], sa_ref[...], ch_ref[...], sh_ref[...]
        qa = _rope(qa, ca, sa, QK_ROPE // 2)
        kpe_rot = _rope(kpe_slot, ca, sa, QK_ROPE // 2)
        qd_ref[...] = _rope(qd, ch, sh, DH_B // 2).astype(BF16)
        kd_ref[...] = _rope(kd, ch, sh, DH_B // 2).astype(BF16)
    else:
        kpe_rot = kpe_slot
        qd_ref[...] = qd.astype(BF16)
        kd_ref[...] = kd.astype(BF16)
        ckv_st_ref[...] = ckv
        kpe_st_ref[...] = kpe_slot[:, QK_NOPE:QK_NOPE + QK_ROPE]
        kd_st_ref[...] = kd.reshape(kd.shape[0], H_B, 2 * DH_B)
        vd_st_ref[...] = vd.reshape(vd.shape[0], H_B, 2 * DH_B)
    qa_ref[...] = qa.astype(BF16)
    for hd in range(H_A):
        sl = slice(hd * MLA_SLOT, (hd + 1) * MLA_SLOT)
        ka_ref[:, sl] = (kn[:, sl] + kpe_rot).astype(BF16)


def _proj_even(h, mod, layer, sample, nw, w, rope):
    n = h.shape[0]
    tm = ROW_TILE
    wide = H_A * MLA_SLOT
    ins = [h, mod, nw, w["w_in"], w["q_norm_w"], w["kv_norm_w"], w["w_uq"], w["w_uk"], w["w_uv"]]
    in_specs = [_rows(tm, D_MODEL), _mod_spec(layer, sample, tm), _full((1, D_MODEL)), _full((D_MODEL, EVEN_IN)),
                _full((1, Q_LORA)), _full((1, KV_LORA)), _full((Q_LORA, wide)), _full((KV_LORA, wide)),
                _full((KV_LORA, 512))]
    widths = [wide, wide, 512, 512, 512, 512]
    out_shape = [jax.ShapeDtypeStruct((n, wd), BF16) for wd in widths]
    out_specs = [_rows(tm, wd) for wd in widths]
    if sample:
        ins += [rope["ca"], rope["sa"], rope["ch"], rope["sh"]]
        in_specs += [_rope_spec(tm)] * 4
    else:
        for wd in (KV_LORA, QK_ROPE):
            out_shape.append(jax.ShapeDtypeStruct((n, wd), F32))
            out_specs.append(_rows(tm, wd))
        for _ in range(2):
            out_shape.append(jax.ShapeDtypeStruct((n, H_B, 2 * DH_B), F32))
            out_specs.append(pl.BlockSpec((tm, H_B, 2 * DH_B), lambda i: (i, 0, 0)))
    return pl.pallas_call(
        functools.partial(_proj_even_kernel, sample),
        grid=(n // tm,),
        in_specs=in_specs,
        out_specs=out_specs,
        out_shape=out_shape,
        compiler_params=_cparams("parallel"),
        name="proj_even_s" if sample else "proj_even_p",
    )(*ins)


def _attn_even_kernel(sample, lam_init, *refs):
    if sample:
        (qa_ref, ka_ref, va_ref, qd_ref, kd_ref, vd_ref,
         cckv_ref, ckpe_ref, cdk_ref, cdv_ref, wuk_ref, wuv_ref, lam_ref, sub_ref,
         h_ref, wout_ref, mod_ref, nw1_ref, h1_ref, s_scr, p_scr, o_ref) = refs
    else:
        (qa_ref, ka_ref, va_ref, qd_ref, kd_ref, vd_ref, lam_ref, sub_ref,
         h_ref, wout_ref, mod_ref, nw1_ref, h1_ref, s_scr, p_scr, o_ref) = refs
    lf = lam_ref[...]
    lam = (jnp.exp(jnp.sum(lf[0:1] * lf[1:2], axis=-1, keepdims=True))
           - jnp.exp(jnp.sum(lf[2:3] * lf[3:4], axis=-1, keepdims=True)) + lam_init)
    if sample:
        cckv = cckv_ref[...].astype(BF16)
        kn_ctx = _dot(cckv, wuk_ref[...])
        va_ctx = _dot(cckv, wuv_ref[...]).astype(BF16)
        n_ctx = cckv.shape[0]
        kpe_ctx = jnp.concatenate([jnp.zeros((n_ctx, QK_NOPE), F32), ckpe_ref[...],
                                   jnp.zeros((n_ctx, MLA_SLOT - QK_NOPE - QK_ROPE), F32)], axis=1)
        kd_ctx = cdk_ref[...].astype(BF16)
        vd_ctx = cdv_ref[...].astype(BF16)
    maps = []
    c_a = (QK_NOPE + QK_ROPE) ** -0.5 * LOG2E
    for hd in range(H_A):
        sl = slice(hd * MLA_SLOT, (hd + 1) * MLA_SLOT)
        if hd % 2 == 0:
            sinks = _pair_store(o_ref, hd * V_A)

        def scores(sl=sl):
            q = qa_ref[:, sl]
            out = [_dot_nt(q, ka_ref[:, sl])]
            if sample:
                out.append(_dot_nt(q, (kn_ctx[:, sl] + kpe_ctx).astype(BF16)))
            return out

        def values(vs=slice(hd // 2 * LANES, (hd // 2 + 1) * LANES)):
            return [va_ref[:, vs]] + ([va_ctx[:, vs]] if sample else [])

        maps.append((scores, values, c_a, None, sinks[hd % 2]))
    c_b = DH_B ** -0.5 * LOG2E
    base = H_A * V_A
    sub_w = sub_ref[...]
    for hd in range(H_B):
        hs = slice(hd * 2 * DH_B, (hd + 1) * 2 * DH_B)
        outs = []

        def sink(o, outs=outs, hs=hs):
            outs.append(o)
            if len(outs) == 2:
                ob = _rms(outs[0] - lam * outs[1], sub_w) * (1.0 - lam_init)
                o_ref[:, base + hs.start:base + hs.stop] = ob.astype(BF16)

        def values(hs=hs):
            return [vd_ref[:, hs]] + ([vd_ctx[:, hs]] if sample else [])

        for comp in range(2):
            def scores(hs=hs, comp=comp):
                q = _keep_half(qd_ref[:, hs], comp)
                out = [_dot_nt(q, kd_ref[:, hs])]
                if sample:
                    out.append(_dot_nt(q, kd_ctx[:, hs]))
                return out

            maps.append((scores, values, c_b, None, sink))
    _attention(maps, s_scr, p_scr, s_scr.shape[0] // 2, mxu_sum=not sample)
    _out_proj_residual(o_ref, h_ref, wout_ref, mod_ref, nw1_ref, h1_ref)


def _attn_even(p, sample, lam_init, caches, w, h, w_out, mod, layer, nw1):
    qa, ka, va, qd, kd, vd = p
    n = qa.shape[0]
    wide = H_A * MLA_SLOT
    if sample:
        tq, per = Q_TILE, DEC_SEQ // Q_TILE
        grid = (DEC_BATCH, per)
        qspec = lambda wd: pl.BlockSpec((tq, wd), lambda b, i: (b * per + i, 0))
        kspec = lambda wd: pl.BlockSpec((DEC_SEQ, wd), lambda b, i: (b, 0))
        cspec = lambda wd: pl.BlockSpec((None, PAST_LEN, wd), lambda b, i: (b, 0, 0))
        ins = [qa, ka, va, qd, kd, vd, *caches, w["w_uk"], w["w_uv"], w["diff_lam"], w["diff_subln_w"]]
        in_specs = [qspec(wide), kspec(wide), kspec(512), qspec(512), kspec(512), kspec(512),
                    cspec(KV_LORA), cspec(QK_ROPE), cspec(512), cspec(512),
                    _full((KV_LORA, wide)), _full((KV_LORA, 512)), _full((4, DH_B)), _full((1, 2 * DH_B))]
        out_spec = qspec(D_MODEL)
        mod_spec = pl.BlockSpec((None, None, 6, D_MODEL), lambda b, i: (layer, 1 + b, 0, 0))
        sem = ("parallel", "parallel")
        n_keys, group = DEC_SEQ + PAST_LEN, LATENT_GROUP
    else:
        tq = SEQ
        grid = (BATCH,)
        spec = lambda wd: pl.BlockSpec((SEQ, wd), lambda b: (b, 0))
        ins = [qa, ka, va, qd, kd, vd, w["diff_lam"], w["diff_subln_w"]]
        in_specs = [spec(wide), spec(wide), spec(512), spec(512), spec(512), spec(512),
                    _full((4, DH_B)), _full((1, 2 * DH_B))]
        out_spec = spec(D_MODEL)
        mod_spec = pl.BlockSpec((None, None, 6, D_MODEL), lambda b: (layer, 0, 0, 0))
        sem = ("parallel",)
        n_keys, group = SEQ, CONTEXT_GROUP
    ins += [h, w_out, mod, nw1]
    in_specs += [out_spec, _full((D_MODEL, D_MODEL)), mod_spec, _full((1, D_MODEL))]
    return pl.pallas_call(
        functools.partial(_attn_even_kernel, sample, lam_init),
        grid=grid,
        in_specs=in_specs,
        out_specs=out_spec,
        out_shape=jax.ShapeDtypeStruct((n, D_MODEL), F32),
        scratch_shapes=_attn_scratch(tq, n_keys, group) + [pltpu.VMEM((tq, D_MODEL), BF16)],
        compiler_params=_cparams(*sem),
        name="attn_even_s" if sample else "attn_even_p",
    )(*ins)


ODD_IN = 2304


def _proj_odd_kernel(sample, *refs):
    if sample:
        (h_ref, mod_ref, nw_ref, win_ref, qw_ref, kw_ref, gq_ref, gk_ref, ch_ref, sh_ref,
         qc_ref, kc_ref, vc_ref, qn_ref, kn_ref, vn_ref) = refs
    else:
        (h_ref, mod_ref, nw_ref, win_ref, qw_ref, kw_ref, gq_ref, gk_ref,
         qc_ref, kc_ref, vc_ref, qn_ref, kn_ref, vn_ref,
         kc_st_ref, vc_st_ref, kn_st_ref, vn_st_ref) = refs
    mod = mod_ref[...]
    u = _rms(h_ref[...], nw_ref[...]) * (1.0 + mod[1:2]) + mod[0:1]
    z = _dot(u.astype(BF16), win_ref[...])
    qc = _group_rms(z[:, 0:512], qw_ref[...], gq_ref[...])
    kc = _group_rms(z[:, 512:640], kw_ref[...], gk_ref[...])
    vc = z[:, 640:768]
    kn = z[:, 1280:1792]
    vn = z[:, 1792:2304]
    vc_ref[...] = vc.astype(BF16)
    qn_ref[...] = z[:, 768:1280].astype(BF16)
    kn_ref[...] = kn.astype(BF16)
    vn_ref[...] = vn.astype(BF16)
    if sample:
        ch, sh = ch_ref[...], sh_ref[...]
        qc_ref[...] = _rope(qc, ch, sh, DH_C // 2).astype(BF16)
        kc_ref[...] = _rope(kc, ch, sh, DH_C // 2).astype(BF16)
    else:
        qc_ref[...] = qc.astype(BF16)
        kc_ref[...] = kc.astype(BF16)
        kc_st_ref[...] = kc.reshape(kc.shape[0], KV_C, DH_C)
        vc_st_ref[...] = vc.reshape(vc.shape[0], KV_C, DH_C)
        kn_st_ref[...] = kn.reshape(kn.shape[0], H_D, DH_D)
        vn_st_ref[...] = vn.reshape(vn.shape[0], H_D, DH_D)


def _proj_odd(h, mod, layer, sample, nw, w, rope):
    n = h.shape[0]
    tm = ROW_TILE
    ins = [h, mod, nw, w["w_in"], w["q_w"], w["k_w"], w["gq"], w["gk"]]
    in_specs = [_rows(tm, D_MODEL), _mod_spec(layer, sample, tm), _full((1, D_MODEL)), _full((D_MODEL, ODD_IN)),
                _full((1, 512)), _full((1, 128)), _full((512, 512)), _full((128, 128))]
    widths = [512, 128, 128, 512, 512, 512]
    out_shape = [jax.ShapeDtypeStruct((n, wd), BF16) for wd in widths]
    out_specs = [_rows(tm, wd) for wd in widths]
    if sample:
        ins += [rope["ch"], rope["sh"]]
        in_specs += [_rope_spec(tm)] * 2
    else:
        for heads, dh in ((KV_C, DH_C), (KV_C, DH_C), (H_D, DH_D), (H_D, DH_D)):
            out_shape.append(jax.ShapeDtypeStruct((n, heads, dh), F32))
            out_specs.append(pl.BlockSpec((tm, heads, dh), lambda i: (i, 0, 0)))
    return pl.pallas_call(
        functools.partial(_proj_odd_kernel, sample),
        grid=(n // tm,),
        in_specs=in_specs,
        out_specs=out_specs,
        out_shape=out_shape,
        compiler_params=_cparams("parallel"),
        name="proj_odd_s" if sample else "proj_odd_p",
    )(*ins)


def _attn_odd_prompt_kernel(qc_ref, kc_ref, vc_ref, qn_ref, kn_ref, vn_ref, h_ref, wout_ref, mod_ref, nw1_ref,
                            h1_ref, s_scr, p_scr, o_ref):
    c = DH_C ** -0.5 * LOG2E
    group = H_C // KV_C
    kc = (kc_ref[...], _swap_halves(kc_ref[...]))
    vc = (vc_ref[...], _swap_halves(vc_ref[...]))
    maps = []
    for hd in range(H_C):
        half, swap = hd % 2, (hd // group) != (hd % 2)
        ps = slice(hd // 2 * LANES, (hd // 2 + 1) * LANES)
        if half == 0:
            sinks = _pair_store(o_ref, ps.start)
        maps.append((lambda ps=ps, half=half, swap=swap: [_dot_nt(_keep_half(qc_ref[:, ps], half), kc[swap])],
                     lambda swap=swap: [vc[swap]], c, None, sinks[half]))
    base = H_C * DH_C
    c = DH_D ** -0.5 * LOG2E
    for hd in range(H_D):
        half = hd % 2
        ps = slice(hd // 2 * LANES, (hd // 2 + 1) * LANES)
        if half == 0:
            sinks = _pair_store(o_ref, base + ps.start)
        maps.append((lambda ps=ps, half=half: [_dot_nt(_keep_half(qn_ref[:, ps], half), kn_ref[:, ps])],
                     lambda ps=ps: [vn_ref[:, ps]], c, None, sinks[half]))
    _attention(maps, s_scr, p_scr, s_scr.shape[0] // 2, mxu_sum=True)
    _out_proj_residual(o_ref, h_ref, wout_ref, mod_ref, nw1_ref, h1_ref)


def _attn_odd_prompt(p, h, w_out, mod, layer, nw1):
    qc, kc, vc, qn, kn, vn = p
    spec = lambda wd: pl.BlockSpec((SEQ, wd), lambda b: (b, 0))
    return pl.pallas_call(
        _attn_odd_prompt_kernel,
        grid=(BATCH,),
        in_specs=[spec(512), spec(128), spec(128), spec(512), spec(512), spec(512),
                  spec(D_MODEL), _full((D_MODEL, D_MODEL)),
                  pl.BlockSpec((None, None, 6, D_MODEL), lambda b: (layer, 0, 0, 0)), _full((1, D_MODEL))],
        out_specs=spec(D_MODEL),
        out_shape=jax.ShapeDtypeStruct((qc.shape[0], D_MODEL), F32),
        scratch_shapes=_attn_scratch(SEQ, SEQ, CONTEXT_GROUP) + [pltpu.VMEM((SEQ, D_MODEL), BF16)],
        compiler_params=_cparams("parallel"),
        name="attn_odd_p",
    )(qc, kc, vc, qn, kn, vn, h, w_out, mod, nw1)


def _gqa_sample_kernel(q_ref, k_ref, v_ref, ck_ref, cv_ref, o_ref, s_scr, p_scr):
    c = DH_C ** -0.5 * LOG2E
    group = H_C // KV_C
    k_loc = (k_ref[...], _swap_halves(k_ref[...]))
    v_loc = (v_ref[...], _swap_halves(v_ref[...]))
    k_ctx = ck_ref[...].astype(BF16)
    v_ctx = cv_ref[...].astype(BF16)
    k_ctx = (k_ctx, _swap_halves(k_ctx))
    v_ctx = (v_ctx, _swap_halves(v_ctx))
    maps = []
    for hd in range(H_C):
        half, swap = hd % 2, (hd // group) != (hd % 2)
        ps = slice(hd // 2 * LANES, (hd // 2 + 1) * LANES)
        if half == 0:
            sinks = _pair_store(o_ref, ps.start)

        def scores(ps=ps, half=half, swap=swap):
            q = _keep_half(q_ref[:, ps], half)
            return [_dot_nt(q, k_loc[swap]), _dot_nt(q, k_ctx[swap])]

        maps.append((scores, lambda swap=swap: [v_loc[swap], v_ctx[swap]], c, None, sinks[half]))
    _attention(maps, s_scr, p_scr, s_scr.shape[0] // 2)


def _gqa_sample(qc, kc, vc, cache_k, cache_v):
    tq, per = Q_TILE, DEC_SEQ // Q_TILE
    return pl.pallas_call(
        _gqa_sample_kernel,
        grid=(DEC_BATCH, per),
        in_specs=[pl.BlockSpec((tq, 512), lambda b, i: (b * per + i, 0)),
                  pl.BlockSpec((DEC_SEQ, 128), lambda b, i: (b, 0)),
                  pl.BlockSpec((DEC_SEQ, 128), lambda b, i: (b, 0)),
                  pl.BlockSpec((None, PAST_LEN, 128), lambda b, i: (b, 0, 0)),
                  pl.BlockSpec((None, PAST_LEN, 128), lambda b, i: (b, 0, 0))],
        out_specs=pl.BlockSpec((tq, 512), lambda b, i: (b * per + i, 0)),
        out_shape=jax.ShapeDtypeStruct((qc.shape[0], 512), BF16),
        scratch_shapes=_attn_scratch(tq, DEC_SEQ + PAST_LEN, LATENT_GROUP),
        compiler_params=_cparams("parallel", "parallel"),
        name="gqa_s",
    )(qc, kc, vc, cache_k, cache_v)


NA_ROWS = DEC_SEQ // GRID_W
NA_KR = min(NA_WIN_ROWS, NA_ROWS)
NA_LOC = NA_KR * GRID_W


def _na_sample_kernel(q_ref, k_ref, v_ref, ck_ref, cv_ref, bias_ref, o_ref, s_scr, p_scr):
    r = pl.program_id(1)
    rs = jnp.clip(r - NA_KR // 2, 0, NA_ROWS - NA_KR)
    start = pl.multiple_of(rs * GRID_W, GRID_W)
    scale = DH_D ** -0.5
    k_loc = k_ref[pl.ds(start, NA_LOC), :]
    v_loc = v_ref[pl.ds(start, NA_LOC), :]
    k_ctx = ck_ref[...].astype(BF16)
    v_ctx = cv_ref[...].astype(BF16)
    rb = _softmax_block_rows(NA_LOC + PAST_LEN)
    col_ok = {}
    for r0 in range(0, GRID_W, rb):
        wq = lax.broadcasted_iota(jnp.int32, (rb, NA_LOC), 0) + r0
        wk = lax.broadcasted_iota(jnp.int32, (rb, NA_LOC), 1) % GRID_W
        cs = jnp.clip(wq - NA_WIN_COLS // 2, 0, GRID_W - NA_WIN_COLS)
        col_ok[r0] = (wk >= cs) & (wk < cs + NA_WIN_COLS)
    maps = []
    for hd in range(H_D):
        half = hd % 2
        ps = slice(hd // 2 * LANES, (hd // 2 + 1) * LANES)
        if half == 0:
            sinks = _pair_store(o_ref, ps.start)

        def fix(s, r0, hd=hd):
            loc = s[:, :NA_LOC] * scale + bias_ref[hd, r0:r0 + rb, :]
            return jnp.concatenate([jnp.where(col_ok[r0], loc, NEG_INF), s[:, NA_LOC:] * scale], axis=1)

        def scores(ps=ps, half=half):
            q = _keep_half(q_ref[:, ps], half)
            return [_dot_nt(q, k_loc[:, ps]), _dot_nt(q, k_ctx[:, ps])]

        maps.append((scores, lambda ps=ps: [v_loc[:, ps], v_ctx[:, ps]], LOG2E, fix, sinks[half]))
    _attention(maps, s_scr, p_scr, s_scr.shape[0] // 2)


def _na_sample(qn, kn, vn, cache_k, cache_v, bias):
    def bias_map(b, r):
        rs = jnp.clip(r - NA_KR // 2, 0, NA_ROWS - NA_KR)
        return (0, rs - r + NA_WIN_ROWS - 1, 0, 0)
    return pl.pallas_call(
        _na_sample_kernel,
        grid=(DEC_BATCH, NA_ROWS),
        in_specs=[pl.BlockSpec((GRID_W, 512), lambda b, r: (b * NA_ROWS + r, 0)),
                  pl.BlockSpec((DEC_SEQ, 512), lambda b, r: (b, 0)),
                  pl.BlockSpec((DEC_SEQ, 512), lambda b, r: (b, 0)),
                  pl.BlockSpec((None, PAST_LEN, 512), lambda b, r: (b, 0, 0)),
                  pl.BlockSpec((None, PAST_LEN, 512), lambda b, r: (b, 0, 0)),
                  pl.BlockSpec((H_D, None, GRID_W, NA_LOC), bias_map)],
        out_specs=pl.BlockSpec((GRID_W, 512), lambda b, r: (b * NA_ROWS + r, 0)),
        out_shape=jax.ShapeDtypeStruct((qn.shape[0], 512), BF16),
        scratch_shapes=_attn_scratch(GRID_W, NA_LOC + PAST_LEN, CONTEXT_GROUP),
        compiler_params=_cparams("parallel", "parallel"),
        name="na_s",
    )(qn, kn, vn, cache_k, cache_v, bias)


def _na_bias_table(rpb):
    edge = GRID_W - NA_WIN_COLS
    n_dr = 2 * NA_WIN_ROWS - 1
    v = jnp.pad(rpb.astype(F32), ((0, 0), (0, 0), (edge, edge + 1)), mode="edge")
    skew = jnp.tile(v, (1, 1, GRID_W))[:, :, :GRID_W * (2 * GRID_W - 1)]
    toep = skew.reshape(H_D, n_dr, GRID_W, 2 * GRID_W - 1)[:, :, :, GRID_W - 1:]
    flat = toep.transpose(0, 2, 1, 3).reshape(H_D, GRID_W, n_dr * GRID_W)
    return jnp.stack([flat[:, :, d0 * GRID_W:d0 * GRID_W + NA_LOC] for d0 in range(NA_WIN_ROWS)], axis=1)


def _post_attn_kernel(n_parts, *refs):
    o_refs = refs[:n_parts]
    h_ref, wout_ref, mod_ref, nw1_ref, h1_ref = refs[n_parts:]
    y = None
    off = 0
    for o_ref in o_refs:
        wd = o_ref.shape[1]
        part = _dot(o_ref[...], wout_ref[off:off + wd, :])
        y = part if y is None else y + part
        off += wd
    mod = mod_ref[...]
    h1_ref[...] = h_ref[...] + mod[2:3] * _rms(y, nw1_ref[...])


def _post_attn(o_parts, h, w_out, mod, layer, sample, nw1):
    n = h.shape[0]
    tm = ROW_TILE
    in_specs = [_rows(tm, o.shape[1]) for o in o_parts]
    in_specs += [_rows(tm, D_MODEL), _full((D_MODEL, D_MODEL)), _mod_spec(layer, sample, tm), _full((1, D_MODEL))]
    return pl.pallas_call(
        functools.partial(_post_attn_kernel, len(o_parts)),
        grid=(n // tm,),
        in_specs=in_specs,
        out_specs=_rows(tm, D_MODEL),
        out_shape=jax.ShapeDtypeStruct((n, D_MODEL), F32),
        compiler_params=_cparams("parallel"),
        name="post_attn_s" if sample else "post_attn_p",
    )(*o_parts, h, w_out, mod, nw1)


FF_PAIR = 2 * FF_CHUNK
N_FF = D_FF // FF_CHUNK
SUBLANES = 8
ACT_TILES = 4
ROW_BLOCK = 1024
NORM_ROWS = 32


def _ffn_kernel(seq_len, h_ref, wup_ref, cw_ref, cb_ref, wd_ref, mod_ref, nw2_ref, nw3_ref, o_ref,
                u_ref, z0_ref, z1_ref, a_ref, a_last_ref):
    tm = h_ref.shape[0]
    n_blocks = tm // ROW_BLOCK
    rows = ACT_TILES * SUBLANES
    mod = mod_ref[...]
    sub = lax.broadcasted_iota(jnp.int32, (SUBLANES, LANES), 0)
    zero_rows = jnp.zeros((SUBLANES, FF_PAIR), F32)
    for z_ref in (z0_ref, z1_ref):
        z_ref[0:SUBLANES, :] = zero_rows
        z_ref[SUBLANES + tm:2 * SUBLANES + tm, :] = zero_rows

    nw2 = nw2_ref[...]

    def pre_norm(blk):
        for c in range(ROW_BLOCK // NORM_ROWS):
            rs = slice(blk * ROW_BLOCK + c * NORM_ROWS, blk * ROW_BLOCK + (c + 1) * NORM_ROWS)
            u = _rms(h_ref[rs, :], nw2) * (1.0 + mod[4:5]) + mod[3:4]
            u_ref[rs, :] = u.astype(BF16)

    def up(j, z_ref, blk):
        r0 = blk * ROW_BLOCK
        u = u_ref[r0:r0 + ROW_BLOCK, :]
        rows_ = slice(SUBLANES + r0, SUBLANES + r0 + ROW_BLOCK)
        for half in range(2):
            c0 = half * D_FF + j * FF_CHUNK
            if not isinstance(c0, int):
                c0 = pl.multiple_of(c0, FF_CHUNK)
            z_ref[rows_, half * FF_CHUNK:(half + 1) * FF_CHUNK] = _dot(u, wup_ref[:, pl.ds(c0, FF_CHUNK)])

    def act(j, z_ref, col, blk, dst_ref=a_ref):
        cw = cw_ref[j]
        cb = cb_ref[j]
        for lc in range(FF_CHUNK // LANES):
            taps = []
            for lane0 in (lc * LANES, FF_CHUNK + lc * LANES):
                lanes = slice(lane0, lane0 + LANES)
                taps.append([jnp.broadcast_to(cw[k:k + 1, lanes], (SUBLANES, LANES)) for k in range(3)]
                            + [jnp.broadcast_to(cb[:, lanes], (SUBLANES, LANES))])
            for c in range(ROW_BLOCK // rows):
                r = blk * ROW_BLOCK + c * rows
                first = r % seq_len == 0
                last = (r + rows) % seq_len == 0

                def conv(lane0, tap):
                    ext = z_ref[r:r + rows + 2 * SUBLANES, lane0:lane0 + LANES]
                    tiles = [ext[t * SUBLANES:(t + 1) * SUBLANES] for t in range(ACT_TILES + 2)]
                    down = [pltpu.roll(t, 1, 0) for t in tiles[:-1]]
                    up_ = [pltpu.roll(t, SUBLANES - 1, 0) for t in tiles[1:]]
                    out = []
                    for t in range(ACT_TILES):
                        above = 0.0 if (first and t == 0) else down[t]
                        below = 0.0 if (last and t == ACT_TILES - 1) else up_[t + 1]
                        prev = jnp.where(sub == 0, above, down[t + 1])
                        nxt = jnp.where(sub == SUBLANES - 1, below, up_[t])
                        out.append(prev * tap[0] + tiles[t + 1] * tap[1] + nxt * tap[2] + tap[3])
                    return jnp.concatenate(out, axis=0)

                g = conv(lc * LANES, taps[0])
                v = conv(FF_CHUNK + lc * LANES, taps[1])
                a = (g / (1.0 + jnp.exp2(g * -LOG2E))) * v
                lane = col + lc * LANES
                if not isinstance(lane, int):
                    lane = pl.multiple_of(lane, LANES)
                dst_ref[r:r + rows, pl.ds(lane, LANES)] = a.astype(BF16)

    for blk in range(n_blocks):
        pre_norm(blk)
        up(0, z0_ref, blk)

    def pair(i, carry):
        j = 2 * i
        col = pl.multiple_of(j * FF_CHUNK, FF_CHUNK)
        for blk in range(n_blocks):
            up(j + 1, z1_ref, blk)
            act(j, z0_ref, col, blk)
        for blk in range(n_blocks):
            up(j + 2, z0_ref, blk)
            act(j + 1, z1_ref, col + FF_CHUNK, blk)
        return carry

    lax.fori_loop(0, (N_FF - 1) // 2, pair, 0)
    nw3 = nw3_ref[...]
    k_main = (N_FF - 1) * FF_CHUNK
    for blk in range(n_blocks):
        r0 = blk * ROW_BLOCK
        y = _dot(a_ref[r0:r0 + ROW_BLOCK, 0:k_main], wd_ref[0:k_main, :])
        act(N_FF - 1, z0_ref, 0, blk, a_last_ref)
        y = y + _dot(a_last_ref[r0:r0 + ROW_BLOCK, :], wd_ref[k_main:D_FF, :])
        for c in range(ROW_BLOCK // NORM_ROWS):
            rs = slice(r0 + c * NORM_ROWS, r0 + (c + 1) * NORM_ROWS)
            o_ref[rs, :] = h_ref[rs, :] + mod[5:6] * _rms(y[c * NORM_ROWS:(c + 1) * NORM_ROWS], nw3)


def _ffn(h1, w, mod, layer, sample, nw2, nw3):
    n = h1.shape[0]
    tm = FFN_ROW_TILE
    seq_len = DEC_SEQ if sample else SEQ
    once = pl.Buffered(1)
    in_specs = [
        _rows(tm, D_MODEL),
        pl.BlockSpec((None, D_MODEL, 2 * D_FF), lambda i: (layer, 0, 0), pipeline_mode=once),
        pl.BlockSpec((N_FF, 3, FF_PAIR), lambda i: (0, 0, 0), pipeline_mode=once),
        pl.BlockSpec((N_FF, 1, FF_PAIR), lambda i: (0, 0, 0), pipeline_mode=once),
        pl.BlockSpec((None, D_FF, D_MODEL), lambda i: (layer, 0, 0), pipeline_mode=once),
        _mod_spec(layer, sample, tm),
        _full((1, D_MODEL)),
        _full((1, D_MODEL)),
    ]
    return pl.pallas_call(
        functools.partial(_ffn_kernel, seq_len),
        grid=(n // tm,),
        in_specs=in_specs,
        out_specs=_rows(tm, D_MODEL),
        out_shape=jax.ShapeDtypeStruct((n, D_MODEL), F32),
        scratch_shapes=[pltpu.VMEM((tm, D_MODEL), BF16), pltpu.VMEM((tm + 2 * SUBLANES, FF_PAIR), F32),
                        pltpu.VMEM((tm + 2 * SUBLANES, FF_PAIR), F32), pltpu.VMEM((tm, D_FF - FF_CHUNK), BF16),
                        pltpu.VMEM((tm, FF_CHUNK), BF16)],
        compiler_params=pltpu.CompilerParams(dimension_semantics=("parallel",), vmem_limit_bytes=FFN_VMEM_LIMIT),
        name="ffn_s" if sample else "ffn_p",
    )(h1, w["w_up"], w["conv_w"], w["conv_b"], w["w_down"], mod, nw2, nw3)


def _pair_chunks(x):
    lead = x.shape[:-1]
    x = x.reshape(lead + (2, N_FF, FF_CHUNK))
    x = jnp.moveaxis(x, -2, 0)
    return x.reshape((N_FF,) + lead + (FF_PAIR,))


def _rope_tables():
    def table(rot_dim):
        t = np.arange(DEC_SEQ)
        n_freq = rot_dim // 4
        inv = 1.0 / (ROPE_THETA ** (np.arange(n_freq) / n_freq))
        ang = np.concatenate([(t // GRID_W)[:, None] * inv[None, :], (t % GRID_W)[:, None] * inv[None, :]], axis=-1)
        cos = np.cos(ang).astype(np.float32)
        sin = np.sin(ang).astype(np.float32)
        reps = LANES // rot_dim
        return (np.tile(np.concatenate([cos, cos], axis=-1), (1, reps)),
                np.tile(np.concatenate([-sin, sin], axis=-1), (1, reps)))
    ca, sa = table(QK_ROPE)
    ch, sh = table(HEAD_DIM)
    rope_lanes = (np.arange(MLA_SLOT) >= QK_NOPE) & (np.arange(MLA_SLOT) < QK_NOPE + QK_ROPE)
    ca = np.where(rope_lanes[None, :], ca, 1.0).astype(np.float32)
    sa = np.where(rope_lanes[None, :], sa, 0.0).astype(np.float32)
    return {"ca": jnp.asarray(ca), "sa": jnp.asarray(sa), "ch": jnp.asarray(ch), "sh": jnp.asarray(sh)}


def _group_mean_matrix(width):
    idx = np.arange(width) // HEAD_DIM
    return jnp.asarray((idx[:, None] == idx[None, :]).astype(np.float32) / HEAD_DIM, BF16)


def kernel(x_prompt, x_sample, c, cache_mla_ckv, cache_mla_kpe, cache_diff_k, cache_diff_v, cache_gqa_k, cache_gqa_v, cache_na_k, cache_na_v, c_ctx, norm_w, w_mod, b_mod, w_in_even, w_out_even, w_uq, q_norm_w, kv_norm_w, w_uk, w_uv, diff_lam, diff_subln_w, w_in_odd, w_out_odd, qk_norm_w, na_rpb, w_up, conv_w, conv_b, w_down):
    rope = _rope_tables()
    n_p = BATCH * SEQ
    n_s = DEC_BATCH * DEC_SEQ
    cvecs = jnp.concatenate([c_ctx[None, :], c, jnp.zeros((MOD_ROWS - 1 - DEC_BATCH, D_MODEL), F32)], axis=0)
    mod = _modulation(cvecs, w_mod, b_mod).reshape(DEPTH, MOD_ROWS, 6, D_MODEL)
    hp = x_prompt.reshape(n_p, D_MODEL)
    hs = x_sample.reshape(n_s, D_MODEL)
    even_states, odd_states = [], []
    w_up_b = w_up.astype(BF16)
    w_down_b = w_down.astype(BF16)
    for l in range(DEPTH):
        i = l // 2
        nw = [norm_w[l, k][None, :] for k in range(4)]
        if l % 2 == 0:
            lam_init = 0.8 - 0.6 * math.exp(-0.3 * l)
            wi = w_in_even[i]
            w_uq3 = w_uq[i].reshape(Q_LORA, H_A, QK_NOPE + QK_ROPE)
            w = {
                "w_in": jnp.concatenate([wi[:, :384], jnp.zeros((D_MODEL, QK_NOPE), F32), wi[:, 384:416],
                                         jnp.zeros((D_MODEL, MLA_SLOT - QK_NOPE - QK_ROPE), F32), wi[:, 416:]],
                                        axis=1).astype(BF16),
                "q_norm_w": q_norm_w[i][None, :],
                "kv_norm_w": kv_norm_w[i][None, :],
                "w_uq": jnp.pad(w_uq3, ((0, 0), (0, 0), (0, MLA_SLOT - QK_NOPE - QK_ROPE))
                                ).reshape(Q_LORA, H_A * MLA_SLOT).astype(BF16),
                "w_uk": jnp.pad(w_uk[i].reshape(KV_LORA, H_A, QK_NOPE), ((0, 0), (0, 0), (0, MLA_SLOT - QK_NOPE))
                                ).reshape(KV_LORA, H_A * MLA_SLOT).astype(BF16),
                "w_uv": w_uv[i].astype(BF16),
                "diff_lam": diff_lam[i],
                "diff_subln_w": diff_subln_w[i][None, :],
            }
            outs_p = _proj_even(hp, mod, l, False, nw[0], w, rope)
            outs_s = _proj_even(hs, mod, l, True, nw[0], w, rope)
            even_states.append(outs_p[6:])
            w_out = w_out_even[i].astype(BF16)
            h1p = _attn_even(outs_p[:6], False, lam_init, None, w, hp, w_out, mod, l, nw[1])
            caches = (cache_mla_ckv[:, i], cache_mla_kpe[:, i],
                      cache_diff_k[:, i].reshape(DEC_BATCH, PAST_LEN, 512),
                      cache_diff_v[:, i].reshape(DEC_BATCH, PAST_LEN, 512))
            h1s = _attn_even(outs_s, True, lam_init, caches, w, hs, w_out, mod, l, nw[1])
        else:
            q_w = jnp.tile(qk_norm_w[i, 0], H_C)[None, :]
            k_w = jnp.tile(qk_norm_w[i, 1], KV_C)[None, :]
            w = {"w_in": w_in_odd[i].astype(BF16), "q_w": q_w, "k_w": k_w,
                 "gq": _group_mean_matrix(512), "gk": _group_mean_matrix(128)}
            outs_p = _proj_odd(hp, mod, l, False, nw[0], w, rope)
            outs_s = _proj_odd(hs, mod, l, True, nw[0], w, rope)
            odd_states.append(outs_p[6:])
            w_out = w_out_odd[i].astype(BF16)
            h1p = _attn_odd_prompt(outs_p[:6], hp, w_out, mod, l, nw[1])
            qc, kc, vc, qn, kn, vn = outs_s
            o_c = _gqa_sample(qc, kc, vc, cache_gqa_k[:, i].reshape(DEC_BATCH, PAST_LEN, 128),
                              cache_gqa_v[:, i].reshape(DEC_BATCH, PAST_LEN, 128))
            o_d = _na_sample(qn, kn, vn, cache_na_k[:, i].reshape(DEC_BATCH, PAST_LEN, 512),
                             cache_na_v[:, i].reshape(DEC_BATCH, PAST_LEN, 512), _na_bias_table(na_rpb[i]))
            h1s = _post_attn([o_c, o_d], hs, w_out, mod, l, True, nw[1])
        wf = {"w_up": w_up_b, "conv_w": _pair_chunks(conv_w[l]),
              "conv_b": _pair_chunks(conv_b[l][None, :]), "w_down": w_down_b}
        hp = _ffn(h1p, wf, mod, l, False, nw[2], nw[3])
        hs = _ffn(h1s, wf, mod, l, True, nw[2], nw[3])

    def stack(states, k, shape):
        return jnp.stack([st[k].reshape((BATCH, SEQ) + shape) for st in states], axis=1)

    new_mla_ckv = stack(even_states, 0, (KV_LORA,))
    new_mla_kpe = stack(even_states, 1, (QK_ROPE,))
    new_diff_k = stack(even_states, 2, (H_B, 2 * DH_B))
    new_diff_v = stack(even_states, 3, (H_B, 2 * DH_B))
    new_gqa_k = stack(odd_states, 0, (KV_C, DH_C))
    new_gqa_v = stack(odd_states, 1, (KV_C, DH_C))
    new_na_k = stack(odd_states, 2, (H_D, DH_D))
    new_na_v = stack(odd_states, 3, (H_D, DH_D))
    return (hp.reshape(BATCH, SEQ, D_MODEL), hs.reshape(DEC_BATCH, DEC_SEQ, D_MODEL),
            new_mla_ckv, new_mla_kpe, new_diff_k, new_diff_v, new_gqa_k, new_gqa_v, new_na_k, new_na_v)
```

```python
import functools
import math

import numpy as np
import jax
import jax.numpy as jnp
from jax import lax
from jax.experimental import pallas as pl
from jax.experimental.pallas import tpu as pltpu

D_MODEL = 1024
BATCH = 32
SEQ = 256
DEPTH = 2
DEC_BATCH = 4
DEC_SEQ = 1024
PAST_LEN = 256
GRID_W = 64
HEAD_DIM = 64
H_A = 8
QK_NOPE = 64
QK_ROPE = 32
V_A = 64
Q_LORA = 256
KV_LORA = 128
H_B = 4
DH_B = HEAD_DIM
H_C = 8
KV_C = 2
DH_C = HEAD_DIM
H_D = 8
DH_D = HEAD_DIM
NA_WIN_ROWS = 8
NA_WIN_COLS = 16
D_FF = 2816
ROPE_THETA = 10000.0
EPS = 1e-6
NEG_INF = -1e30

LANES = 128
MOD_ROWS = 8
ROW_TILE = 512
FFN_ROW_TILE = 1024
FF_CHUNK = 256
Q_TILE = 256
LATENT_GROUP = 1
CONTEXT_GROUP = 4
VMEM_LIMIT = 48 * 1024 * 1024
FFN_VMEM_LIMIT = 56 * 1024 * 1024

F32 = jnp.float32
BF16 = jnp.bfloat16
LOG2E = math.log2(math.e)


def _cparams(*sem):
    return pltpu.CompilerParams(dimension_semantics=sem, vmem_limit_bytes=VMEM_LIMIT)


def _dot(a, b):
    return jnp.dot(a, b, preferred_element_type=F32)


def _dot_nt(a, b):
    return lax.dot_general(a, b, (((1,), (1,)), ((), ())), preferred_element_type=F32)


def _rms(x, w):
    return x * lax.rsqrt(jnp.mean(x * x, axis=-1, keepdims=True) + EPS) * w


def _group_rms(x, w, gmat):
    x2 = x * x
    hi = x2.astype(BF16)
    lo = (x2 - hi.astype(F32)).astype(BF16)
    ms = _dot(hi, gmat) + _dot(lo, gmat)
    return x * lax.rsqrt(ms + EPS) * w


def _rope(x, cos, sin_signed, half):
    outs = []
    for j in range(x.shape[1] // LANES):
        xc = x[:, j * LANES:(j + 1) * LANES]
        lane = lax.broadcasted_iota(jnp.int32, xc.shape, 1)
        first = (lane % (2 * half)) < half
        partner = jnp.where(first, pltpu.roll(xc, LANES - half, 1), pltpu.roll(xc, half, 1))
        outs.append(xc * cos + partner * sin_signed)
    return outs[0] if len(outs) == 1 else jnp.concatenate(outs, axis=1)


def _softmax_block_rows(n_keys):
    return max(16, min(64, (16 * 1280 // n_keys) // 16 * 16))


def _attention(maps, s_scr, p_scr, group, mxu_sum=False):
    slots = s_scr.shape[0]
    staged = {}

    def on_mxu(i):
        return bool(mxu_sum) and (mxu_sum != "alternate" or i % 2 == 1)

    def stage(i):
        s_ref = s_scr.at[i % slots]
        offs, off = [], 0
        for s in maps[i][0]():
            s_ref[:, off:off + s.shape[1]] = s
            offs.append(off)
            off += s.shape[1]
        staged[i] = (offs, off)

    def softmax(i):
        _, _, c, fix, _ = maps[i]
        n_keys = staged[i][1]
        s_ref, p_ref = s_scr.at[i % slots], p_scr.at[i % slots]
        rb = _softmax_block_rows(n_keys)
        sums = []
        for r0 in range(0, s_ref.shape[0], rb):
            s = s_ref[r0:r0 + rb, 0:n_keys]
            if fix is not None:
                s = fix(s, r0)
            m = jnp.max(s, axis=-1, keepdims=True)
            p = jnp.exp2((s - m) * c)
            if not on_mxu(i):
                sums.append(jnp.sum(p, axis=-1, keepdims=True))
            p_ref[r0:r0 + rb, 0:n_keys] = p.astype(BF16)
        return None if on_mxu(i) else jnp.concatenate(sums, axis=0)

    def weighted_values(i, den):
        _, values, _, _, sink = maps[i]
        p_ref = p_scr.at[i % slots]
        offs, n_keys = staged.pop(i)
        acc = None
        for o, v in zip(offs, values()):
            part = _dot(p_ref[:, o:o + v.shape[0]], v)
            acc = part if acc is None else acc + part
        if on_mxu(i):
            den = _dot(p_ref[:, 0:n_keys], jnp.ones((n_keys, LANES), BF16))
        sink(acc / den)

    groups = [range(g, min(g + group, len(maps))) for g in range(0, len(maps), group)]
    for i in groups[0]:
        stage(i)
    for gi, grp in enumerate(groups):
        if gi + 1 < len(groups):
            for i in groups[gi + 1]:
                stage(i)
        dens = [softmax(i) for i in grp]
        for i, den in zip(grp, dens):
            weighted_values(i, den)


def _attn_scratch(tq, n_keys, group):
    return [pltpu.VMEM((2 * group, tq, n_keys), F32), pltpu.VMEM((2 * group, tq, n_keys), BF16)]


def _upper_half(shape):
    return lax.broadcasted_iota(jnp.int32, shape, 1) >= HEAD_DIM


def _keep_half(x, half):
    upper = _upper_half(x.shape)
    return jnp.where(upper if half else ~upper, x, jnp.zeros_like(x))


def _swap_halves(x):
    return jnp.concatenate([x[:, HEAD_DIM:], x[:, :HEAD_DIM]], axis=1)


def _pair_store(o_ref, c0, rows=slice(None)):
    got = {}

    def make(half):
        def sink(o):
            got[half] = o
            if len(got) == 2:
                o_ref[rows, c0:c0 + LANES] = jnp.where(_upper_half(o.shape), got[1], got[0]).astype(BF16)
        return sink
    return make(0), make(1)


def _out_proj_residual(o_scr, h_ref, wout_ref, mod_ref, nw1_ref, h1_ref):
    y = _dot(o_scr[...], wout_ref[...])
    gate = mod_ref[2:3, :]
    nw1 = nw1_ref[...]
    for c in range(y.shape[0] // NORM_ROWS):
        rs = slice(c * NORM_ROWS, (c + 1) * NORM_ROWS)
        h1_ref[rs, :] = h_ref[rs, :] + gate * _rms(y[rs], nw1)


def _mod_kernel(c_ref, w_ref, b_ref, o_ref):
    cv = c_ref[...]
    act = cv / (1.0 + jnp.exp(-cv))
    o_ref[...] = _dot(act.astype(BF16), w_ref[...].astype(BF16)) + b_ref[...]


def _modulation(cvecs, w_mod, b_mod):
    tn = 1024
    n = 6 * D_MODEL
    return pl.pallas_call(
        _mod_kernel,
        grid=(DEPTH, n // tn),
        in_specs=[
            pl.BlockSpec((MOD_ROWS, D_MODEL), lambda l, j: (0, 0)),
            pl.BlockSpec((None, D_MODEL, tn), lambda l, j: (l, 0, j)),
            pl.BlockSpec((None, 1, tn), lambda l, j: (l, 0, j)),
        ],
        out_specs=pl.BlockSpec((None, MOD_ROWS, tn), lambda l, j: (l, 0, j)),
        out_shape=jax.ShapeDtypeStruct((DEPTH, MOD_ROWS, n), F32),
        compiler_params=_cparams("parallel", "parallel"),
        name="adaln_mod",
    )(cvecs, w_mod, b_mod.reshape(DEPTH, 1, n))


def _mod_spec(layer, sample, tm):
    if sample:
        per = DEC_SEQ // tm
        return pl.BlockSpec((None, None, 6, D_MODEL), lambda i, *_: (layer, 1 + i // per, 0, 0))
    return pl.BlockSpec((None, None, 6, D_MODEL), lambda i, *_: (layer, 0, 0, 0))


def _full(shape):
    nd = len(shape)
    return pl.BlockSpec(shape, lambda *_: (0,) * nd)


def _rows(tm, width):
    return pl.BlockSpec((tm, width), lambda i, *_: (i, 0))


def _rope_spec(tm):
    per = DEC_SEQ // tm
    return pl.BlockSpec((tm, LANES), lambda i, *_: (i % per, 0))


EVEN_IN = 2048
MLA_SLOT = 128


def _proj_even_kernel(sample, *refs):
    if sample:
        (h_ref, mod_ref, nw_ref, win_ref, qnw_ref, kvnw_ref, wuq_ref, wuk_ref, wuv_ref,
         ca_ref, sa_ref, ch_ref, sh_ref,
         qa_ref, ka_ref, va_ref, qd_ref, kd_ref, vd_ref) = refs
    else:
        (h_ref, mod_ref, nw_ref, win_ref, qnw_ref, kvnw_ref, wuq_ref, wuk_ref, wuv_ref,
         qa_ref, ka_ref, va_ref, qd_ref, kd_ref, vd_ref,
         ckv_st_ref, kpe_st_ref, kd_st_ref, vd_st_ref) = refs
    mod = mod_ref[...]
    u = _rms(h_ref[...], nw_ref[...]) * (1.0 + mod[1:2]) + mod[0:1]
    z = _dot(u.astype(BF16), win_ref[...])
    cq = _rms(z[:, 0:256], qnw_ref[...]).astype(BF16)
    qa = _dot(cq, wuq_ref[...])
    ckv = _rms(z[:, 256:384], kvnw_ref[...])
    ckv_b = ckv.astype(BF16)
    kn = _dot(ckv_b, wuk_ref[...])
    va_ref[...] = _dot(ckv_b, wuv_ref[...]).astype(BF16)
    kpe_slot = z[:, 384:512]
    qd = z[:, 512:1024]
    kd = z[:, 1024:1536]
    vd = z[:, 1536:2048]
    vd_ref[...] = vd.astype(BF16)
    if sample:
        ca, sa, ch, sh = ca_ref[...], sa_ref[...], ch_ref[...], sh_ref[...]
        qa = _rope(qa, ca, sa, QK_ROPE // 2)
        kpe_rot = _rope(kpe_slot, ca, sa, QK_ROPE // 2)
        qd_ref[...] = _rope(qd, ch, sh, DH_B // 2).astype(BF16)
        kd_ref[...] = _rope(kd, ch, sh, DH_B // 2).astype(BF16)
    else:
        kpe_rot = kpe_slot
        qd_ref[...] = qd.astype(BF16)
        kd_ref[...] = kd.astype(BF16)
        ckv_st_ref[...] = ckv
        kpe_st_ref[...] = kpe_slot[:, QK_NOPE:QK_NOPE + QK_ROPE]
        kd_st_ref[...] = kd.reshape(kd.shape[0], H_B, 2 * DH_B)
        vd_st_ref[...] = vd.reshape(vd.shape[0], H_B, 2 * DH_B)
    qa_ref[...] = qa.astype(BF16)
    for hd in range(H_A):
        sl = slice(hd * MLA_SLOT, (hd + 1) * MLA_SLOT)
        ka_ref[:, sl] = (kn[:, sl] + kpe_rot).astype(BF16)


def _proj_even(h, mod, layer, sample, nw, w, rope):
    n = h.shape[0]
    tm = ROW_TILE
    wide = H_A * MLA_SLOT
    ins = [h, mod, nw, w["w_in"], w["q_norm_w"], w["kv_norm_w"], w["w_uq"], w["w_uk"], w["w_uv"]]
    in_specs = [_rows(tm, D_MODEL), _mod_spec(layer, sample, tm), _full((1, D_MODEL)), _full((D_MODEL, EVEN_IN)),
                _full((1, Q_LORA)), _full((1, KV_LORA)), _full((Q_LORA, wide)), _full((KV_LORA, wide)),
                _full((KV_LORA, 512))]
    widths = [wide, wide, 512, 512, 512, 512]
    out_shape = [jax.ShapeDtypeStruct((n, wd), BF16) for wd in widths]
    out_specs = [_rows(tm, wd) for wd in widths]
    if sample:
        ins += [rope["ca"], rope["sa"], rope["ch"], rope["sh"]]
        in_specs += [_rope_spec(tm)] * 4
    else:
        for wd in (KV_LORA, QK_ROPE):
            out_shape.append(jax.ShapeDtypeStruct((n, wd), F32))
            out_specs.append(_rows(tm, wd))
        for _ in range(2):
            out_shape.append(jax.ShapeDtypeStruct((n, H_B, 2 * DH_B), F32))
            out_specs.append(pl.BlockSpec((tm, H_B, 2 * DH_B), lambda i: (i, 0, 0)))
    return pl.pallas_call(
        functools.partial(_proj_even_kernel, sample),
        grid=(n // tm,),
        in_specs=in_specs,
        out_specs=out_specs,
        out_shape=out_shape,
        compiler_params=_cparams("parallel"),
        name="proj_even_s" if sample else "proj_even_p",
    )(*ins)


def _attn_even_kernel(sample, lam_init, *refs):
    if sample:
        (qa_ref, ka_ref, va_ref, qd_ref, kd_ref, vd_ref,
         cckv_ref, ckpe_ref, cdk_ref, cdv_ref, wuk_ref, wuv_ref, lam_ref, sub_ref,
         h_ref, wout_ref, mod_ref, nw1_ref, h1_ref, s_scr, p_scr, o_ref) = refs
    else:
        (qa_ref, ka_ref, va_ref, qd_ref, kd_ref, vd_ref, lam_ref, sub_ref,
         h_ref, wout_ref, mod_ref, nw1_ref, h1_ref, s_scr, p_scr, o_ref) = refs
    lf = lam_ref[...]
    lam = (jnp.exp(jnp.sum(lf[0:1] * lf[1:2], axis=-1, keepdims=True))
           - jnp.exp(jnp.sum(lf[2:3] * lf[3:4], axis=-1, keepdims=True)) + lam_init)
    if sample:
        cckv = cckv_ref[...].astype(BF16)
        kn_ctx = _dot(cckv, wuk_ref[...])
        va_ctx = _dot(cckv, wuv_ref[...]).astype(BF16)
        n_ctx = cckv.shape[0]
        kpe_ctx = jnp.concatenate([jnp.zeros((n_ctx, QK_NOPE), F32), ckpe_ref[...],
                                   jnp.zeros((n_ctx, MLA_SLOT - QK_NOPE - QK_ROPE), F32)], axis=1)
        kd_ctx = cdk_ref[...].astype(BF16)
        vd_ctx = cdv_ref[...].astype(BF16)
    maps = []
    c_a = (QK_NOPE + QK_ROPE) ** -0.5 * LOG2E
    for hd in range(H_A):
        sl = slice(hd * MLA_SLOT, (hd + 1) * MLA_SLOT)
        if hd % 2 == 0:
            sinks = _pair_store(o_ref, hd * V_A)

        def scores(sl=sl):
            q = qa_ref[:, sl]
            out = [_dot_nt(q, ka_ref[:, sl])]
            if sample:
                out.append(_dot_nt(q, (kn_ctx[:, sl] + kpe_ctx).astype(BF16)))
            return out

        def values(vs=slice(hd // 2 * LANES, (hd // 2 + 1) * LANES)):
            return [va_ref[:, vs]] + ([va_ctx[:, vs]] if sample else [])

        maps.append((scores, values, c_a, None, sinks[hd % 2]))
    c_b = DH_B ** -0.5 * LOG2E
    base = H_A * V_A
    sub_w = sub_ref[...]
    for hd in range(H_B):
        hs = slice(hd * 2 * DH_B, (hd + 1) * 2 * DH_B)
        outs = []

        def sink(o, outs=outs, hs=hs):
            outs.append(o)
            if len(outs) == 2:
                ob = _rms(outs[0] - lam * outs[1], sub_w) * (1.0 - lam_init)
                o_ref[:, base + hs.start:base + hs.stop] = ob.astype(BF16)

        def values(hs=hs):
            return [vd_ref[:, hs]] + ([vd_ctx[:, hs]] if sample else [])

        for comp in range(2):
            def scores(hs=hs, comp=comp):
                q = _keep_half(qd_ref[:, hs], comp)
                out = [_dot_nt(q, kd_ref[:, hs])]
                if sample:
                    out.append(_dot_nt(q, kd_ctx[:, hs]))
                return out

            maps.append((scores, values, c_b, None, sink))
    _attention(maps, s_scr, p_scr, s_scr.shape[0] // 2, mxu_sum=not sample)
    _out_proj_residual(o_ref, h_ref, wout_ref, mod_ref, nw1_ref, h1_ref)


def _attn_even(p, sample, lam_init, caches, w, h, w_out, mod, layer, nw1):
    qa, ka, va, qd, kd, vd = p
    n = qa.shape[0]
    wide = H_A * MLA_SLOT
    if sample:
        tq, per = Q_TILE, DEC_SEQ // Q_TILE
        grid = (DEC_BATCH, per)
        qspec = lambda wd: pl.BlockSpec((tq, wd), lambda b, i: (b * per + i, 0))
        kspec = lambda wd: pl.BlockSpec((DEC_SEQ, wd), lambda b, i: (b, 0))
        cspec = lambda wd: pl.BlockSpec((None, PAST_LEN, wd), lambda b, i: (b, 0, 0))
        ins = [qa, ka, va, qd, kd, vd, *caches, w["w_uk"], w["w_uv"], w["diff_lam"], w["diff_subln_w"]]
        in_specs = [qspec(wide), kspec(wide), kspec(512), qspec(512), kspec(512), kspec(512),
                    cspec(KV_LORA), cspec(QK_ROPE), cspec(512), cspec(512),
                    _full((KV_LORA, wide)), _full((KV_LORA, 512)), _full((4, DH_B)), _full((1, 2 * DH_B))]
        out_spec = qspec(D_MODEL)
        mod_spec = pl.BlockSpec((None, None, 6, D_MODEL), lambda b, i: (layer, 1 + b, 0, 0))
        sem = ("parallel", "parallel")
        n_keys, group = DEC_SEQ + PAST_LEN, LATENT_GROUP
    else:
        tq = SEQ
        grid = (BATCH,)
        spec = lambda wd: pl.BlockSpec((SEQ, wd), lambda b: (b, 0))
        ins = [qa, ka, va, qd, kd, vd, w["diff_lam"], w["diff_subln_w"]]
        in_specs = [spec(wide), spec(wide), spec(512), spec(512), spec(512), spec(512),
                    _full((4, DH_B)), _full((1, 2 * DH_B))]
        out_spec = spec(D_MODEL)
        mod_spec = pl.BlockSpec((None, None, 6, D_MODEL), lambda b: (layer, 0, 0, 0))
        sem = ("parallel",)
        n_keys, group = SEQ, CONTEXT_GROUP
    ins += [h, w_out, mod, nw1]
    in_specs += [out_spec, _full((D_MODEL, D_MODEL)), mod_spec, _full((1, D_MODEL))]
    return pl.pallas_call(
        functools.partial(_attn_even_kernel, sample, lam_init),
        grid=grid,
        in_specs=in_specs,
        out_specs=out_spec,
        out_shape=jax.ShapeDtypeStruct((n, D_MODEL), F32),
        scratch_shapes=_attn_scratch(tq, n_keys, group) + [pltpu.VMEM((tq, D_MODEL), BF16)],
        compiler_params=_cparams(*sem),
        name="attn_even_s" if sample else "attn_even_p",
    )(*ins)


ODD_IN = 2304


def _proj_odd_kernel(sample, *refs):
    if sample:
        (h_ref, mod_ref, nw_ref, win_ref, qw_ref, kw_ref, gq_ref, gk_ref, ch_ref, sh_ref,
         qc_ref, kc_ref, vc_ref, qn_ref, kn_ref, vn_ref) = refs
    else:
        (h_ref, mod_ref, nw_ref, win_ref, qw_ref, kw_ref, gq_ref, gk_ref,
         qc_ref, kc_ref, vc_ref, qn_ref, kn_ref, vn_ref,
         kc_st_ref, vc_st_ref, kn_st_ref, vn_st_ref) = refs
    mod = mod_ref[...]
    u = _rms(h_ref[...], nw_ref[...]) * (1.0 + mod[1:2]) + mod[0:1]
    z = _dot(u.astype(BF16), win_ref[...])
    qc = _group_rms(z[:, 0:512], qw_ref[...], gq_ref[...])
    kc = _group_rms(z[:, 512:640], kw_ref[...], gk_ref[...])
    vc = z[:, 640:768]
    kn = z[:, 1280:1792]
    vn = z[:, 1792:2304]
    vc_ref[...] = vc.astype(BF16)
    qn_ref[...] = z[:, 768:1280].astype(BF16)
    kn_ref[...] = kn.astype(BF16)
    vn_ref[...] = vn.astype(BF16)
    if sample:
        ch, sh = ch_ref[...], sh_ref[...]
        qc_ref[...] = _rope(qc, ch, sh, DH_C // 2).astype(BF16)
        kc_ref[...] = _rope(kc, ch, sh, DH_C // 2).astype(BF16)
    else:
        qc_ref[...] = qc.astype(BF16)
        kc_ref[...] = kc.astype(BF16)
        kc_st_ref[...] = kc.reshape(kc.shape[0], KV_C, DH_C)
        vc_st_ref[...] = vc.reshape(vc.shape[0], KV_C, DH_C)
        kn_st_ref[...] = kn.reshape(kn.shape[0], H_D, DH_D)
        vn_st_ref[...] = vn.reshape(vn.shape[0], H_D, DH_D)


def _proj_odd(h, mod, layer, sample, nw, w, rope):
    n = h.shape[0]
    tm = ROW_TILE
    ins = [h, mod, nw, w["w_in"], w["q_w"], w["k_w"], w["gq"], w["gk"]]
    in_specs = [_rows(tm, D_MODEL), _mod_spec(layer, sample, tm), _full((1, D_MODEL)), _full((D_MODEL, ODD_IN)),
                _full((1, 512)), _full((1, 128)), _full((512, 512)), _full((128, 128))]
    widths = [512, 128, 128, 512, 512, 512]
    out_shape = [jax.ShapeDtypeStruct((n, wd), BF16) for wd in widths]
    out_specs = [_rows(tm, wd) for wd in widths]
    if sample:
        ins += [rope["ch"], rope["sh"]]
        in_specs += [_rope_spec(tm)] * 2
    else:
        for heads, dh in ((KV_C, DH_C), (KV_C, DH_C), (H_D, DH_D), (H_D, DH_D)):
            out_shape.append(jax.ShapeDtypeStruct((n, heads, dh), F32))
            out_specs.append(pl.BlockSpec((tm, heads, dh), lambda i: (i, 0, 0)))
    return pl.pallas_call(
        functools.partial(_proj_odd_kernel, sample),
        grid=(n // tm,),
        in_specs=in_specs,
        out_specs=out_specs,
        out_shape=out_shape,
        compiler_params=_cparams("parallel"),
        name="proj_odd_s" if sample else "proj_odd_p",
    )(*ins)


def _attn_odd_prompt_kernel(qc_ref, kc_ref, vc_ref, qn_ref, kn_ref, vn_ref, h_ref, wout_ref, mod_ref, nw1_ref,
                            h1_ref, s_scr, p_scr, o_ref):
    c = DH_C ** -0.5 * LOG2E
    group = H_C // KV_C
    kc = (kc_ref[...], _swap_halves(kc_ref[...]))
    vc = (vc_ref[...], _swap_halves(vc_ref[...]))
    maps = []
    for hd in range(H_C):
        half, swap = hd % 2, (hd // group) != (hd % 2)
        ps = slice(hd // 2 * LANES, (hd // 2 + 1) * LANES)
        if half == 0:
            sinks = _pair_store(o_ref, ps.start)
        maps.append((lambda ps=ps, half=half, swap=swap: [_dot_nt(_keep_half(qc_ref[:, ps], half), kc[swap])],
                     lambda swap=swap: [vc[swap]], c, None, sinks[half]))
    base = H_C * DH_C
    c = DH_D ** -0.5 * LOG2E
    for hd in range(H_D):
        half = hd % 2
        ps = slice(hd // 2 * LANES, (hd // 2 + 1) * LANES)
        if half == 0:
            sinks = _pair_store(o_ref, base + ps.start)
        maps.append((lambda ps=ps, half=half: [_dot_nt(_keep_half(qn_ref[:, ps], half), kn_ref[:, ps])],
                     lambda ps=ps: [vn_ref[:, ps]], c, None, sinks[half]))
    _attention(maps, s_scr, p_scr, s_scr.shape[0] // 2, mxu_sum="alternate")
    _out_proj_residual(o_ref, h_ref, wout_ref, mod_ref, nw1_ref, h1_ref)


def _attn_odd_prompt(p, h, w_out, mod, layer, nw1):
    qc, kc, vc, qn, kn, vn = p
    spec = lambda wd: pl.BlockSpec((SEQ, wd), lambda b: (b, 0))
    return pl.pallas_call(
        _attn_odd_prompt_kernel,
        grid=(BATCH,),
        in_specs=[spec(512), spec(128), spec(128), spec(512), spec(512), spec(512),
                  spec(D_MODEL), _full((D_MODEL, D_MODEL)),
                  pl.BlockSpec((None, None, 6, D_MODEL), lambda b: (layer, 0, 0, 0)), _full((1, D_MODEL))],
        out_specs=spec(D_MODEL),
        out_shape=jax.ShapeDtypeStruct((qc.shape[0], D_MODEL), F32),
        scratch_shapes=_attn_scratch(SEQ, SEQ, CONTEXT_GROUP) + [pltpu.VMEM((SEQ, D_MODEL), BF16)],
        compiler_params=_cparams("parallel"),
        name="attn_odd_p",
    )(qc, kc, vc, qn, kn, vn, h, w_out, mod, nw1)


def _gqa_sample_kernel(q_ref, k_ref, v_ref, ck_ref, cv_ref, o_ref, s_scr, p_scr):
    c = DH_C ** -0.5 * LOG2E
    group = H_C // KV_C
    k_loc = (k_ref[...], _swap_halves(k_ref[...]))
    v_loc = (v_ref[...], _swap_halves(v_ref[...]))
    k_ctx = ck_ref[...].astype(BF16)
    v_ctx = cv_ref[...].astype(BF16)
    k_ctx = (k_ctx, _swap_halves(k_ctx))
    v_ctx = (v_ctx, _swap_halves(v_ctx))
    maps = []
    for hd in range(H_C):
        half, swap = hd % 2, (hd // group) != (hd % 2)
        ps = slice(hd // 2 * LANES, (hd // 2 + 1) * LANES)
        if half == 0:
            sinks = _pair_store(o_ref, ps.start)

        def scores(ps=ps, half=half, swap=swap):
            q = _keep_half(q_ref[:, ps], half)
            return [_dot_nt(q, k_loc[swap]), _dot_nt(q, k_ctx[swap])]

        maps.append((scores, lambda swap=swap: [v_loc[swap], v_ctx[swap]], c, None, sinks[half]))
    _attention(maps, s_scr, p_scr, s_scr.shape[0] // 2)


def _gqa_sample(qc, kc, vc, cache_k, cache_v):
    tq, per = Q_TILE, DEC_SEQ // Q_TILE
    return pl.pallas_call(
        _gqa_sample_kernel,
        grid=(DEC_BATCH, per),
        in_specs=[pl.BlockSpec((tq, 512), lambda b, i: (b * per + i, 0)),
                  pl.BlockSpec((DEC_SEQ, 128), lambda b, i: (b, 0)),
                  pl.BlockSpec((DEC_SEQ, 128), lambda b, i: (b, 0)),
                  pl.BlockSpec((None, PAST_LEN, 128), lambda b, i: (b, 0, 0)),
                  pl.BlockSpec((None, PAST_LEN, 128), lambda b, i: (b, 0, 0))],
        out_specs=pl.BlockSpec((tq, 512), lambda b, i: (b * per + i, 0)),
        out_shape=jax.ShapeDtypeStruct((qc.shape[0], 512), BF16),
        scratch_shapes=_attn_scratch(tq, DEC_SEQ + PAST_LEN, LATENT_GROUP),
        compiler_params=_cparams("parallel", "parallel"),
        name="gqa_s",
    )(qc, kc, vc, cache_k, cache_v)


NA_ROWS = DEC_SEQ // GRID_W
NA_KR = min(NA_WIN_ROWS, NA_ROWS)
NA_LOC = NA_KR * GRID_W


NA_STEP_ROWS = 4


def _na_first_key_row(r):
    return jnp.clip(r - NA_KR // 2, 0, NA_ROWS - NA_KR)


def _na_sample_kernel(q_ref, k_ref, v_ref, ck_ref, cv_ref, *rest):
    bias_refs = rest[:NA_STEP_ROWS]
    o_ref, s_scr, p_scr = rest[NA_STEP_ROWS:]
    scale = DH_D ** -0.5
    k_ctx = ck_ref[...].astype(BF16)
    v_ctx = cv_ref[...].astype(BF16)
    rb = _softmax_block_rows(NA_LOC + PAST_LEN)
    col_ok = {}
    for r0 in range(0, GRID_W, rb):
        wq = lax.broadcasted_iota(jnp.int32, (rb, NA_LOC), 0) + r0
        wk = lax.broadcasted_iota(jnp.int32, (rb, NA_LOC), 1) % GRID_W
        cs = jnp.clip(wq - NA_WIN_COLS // 2, 0, GRID_W - NA_WIN_COLS)
        col_ok[r0] = (wk >= cs) & (wk < cs + NA_WIN_COLS)
    maps = []
    for t in range(NA_STEP_ROWS):
        r = pl.program_id(1) * NA_STEP_ROWS + t
        keys = pl.ds(pl.multiple_of(_na_first_key_row(r) * GRID_W, GRID_W), NA_LOC)
        rows = slice(t * GRID_W, (t + 1) * GRID_W)
        for hd in range(H_D):
            half = hd % 2
            ps = slice(hd // 2 * LANES, (hd // 2 + 1) * LANES)
            if half == 0:
                sinks = _pair_store(o_ref, ps.start, rows)

            def fix(s, r0, hd=hd, t=t):
                loc = s[:, :NA_LOC] * scale + bias_refs[t][hd, r0:r0 + rb, :]
                return jnp.concatenate([jnp.where(col_ok[r0], loc, NEG_INF), s[:, NA_LOC:] * scale], axis=1)

            def scores(ps=ps, half=half, rows=rows, keys=keys):
                q = _keep_half(q_ref[rows, ps], half)
                return [_dot_nt(q, k_ref[keys, ps]), _dot_nt(q, k_ctx[:, ps])]

            maps.append((scores, lambda ps=ps, keys=keys: [v_ref[keys, ps], v_ctx[:, ps]], LOG2E, fix, sinks[half]))
    _attention(maps, s_scr, p_scr, s_scr.shape[0] // 2)


def _na_sample(qn, kn, vn, cache_k, cache_v, bias):
    steps = NA_ROWS // NA_STEP_ROWS
    tq = NA_STEP_ROWS * GRID_W

    def bias_spec(t):
        def index_map(b, i):
            r = i * NA_STEP_ROWS + t
            return (0, _na_first_key_row(r) - r + NA_WIN_ROWS - 1, 0, 0)
        return pl.BlockSpec((H_D, None, GRID_W, NA_LOC), index_map)

    return pl.pallas_call(
        _na_sample_kernel,
        grid=(DEC_BATCH, steps),
        in_specs=[pl.BlockSpec((tq, 512), lambda b, i: (b * steps + i, 0)),
                  pl.BlockSpec((DEC_SEQ, 512), lambda b, i: (b, 0)),
                  pl.BlockSpec((DEC_SEQ, 512), lambda b, i: (b, 0)),
                  pl.BlockSpec((None, PAST_LEN, 512), lambda b, i: (b, 0, 0)),
                  pl.BlockSpec((None, PAST_LEN, 512), lambda b, i: (b, 0, 0))]
                 + [bias_spec(t) for t in range(NA_STEP_ROWS)],
        out_specs=pl.BlockSpec((tq, 512), lambda b, i: (b * steps + i, 0)),
        out_shape=jax.ShapeDtypeStruct((qn.shape[0], 512), BF16),
        scratch_shapes=_attn_scratch(GRID_W, NA_LOC + PAST_LEN, CONTEXT_GROUP),
        compiler_params=_cparams("parallel", "parallel"),
        name="na_s",
    )(qn, kn, vn, cache_k, cache_v, *([bias] * NA_STEP_ROWS))


def _na_bias_table(rpb):
    edge = GRID_W - NA_WIN_COLS
    n_dr = 2 * NA_WIN_ROWS - 1
    v = jnp.pad(rpb.astype(F32), ((0, 0), (0, 0), (edge, edge + 1)), mode="edge")
    skew = jnp.tile(v, (1, 1, GRID_W))[:, :, :GRID_W * (2 * GRID_W - 1)]
    toep = skew.reshape(H_D, n_dr, GRID_W, 2 * GRID_W - 1)[:, :, :, GRID_W - 1:]
    flat = toep.transpose(0, 2, 1, 3).reshape(H_D, GRID_W, n_dr * GRID_W)
    return jnp.stack([flat[:, :, d0 * GRID_W:d0 * GRID_W + NA_LOC] for d0 in range(NA_WIN_ROWS)], axis=1)


def _post_attn_kernel(n_parts, *refs):
    o_refs = refs[:n_parts]
    h_ref, wout_ref, mod_ref, nw1_ref, h1_ref = refs[n_parts:]
    y = None
    off = 0
    for o_ref in o_refs:
        wd = o_ref.shape[1]
        part = _dot(o_ref[...], wout_ref[off:off + wd, :])
        y = part if y is None else y + part
        off += wd
    mod = mod_ref[...]
    h1_ref[...] = h_ref[...] + mod[2:3] * _rms(y, nw1_ref[...])


def _post_attn(o_parts, h, w_out, mod, layer, sample, nw1):
    n = h.shape[0]
    tm = ROW_TILE
    in_specs = [_rows(tm, o.shape[1]) for o in o_parts]
    in_specs += [_rows(tm, D_MODEL), _full((D_MODEL, D_MODEL)), _mod_spec(layer, sample, tm), _full((1, D_MODEL))]
    return pl.pallas_call(
        functools.partial(_post_attn_kernel, len(o_parts)),
        grid=(n // tm,),
        in_specs=in_specs,
        out_specs=_rows(tm, D_MODEL),
        out_shape=jax.ShapeDtypeStruct((n, D_MODEL), F32),
        compiler_params=_cparams("parallel"),
        name="post_attn_s" if sample else "post_attn_p",
    )(*o_parts, h, w_out, mod, nw1)


FF_PAIR = 2 * FF_CHUNK
N_FF = D_FF // FF_CHUNK
SUBLANES = 8
ACT_TILES = 4
ROW_BLOCK = 1024
NORM_ROWS = 32


def _ffn_kernel(seq_len, h_ref, wup_ref, cw_ref, cb_ref, wd_ref, mod_ref, nw2_ref, nw3_ref, o_ref,
                u_ref, z0_ref, z1_ref, a_ref, a_last_ref):
    tm = h_ref.shape[0]
    n_blocks = tm // ROW_BLOCK
    rows = ACT_TILES * SUBLANES
    mod = mod_ref[...]
    sub = lax.broadcasted_iota(jnp.int32, (SUBLANES, LANES), 0)
    zero_rows = jnp.zeros((SUBLANES, FF_PAIR), F32)
    for z_ref in (z0_ref, z1_ref):
        z_ref[0:SUBLANES, :] = zero_rows
        z_ref[SUBLANES + tm:2 * SUBLANES + tm, :] = zero_rows

    nw2 = nw2_ref[...]

    def pre_norm(blk):
        for c in range(ROW_BLOCK // NORM_ROWS):
            rs = slice(blk * ROW_BLOCK + c * NORM_ROWS, blk * ROW_BLOCK + (c + 1) * NORM_ROWS)
            u = _rms(h_ref[rs, :], nw2) * (1.0 + mod[4:5]) + mod[3:4]
            u_ref[rs, :] = u.astype(BF16)

    def up(j, z_ref, blk):
        r0 = blk * ROW_BLOCK
        u = u_ref[r0:r0 + ROW_BLOCK, :]
        rows_ = slice(SUBLANES + r0, SUBLANES + r0 + ROW_BLOCK)
        for half in range(2):
            c0 = half * D_FF + j * FF_CHUNK
            if not isinstance(c0, int):
                c0 = pl.multiple_of(c0, FF_CHUNK)
            z_ref[rows_, half * FF_CHUNK:(half + 1) * FF_CHUNK] = _dot(u, wup_ref[:, pl.ds(c0, FF_CHUNK)])

    def act(j, z_ref, col, blk, dst_ref=a_ref):
        cw = cw_ref[j]
        cb = cb_ref[j]
        for lc in range(FF_CHUNK // LANES):
            taps = []
            for lane0 in (lc * LANES, FF_CHUNK + lc * LANES):
                lanes = slice(lane0, lane0 + LANES)
                taps.append([jnp.broadcast_to(cw[k:k + 1, lanes], (SUBLANES, LANES)) for k in range(3)]
                            + [jnp.broadcast_to(cb[:, lanes], (SUBLANES, LANES))])
            for c in range(ROW_BLOCK // rows):
                r = blk * ROW_BLOCK + c * rows
                first = r % seq_len == 0
                last = (r + rows) % seq_len == 0

                def conv(lane0, tap):
                    ext = z_ref[r:r + rows + 2 * SUBLANES, lane0:lane0 + LANES]
                    tiles = [ext[t * SUBLANES:(t + 1) * SUBLANES] for t in range(ACT_TILES + 2)]
                    down = [pltpu.roll(t, 1, 0) for t in tiles[:-1]]
                    up_ = [pltpu.roll(t, SUBLANES - 1, 0) for t in tiles[1:]]
                    out = []
                    for t in range(ACT_TILES):
                        above = 0.0 if (first and t == 0) else down[t]
                        below = 0.0 if (last and t == ACT_TILES - 1) else up_[t + 1]
                        prev = jnp.where(sub == 0, above, down[t + 1])
                        nxt = jnp.where(sub == SUBLANES - 1, below, up_[t])
                        out.append(prev * tap[0] + tiles[t + 1] * tap[1] + nxt * tap[2] + tap[3])
                    return jnp.concatenate(out, axis=0)

                g = conv(lc * LANES, taps[0])
                v = conv(FF_CHUNK + lc * LANES, taps[1])
                a = (g / (1.0 + jnp.exp2(g * -LOG2E))) * v
                lane = col + lc * LANES
                if not isinstance(lane, int):
                    lane = pl.multiple_of(lane, LANES)
                dst_ref[r:r + rows, pl.ds(lane, LANES)] = a.astype(BF16)

    for blk in range(n_blocks):
        pre_norm(blk)
        up(0, z0_ref, blk)

    def pair(i, carry):
        j = 2 * i
        col = pl.multiple_of(j * FF_CHUNK, FF_CHUNK)
        for blk in range(n_blocks):
            up(j + 1, z1_ref, blk)
            act(j, z0_ref, col, blk)
        for blk in range(n_blocks):
            up(j + 2, z0_ref, blk)
            act(j + 1, z1_ref, col + FF_CHUNK, blk)
        return carry

    lax.fori_loop(0, (N_FF - 1) // 2, pair, 0)
    nw3 = nw3_ref[...]
    k_main = (N_FF - 1) * FF_CHUNK
    for blk in range(n_blocks):
        r0 = blk * ROW_BLOCK
        y = _dot(a_ref[r0:r0 + ROW_BLOCK, 0:k_main], wd_ref[0:k_main, :])
        act(N_FF - 1, z0_ref, 0, blk, a_last_ref)
        y = y + _dot(a_last_ref[r0:r0 + ROW_BLOCK, :], wd_ref[k_main:D_FF, :])
        for c in range(ROW_BLOCK // NORM_ROWS):
            rs = slice(r0 + c * NORM_ROWS, r0 + (c + 1) * NORM_ROWS)
            o_ref[rs, :] = h_ref[rs, :] + mod[5:6] * _rms(y[c * NORM_ROWS:(c + 1) * NORM_ROWS], nw3)


def _ffn(h1, w, mod, layer, sample, nw2, nw3):
    n = h1.shape[0]
    tm = FFN_ROW_TILE
    seq_len = DEC_SEQ if sample else SEQ
    once = pl.Buffered(1)
    in_specs = [
        _rows(tm, D_MODEL),
        pl.BlockSpec((None, D_MODEL, 2 * D_FF), lambda i: (layer, 0, 0), pipeline_mode=once),
        pl.BlockSpec((N_FF, 3, FF_PAIR), lambda i: (0, 0, 0), pipeline_mode=once),
        pl.BlockSpec((N_FF, 1, FF_PAIR), lambda i: (0, 0, 0), pipeline_mode=once),
        pl.BlockSpec((None, D_FF, D_MODEL), lambda i: (layer, 0, 0), pipeline_mode=once),
        _mod_spec(layer, sample, tm),
        _full((1, D_MODEL)),
        _full((1, D_MODEL)),
    ]
    return pl.pallas_call(
        functools.partial(_ffn_kernel, seq_len),
        grid=(n // tm,),
        in_specs=in_specs,
        out_specs=_rows(tm, D_MODEL),
        out_shape=jax.ShapeDtypeStruct((n, D_MODEL), F32),
        scratch_shapes=[pltpu.VMEM((tm, D_MODEL), BF16), pltpu.VMEM((tm + 2 * SUBLANES, FF_PAIR), F32),
                        pltpu.VMEM((tm + 2 * SUBLANES, FF_PAIR), F32), pltpu.VMEM((tm, D_FF - FF_CHUNK), BF16),
                        pltpu.VMEM((tm, FF_CHUNK), BF16)],
        compiler_params=pltpu.CompilerParams(dimension_semantics=("parallel",), vmem_limit_bytes=FFN_VMEM_LIMIT),
        name="ffn_s" if sample else "ffn_p",
    )(h1, w["w_up"], w["conv_w"], w["conv_b"], w["w_down"], mod, nw2, nw3)


def _pair_chunks(x):
    lead = x.shape[:-1]
    x = x.reshape(lead + (2, N_FF, FF_CHUNK))
    x = jnp.moveaxis(x, -2, 0)
    return x.reshape((N_FF,) + lead + (FF_PAIR,))


def _rope_tables():
    def table(rot_dim):
        t = np.arange(DEC_SEQ)
        n_freq = rot_dim // 4
        inv = 1.0 / (ROPE_THETA ** (np.arange(n_freq) / n_freq))
        ang = np.concatenate([(t // GRID_W)[:, None] * inv[None, :], (t % GRID_W)[:, None] * inv[None, :]], axis=-1)
        cos = np.cos(ang).astype(np.float32)
        sin = np.sin(ang).astype(np.float32)
        reps = LANES // rot_dim
        return (np.tile(np.concatenate([cos, cos], axis=-1), (1, reps)),
                np.tile(np.concatenate([-sin, sin], axis=-1), (1, reps)))
    ca, sa = table(QK_ROPE)
    ch, sh = table(HEAD_DIM)
    rope_lanes = (np.arange(MLA_SLOT) >= QK_NOPE) & (np.arange(MLA_SLOT) < QK_NOPE + QK_ROPE)
    ca = np.where(rope_lanes[None, :], ca, 1.0).astype(np.float32)
    sa = np.where(rope_lanes[None, :], sa, 0.0).astype(np.float32)
    return {"ca": jnp.asarray(ca), "sa": jnp.asarray(sa), "ch": jnp.asarray(ch), "sh": jnp.asarray(sh)}


def _group_mean_matrix(width):
    idx = np.arange(width) // HEAD_DIM
    return jnp.asarray((idx[:, None] == idx[None, :]).astype(np.float32) / HEAD_DIM, BF16)


def kernel(x_prompt, x_sample, c, cache_mla_ckv, cache_mla_kpe, cache_diff_k, cache_diff_v, cache_gqa_k, cache_gqa_v, cache_na_k, cache_na_v, c_ctx, norm_w, w_mod, b_mod, w_in_even, w_out_even, w_uq, q_norm_w, kv_norm_w, w_uk, w_uv, diff_lam, diff_subln_w, w_in_odd, w_out_odd, qk_norm_w, na_rpb, w_up, conv_w, conv_b, w_down):
    rope = _rope_tables()
    n_p = BATCH * SEQ
    n_s = DEC_BATCH * DEC_SEQ
    cvecs = jnp.concatenate([c_ctx[None, :], c, jnp.zeros((MOD_ROWS - 1 - DEC_BATCH, D_MODEL), F32)], axis=0)
    mod = _modulation(cvecs, w_mod, b_mod).reshape(DEPTH, MOD_ROWS, 6, D_MODEL)
    hp = x_prompt.reshape(n_p, D_MODEL)
    hs = x_sample.reshape(n_s, D_MODEL)
    even_states, odd_states = [], []
    w_up_b = w_up.astype(BF16)
    w_down_b = w_down.astype(BF16)
    for l in range(DEPTH):
        i = l // 2
        nw = [norm_w[l, k][None, :] for k in range(4)]
        if l % 2 == 0:
            lam_init = 0.8 - 0.6 * math.exp(-0.3 * l)
            wi = w_in_even[i]
            w_uq3 = w_uq[i].reshape(Q_LORA, H_A, QK_NOPE + QK_ROPE)
            w = {
                "w_in": jnp.concatenate([wi[:, :384], jnp.zeros((D_MODEL, QK_NOPE), F32), wi[:, 384:416],
                                         jnp.zeros((D_MODEL, MLA_SLOT - QK_NOPE - QK_ROPE), F32), wi[:, 416:]],
                                        axis=1).astype(BF16),
                "q_norm_w": q_norm_w[i][None, :],
                "kv_norm_w": kv_norm_w[i][None, :],
                "w_uq": jnp.pad(w_uq3, ((0, 0), (0, 0), (0, MLA_SLOT - QK_NOPE - QK_ROPE))
                                ).reshape(Q_LORA, H_A * MLA_SLOT).astype(BF16),
                "w_uk": jnp.pad(w_uk[i].reshape(KV_LORA, H_A, QK_NOPE), ((0, 0), (0, 0), (0, MLA_SLOT - QK_NOPE))
                                ).reshape(KV_LORA, H_A * MLA_SLOT).astype(BF16),
                "w_uv": w_uv[i].astype(BF16),
                "diff_lam": diff_lam[i],
                "diff_subln_w": diff_subln_w[i][None, :],
            }
            outs_p = _proj_even(hp, mod, l, False, nw[0], w, rope)
            outs_s = _proj_even(hs, mod, l, True, nw[0], w, rope)
            even_states.append(outs_p[6:])
            w_out = w_out_even[i].astype(BF16)
            h1p = _attn_even(outs_p[:6], False, lam_init, None, w, hp, w_out, mod, l, nw[1])
            caches = (cache_mla_ckv[:, i], cache_mla_kpe[:, i],
                      cache_diff_k[:, i].reshape(DEC_BATCH, PAST_LEN, 512),
                      cache_diff_v[:, i].reshape(DEC_BATCH, PAST_LEN, 512))
            h1s = _attn_even(outs_s, True, lam_init, caches, w, hs, w_out, mod, l, nw[1])
        else:
            q_w = jnp.tile(qk_norm_w[i, 0], H_C)[None, :]
            k_w = jnp.tile(qk_norm_w[i, 1], KV_C)[None, :]
            w = {"w_in": w_in_odd[i].astype(BF16), "q_w": q_w, "k_w": k_w,
                 "gq": _group_mean_matrix(512), "gk": _group_mean_matrix(128)}
            outs_p = _proj_odd(hp, mod, l, False, nw[0], w, rope)
            outs_s = _proj_odd(hs, mod, l, True, nw[0], w, rope)
            odd_states.append(outs_p[6:])
            w_out = w_out_odd[i].astype(BF16)
            h1p = _attn_odd_prompt(outs_p[:6], hp, w_out, mod, l, nw[1])
            qc, kc, vc, qn, kn, vn = outs_s
            o_c = _gqa_sample(qc, kc, vc, cache_gqa_k[:, i].reshape(DEC_BATCH, PAST_LEN, 128),
                              cache_gqa_v[:, i].reshape(DEC_BATCH, PAST_LEN, 128))
            o_d = _na_sample(qn, kn, vn, cache_na_k[:, i].reshape(DEC_BATCH, PAST_LEN, 512),
                             cache_na_v[:, i].reshape(DEC_BATCH, PAST_LEN, 512), _na_bias_table(na_rpb[i]))
            h1s = _post_attn([o_c, o_d], hs, w_out, mod, l, True, nw[1])
        wf = {"w_up": w_up_b, "conv_w": _pair_chunks(conv_w[l]),
              "conv_b": _pair_chunks(conv_b[l][None, :]), "w_down": w_down_b}
        hp = _ffn(h1p, wf, mod, l, False, nw[2], nw[3])
        hs = _ffn(h1s, wf, mod, l, True, nw[2], nw[3])

    def stack(states, k, shape):
        return jnp.stack([st[k].reshape((BATCH, SEQ) + shape) for st in states], axis=1)

    new_mla_ckv = stack(even_states, 0, (KV_LORA,))
    new_mla_kpe = stack(even_states, 1, (QK_ROPE,))
    new_diff_k = stack(even_states, 2, (H_B, 2 * DH_B))
    new_diff_v = stack(even_states, 3, (H_B, 2 * DH_B))
    new_gqa_k = stack(odd_states, 0, (KV_C, DH_C))
    new_gqa_v = stack(odd_states, 1, (KV_C, DH_C))
    new_na_k = stack(odd_states, 2, (H_D, DH_D))
    new_na_v = stack(odd_states, 3, (H_D, DH_D))
    return (hp.reshape(BATCH, SEQ, D_MODEL), hs.reshape(DEC_BATCH, DEC_SEQ, D_MODEL),
            new_mla_ckv, new_mla_kpe, new_diff_k, new_diff_v, new_gqa_k, new_gqa_v, new_na_k, new_na_v)
```

```python
import functools
import math

import numpy as np
import jax
import jax.numpy as jnp
from jax import lax
from jax.experimental import pallas as pl
from jax.experimental.pallas import tpu as pltpu

D_MODEL = 1024
BATCH = 32
SEQ = 256
DEPTH = 2
DEC_BATCH = 4
DEC_SEQ = 1024
PAST_LEN = 256
GRID_W = 64
HEAD_DIM = 64
H_A = 8
QK_NOPE = 64
QK_ROPE = 32
V_A = 64
Q_LORA = 256
KV_LORA = 128
H_B = 4
DH_B = HEAD_DIM
H_C = 8
KV_C = 2
DH_C = HEAD_DIM
H_D = 8
DH_D = HEAD_DIM
NA_WIN_ROWS = 8
NA_WIN_COLS = 16
D_FF = 2816
ROPE_THETA = 10000.0
EPS = 1e-6
NEG_INF = -1e30

LANES = 128
MOD_ROWS = 8
ROW_TILE = 512
FFN_ROW_TILE = 1024
FF_CHUNK = 256
Q_TILE = 256
LATENT_GROUP = 1
CONTEXT_GROUP = 4
VMEM_LIMIT = 48 * 1024 * 1024
FFN_VMEM_LIMIT = 56 * 1024 * 1024

F32 = jnp.float32
BF16 = jnp.bfloat16
LOG2E = math.log2(math.e)


def _cparams(*sem):
    return pltpu.CompilerParams(dimension_semantics=sem, vmem_limit_bytes=VMEM_LIMIT)


def _dot(a, b):
    return jnp.dot(a, b, preferred_element_type=F32)


def _dot_nt(a, b):
    return lax.dot_general(a, b, (((1,), (1,)), ((), ())), preferred_element_type=F32)


def _rms(x, w):
    return x * lax.rsqrt(jnp.mean(x * x, axis=-1, keepdims=True) + EPS) * w


def _group_rms(x, w, gmat):
    x2 = x * x
    hi = x2.astype(BF16)
    lo = (x2 - hi.astype(F32)).astype(BF16)
    ms = _dot(hi, gmat) + _dot(lo, gmat)
    return x * lax.rsqrt(ms + EPS) * w


def _rope(x, cos, sin_signed, half):
    outs = []
    for j in range(x.shape[1] // LANES):
        xc = x[:, j * LANES:(j + 1) * LANES]
        lane = lax.broadcasted_iota(jnp.int32, xc.shape, 1)
        first = (lane % (2 * half)) < half
        partner = jnp.where(first, pltpu.roll(xc, LANES - half, 1), pltpu.roll(xc, half, 1))
        outs.append(xc * cos + partner * sin_signed)
    return outs[0] if len(outs) == 1 else jnp.concatenate(outs, axis=1)


def _softmax_block_rows(n_keys):
    return max(16, min(64, (16 * 1280 // n_keys) // 16 * 16))


def _attention(maps, s_scr, p_scr, group, mxu_sum=False):
    slots = s_scr.shape[0]
    staged = {}

    def on_mxu(i):
        return bool(mxu_sum) and (mxu_sum != "alternate" or i % 2 == 1)

    def stage(i):
        s_ref = s_scr.at[i % slots]
        offs, off = [], 0
        for s in maps[i][0]():
            s_ref[:, off:off + s.shape[1]] = s
            offs.append(off)
            off += s.shape[1]
        staged[i] = (offs, off)

    def softmax(i):
        _, _, c, fix, _ = maps[i]
        n_keys = staged[i][1]
        s_ref, p_ref = s_scr.at[i % slots], p_scr.at[i % slots]
        rb = _softmax_block_rows(n_keys)
        sums = []
        for r0 in range(0, s_ref.shape[0], rb):
            s = s_ref[r0:r0 + rb, 0:n_keys]
            if fix is not None:
                s = fix(s, r0)
            m = jnp.max(s, axis=-1, keepdims=True)
            p = jnp.exp2((s - m) * c)
            if not on_mxu(i):
                sums.append(jnp.sum(p, axis=-1, keepdims=True))
            p_ref[r0:r0 + rb, 0:n_keys] = p.astype(BF16)
        return None if on_mxu(i) else jnp.concatenate(sums, axis=0)

    def weighted_values(i, den):
        _, values, _, _, sink = maps[i]
        p_ref = p_scr.at[i % slots]
        offs, n_keys = staged.pop(i)
        acc = None
        for o, v in zip(offs, values()):
            part = _dot(p_ref[:, o:o + v.shape[0]], v)
            acc = part if acc is None else acc + part
        if on_mxu(i):
            den = _dot(p_ref[:, 0:n_keys], jnp.ones((n_keys, LANES), BF16))
        sink(acc / den)

    groups = [range(g, min(g + group, len(maps))) for g in range(0, len(maps), group)]
    for i in groups[0]:
        stage(i)
    for gi, grp in enumerate(groups):
        if gi + 1 < len(groups):
            for i in groups[gi + 1]:
                stage(i)
        dens = [softmax(i) for i in grp]
        for i, den in zip(grp, dens):
            weighted_values(i, den)


def _attn_scratch(tq, n_keys, group):
    return [pltpu.VMEM((2 * group, tq, n_keys), F32), pltpu.VMEM((2 * group, tq, n_keys), BF16)]


def _upper_half(shape):
    return lax.broadcasted_iota(jnp.int32, shape, 1) >= HEAD_DIM


def _keep_half(x, half):
    upper = _upper_half(x.shape)
    return jnp.where(upper if half else ~upper, x, jnp.zeros_like(x))


def _swap_halves(x):
    return jnp.concatenate([x[:, HEAD_DIM:], x[:, :HEAD_DIM]], axis=1)


def _pair_store(o_ref, c0, rows=slice(None)):
    got = {}

    def make(half):
        def sink(o):
            got[half] = o
            if len(got) == 2:
                o_ref[rows, c0:c0 + LANES] = jnp.where(_upper_half(o.shape), got[1], got[0]).astype(BF16)
        return sink
    return make(0), make(1)


def _out_proj_residual(o_scr, h_ref, wout_ref, mod_ref, nw1_ref, h1_ref):
    y = _dot(o_scr[...], wout_ref[...])
    gate = mod_ref[2:3, :]
    nw1 = nw1_ref[...]
    for c in range(y.shape[0] // NORM_ROWS):
        rs = slice(c * NORM_ROWS, (c + 1) * NORM_ROWS)
        h1_ref[rs, :] = h_ref[rs, :] + gate * _rms(y[rs], nw1)


def _mod_kernel(c_ref, w_ref, b_ref, o_ref):
    cv = c_ref[...]
    act = cv / (1.0 + jnp.exp(-cv))
    o_ref[...] = _dot(act.astype(BF16), w_ref[...].astype(BF16)) + b_ref[...]


def _modulation(cvecs, w_mod, b_mod):
    tn = 1024
    n = 6 * D_MODEL
    return pl.pallas_call(
        _mod_kernel,
        grid=(DEPTH, n // tn),
        in_specs=[
            pl.BlockSpec((MOD_ROWS, D_MODEL), lambda l, j: (0, 0)),
            pl.BlockSpec((None, D_MODEL, tn), lambda l, j: (l, 0, j)),
            pl.BlockSpec((None, 1, tn), lambda l, j: (l, 0, j)),
        ],
        out_specs=pl.BlockSpec((None, MOD_ROWS, tn), lambda l, j: (l, 0, j)),
        out_shape=jax.ShapeDtypeStruct((DEPTH, MOD_ROWS, n), F32),
        compiler_params=_cparams("parallel", "parallel"),
        name="adaln_mod",
    )(cvecs, w_mod, b_mod.reshape(DEPTH, 1, n))


def _mod_spec(layer, sample, tm):
    if sample:
        per = DEC_SEQ // tm
        return pl.BlockSpec((None, None, 6, D_MODEL), lambda i, *_: (layer, 1 + i // per, 0, 0))
    return pl.BlockSpec((None, None, 6, D_MODEL), lambda i, *_: (layer, 0, 0, 0))


def _full(shape):
    nd = len(shape)
    return pl.BlockSpec(shape, lambda *_: (0,) * nd)


def _rows(tm, width):
    return pl.BlockSpec((tm, width), lambda i, *_: (i, 0))


def _rope_spec(tm):
    per = DEC_SEQ // tm
    return pl.BlockSpec((tm, LANES), lambda i, *_: (i % per, 0))


EVEN_IN = 2048
MLA_SLOT = 128


def _proj_even_kernel(sample, *refs):
    if sample:
        (h_ref, mod_ref, nw_ref, win_ref, qnw_ref, kvnw_ref, wuq_ref, wuk_ref, wuv_ref,
         ca_ref, sa_ref, ch_ref, sh_ref,
         qa_ref, ka_ref, va_ref, qd_ref, kd_ref, vd_ref) = refs
    else:
        (h_ref, mod_ref, nw_ref, win_ref, qnw_ref, kvnw_ref, wuq_ref, wuk_ref, wuv_ref,
         qa_ref, ka_ref, va_ref, qd_ref, kd_ref, vd_ref,
         ckv_st_ref, kpe_st_ref, kd_st_ref, vd_st_ref) = refs
    mod = mod_ref[...]
    u = _rms(h_ref[...], nw_ref[...]) * (1.0 + mod[1:2]) + mod[0:1]
    z = _dot(u.astype(BF16), win_ref[...])
    cq = _rms(z[:, 0:256], qnw_ref[...]).astype(BF16)
    qa = _dot(cq, wuq_ref[...])
    ckv = _rms(z[:, 256:384], kvnw_ref[...])
    ckv_b = ckv.astype(BF16)
    kn = _dot(ckv_b, wuk_ref[...])
    va_ref[...] = _dot(ckv_b, wuv_ref[...]).astype(BF16)
    kpe_slot = z[:, 384:512]
    qd = z[:, 512:1024]
    kd = z[:, 1024:1536]
    vd = z[:, 1536:2048]
    vd_ref[...] = vd.astype(BF16)
    if sample:
        ca, sa, ch, sh = ca_ref[...], sa_ref[...], ch_ref[...], sh_ref[...]
        qa = _rope(qa, ca, sa, QK_ROPE // 2)
        kpe_rot = _rope(kpe_slot, ca, sa, QK_ROPE // 2)
        qd_ref[...] = _rope(qd, ch, sh, DH_B // 2).astype(BF16)
        kd_ref[...] = _rope(kd, ch, sh, DH_B // 2).astype(BF16)
    else:
        kpe_rot = kpe_slot
        qd_ref[...] = qd.astype(BF16)
        kd_ref[...] = kd.astype(BF16)
        ckv_st_ref[...] = ckv
        kpe_st_ref[...] = kpe_slot[:, QK_NOPE:QK_NOPE + QK_ROPE]
        kd_st_ref[...] = kd.reshape(kd.shape[0], H_B, 2 * DH_B)
        vd_st_ref[...] = vd.reshape(vd.shape[0], H_B, 2 * DH_B)
    qa_ref[...] = qa.astype(BF16)
    for hd in range(H_A):
        sl = slice(hd * MLA_SLOT, (hd + 1) * MLA_SLOT)
        ka_ref[:, sl] = (kn[:, sl] + kpe_rot).astype(BF16)


def _proj_even(h, mod, layer, sample, nw, w, rope):
    n = h.shape[0]
    tm = ROW_TILE
    wide = H_A * MLA_SLOT
    ins = [h, mod, nw, w["w_in"], w["q_norm_w"], w["kv_norm_w"], w["w_uq"], w["w_uk"], w["w_uv"]]
    in_specs = [_rows(tm, D_MODEL), _mod_spec(layer, sample, tm), _full((1, D_MODEL)), _full((D_MODEL, EVEN_IN)),
                _full((1, Q_LORA)), _full((1, KV_LORA)), _full((Q_LORA, wide)), _full((KV_LORA, wide)),
                _full((KV_LORA, 512))]
    widths = [wide, wide, 512, 512, 512, 512]
    out_shape = [jax.ShapeDtypeStruct((n, wd), BF16) for wd in widths]
    out_specs = [_rows(tm, wd) for wd in widths]
    if sample:
        ins += [rope["ca"], rope["sa"], rope["ch"], rope["sh"]]
        in_specs += [_rope_spec(tm)] * 4
    else:
        for wd in (KV_LORA, QK_ROPE):
            out_shape.append(jax.ShapeDtypeStruct((n, wd), F32))
            out_specs.append(_rows(tm, wd))
        for _ in range(2):
            out_shape.append(jax.ShapeDtypeStruct((n, H_B, 2 * DH_B), F32))
            out_specs.append(pl.BlockSpec((tm, H_B, 2 * DH_B), lambda i: (i, 0, 0)))
    return pl.pallas_call(
        functools.partial(_proj_even_kernel, sample),
        grid=(n // tm,),
        in_specs=in_specs,
        out_specs=out_specs,
        out_shape=out_shape,
        compiler_params=_cparams("parallel"),
        name="proj_even_s" if sample else "proj_even_p",
    )(*ins)


def _attn_even_kernel(sample, lam_init, *refs):
    if sample:
        (qa_ref, ka_ref, va_ref, qd_ref, kd_ref, vd_ref,
         cckv_ref, ckpe_ref, cdk_ref, cdv_ref, wuk_ref, wuv_ref, lam_ref, sub_ref,
         h_ref, wout_ref, mod_ref, nw1_ref, h1_ref, s_scr, p_scr, o_ref) = refs
    else:
        (qa_ref, ka_ref, va_ref, qd_ref, kd_ref, vd_ref, lam_ref, sub_ref,
         h_ref, wout_ref, mod_ref, nw1_ref, h1_ref, s_scr, p_scr, o_ref) = refs
    lf = lam_ref[...]
    lam = (jnp.exp(jnp.sum(lf[0:1] * lf[1:2], axis=-1, keepdims=True))
           - jnp.exp(jnp.sum(lf[2:3] * lf[3:4], axis=-1, keepdims=True)) + lam_init)
    if sample:
        cckv = cckv_ref[...].astype(BF16)
        kn_ctx = _dot(cckv, wuk_ref[...])
        va_ctx = _dot(cckv, wuv_ref[...]).astype(BF16)
        n_ctx = cckv.shape[0]
        kpe_ctx = jnp.concatenate([jnp.zeros((n_ctx, QK_NOPE), F32), ckpe_ref[...],
                                   jnp.zeros((n_ctx, MLA_SLOT - QK_NOPE - QK_ROPE), F32)], axis=1)
        kd_ctx = cdk_ref[...].astype(BF16)
        vd_ctx = cdv_ref[...].astype(BF16)
    maps = []
    c_a = (QK_NOPE + QK_ROPE) ** -0.5 * LOG2E
    for hd in range(H_A):
        sl = slice(hd * MLA_SLOT, (hd + 1) * MLA_SLOT)
        if hd % 2 == 0:
            sinks = _pair_store(o_ref, hd * V_A)

        def scores(sl=sl):
            q = qa_ref[:, sl]
            out = [_dot_nt(q, ka_ref[:, sl])]
            if sample:
                out.append(_dot_nt(q, (kn_ctx[:, sl] + kpe_ctx).astype(BF16)))
            return out

        def values(vs=slice(hd // 2 * LANES, (hd // 2 + 1) * LANES)):
            return [va_ref[:, vs]] + ([va_ctx[:, vs]] if sample else [])

        maps.append((scores, values, c_a, None, sinks[hd % 2]))
    c_b = DH_B ** -0.5 * LOG2E
    base = H_A * V_A
    sub_w = sub_ref[...]
    for hd in range(H_B):
        hs = slice(hd * 2 * DH_B, (hd + 1) * 2 * DH_B)
        outs = []

        def sink(o, outs=outs, hs=hs):
            outs.append(o)
            if len(outs) == 2:
                ob = _rms(outs[0] - lam * outs[1], sub_w) * (1.0 - lam_init)
                o_ref[:, base + hs.start:base + hs.stop] = ob.astype(BF16)

        def values(hs=hs):
            return [vd_ref[:, hs]] + ([vd_ctx[:, hs]] if sample else [])

        for comp in range(2):
            def scores(hs=hs, comp=comp):
                q = _keep_half(qd_ref[:, hs], comp)
                out = [_dot_nt(q, kd_ref[:, hs])]
                if sample:
                    out.append(_dot_nt(q, kd_ctx[:, hs]))
                return out

            maps.append((scores, values, c_b, None, sink))
    _attention(maps, s_scr, p_scr, s_scr.shape[0] // 2, mxu_sum=not sample)
    _out_proj_residual(o_ref, h_ref, wout_ref, mod_ref, nw1_ref, h1_ref)


def _attn_even(p, sample, lam_init, caches, w, h, w_out, mod, layer, nw1):
    qa, ka, va, qd, kd, vd = p
    n = qa.shape[0]
    wide = H_A * MLA_SLOT
    if sample:
        tq, per = Q_TILE, DEC_SEQ // Q_TILE
        grid = (DEC_BATCH, per)
        qspec = lambda wd: pl.BlockSpec((tq, wd), lambda b, i: (b * per + i, 0))
        kspec = lambda wd: pl.BlockSpec((DEC_SEQ, wd), lambda b, i: (b, 0))
        cspec = lambda wd: pl.BlockSpec((None, PAST_LEN, wd), lambda b, i: (b, 0, 0))
        ins = [qa, ka, va, qd, kd, vd, *caches, w["w_uk"], w["w_uv"], w["diff_lam"], w["diff_subln_w"]]
        in_specs = [qspec(wide), kspec(wide), kspec(512), qspec(512), kspec(512), kspec(512),
                    cspec(KV_LORA), cspec(QK_ROPE), cspec(512), cspec(512),
                    _full((KV_LORA, wide)), _full((KV_LORA, 512)), _full((4, DH_B)), _full((1, 2 * DH_B))]
        out_spec = qspec(D_MODEL)
        mod_spec = pl.BlockSpec((None, None, 6, D_MODEL), lambda b, i: (layer, 1 + b, 0, 0))
        sem = ("parallel", "parallel")
        n_keys, group = DEC_SEQ + PAST_LEN, LATENT_GROUP
    else:
        tq = SEQ
        grid = (BATCH,)
        spec = lambda wd: pl.BlockSpec((SEQ, wd), lambda b: (b, 0))
        ins = [qa, ka, va, qd, kd, vd, w["diff_lam"], w["diff_subln_w"]]
        in_specs = [spec(wide), spec(wide), spec(512), spec(512), spec(512), spec(512),
                    _full((4, DH_B)), _full((1, 2 * DH_B))]
        out_spec = spec(D_MODEL)
        mod_spec = pl.BlockSpec((None, None, 6, D_MODEL), lambda b: (layer, 0, 0, 0))
        sem = ("parallel",)
        n_keys, group = SEQ, CONTEXT_GROUP
    ins += [h, w_out, mod, nw1]
    in_specs += [out_spec, _full((D_MODEL, D_MODEL)), mod_spec, _full((1, D_MODEL))]
    return pl.pallas_call(
        functools.partial(_attn_even_kernel, sample, lam_init),
        grid=grid,
        in_specs=in_specs,
        out_specs=out_spec,
        out_shape=jax.ShapeDtypeStruct((n, D_MODEL), F32),
        scratch_shapes=_attn_scratch(tq, n_keys, group) + [pltpu.VMEM((tq, D_MODEL), BF16)],
        compiler_params=_cparams(*sem),
        name="attn_even_s" if sample else "attn_even_p",
    )(*ins)


ODD_IN = 2304


def _proj_odd_kernel(sample, *refs):
    if sample:
        (h_ref, mod_ref, nw_ref, win_ref, qw_ref, kw_ref, gq_ref, gk_ref, ch_ref, sh_ref,
         qc_ref, kc_ref, vc_ref, qn_ref, kn_ref, vn_ref) = refs
    else:
        (h_ref, mod_ref, nw_ref, win_ref, qw_ref, kw_ref, gq_ref, gk_ref,
         qc_ref, kc_ref, vc_ref, qn_ref, kn_ref, vn_ref,
         kc_st_ref, vc_st_ref, kn_st_ref, vn_st_ref) = refs
    mod = mod_ref[...]
    u = _rms(h_ref[...], nw_ref[...]) * (1.0 + mod[1:2]) + mod[0:1]
    z = _dot(u.astype(BF16), win_ref[...])
    qc = _group_rms(z[:, 0:512], qw_ref[...], gq_ref[...])
    kc = _group_rms(z[:, 512:640], kw_ref[...], gk_ref[...])
    vc = z[:, 640:768]
    kn = z[:, 1280:1792]
    vn = z[:, 1792:2304]
    vc_ref[...] = vc.astype(BF16)
    qn_ref[...] = z[:, 768:1280].astype(BF16)
    kn_ref[...] = kn.astype(BF16)
    vn_ref[...] = vn.astype(BF16)
    if sample:
        ch, sh = ch_ref[...], sh_ref[...]
        qc_ref[...] = _rope(qc, ch, sh, DH_C // 2).astype(BF16)
        kc_ref[...] = _rope(kc, ch, sh, DH_C // 2).astype(BF16)
    else:
        qc_ref[...] = qc.astype(BF16)
        kc_ref[...] = kc.astype(BF16)
        kc_st_ref[...] = kc.reshape(kc.shape[0], KV_C, DH_C)
        vc_st_ref[...] = vc.reshape(vc.shape[0], KV_C, DH_C)
        kn_st_ref[...] = kn.reshape(kn.shape[0], H_D, DH_D)
        vn_st_ref[...] = vn.reshape(vn.shape[0], H_D, DH_D)


def _proj_odd(h, mod, layer, sample, nw, w, rope):
    n = h.shape[0]
    tm = ROW_TILE
    ins = [h, mod, nw, w["w_in"], w["q_w"], w["k_w"], w["gq"], w["gk"]]
    in_specs = [_rows(tm, D_MODEL), _mod_spec(layer, sample, tm), _full((1, D_MODEL)), _full((D_MODEL, ODD_IN)),
                _full((1, 512)), _full((1, 128)), _full((512, 512)), _full((128, 128))]
    widths = [512, 128, 128, 512, 512, 512]
    out_shape = [jax.ShapeDtypeStruct((n, wd), BF16) for wd in widths]
    out_specs = [_rows(tm, wd) for wd in widths]
    if sample:
        ins += [rope["ch"], rope["sh"]]
        in_specs += [_rope_spec(tm)] * 2
    else:
        for heads, dh in ((KV_C, DH_C), (KV_C, DH_C), (H_D, DH_D), (H_D, DH_D)):
            out_shape.append(jax.ShapeDtypeStruct((n, heads, dh), F32))
            out_specs.append(pl.BlockSpec((tm, heads, dh), lambda i: (i, 0, 0)))
    return pl.pallas_call(
        functools.partial(_proj_odd_kernel, sample),
        grid=(n // tm,),
        in_specs=in_specs,
        out_specs=out_specs,
        out_shape=out_shape,
        compiler_params=_cparams("parallel"),
        name="proj_odd_s" if sample else "proj_odd_p",
    )(*ins)


def _attn_odd_prompt_kernel(qc_ref, kc_ref, vc_ref, qn_ref, kn_ref, vn_ref, h_ref, wout_ref, mod_ref, nw1_ref,
                            h1_ref, s_scr, p_scr, o_ref):
    c = DH_C ** -0.5 * LOG2E
    group = H_C // KV_C
    kc = (kc_ref[...], _swap_halves(kc_ref[...]))
    vc = (vc_ref[...], _swap_halves(vc_ref[...]))
    maps = []
    for hd in range(H_C):
        half, swap = hd % 2, (hd // group) != (hd % 2)
        ps = slice(hd // 2 * LANES, (hd // 2 + 1) * LANES)
        if half == 0:
            sinks = _pair_store(o_ref, ps.start)
        maps.append((lambda ps=ps, half=half, swap=swap: [_dot_nt(_keep_half(qc_ref[:, ps], half), kc[swap])],
                     lambda swap=swap: [vc[swap]], c, None, sinks[half]))
    base = H_C * DH_C
    c = DH_D ** -0.5 * LOG2E
    for hd in range(H_D):
        half = hd % 2
        ps = slice(hd // 2 * LANES, (hd // 2 + 1) * LANES)
        if half == 0:
            sinks = _pair_store(o_ref, base + ps.start)
        maps.append((lambda ps=ps, half=half: [_dot_nt(_keep_half(qn_ref[:, ps], half), kn_ref[:, ps])],
                     lambda ps=ps: [vn_ref[:, ps]], c, None, sinks[half]))
    _attention(maps, s_scr, p_scr, s_scr.shape[0] // 2, mxu_sum="alternate")
    _out_proj_residual(o_ref, h_ref, wout_ref, mod_ref, nw1_ref, h1_ref)


def _attn_odd_prompt(p, h, w_out, mod, layer, nw1):
    qc, kc, vc, qn, kn, vn = p
    spec = lambda wd: pl.BlockSpec((SEQ, wd), lambda b: (b, 0))
    return pl.pallas_call(
        _attn_odd_prompt_kernel,
        grid=(BATCH,),
        in_specs=[spec(512), spec(128), spec(128), spec(512), spec(512), spec(512),
                  spec(D_MODEL), _full((D_MODEL, D_MODEL)),
                  pl.BlockSpec((None, None, 6, D_MODEL), lambda b: (layer, 0, 0, 0)), _full((1, D_MODEL))],
        out_specs=spec(D_MODEL),
        out_shape=jax.ShapeDtypeStruct((qc.shape[0], D_MODEL), F32),
        scratch_shapes=_attn_scratch(SEQ, SEQ, CONTEXT_GROUP) + [pltpu.VMEM((SEQ, D_MODEL), BF16)],
        compiler_params=_cparams("parallel"),
        name="attn_odd_p",
    )(qc, kc, vc, qn, kn, vn, h, w_out, mod, nw1)


def _gqa_sample_kernel(q_ref, k_ref, v_ref, ck_ref, cv_ref, o_ref, s_scr, p_scr):
    c = DH_C ** -0.5 * LOG2E
    group = H_C // KV_C
    k_loc = (k_ref[...], _swap_halves(k_ref[...]))
    v_loc = (v_ref[...], _swap_halves(v_ref[...]))
    k_ctx = ck_ref[...].astype(BF16)
    v_ctx = cv_ref[...].astype(BF16)
    k_ctx = (k_ctx, _swap_halves(k_ctx))
    v_ctx = (v_ctx, _swap_halves(v_ctx))
    maps = []
    for hd in range(H_C):
        half, swap = hd % 2, (hd // group) != (hd % 2)
        ps = slice(hd // 2 * LANES, (hd // 2 + 1) * LANES)
        if half == 0:
            sinks = _pair_store(o_ref, ps.start)

        def scores(ps=ps, half=half, swap=swap):
            q = _keep_half(q_ref[:, ps], half)
            return [_dot_nt(q, k_loc[swap]), _dot_nt(q, k_ctx[swap])]

        maps.append((scores, lambda swap=swap: [v_loc[swap], v_ctx[swap]], c, None, sinks[half]))
    _attention(maps, s_scr, p_scr, s_scr.shape[0] // 2)


def _gqa_sample(qc, kc, vc, cache_k, cache_v):
    tq, per = Q_TILE, DEC_SEQ // Q_TILE
    return pl.pallas_call(
        _gqa_sample_kernel,
        grid=(DEC_BATCH, per),
        in_specs=[pl.BlockSpec((tq, 512), lambda b, i: (b * per + i, 0)),
                  pl.BlockSpec((DEC_SEQ, 128), lambda b, i: (b, 0)),
                  pl.BlockSpec((DEC_SEQ, 128), lambda b, i: (b, 0)),
                  pl.BlockSpec((None, PAST_LEN, 128), lambda b, i: (b, 0, 0)),
                  pl.BlockSpec((None, PAST_LEN, 128), lambda b, i: (b, 0, 0))],
        out_specs=pl.BlockSpec((tq, 512), lambda b, i: (b * per + i, 0)),
        out_shape=jax.ShapeDtypeStruct((qc.shape[0], 512), BF16),
        scratch_shapes=_attn_scratch(tq, DEC_SEQ + PAST_LEN, LATENT_GROUP),
        compiler_params=_cparams("parallel", "parallel"),
        name="gqa_s",
    )(qc, kc, vc, cache_k, cache_v)


NA_ROWS = DEC_SEQ // GRID_W
NA_KR = min(NA_WIN_ROWS, NA_ROWS)
NA_LOC = NA_KR * GRID_W


NA_STEP_ROWS = 4


def _na_first_key_row(r):
    return jnp.clip(r - NA_KR // 2, 0, NA_ROWS - NA_KR)


def _na_sample_kernel(q_ref, k_ref, v_ref, ck_ref, cv_ref, bias_ref, o_ref, s_scr, p_scr):
    scale = DH_D ** -0.5
    k_ctx = ck_ref[...].astype(BF16)
    v_ctx = cv_ref[...].astype(BF16)
    rb = _softmax_block_rows(NA_LOC + PAST_LEN)
    col_ok = {}
    for r0 in range(0, GRID_W, rb):
        wq = lax.broadcasted_iota(jnp.int32, (rb, NA_LOC), 0) + r0
        wk = lax.broadcasted_iota(jnp.int32, (rb, NA_LOC), 1) % GRID_W
        cs = jnp.clip(wq - NA_WIN_COLS // 2, 0, GRID_W - NA_WIN_COLS)
        col_ok[r0] = (wk >= cs) & (wk < cs + NA_WIN_COLS)
    maps = []
    for t in range(NA_STEP_ROWS):
        r = pl.program_id(1) * NA_STEP_ROWS + t
        keys = pl.ds(pl.multiple_of(_na_first_key_row(r) * GRID_W, GRID_W), NA_LOC)
        band = _na_first_key_row(r) - r + NA_WIN_ROWS - 1
        rows = slice(t * GRID_W, (t + 1) * GRID_W)
        for hd in range(H_D):
            half = hd % 2
            ps = slice(hd // 2 * LANES, (hd // 2 + 1) * LANES)
            if half == 0:
                sinks = _pair_store(o_ref, ps.start, rows)

            def fix(s, r0, hd=hd, band=band):
                loc = s[:, :NA_LOC] * scale + bias_ref[hd, band, r0:r0 + rb, :]
                return jnp.concatenate([jnp.where(col_ok[r0], loc, NEG_INF), s[:, NA_LOC:] * scale], axis=1)

            def scores(ps=ps, half=half, rows=rows, keys=keys):
                q = _keep_half(q_ref[rows, ps], half)
                return [_dot_nt(q, k_ref[keys, ps]), _dot_nt(q, k_ctx[:, ps])]

            maps.append((scores, lambda ps=ps, keys=keys: [v_ref[keys, ps], v_ctx[:, ps]], LOG2E, fix, sinks[half]))
    _attention(maps, s_scr, p_scr, s_scr.shape[0] // 2)


def _na_sample(qn, kn, vn, cache_k, cache_v, bias):
    steps = NA_ROWS // NA_STEP_ROWS
    tq = NA_STEP_ROWS * GRID_W

    return pl.pallas_call(
        _na_sample_kernel,
        grid=(DEC_BATCH, steps),
        in_specs=[pl.BlockSpec((tq, 512), lambda b, i: (b * steps + i, 0)),
                  pl.BlockSpec((DEC_SEQ, 512), lambda b, i: (b, 0)),
                  pl.BlockSpec((DEC_SEQ, 512), lambda b, i: (b, 0)),
                  pl.BlockSpec((None, PAST_LEN, 512), lambda b, i: (b, 0, 0)),
                  pl.BlockSpec((None, PAST_LEN, 512), lambda b, i: (b, 0, 0)),
                  pl.BlockSpec((H_D, NA_WIN_ROWS, GRID_W, NA_LOC), lambda b, i: (0, 0, 0, 0),
                               pipeline_mode=pl.Buffered(1))],
        out_specs=pl.BlockSpec((tq, 512), lambda b, i: (b * steps + i, 0)),
        out_shape=jax.ShapeDtypeStruct((qn.shape[0], 512), BF16),
        scratch_shapes=_attn_scratch(GRID_W, NA_LOC + PAST_LEN, CONTEXT_GROUP),
        compiler_params=_cparams("parallel", "parallel"),
        name="na_s",
    )(qn, kn, vn, cache_k, cache_v, bias)


def _na_bias_table(rpb):
    edge = GRID_W - NA_WIN_COLS
    n_dr = 2 * NA_WIN_ROWS - 1
    v = jnp.pad(rpb.astype(F32), ((0, 0), (0, 0), (edge, edge + 1)), mode="edge")
    skew = jnp.tile(v, (1, 1, GRID_W))[:, :, :GRID_W * (2 * GRID_W - 1)]
    toep = skew.reshape(H_D, n_dr, GRID_W, 2 * GRID_W - 1)[:, :, :, GRID_W - 1:]
    flat = toep.transpose(0, 2, 1, 3).reshape(H_D, GRID_W, n_dr * GRID_W)
    return jnp.stack([flat[:, :, d0 * GRID_W:d0 * GRID_W + NA_LOC] for d0 in range(NA_WIN_ROWS)], axis=1)


def _post_attn_kernel(n_parts, *refs):
    o_refs = refs[:n_parts]
    h_ref, wout_ref, mod_ref, nw1_ref, h1_ref = refs[n_parts:]
    y = None
    off = 0
    for o_ref in o_refs:
        wd = o_ref.shape[1]
        part = _dot(o_ref[...], wout_ref[off:off + wd, :])
        y = part if y is None else y + part
        off += wd
    mod = mod_ref[...]
    h1_ref[...] = h_ref[...] + mod[2:3] * _rms(y, nw1_ref[...])


def _post_attn(o_parts, h, w_out, mod, layer, sample, nw1):
    n = h.shape[0]
    tm = ROW_TILE
    in_specs = [_rows(tm, o.shape[1]) for o in o_parts]
    in_specs += [_rows(tm, D_MODEL), _full((D_MODEL, D_MODEL)), _mod_spec(layer, sample, tm), _full((1, D_MODEL))]
    return pl.pallas_call(
        functools.partial(_post_attn_kernel, len(o_parts)),
        grid=(n // tm,),
        in_specs=in_specs,
        out_specs=_rows(tm, D_MODEL),
        out_shape=jax.ShapeDtypeStruct((n, D_MODEL), F32),
        compiler_params=_cparams("parallel"),
        name="post_attn_s" if sample else "post_attn_p",
    )(*o_parts, h, w_out, mod, nw1)


FF_PAIR = 2 * FF_CHUNK
N_FF = D_FF // FF_CHUNK
SUBLANES = 8
ACT_TILES = 4
ROW_BLOCK = 1024
NORM_ROWS = 32


def _ffn_kernel(seq_len, h_ref, wup_ref, cw_ref, cb_ref, wd_ref, mod_ref, nw2_ref, nw3_ref, o_ref,
                u_ref, z0_ref, z1_ref, a_ref, a_last_ref):
    tm = h_ref.shape[0]
    n_blocks = tm // ROW_BLOCK
    rows = ACT_TILES * SUBLANES
    mod = mod_ref[...]
    sub = lax.broadcasted_iota(jnp.int32, (SUBLANES, LANES), 0)
    zero_rows = jnp.zeros((SUBLANES, FF_PAIR), F32)
    for z_ref in (z0_ref, z1_ref):
        z_ref[0:SUBLANES, :] = zero_rows
        z_ref[SUBLANES + tm:2 * SUBLANES + tm, :] = zero_rows

    nw2 = nw2_ref[...]

    def pre_norm(blk):
        for c in range(ROW_BLOCK // NORM_ROWS):
            rs = slice(blk * ROW_BLOCK + c * NORM_ROWS, blk * ROW_BLOCK + (c + 1) * NORM_ROWS)
            u = _rms(h_ref[rs, :], nw2) * (1.0 + mod[4:5]) + mod[3:4]
            u_ref[rs, :] = u.astype(BF16)

    def up(j, z_ref, blk):
        r0 = blk * ROW_BLOCK
        u = u_ref[r0:r0 + ROW_BLOCK, :]
        rows_ = slice(SUBLANES + r0, SUBLANES + r0 + ROW_BLOCK)
        for half in range(2):
            c0 = half * D_FF + j * FF_CHUNK
            if not isinstance(c0, int):
                c0 = pl.multiple_of(c0, FF_CHUNK)
            z_ref[rows_, half * FF_CHUNK:(half + 1) * FF_CHUNK] = _dot(u, wup_ref[:, pl.ds(c0, FF_CHUNK)])

    def act(j, z_ref, col, blk, dst_ref=a_ref):
        cw = cw_ref[j]
        cb = cb_ref[j]
        for lc in range(FF_CHUNK // LANES):
            taps = []
            for lane0 in (lc * LANES, FF_CHUNK + lc * LANES):
                lanes = slice(lane0, lane0 + LANES)
                taps.append([jnp.broadcast_to(cw[k:k + 1, lanes], (SUBLANES, LANES)) for k in range(3)]
                            + [jnp.broadcast_to(cb[:, lanes], (SUBLANES, LANES))])
            for c in range(ROW_BLOCK // rows):
                r = blk * ROW_BLOCK + c * rows
                first = r % seq_len == 0
                last = (r + rows) % seq_len == 0

                def conv(lane0, tap):
                    ext = z_ref[r:r + rows + 2 * SUBLANES, lane0:lane0 + LANES]
                    tiles = [ext[t * SUBLANES:(t + 1) * SUBLANES] for t in range(ACT_TILES + 2)]
                    down = [pltpu.roll(t, 1, 0) for t in tiles[:-1]]
                    up_ = [pltpu.roll(t, SUBLANES - 1, 0) for t in tiles[1:]]
                    out = []
                    for t in range(ACT_TILES):
                        above = 0.0 if (first and t == 0) else down[t]
                        below = 0.0 if (last and t == ACT_TILES - 1) else up_[t + 1]
                        prev = jnp.where(sub == 0, above, down[t + 1])
                        nxt = jnp.where(sub == SUBLANES - 1, below, up_[t])
                        out.append(prev * tap[0] + tiles[t + 1] * tap[1] + nxt * tap[2] + tap[3])
                    return jnp.concatenate(out, axis=0)

                g = conv(lc * LANES, taps[0])
                v = conv(FF_CHUNK + lc * LANES, taps[1])
                a = (g / (1.0 + jnp.exp2(g * -LOG2E))) * v
                lane = col + lc * LANES
                if not isinstance(lane, int):
                    lane = pl.multiple_of(lane, LANES)
                dst_ref[r:r + rows, pl.ds(lane, LANES)] = a.astype(BF16)

    for blk in range(n_blocks):
        pre_norm(blk)
        up(0, z0_ref, blk)

    def pair(i, carry):
        j = 2 * i
        col = pl.multiple_of(j * FF_CHUNK, FF_CHUNK)
        for blk in range(n_blocks):
            up(j + 1, z1_ref, blk)
            act(j, z0_ref, col, blk)
        for blk in range(n_blocks):
            up(j + 2, z0_ref, blk)
            act(j + 1, z1_ref, col + FF_CHUNK, blk)
        return carry

    lax.fori_loop(0, (N_FF - 1) // 2, pair, 0)
    nw3 = nw3_ref[...]
    k_main = (N_FF - 1) * FF_CHUNK
    for blk in range(n_blocks):
        r0 = blk * ROW_BLOCK
        y = _dot(a_ref[r0:r0 + ROW_BLOCK, 0:k_main], wd_ref[0:k_main, :])
        act(N_FF - 1, z0_ref, 0, blk, a_last_ref)
        y = y + _dot(a_last_ref[r0:r0 + ROW_BLOCK, :], wd_ref[k_main:D_FF, :])
        for c in range(ROW_BLOCK // NORM_ROWS):
            rs = slice(r0 + c * NORM_ROWS, r0 + (c + 1) * NORM_ROWS)
            o_ref[rs, :] = h_ref[rs, :] + mod[5:6] * _rms(y[c * NORM_ROWS:(c + 1) * NORM_ROWS], nw3)


def _ffn(h1, w, mod, layer, sample, nw2, nw3):
    n = h1.shape[0]
    tm = FFN_ROW_TILE
    seq_len = DEC_SEQ if sample else SEQ
    once = pl.Buffered(1)
    in_specs = [
        _rows(tm, D_MODEL),
        pl.BlockSpec((None, D_MODEL, 2 * D_FF), lambda i: (layer, 0, 0), pipeline_mode=once),
        pl.BlockSpec((N_FF, 3, FF_PAIR), lambda i: (0, 0, 0), pipeline_mode=once),
        pl.BlockSpec((N_FF, 1, FF_PAIR), lambda i: (0, 0, 0), pipeline_mode=once),
        pl.BlockSpec((None, D_FF, D_MODEL), lambda i: (layer, 0, 0), pipeline_mode=once),
        _mod_spec(layer, sample, tm),
        _full((1, D_MODEL)),
        _full((1, D_MODEL)),
    ]
    return pl.pallas_call(
        functools.partial(_ffn_kernel, seq_len),
        grid=(n // tm,),
        in_specs=in_specs,
        out_specs=_rows(tm, D_MODEL),
        out_shape=jax.ShapeDtypeStruct((n, D_MODEL), F32),
        scratch_shapes=[pltpu.VMEM((tm, D_MODEL), BF16), pltpu.VMEM((tm + 2 * SUBLANES, FF_PAIR), F32),
                        pltpu.VMEM((tm + 2 * SUBLANES, FF_PAIR), F32), pltpu.VMEM((tm, D_FF - FF_CHUNK), BF16),
                        pltpu.VMEM((tm, FF_CHUNK), BF16)],
        compiler_params=pltpu.CompilerParams(dimension_semantics=("parallel",), vmem_limit_bytes=FFN_VMEM_LIMIT),
        name="ffn_s" if sample else "ffn_p",
    )(h1, w["w_up"], w["conv_w"], w["conv_b"], w["w_down"], mod, nw2, nw3)


def _pair_chunks(x):
    lead = x.shape[:-1]
    x = x.reshape(lead + (2, N_FF, FF_CHUNK))
    x = jnp.moveaxis(x, -2, 0)
    return x.reshape((N_FF,) + lead + (FF_PAIR,))


def _rope_tables():
    def table(rot_dim):
        t = np.arange(DEC_SEQ)
        n_freq = rot_dim // 4
        inv = 1.0 / (ROPE_THETA ** (np.arange(n_freq) / n_freq))
        ang = np.concatenate([(t // GRID_W)[:, None] * inv[None, :], (t % GRID_W)[:, None] * inv[None, :]], axis=-1)
        cos = np.cos(ang).astype(np.float32)
        sin = np.sin(ang).astype(np.float32)
        reps = LANES // rot_dim
        return (np.tile(np.concatenate([cos, cos], axis=-1), (1, reps)),
                np.tile(np.concatenate([-sin, sin], axis=-1), (1, reps)))
    ca, sa = table(QK_ROPE)
    ch, sh = table(HEAD_DIM)
    rope_lanes = (np.arange(MLA_SLOT) >= QK_NOPE) & (np.arange(MLA_SLOT) < QK_NOPE + QK_ROPE)
    ca = np.where(rope_lanes[None, :], ca, 1.0).astype(np.float32)
    sa = np.where(rope_lanes[None, :], sa, 0.0).astype(np.float32)
    return {"ca": jnp.asarray(ca), "sa": jnp.asarray(sa), "ch": jnp.asarray(ch), "sh": jnp.asarray(sh)}


def _group_mean_matrix(width):
    idx = np.arange(width) // HEAD_DIM
    return jnp.asarray((idx[:, None] == idx[None, :]).astype(np.float32) / HEAD_DIM, BF16)


def kernel(x_prompt, x_sample, c, cache_mla_ckv, cache_mla_kpe, cache_diff_k, cache_diff_v, cache_gqa_k, cache_gqa_v, cache_na_k, cache_na_v, c_ctx, norm_w, w_mod, b_mod, w_in_even, w_out_even, w_uq, q_norm_w, kv_norm_w, w_uk, w_uv, diff_lam, diff_subln_w, w_in_odd, w_out_odd, qk_norm_w, na_rpb, w_up, conv_w, conv_b, w_down):
    rope = _rope_tables()
    n_p = BATCH * SEQ
    n_s = DEC_BATCH * DEC_SEQ
    cvecs = jnp.concatenate([c_ctx[None, :], c, jnp.zeros((MOD_ROWS - 1 - DEC_BATCH, D_MODEL), F32)], axis=0)
    mod = _modulation(cvecs, w_mod, b_mod).reshape(DEPTH, MOD_ROWS, 6, D_MODEL)
    hp = x_prompt.reshape(n_p, D_MODEL)
    hs = x_sample.reshape(n_s, D_MODEL)
    even_states, odd_states = [], []
    w_up_b = w_up.astype(BF16)
    w_down_b = w_down.astype(BF16)
    for l in range(DEPTH):
        i = l // 2
        nw = [norm_w[l, k][None, :] for k in range(4)]
        if l % 2 == 0:
            lam_init = 0.8 - 0.6 * math.exp(-0.3 * l)
            wi = w_in_even[i]
            w_uq3 = w_uq[i].reshape(Q_LORA, H_A, QK_NOPE + QK_ROPE)
            w = {
                "w_in": jnp.concatenate([wi[:, :384], jnp.zeros((D_MODEL, QK_NOPE), F32), wi[:, 384:416],
                                         jnp.zeros((D_MODEL, MLA_SLOT - QK_NOPE - QK_ROPE), F32), wi[:, 416:]],
                                        axis=1).astype(BF16),
                "q_norm_w": q_norm_w[i][None, :],
                "kv_norm_w": kv_norm_w[i][None, :],
                "w_uq": jnp.pad(w_uq3, ((0, 0), (0, 0), (0, MLA_SLOT - QK_NOPE - QK_ROPE))
                                ).reshape(Q_LORA, H_A * MLA_SLOT).astype(BF16),
                "w_uk": jnp.pad(w_uk[i].reshape(KV_LORA, H_A, QK_NOPE), ((0, 0), (0, 0), (0, MLA_SLOT - QK_NOPE))
                                ).reshape(KV_LORA, H_A * MLA_SLOT).astype(BF16),
                "w_uv": w_uv[i].astype(BF16),
                "diff_lam": diff_lam[i],
                "diff_subln_w": diff_subln_w[i][None, :],
            }
            outs_p = _proj_even(hp, mod, l, False, nw[0], w, rope)
            outs_s = _proj_even(hs, mod, l, True, nw[0], w, rope)
            even_states.append(outs_p[6:])
            w_out = w_out_even[i].astype(BF16)
            h1p = _attn_even(outs_p[:6], False, lam_init, None, w, hp, w_out, mod, l, nw[1])
            caches = (cache_mla_ckv[:, i], cache_mla_kpe[:, i],
                      cache_diff_k[:, i].reshape(DEC_BATCH, PAST_LEN, 512),
                      cache_diff_v[:, i].reshape(DEC_BATCH, PAST_LEN, 512))
            h1s = _attn_even(outs_s, True, lam_init, caches, w, hs, w_out, mod, l, nw[1])
        else:
            q_w = jnp.tile(qk_norm_w[i, 0], H_C)[None, :]
            k_w = jnp.tile(qk_norm_w[i, 1], KV_C)[None, :]
            w = {"w_in": w_in_odd[i].astype(BF16), "q_w": q_w, "k_w": k_w,
                 "gq": _group_mean_matrix(512), "gk": _group_mean_matrix(128)}
            outs_p = _proj_odd(hp, mod, l, False, nw[0], w, rope)
            outs_s = _proj_odd(hs, mod, l, True, nw[0], w, rope)
            odd_states.append(outs_p[6:])
            w_out = w_out_odd[i].astype(BF16)
            h1p = _attn_odd_prompt(outs_p[:6], hp, w_out, mod, l, nw[1])
            qc, kc, vc, qn, kn, vn = outs_s
            o_c = _gqa_sample(qc, kc, vc, cache_gqa_k[:, i].reshape(DEC_BATCH, PAST_LEN, 128),
                              cache_gqa_v[:, i].reshape(DEC_BATCH, PAST_LEN, 128))
            o_d = _na_sample(qn, kn, vn, cache_na_k[:, i].reshape(DEC_BATCH, PAST_LEN, 512),
                             cache_na_v[:, i].reshape(DEC_BATCH, PAST_LEN, 512), _na_bias_table(na_rpb[i]))
            h1s = _post_attn([o_c, o_d], hs, w_out, mod, l, True, nw[1])
        wf = {"w_up": w_up_b, "conv_w": _pair_chunks(conv_w[l]),
              "conv_b": _pair_chunks(conv_b[l][None, :]), "w_down": w_down_b}
        hp = _ffn(h1p, wf, mod, l, False, nw[2], nw[3])
        hs = _ffn(h1s, wf, mod, l, True, nw[2], nw[3])

    def stack(states, k, shape):
        return jnp.stack([st[k].reshape((BATCH, SEQ) + shape) for st in states], axis=1)

    new_mla_ckv = stack(even_states, 0, (KV_LORA,))
    new_mla_kpe = stack(even_states, 1, (QK_ROPE,))
    new_diff_k = stack(even_states, 2, (H_B, 2 * DH_B))
    new_diff_v = stack(even_states, 3, (H_B, 2 * DH_B))
    new_gqa_k = stack(odd_states, 0, (KV_C, DH_C))
    new_gqa_v = stack(odd_states, 1, (KV_C, DH_C))
    new_na_k = stack(odd_states, 2, (H_D, DH_D))
    new_na_v = stack(odd_states, 3, (H_D, DH_D))
    return (hp.reshape(BATCH, SEQ, D_MODEL), hs.reshape(DEC_BATCH, DEC_SEQ, D_MODEL),
            new_mla_ckv, new_mla_kpe, new_diff_k, new_diff_v, new_gqa_k, new_gqa_v, new_na_k, new_na_v)
```

```python
import functools
import math

import numpy as np
import jax
import jax.numpy as jnp
from jax import lax
from jax.experimental import pallas as pl
from jax.experimental.pallas import tpu as pltpu

D_MODEL = 1024
BATCH = 32
SEQ = 256
DEPTH = 2
DEC_BATCH = 4
DEC_SEQ = 1024
PAST_LEN = 256
GRID_W = 64
HEAD_DIM = 64
H_A = 8
QK_NOPE = 64
QK_ROPE = 32
V_A = 64
Q_LORA = 256
KV_LORA = 128
H_B = 4
DH_B = HEAD_DIM
H_C = 8
KV_C = 2
DH_C = HEAD_DIM
H_D = 8
DH_D = HEAD_DIM
NA_WIN_ROWS = 8
NA_WIN_COLS = 16
D_FF = 2816
ROPE_THETA = 10000.0
EPS = 1e-6
NEG_INF = -1e30

LANES = 128
MOD_ROWS = 8
ROW_TILE = 512
FFN_ROW_TILE = 1024
FF_CHUNK = 256
Q_TILE = 512
LATENT_GROUP = 1
CONTEXT_GROUP = 4
VMEM_LIMIT = 56 * 1024 * 1024
FFN_VMEM_LIMIT = 56 * 1024 * 1024

F32 = jnp.float32
BF16 = jnp.bfloat16
LOG2E = math.log2(math.e)


def _cparams(*sem):
    return pltpu.CompilerParams(dimension_semantics=sem, vmem_limit_bytes=VMEM_LIMIT)


def _dot(a, b):
    return jnp.dot(a, b, preferred_element_type=F32)


def _dot_nt(a, b):
    return lax.dot_general(a, b, (((1,), (1,)), ((), ())), preferred_element_type=F32)


def _rms(x, w):
    return x * lax.rsqrt(jnp.mean(x * x, axis=-1, keepdims=True) + EPS) * w


def _group_rms(x, w, gmat):
    x2 = x * x
    hi = x2.astype(BF16)
    lo = (x2 - hi.astype(F32)).astype(BF16)
    ms = _dot(hi, gmat) + _dot(lo, gmat)
    return x * lax.rsqrt(ms + EPS) * w


def _rope(x, cos, sin_signed, half):
    outs = []
    for j in range(x.shape[1] // LANES):
        xc = x[:, j * LANES:(j + 1) * LANES]
        lane = lax.broadcasted_iota(jnp.int32, xc.shape, 1)
        first = (lane % (2 * half)) < half
        partner = jnp.where(first, pltpu.roll(xc, LANES - half, 1), pltpu.roll(xc, half, 1))
        outs.append(xc * cos + partner * sin_signed)
    return outs[0] if len(outs) == 1 else jnp.concatenate(outs, axis=1)


def _softmax_block_rows(n_keys):
    return max(16, min(64, (16 * 1280 // n_keys) // 16 * 16))


def _attention(maps, s_scr, p_scr, group, mxu_sum=False):
    slots = s_scr.shape[0]
    staged = {}

    def on_mxu(i):
        return bool(mxu_sum) and (mxu_sum != "alternate" or i % 2 == 1)

    def stage(i):
        s_ref = s_scr.at[i % slots]
        offs, off = [], 0
        for s in maps[i][0]():
            s_ref[:, off:off + s.shape[1]] = s
            offs.append(off)
            off += s.shape[1]
        staged[i] = (offs, off)

    def softmax(i):
        _, _, c, fix, _ = maps[i]
        n_keys = staged[i][1]
        s_ref, p_ref = s_scr.at[i % slots], p_scr.at[i % slots]
        rb = _softmax_block_rows(n_keys)
        sums = []
        for r0 in range(0, s_ref.shape[0], rb):
            s = s_ref[r0:r0 + rb, 0:n_keys]
            if fix is not None:
                s = fix(s, r0)
            m = jnp.max(s, axis=-1, keepdims=True)
            p = jnp.exp2((s - m) * c)
            if not on_mxu(i):
                sums.append(jnp.sum(p, axis=-1, keepdims=True))
            p_ref[r0:r0 + rb, 0:n_keys] = p.astype(BF16)
        return None if on_mxu(i) else jnp.concatenate(sums, axis=0)

    def weighted_values(i, den):
        _, values, _, _, sink = maps[i]
        p_ref = p_scr.at[i % slots]
        offs, n_keys = staged.pop(i)
        acc = None
        for o, v in zip(offs, values()):
            part = _dot(p_ref[:, o:o + v.shape[0]], v)
            acc = part if acc is None else acc + part
        if on_mxu(i):
            den = _dot(p_ref[:, 0:n_keys], jnp.ones((n_keys, LANES), BF16))
        sink(acc / den)

    groups = [range(g, min(g + group, len(maps))) for g in range(0, len(maps), group)]
    for i in groups[0]:
        stage(i)
    for gi, grp in enumerate(groups):
        if gi + 1 < len(groups):
            for i in groups[gi + 1]:
                stage(i)
        dens = [softmax(i) for i in grp]
        for i, den in zip(grp, dens):
            weighted_values(i, den)


def _attn_scratch(tq, n_keys, group):
    return [pltpu.VMEM((2 * group, tq, n_keys), F32), pltpu.VMEM((2 * group, tq, n_keys), BF16)]


def _upper_half(shape):
    return lax.broadcasted_iota(jnp.int32, shape, 1) >= HEAD_DIM


def _keep_half(x, half):
    upper = _upper_half(x.shape)
    return jnp.where(upper if half else ~upper, x, jnp.zeros_like(x))


def _swap_halves(x):
    return jnp.concatenate([x[:, HEAD_DIM:], x[:, :HEAD_DIM]], axis=1)


def _pair_store(o_ref, c0, rows=slice(None)):
    got = {}

    def make(half):
        def sink(o):
            got[half] = o
            if len(got) == 2:
                o_ref[rows, c0:c0 + LANES] = jnp.where(_upper_half(o.shape), got[1], got[0]).astype(BF16)
        return sink
    return make(0), make(1)


def _out_proj_residual(o_scr, h_ref, wout_ref, mod_ref, nw1_ref, h1_ref):
    y = _dot(o_scr[...], wout_ref[...])
    gate = mod_ref[2:3, :]
    nw1 = nw1_ref[...]
    for c in range(y.shape[0] // NORM_ROWS):
        rs = slice(c * NORM_ROWS, (c + 1) * NORM_ROWS)
        h1_ref[rs, :] = h_ref[rs, :] + gate * _rms(y[rs], nw1)


def _mod_kernel(c_ref, w_ref, b_ref, o_ref):
    cv = c_ref[...]
    act = cv / (1.0 + jnp.exp(-cv))
    o_ref[...] = _dot(act.astype(BF16), w_ref[...].astype(BF16)) + b_ref[...]


def _modulation(cvecs, w_mod, b_mod):
    tn = 1024
    n = 6 * D_MODEL
    return pl.pallas_call(
        _mod_kernel,
        grid=(DEPTH, n // tn),
        in_specs=[
            pl.BlockSpec((MOD_ROWS, D_MODEL), lambda l, j: (0, 0)),
            pl.BlockSpec((None, D_MODEL, tn), lambda l, j: (l, 0, j)),
            pl.BlockSpec((None, 1, tn), lambda l, j: (l, 0, j)),
        ],
        out_specs=pl.BlockSpec((None, MOD_ROWS, tn), lambda l, j: (l, 0, j)),
        out_shape=jax.ShapeDtypeStruct((DEPTH, MOD_ROWS, n), F32),
        compiler_params=_cparams("parallel", "parallel"),
        name="adaln_mod",
    )(cvecs, w_mod, b_mod.reshape(DEPTH, 1, n))


def _mod_spec(layer, sample, tm):
    if sample:
        per = DEC_SEQ // tm
        return pl.BlockSpec((None, None, 6, D_MODEL), lambda i, *_: (layer, 1 + i // per, 0, 0))
    return pl.BlockSpec((None, None, 6, D_MODEL), lambda i, *_: (layer, 0, 0, 0))


def _full(shape):
    nd = len(shape)
    return pl.BlockSpec(shape, lambda *_: (0,) * nd)


def _rows(tm, width):
    return pl.BlockSpec((tm, width), lambda i, *_: (i, 0))


def _rope_spec(tm):
    per = DEC_SEQ // tm
    return pl.BlockSpec((tm, LANES), lambda i, *_: (i % per, 0))


EVEN_IN = 2048
MLA_SLOT = 128


def _proj_even_kernel(sample, *refs):
    if sample:
        (h_ref, mod_ref, nw_ref, win_ref, qnw_ref, kvnw_ref, wuq_ref, wuk_ref, wuv_ref,
         ca_ref, sa_ref, ch_ref, sh_ref,
         qa_ref, ka_ref, va_ref, qd_ref, kd_ref, vd_ref) = refs
    else:
        (h_ref, mod_ref, nw_ref, win_ref, qnw_ref, kvnw_ref, wuq_ref, wuk_ref, wuv_ref,
         qa_ref, ka_ref, va_ref, qd_ref, kd_ref, vd_ref,
         ckv_st_ref, kpe_st_ref, kd_st_ref, vd_st_ref) = refs
    mod = mod_ref[...]
    u = _rms(h_ref[...], nw_ref[...]) * (1.0 + mod[1:2]) + mod[0:1]
    z = _dot(u.astype(BF16), win_ref[...])
    cq = _rms(z[:, 0:256], qnw_ref[...]).astype(BF16)
    qa = _dot(cq, wuq_ref[...])
    ckv = _rms(z[:, 256:384], kvnw_ref[...])
    ckv_b = ckv.astype(BF16)
    kn = _dot(ckv_b, wuk_ref[...])
    va_ref[...] = _dot(ckv_b, wuv_ref[...]).astype(BF16)
    kpe_slot = z[:, 384:512]
    qd = z[:, 512:1024]
    kd = z[:, 1024:1536]
    vd = z[:, 1536:2048]
    vd_ref[...] = vd.astype(BF16)
    if sample:
        ca, sa, ch, sh = ca_ref[...], sa_ref[...], ch_ref[...], sh_ref[...]
        qa = _rope(qa, ca, sa, QK_ROPE // 2)
        kpe_rot = _rope(kpe_slot, ca, sa, QK_ROPE // 2)
        qd_ref[...] = _rope(qd, ch, sh, DH_B // 2).astype(BF16)
        kd_ref[...] = _rope(kd, ch, sh, DH_B // 2).astype(BF16)
    else:
        kpe_rot = kpe_slot
        qd_ref[...] = qd.astype(BF16)
        kd_ref[...] = kd.astype(BF16)
        ckv_st_ref[...] = ckv
        kpe_st_ref[...] = kpe_slot[:, QK_NOPE:QK_NOPE + QK_ROPE]
        kd_st_ref[...] = kd.reshape(kd.shape[0], H_B, 2 * DH_B)
        vd_st_ref[...] = vd.reshape(vd.shape[0], H_B, 2 * DH_B)
    qa_ref[...] = qa.astype(BF16)
    for hd in range(H_A):
        sl = slice(hd * MLA_SLOT, (hd + 1) * MLA_SLOT)
        ka_ref[:, sl] = (kn[:, sl] + kpe_rot).astype(BF16)


def _proj_even(h, mod, layer, sample, nw, w, rope):
    n = h.shape[0]
    tm = ROW_TILE
    wide = H_A * MLA_SLOT
    ins = [h, mod, nw, w["w_in"], w["q_norm_w"], w["kv_norm_w"], w["w_uq"], w["w_uk"], w["w_uv"]]
    in_specs = [_rows(tm, D_MODEL), _mod_spec(layer, sample, tm), _full((1, D_MODEL)), _full((D_MODEL, EVEN_IN)),
                _full((1, Q_LORA)), _full((1, KV_LORA)), _full((Q_LORA, wide)), _full((KV_LORA, wide)),
                _full((KV_LORA, 512))]
    widths = [wide, wide, 512, 512, 512, 512]
    out_shape = [jax.ShapeDtypeStruct((n, wd), BF16) for wd in widths]
    out_specs = [_rows(tm, wd) for wd in widths]
    if sample:
        ins += [rope["ca"], rope["sa"], rope["ch"], rope["sh"]]
        in_specs += [_rope_spec(tm)] * 4
    else:
        for wd in (KV_LORA, QK_ROPE):
            out_shape.append(jax.ShapeDtypeStruct((n, wd), F32))
            out_specs.append(_rows(tm, wd))
        for _ in range(2):
            out_shape.append(jax.ShapeDtypeStruct((n, H_B, 2 * DH_B), F32))
            out_specs.append(pl.BlockSpec((tm, H_B, 2 * DH_B), lambda i: (i, 0, 0)))
    return pl.pallas_call(
        functools.partial(_proj_even_kernel, sample),
        grid=(n // tm,),
        in_specs=in_specs,
        out_specs=out_specs,
        out_shape=out_shape,
        compiler_params=_cparams("parallel"),
        name="proj_even_s" if sample else "proj_even_p",
    )(*ins)


def _attn_even_kernel(sample, lam_init, *refs):
    if sample:
        (qa_ref, ka_ref, va_ref, qd_ref, kd_ref, vd_ref,
         cckv_ref, ckpe_ref, cdk_ref, cdv_ref, wuk_ref, wuv_ref, lam_ref, sub_ref,
         h_ref, wout_ref, mod_ref, nw1_ref, h1_ref, s_scr, p_scr, o_ref) = refs
    else:
        (qa_ref, ka_ref, va_ref, qd_ref, kd_ref, vd_ref, lam_ref, sub_ref,
         h_ref, wout_ref, mod_ref, nw1_ref, h1_ref, s_scr, p_scr, o_ref) = refs
    lf = lam_ref[...]
    lam = (jnp.exp(jnp.sum(lf[0:1] * lf[1:2], axis=-1, keepdims=True))
           - jnp.exp(jnp.sum(lf[2:3] * lf[3:4], axis=-1, keepdims=True)) + lam_init)
    if sample:
        cckv = cckv_ref[...].astype(BF16)
        kn_ctx = _dot(cckv, wuk_ref[...])
        va_ctx = _dot(cckv, wuv_ref[...]).astype(BF16)
        n_ctx = cckv.shape[0]
        kpe_ctx = jnp.concatenate([jnp.zeros((n_ctx, QK_NOPE), F32), ckpe_ref[...],
                                   jnp.zeros((n_ctx, MLA_SLOT - QK_NOPE - QK_ROPE), F32)], axis=1)
        kd_ctx = cdk_ref[...].astype(BF16)
        vd_ctx = cdv_ref[...].astype(BF16)
    maps = []
    c_a = (QK_NOPE + QK_ROPE) ** -0.5 * LOG2E
    for hd in range(H_A):
        sl = slice(hd * MLA_SLOT, (hd + 1) * MLA_SLOT)
        if hd % 2 == 0:
            sinks = _pair_store(o_ref, hd * V_A)

        def scores(sl=sl):
            q = qa_ref[:, sl]
            out = [_dot_nt(q, ka_ref[:, sl])]
            if sample:
                out.append(_dot_nt(q, (kn_ctx[:, sl] + kpe_ctx).astype(BF16)))
            return out

        def values(vs=slice(hd // 2 * LANES, (hd // 2 + 1) * LANES)):
            return [va_ref[:, vs]] + ([va_ctx[:, vs]] if sample else [])

        maps.append((scores, values, c_a, None, sinks[hd % 2]))
    c_b = DH_B ** -0.5 * LOG2E
    base = H_A * V_A
    sub_w = sub_ref[...]
    for hd in range(H_B):
        hs = slice(hd * 2 * DH_B, (hd + 1) * 2 * DH_B)
        outs = []

        def sink(o, outs=outs, hs=hs):
            outs.append(o)
            if len(outs) == 2:
                ob = _rms(outs[0] - lam * outs[1], sub_w) * (1.0 - lam_init)
                o_ref[:, base + hs.start:base + hs.stop] = ob.astype(BF16)

        def values(hs=hs):
            return [vd_ref[:, hs]] + ([vd_ctx[:, hs]] if sample else [])

        for comp in range(2):
            def scores(hs=hs, comp=comp):
                q = _keep_half(qd_ref[:, hs], comp)
                out = [_dot_nt(q, kd_ref[:, hs])]
                if sample:
                    out.append(_dot_nt(q, kd_ctx[:, hs]))
                return out

            maps.append((scores, values, c_b, None, sink))
    _attention(maps, s_scr, p_scr, s_scr.shape[0] // 2, mxu_sum=not sample)
    _out_proj_residual(o_ref, h_ref, wout_ref, mod_ref, nw1_ref, h1_ref)


def _attn_even(p, sample, lam_init, caches, w, h, w_out, mod, layer, nw1):
    qa, ka, va, qd, kd, vd = p
    n = qa.shape[0]
    wide = H_A * MLA_SLOT
    if sample:
        tq, per = Q_TILE, DEC_SEQ // Q_TILE
        grid = (DEC_BATCH, per)
        qspec = lambda wd: pl.BlockSpec((tq, wd), lambda b, i: (b * per + i, 0))
        kspec = lambda wd: pl.BlockSpec((DEC_SEQ, wd), lambda b, i: (b, 0))
        cspec = lambda wd: pl.BlockSpec((None, PAST_LEN, wd), lambda b, i: (b, 0, 0))
        ins = [qa, ka, va, qd, kd, vd, *caches, w["w_uk"], w["w_uv"], w["diff_lam"], w["diff_subln_w"]]
        in_specs = [qspec(wide), kspec(wide), kspec(512), qspec(512), kspec(512), kspec(512),
                    cspec(KV_LORA), cspec(QK_ROPE), cspec(512), cspec(512),
                    _full((KV_LORA, wide)), _full((KV_LORA, 512)), _full((4, DH_B)), _full((1, 2 * DH_B))]
        out_spec = qspec(D_MODEL)
        mod_spec = pl.BlockSpec((None, None, 6, D_MODEL), lambda b, i: (layer, 1 + b, 0, 0))
        sem = ("parallel", "parallel")
        n_keys, group = DEC_SEQ + PAST_LEN, LATENT_GROUP
    else:
        tq = SEQ
        grid = (BATCH,)
        spec = lambda wd: pl.BlockSpec((SEQ, wd), lambda b: (b, 0))
        ins = [qa, ka, va, qd, kd, vd, w["diff_lam"], w["diff_subln_w"]]
        in_specs = [spec(wide), spec(wide), spec(512), spec(512), spec(512), spec(512),
                    _full((4, DH_B)), _full((1, 2 * DH_B))]
        out_spec = spec(D_MODEL)
        mod_spec = pl.BlockSpec((None, None, 6, D_MODEL), lambda b: (layer, 0, 0, 0))
        sem = ("parallel",)
        n_keys, group = SEQ, CONTEXT_GROUP
    ins += [h, w_out, mod, nw1]
    in_specs += [out_spec, _full((D_MODEL, D_MODEL)), mod_spec, _full((1, D_MODEL))]
    return pl.pallas_call(
        functools.partial(_attn_even_kernel, sample, lam_init),
        grid=grid,
        in_specs=in_specs,
        out_specs=out_spec,
        out_shape=jax.ShapeDtypeStruct((n, D_MODEL), F32),
        scratch_shapes=_attn_scratch(tq, n_keys, group) + [pltpu.VMEM((tq, D_MODEL), BF16)],
        compiler_params=_cparams(*sem),
        name="attn_even_s" if sample else "attn_even_p",
    )(*ins)


ODD_IN = 2304


def _proj_odd_kernel(sample, *refs):
    if sample:
        (h_ref, mod_ref, nw_ref, win_ref, qw_ref, kw_ref, gq_ref, gk_ref, ch_ref, sh_ref,
         qc_ref, kc_ref, vc_ref, qn_ref, kn_ref, vn_ref) = refs
    else:
        (h_ref, mod_ref, nw_ref, win_ref, qw_ref, kw_ref, gq_ref, gk_ref,
         qc_ref, kc_ref, vc_ref, qn_ref, kn_ref, vn_ref,
         kc_st_ref, vc_st_ref, kn_st_ref, vn_st_ref) = refs
    mod = mod_ref[...]
    u = _rms(h_ref[...], nw_ref[...]) * (1.0 + mod[1:2]) + mod[0:1]
    z = _dot(u.astype(BF16), win_ref[...])
    qc = _group_rms(z[:, 0:512], qw_ref[...], gq_ref[...])
    kc = _group_rms(z[:, 512:640], kw_ref[...], gk_ref[...])
    vc = z[:, 640:768]
    kn = z[:, 1280:1792]
    vn = z[:, 1792:2304]
    vc_ref[...] = vc.astype(BF16)
    qn_ref[...] = z[:, 768:1280].astype(BF16)
    kn_ref[...] = kn.astype(BF16)
    vn_ref[...] = vn.astype(BF16)
    if sample:
        ch, sh = ch_ref[...], sh_ref[...]
        qc_ref[...] = _rope(qc, ch, sh, DH_C // 2).astype(BF16)
        kc_ref[...] = _rope(kc, ch, sh, DH_C // 2).astype(BF16)
    else:
        qc_ref[...] = qc.astype(BF16)
        kc_ref[...] = kc.astype(BF16)
        kc_st_ref[...] = kc.reshape(kc.shape[0], KV_C, DH_C)
        vc_st_ref[...] = vc.reshape(vc.shape[0], KV_C, DH_C)
        kn_st_ref[...] = kn.reshape(kn.shape[0], H_D, DH_D)
        vn_st_ref[...] = vn.reshape(vn.shape[0], H_D, DH_D)


def _proj_odd(h, mod, layer, sample, nw, w, rope):
    n = h.shape[0]
    tm = ROW_TILE
    ins = [h, mod, nw, w["w_in"], w["q_w"], w["k_w"], w["gq"], w["gk"]]
    in_specs = [_rows(tm, D_MODEL), _mod_spec(layer, sample, tm), _full((1, D_MODEL)), _full((D_MODEL, ODD_IN)),
                _full((1, 512)), _full((1, 128)), _full((512, 512)), _full((128, 128))]
    widths = [512, 128, 128, 512, 512, 512]
    out_shape = [jax.ShapeDtypeStruct((n, wd), BF16) for wd in widths]
    out_specs = [_rows(tm, wd) for wd in widths]
    if sample:
        ins += [rope["ch"], rope["sh"]]
        in_specs += [_rope_spec(tm)] * 2
    else:
        for heads, dh in ((KV_C, DH_C), (KV_C, DH_C), (H_D, DH_D), (H_D, DH_D)):
            out_shape.append(jax.ShapeDtypeStruct((n, heads, dh), F32))
            out_specs.append(pl.BlockSpec((tm, heads, dh), lambda i: (i, 0, 0)))
    return pl.pallas_call(
        functools.partial(_proj_odd_kernel, sample),
        grid=(n // tm,),
        in_specs=in_specs,
        out_specs=out_specs,
        out_shape=out_shape,
        compiler_params=_cparams("parallel"),
        name="proj_odd_s" if sample else "proj_odd_p",
    )(*ins)


def _attn_odd_prompt_kernel(qc_ref, kc_ref, vc_ref, qn_ref, kn_ref, vn_ref, h_ref, wout_ref, mod_ref, nw1_ref,
                            h1_ref, s_scr, p_scr, o_ref):
    c = DH_C ** -0.5 * LOG2E
    group = H_C // KV_C
    kc = (kc_ref[...], _swap_halves(kc_ref[...]))
    vc = (vc_ref[...], _swap_halves(vc_ref[...]))
    maps = []
    for hd in range(H_C):
        half, swap = hd % 2, (hd // group) != (hd % 2)
        ps = slice(hd // 2 * LANES, (hd // 2 + 1) * LANES)
        if half == 0:
            sinks = _pair_store(o_ref, ps.start)
        maps.append((lambda ps=ps, half=half, swap=swap: [_dot_nt(_keep_half(qc_ref[:, ps], half), kc[swap])],
                     lambda swap=swap: [vc[swap]], c, None, sinks[half]))
    base = H_C * DH_C
    c = DH_D ** -0.5 * LOG2E
    for hd in range(H_D):
        half = hd % 2
        ps = slice(hd // 2 * LANES, (hd // 2 + 1) * LANES)
        if half == 0:
            sinks = _pair_store(o_ref, base + ps.start)
        maps.append((lambda ps=ps, half=half: [_dot_nt(_keep_half(qn_ref[:, ps], half), kn_ref[:, ps])],
                     lambda ps=ps: [vn_ref[:, ps]], c, None, sinks[half]))
    _attention(maps, s_scr, p_scr, s_scr.shape[0] // 2, mxu_sum="alternate")
    _out_proj_residual(o_ref, h_ref, wout_ref, mod_ref, nw1_ref, h1_ref)


def _attn_odd_prompt(p, h, w_out, mod, layer, nw1):
    qc, kc, vc, qn, kn, vn = p
    spec = lambda wd: pl.BlockSpec((SEQ, wd), lambda b: (b, 0))
    return pl.pallas_call(
        _attn_odd_prompt_kernel,
        grid=(BATCH,),
        in_specs=[spec(512), spec(128), spec(128), spec(512), spec(512), spec(512),
                  spec(D_MODEL), _full((D_MODEL, D_MODEL)),
                  pl.BlockSpec((None, None, 6, D_MODEL), lambda b: (layer, 0, 0, 0)), _full((1, D_MODEL))],
        out_specs=spec(D_MODEL),
        out_shape=jax.ShapeDtypeStruct((qc.shape[0], D_MODEL), F32),
        scratch_shapes=_attn_scratch(SEQ, SEQ, CONTEXT_GROUP) + [pltpu.VMEM((SEQ, D_MODEL), BF16)],
        compiler_params=_cparams("parallel"),
        name="attn_odd_p",
    )(qc, kc, vc, qn, kn, vn, h, w_out, mod, nw1)


def _gqa_sample_kernel(q_ref, k_ref, v_ref, ck_ref, cv_ref, o_ref, s_scr, p_scr):
    c = DH_C ** -0.5 * LOG2E
    group = H_C // KV_C
    k_loc = (k_ref[...], _swap_halves(k_ref[...]))
    v_loc = (v_ref[...], _swap_halves(v_ref[...]))
    k_ctx = ck_ref[...].astype(BF16)
    v_ctx = cv_ref[...].astype(BF16)
    k_ctx = (k_ctx, _swap_halves(k_ctx))
    v_ctx = (v_ctx, _swap_halves(v_ctx))
    maps = []
    for hd in range(H_C):
        half, swap = hd % 2, (hd // group) != (hd % 2)
        ps = slice(hd // 2 * LANES, (hd // 2 + 1) * LANES)
        if half == 0:
            sinks = _pair_store(o_ref, ps.start)

        def scores(ps=ps, half=half, swap=swap):
            q = _keep_half(q_ref[:, ps], half)
            return [_dot_nt(q, k_loc[swap]), _dot_nt(q, k_ctx[swap])]

        maps.append((scores, lambda swap=swap: [v_loc[swap], v_ctx[swap]], c, None, sinks[half]))
    _attention(maps, s_scr, p_scr, s_scr.shape[0] // 2)


def _gqa_sample(qc, kc, vc, cache_k, cache_v):
    tq, per = Q_TILE, DEC_SEQ // Q_TILE
    return pl.pallas_call(
        _gqa_sample_kernel,
        grid=(DEC_BATCH, per),
        in_specs=[pl.BlockSpec((tq, 512), lambda b, i: (b * per + i, 0)),
                  pl.BlockSpec((DEC_SEQ, 128), lambda b, i: (b, 0)),
                  pl.BlockSpec((DEC_SEQ, 128), lambda b, i: (b, 0)),
                  pl.BlockSpec((None, PAST_LEN, 128), lambda b, i: (b, 0, 0)),
                  pl.BlockSpec((None, PAST_LEN, 128), lambda b, i: (b, 0, 0))],
        out_specs=pl.BlockSpec((tq, 512), lambda b, i: (b * per + i, 0)),
        out_shape=jax.ShapeDtypeStruct((qc.shape[0], 512), BF16),
        scratch_shapes=_attn_scratch(tq, DEC_SEQ + PAST_LEN, LATENT_GROUP),
        compiler_params=_cparams("parallel", "parallel"),
        name="gqa_s",
    )(qc, kc, vc, cache_k, cache_v)


NA_ROWS = DEC_SEQ // GRID_W
NA_KR = min(NA_WIN_ROWS, NA_ROWS)
NA_LOC = NA_KR * GRID_W


NA_STEP_ROWS = 4


def _na_first_key_row(r):
    return jnp.clip(r - NA_KR // 2, 0, NA_ROWS - NA_KR)


def _na_sample_kernel(q_ref, k_ref, v_ref, ck_ref, cv_ref, bias_ref, o_ref, s_scr, p_scr):
    scale = DH_D ** -0.5
    k_ctx = ck_ref[...].astype(BF16)
    v_ctx = cv_ref[...].astype(BF16)
    rb = _softmax_block_rows(NA_LOC + PAST_LEN)
    col_ok = {}
    for r0 in range(0, GRID_W, rb):
        wq = lax.broadcasted_iota(jnp.int32, (rb, NA_LOC), 0) + r0
        wk = lax.broadcasted_iota(jnp.int32, (rb, NA_LOC), 1) % GRID_W
        cs = jnp.clip(wq - NA_WIN_COLS // 2, 0, GRID_W - NA_WIN_COLS)
        col_ok[r0] = (wk >= cs) & (wk < cs + NA_WIN_COLS)
    maps = []
    for t in range(NA_STEP_ROWS):
        r = pl.program_id(1) * NA_STEP_ROWS + t
        keys = pl.ds(pl.multiple_of(_na_first_key_row(r) * GRID_W, GRID_W), NA_LOC)
        band = _na_first_key_row(r) - r + NA_WIN_ROWS - 1
        rows = slice(t * GRID_W, (t + 1) * GRID_W)
        for hd in range(H_D):
            half = hd % 2
            ps = slice(hd // 2 * LANES, (hd // 2 + 1) * LANES)
            if half == 0:
                sinks = _pair_store(o_ref, ps.start, rows)

            def fix(s, r0, hd=hd, band=band):
                loc = s[:, :NA_LOC] * scale + bias_ref[hd, band, r0:r0 + rb, :]
                return jnp.concatenate([jnp.where(col_ok[r0], loc, NEG_INF), s[:, NA_LOC:] * scale], axis=1)

            def scores(ps=ps, half=half, rows=rows, keys=keys):
                q = _keep_half(q_ref[rows, ps], half)
                return [_dot_nt(q, k_ref[keys, ps]), _dot_nt(q, k_ctx[:, ps])]

            maps.append((scores, lambda ps=ps, keys=keys: [v_ref[keys, ps], v_ctx[:, ps]], LOG2E, fix, sinks[half]))
    _attention(maps, s_scr, p_scr, s_scr.shape[0] // 2)


def _na_sample(qn, kn, vn, cache_k, cache_v, bias):
    steps = NA_ROWS // NA_STEP_ROWS
    tq = NA_STEP_ROWS * GRID_W

    return pl.pallas_call(
        _na_sample_kernel,
        grid=(DEC_BATCH, steps),
        in_specs=[pl.BlockSpec((tq, 512), lambda b, i: (b * steps + i, 0)),
                  pl.BlockSpec((DEC_SEQ, 512), lambda b, i: (b, 0)),
                  pl.BlockSpec((DEC_SEQ, 512), lambda b, i: (b, 0)),
                  pl.BlockSpec((None, PAST_LEN, 512), lambda b, i: (b, 0, 0)),
                  pl.BlockSpec((None, PAST_LEN, 512), lambda b, i: (b, 0, 0)),
                  pl.BlockSpec((H_D, NA_WIN_ROWS, GRID_W, NA_LOC), lambda b, i: (0, 0, 0, 0),
                               pipeline_mode=pl.Buffered(1))],
        out_specs=pl.BlockSpec((tq, 512), lambda b, i: (b * steps + i, 0)),
        out_shape=jax.ShapeDtypeStruct((qn.shape[0], 512), BF16),
        scratch_shapes=_attn_scratch(GRID_W, NA_LOC + PAST_LEN, CONTEXT_GROUP),
        compiler_params=_cparams("parallel", "parallel"),
        name="na_s",
    )(qn, kn, vn, cache_k, cache_v, bias)


def _na_bias_table(rpb):
    edge = GRID_W - NA_WIN_COLS
    n_dr = 2 * NA_WIN_ROWS - 1
    v = jnp.pad(rpb.astype(F32), ((0, 0), (0, 0), (edge, edge + 1)), mode="edge")
    skew = jnp.tile(v, (1, 1, GRID_W))[:, :, :GRID_W * (2 * GRID_W - 1)]
    toep = skew.reshape(H_D, n_dr, GRID_W, 2 * GRID_W - 1)[:, :, :, GRID_W - 1:]
    flat = toep.transpose(0, 2, 1, 3).reshape(H_D, GRID_W, n_dr * GRID_W)
    return jnp.stack([flat[:, :, d0 * GRID_W:d0 * GRID_W + NA_LOC] for d0 in range(NA_WIN_ROWS)], axis=1)


def _post_attn_kernel(n_parts, *refs):
    o_refs = refs[:n_parts]
    h_ref, wout_ref, mod_ref, nw1_ref, h1_ref = refs[n_parts:]
    y = None
    off = 0
    for o_ref in o_refs:
        wd = o_ref.shape[1]
        part = _dot(o_ref[...], wout_ref[off:off + wd, :])
        y = part if y is None else y + part
        off += wd
    mod = mod_ref[...]
    h1_ref[...] = h_ref[...] + mod[2:3] * _rms(y, nw1_ref[...])


def _post_attn(o_parts, h, w_out, mod, layer, sample, nw1):
    n = h.shape[0]
    tm = ROW_TILE
    in_specs = [_rows(tm, o.shape[1]) for o in o_parts]
    in_specs += [_rows(tm, D_MODEL), _full((D_MODEL, D_MODEL)), _mod_spec(layer, sample, tm), _full((1, D_MODEL))]
    return pl.pallas_call(
        functools.partial(_post_attn_kernel, len(o_parts)),
        grid=(n // tm,),
        in_specs=in_specs,
        out_specs=_rows(tm, D_MODEL),
        out_shape=jax.ShapeDtypeStruct((n, D_MODEL), F32),
        compiler_params=_cparams("parallel"),
        name="post_attn_s" if sample else "post_attn_p",
    )(*o_parts, h, w_out, mod, nw1)


FF_PAIR = 2 * FF_CHUNK
N_FF = D_FF // FF_CHUNK
SUBLANES = 8
ACT_TILES = 4
ROW_BLOCK = 1024
NORM_ROWS = 32


def _ffn_kernel(seq_len, h_ref, wup_ref, cw_ref, cb_ref, wd_ref, mod_ref, nw2_ref, nw3_ref, o_ref,
                u_ref, z0_ref, z1_ref, a_ref, a_last_ref):
    tm = h_ref.shape[0]
    n_blocks = tm // ROW_BLOCK
    rows = ACT_TILES * SUBLANES
    mod = mod_ref[...]
    sub = lax.broadcasted_iota(jnp.int32, (SUBLANES, LANES), 0)
    zero_rows = jnp.zeros((SUBLANES, FF_PAIR), F32)
    for z_ref in (z0_ref, z1_ref):
        z_ref[0:SUBLANES, :] = zero_rows
        z_ref[SUBLANES + tm:2 * SUBLANES + tm, :] = zero_rows

    nw2 = nw2_ref[...]

    def pre_norm(blk):
        for c in range(ROW_BLOCK // NORM_ROWS):
            rs = slice(blk * ROW_BLOCK + c * NORM_ROWS, blk * ROW_BLOCK + (c + 1) * NORM_ROWS)
            u = _rms(h_ref[rs, :], nw2) * (1.0 + mod[4:5]) + mod[3:4]
            u_ref[rs, :] = u.astype(BF16)

    def up(j, z_ref, blk):
        r0 = blk * ROW_BLOCK
        u = u_ref[r0:r0 + ROW_BLOCK, :]
        rows_ = slice(SUBLANES + r0, SUBLANES + r0 + ROW_BLOCK)
        for half in range(2):
            c0 = half * D_FF + j * FF_CHUNK
            if not isinstance(c0, int):
                c0 = pl.multiple_of(c0, FF_CHUNK)
            z_ref[rows_, half * FF_CHUNK:(half + 1) * FF_CHUNK] = _dot(u, wup_ref[:, pl.ds(c0, FF_CHUNK)])

    def act(j, z_ref, col, blk, dst_ref=a_ref):
        cw = cw_ref[j]
        cb = cb_ref[j]
        for lc in range(FF_CHUNK // LANES):
            taps = []
            for lane0 in (lc * LANES, FF_CHUNK + lc * LANES):
                lanes = slice(lane0, lane0 + LANES)
                taps.append([jnp.broadcast_to(cw[k:k + 1, lanes], (SUBLANES, LANES)) for k in range(3)]
                            + [jnp.broadcast_to(cb[:, lanes], (SUBLANES, LANES))])
            for c in range(ROW_BLOCK // rows):
                r = blk * ROW_BLOCK + c * rows
                first = r % seq_len == 0
                last = (r + rows) % seq_len == 0

                def conv(lane0, tap):
                    ext = z_ref[r:r + rows + 2 * SUBLANES, lane0:lane0 + LANES]
                    tiles = [ext[t * SUBLANES:(t + 1) * SUBLANES] for t in range(ACT_TILES + 2)]
                    down = [pltpu.roll(t, 1, 0) for t in tiles[:-1]]
                    up_ = [pltpu.roll(t, SUBLANES - 1, 0) for t in tiles[1:]]
                    out = []
                    for t in range(ACT_TILES):
                        above = 0.0 if (first and t == 0) else down[t]
                        below = 0.0 if (last and t == ACT_TILES - 1) else up_[t + 1]
                        prev = jnp.where(sub == 0, above, down[t + 1])
                        nxt = jnp.where(sub == SUBLANES - 1, below, up_[t])
                        out.append(prev * tap[0] + tiles[t + 1] * tap[1] + nxt * tap[2] + tap[3])
                    return jnp.concatenate(out, axis=0)

                g = conv(lc * LANES, taps[0])
                v = conv(FF_CHUNK + lc * LANES, taps[1])
                a = (g / (1.0 + jnp.exp2(g * -LOG2E))) * v
                lane = col + lc * LANES
                if not isinstance(lane, int):
                    lane = pl.multiple_of(lane, LANES)
                dst_ref[r:r + rows, pl.ds(lane, LANES)] = a.astype(BF16)

    for blk in range(n_blocks):
        pre_norm(blk)
        up(0, z0_ref, blk)

    def pair(i, carry):
        j = 2 * i
        col = pl.multiple_of(j * FF_CHUNK, FF_CHUNK)
        for blk in range(n_blocks):
            up(j + 1, z1_ref, blk)
            act(j, z0_ref, col, blk)
        for blk in range(n_blocks):
            up(j + 2, z0_ref, blk)
            act(j + 1, z1_ref, col + FF_CHUNK, blk)
        return carry

    lax.fori_loop(0, (N_FF - 1) // 2, pair, 0)
    nw3 = nw3_ref[...]
    k_main = (N_FF - 1) * FF_CHUNK
    for blk in range(n_blocks):
        r0 = blk * ROW_BLOCK
        y = _dot(a_ref[r0:r0 + ROW_BLOCK, 0:k_main], wd_ref[0:k_main, :])
        act(N_FF - 1, z0_ref, 0, blk, a_last_ref)
        y = y + _dot(a_last_ref[r0:r0 + ROW_BLOCK, :], wd_ref[k_main:D_FF, :])
        for c in range(ROW_BLOCK // NORM_ROWS):
            rs = slice(r0 + c * NORM_ROWS, r0 + (c + 1) * NORM_ROWS)
            o_ref[rs, :] = h_ref[rs, :] + mod[5:6] * _rms(y[c * NORM_ROWS:(c + 1) * NORM_ROWS], nw3)


def _ffn(h1, w, mod, layer, sample, nw2, nw3):
    n = h1.shape[0]
    tm = FFN_ROW_TILE
    seq_len = DEC_SEQ if sample else SEQ
    once = pl.Buffered(1)
    in_specs = [
        _rows(tm, D_MODEL),
        pl.BlockSpec((None, D_MODEL, 2 * D_FF), lambda i: (layer, 0, 0), pipeline_mode=once),
        pl.BlockSpec((N_FF, 3, FF_PAIR), lambda i: (0, 0, 0), pipeline_mode=once),
        pl.BlockSpec((N_FF, 1, FF_PAIR), lambda i: (0, 0, 0), pipeline_mode=once),
        pl.BlockSpec((None, D_FF, D_MODEL), lambda i: (layer, 0, 0), pipeline_mode=once),
        _mod_spec(layer, sample, tm),
        _full((1, D_MODEL)),
        _full((1, D_MODEL)),
    ]
    return pl.pallas_call(
        functools.partial(_ffn_kernel, seq_len),
        grid=(n // tm,),
        in_specs=in_specs,
        out_specs=_rows(tm, D_MODEL),
        out_shape=jax.ShapeDtypeStruct((n, D_MODEL), F32),
        scratch_shapes=[pltpu.VMEM((tm, D_MODEL), BF16), pltpu.VMEM((tm + 2 * SUBLANES, FF_PAIR), F32),
                        pltpu.VMEM((tm + 2 * SUBLANES, FF_PAIR), F32), pltpu.VMEM((tm, D_FF - FF_CHUNK), BF16),
                        pltpu.VMEM((tm, FF_CHUNK), BF16)],
        compiler_params=pltpu.CompilerParams(dimension_semantics=("parallel",), vmem_limit_bytes=FFN_VMEM_LIMIT),
        name="ffn_s" if sample else "ffn_p",
    )(h1, w["w_up"], w["conv_w"], w["conv_b"], w["w_down"], mod, nw2, nw3)


def _pair_chunks(x):
    lead = x.shape[:-1]
    x = x.reshape(lead + (2, N_FF, FF_CHUNK))
    x = jnp.moveaxis(x, -2, 0)
    return x.reshape((N_FF,) + lead + (FF_PAIR,))


def _rope_tables():
    def table(rot_dim):
        t = np.arange(DEC_SEQ)
        n_freq = rot_dim // 4
        inv = 1.0 / (ROPE_THETA ** (np.arange(n_freq) / n_freq))
        ang = np.concatenate([(t // GRID_W)[:, None] * inv[None, :], (t % GRID_W)[:, None] * inv[None, :]], axis=-1)
        cos = np.cos(ang).astype(np.float32)
        sin = np.sin(ang).astype(np.float32)
        reps = LANES // rot_dim
        return (np.tile(np.concatenate([cos, cos], axis=-1), (1, reps)),
                np.tile(np.concatenate([-sin, sin], axis=-1), (1, reps)))
    ca, sa = table(QK_ROPE)
    ch, sh = table(HEAD_DIM)
    rope_lanes = (np.arange(MLA_SLOT) >= QK_NOPE) & (np.arange(MLA_SLOT) < QK_NOPE + QK_ROPE)
    ca = np.where(rope_lanes[None, :], ca, 1.0).astype(np.float32)
    sa = np.where(rope_lanes[None, :], sa, 0.0).astype(np.float32)
    return {"ca": jnp.asarray(ca), "sa": jnp.asarray(sa), "ch": jnp.asarray(ch), "sh": jnp.asarray(sh)}


def _group_mean_matrix(width):
    idx = np.arange(width) // HEAD_DIM
    return jnp.asarray((idx[:, None] == idx[None, :]).astype(np.float32) / HEAD_DIM, BF16)


def kernel(x_prompt, x_sample, c, cache_mla_ckv, cache_mla_kpe, cache_diff_k, cache_diff_v, cache_gqa_k, cache_gqa_v, cache_na_k, cache_na_v, c_ctx, norm_w, w_mod, b_mod, w_in_even, w_out_even, w_uq, q_norm_w, kv_norm_w, w_uk, w_uv, diff_lam, diff_subln_w, w_in_odd, w_out_odd, qk_norm_w, na_rpb, w_up, conv_w, conv_b, w_down):
    rope = _rope_tables()
    n_p = BATCH * SEQ
    n_s = DEC_BATCH * DEC_SEQ
    cvecs = jnp.concatenate([c_ctx[None, :], c, jnp.zeros((MOD_ROWS - 1 - DEC_BATCH, D_MODEL), F32)], axis=0)
    mod = _modulation(cvecs, w_mod, b_mod).reshape(DEPTH, MOD_ROWS, 6, D_MODEL)
    hp = x_prompt.reshape(n_p, D_MODEL)
    hs = x_sample.reshape(n_s, D_MODEL)
    even_states, odd_states = [], []
    w_up_b = w_up.astype(BF16)
    w_down_b = w_down.astype(BF16)
    for l in range(DEPTH):
        i = l // 2
        nw = [norm_w[l, k][None, :] for k in range(4)]
        if l % 2 == 0:
            lam_init = 0.8 - 0.6 * math.exp(-0.3 * l)
            wi = w_in_even[i]
            w_uq3 = w_uq[i].reshape(Q_LORA, H_A, QK_NOPE + QK_ROPE)
            w = {
                "w_in": jnp.concatenate([wi[:, :384], jnp.zeros((D_MODEL, QK_NOPE), F32), wi[:, 384:416],
                                         jnp.zeros((D_MODEL, MLA_SLOT - QK_NOPE - QK_ROPE), F32), wi[:, 416:]],
                                        axis=1).astype(BF16),
                "q_norm_w": q_norm_w[i][None, :],
                "kv_norm_w": kv_norm_w[i][None, :],
                "w_uq": jnp.pad(w_uq3, ((0, 0), (0, 0), (0, MLA_SLOT - QK_NOPE - QK_ROPE))
                                ).reshape(Q_LORA, H_A * MLA_SLOT).astype(BF16),
                "w_uk": jnp.pad(w_uk[i].reshape(KV_LORA, H_A, QK_NOPE), ((0, 0), (0, 0), (0, MLA_SLOT - QK_NOPE))
                                ).reshape(KV_LORA, H_A * MLA_SLOT).astype(BF16),
                "w_uv": w_uv[i].astype(BF16),
                "diff_lam": diff_lam[i],
                "diff_subln_w": diff_subln_w[i][None, :],
            }
            outs_p = _proj_even(hp, mod, l, False, nw[0], w, rope)
            outs_s = _proj_even(hs, mod, l, True, nw[0], w, rope)
            even_states.append(outs_p[6:])
            w_out = w_out_even[i].astype(BF16)
            h1p = _attn_even(outs_p[:6], False, lam_init, None, w, hp, w_out, mod, l, nw[1])
            caches = (cache_mla_ckv[:, i], cache_mla_kpe[:, i],
                      cache_diff_k[:, i].reshape(DEC_BATCH, PAST_LEN, 512),
                      cache_diff_v[:, i].reshape(DEC_BATCH, PAST_LEN, 512))
            h1s = _attn_even(outs_s, True, lam_init, caches, w, hs, w_out, mod, l, nw[1])
        else:
            q_w = jnp.tile(qk_norm_w[i, 0], H_C)[None, :]
            k_w = jnp.tile(qk_norm_w[i, 1], KV_C)[None, :]
            w = {"w_in": w_in_odd[i].astype(BF16), "q_w": q_w, "k_w": k_w,
                 "gq": _group_mean_matrix(512), "gk": _group_mean_matrix(128)}
            outs_p = _proj_odd(hp, mod, l, False, nw[0], w, rope)
            outs_s = _proj_odd(hs, mod, l, True, nw[0], w, rope)
            odd_states.append(outs_p[6:])
            w_out = w_out_odd[i].astype(BF16)
            h1p = _attn_odd_prompt(outs_p[:6], hp, w_out, mod, l, nw[1])
            qc, kc, vc, qn, kn, vn = outs_s
            o_c = _gqa_sample(qc, kc, vc, cache_gqa_k[:, i].reshape(DEC_BATCH, PAST_LEN, 128),
                              cache_gqa_v[:, i].reshape(DEC_BATCH, PAST_LEN, 128))
            o_d = _na_sample(qn, kn, vn, cache_na_k[:, i].reshape(DEC_BATCH, PAST_LEN, 512),
                             cache_na_v[:, i].reshape(DEC_BATCH, PAST_LEN, 512), _na_bias_table(na_rpb[i]))
            h1s = _post_attn([o_c, o_d], hs, w_out, mod, l, True, nw[1])
        wf = {"w_up": w_up_b, "conv_w": _pair_chunks(conv_w[l]),
              "conv_b": _pair_chunks(conv_b[l][None, :]), "w_down": w_down_b}
        hp = _ffn(h1p, wf, mod, l, False, nw[2], nw[3])
        hs = _ffn(h1s, wf, mod, l, True, nw[2], nw[3])

    def stack(states, k, shape):
        return jnp.stack([st[k].reshape((BATCH, SEQ) + shape) for st in states], axis=1)

    new_mla_ckv = stack(even_states, 0, (KV_LORA,))
    new_mla_kpe = stack(even_states, 1, (QK_ROPE,))
    new_diff_k = stack(even_states, 2, (H_B, 2 * DH_B))
    new_diff_v = stack(even_states, 3, (H_B, 2 * DH_B))
    new_gqa_k = stack(odd_states, 0, (KV_C, DH_C))
    new_gqa_v = stack(odd_states, 1, (KV_C, DH_C))
    new_na_k = stack(odd_states, 2, (H_D, DH_D))
    new_na_v = stack(odd_states, 3, (H_D, DH_D))
    return (hp.reshape(BATCH, SEQ, D_MODEL), hs.reshape(DEC_BATCH, DEC_SEQ, D_MODEL),
            new_mla_ckv, new_mla_kpe, new_diff_k, new_diff_v, new_gqa_k, new_gqa_v, new_na_k, new_na_v)
```

```python
import functools
import math

import numpy as np
import jax
import jax.numpy as jnp
from jax import lax
from jax.experimental import pallas as pl
from jax.experimental.pallas import tpu as pltpu

D_MODEL = 1024
BATCH = 32
SEQ = 256
DEPTH = 2
DEC_BATCH = 4
DEC_SEQ = 1024
PAST_LEN = 256
GRID_W = 64
HEAD_DIM = 64
H_A = 8
QK_NOPE = 64
QK_ROPE = 32
V_A = 64
Q_LORA = 256
KV_LORA = 128
H_B = 4
DH_B = HEAD_DIM
H_C = 8
KV_C = 2
DH_C = HEAD_DIM
H_D = 8
DH_D = HEAD_DIM
NA_WIN_ROWS = 8
NA_WIN_COLS = 16
D_FF = 2816
ROPE_THETA = 10000.0
EPS = 1e-6
NEG_INF = -1e30

LANES = 128
MOD_ROWS = 8
ROW_TILE = 512
FFN_ROW_TILE = 1024
FF_CHUNK = 256
Q_TILE = 512
LATENT_GROUP = 1
CONTEXT_GROUP = 4
VMEM_LIMIT = 56 * 1024 * 1024
FFN_VMEM_LIMIT = 56 * 1024 * 1024

F32 = jnp.float32
BF16 = jnp.bfloat16
LOG2E = math.log2(math.e)


def _cparams(*sem):
    return pltpu.CompilerParams(dimension_semantics=sem, vmem_limit_bytes=VMEM_LIMIT)


def _dot(a, b):
    return jnp.dot(a, b, preferred_element_type=F32)


def _dot_nt(a, b):
    return lax.dot_general(a, b, (((1,), (1,)), ((), ())), preferred_element_type=F32)


def _rms(x, w):
    return x * lax.rsqrt(jnp.mean(x * x, axis=-1, keepdims=True) + EPS) * w


def _group_rms(x, w, gmat):
    x2 = x * x
    hi = x2.astype(BF16)
    lo = (x2 - hi.astype(F32)).astype(BF16)
    ms = _dot(hi, gmat) + _dot(lo, gmat)
    return x * lax.rsqrt(ms + EPS) * w


def _rope(x, cos, sin_signed, half):
    outs = []
    for j in range(x.shape[1] // LANES):
        xc = x[:, j * LANES:(j + 1) * LANES]
        lane = lax.broadcasted_iota(jnp.int32, xc.shape, 1)
        first = (lane % (2 * half)) < half
        partner = jnp.where(first, pltpu.roll(xc, LANES - half, 1), pltpu.roll(xc, half, 1))
        outs.append(xc * cos + partner * sin_signed)
    return outs[0] if len(outs) == 1 else jnp.concatenate(outs, axis=1)


def _softmax_block_rows(n_keys):
    return max(16, min(64, (16 * 1280 // n_keys) // 16 * 16))


def _attention(maps, s_scr, p_scr, group, mxu_sum=False):
    slots = s_scr.shape[0]
    staged = {}

    def on_mxu(i):
        return bool(mxu_sum) and (mxu_sum != "alternate" or i % 2 == 1)

    def stage(i):
        s_ref = s_scr.at[i % slots]
        offs, off = [], 0
        for s in maps[i][0]():
            s_ref[:, off:off + s.shape[1]] = s
            offs.append(off)
            off += s.shape[1]
        staged[i] = (offs, off)

    def softmax(i):
        _, _, c, fix, _ = maps[i]
        n_keys = staged[i][1]
        s_ref, p_ref = s_scr.at[i % slots], p_scr.at[i % slots]
        rb = _softmax_block_rows(n_keys)
        sums = []
        for r0 in range(0, s_ref.shape[0], rb):
            s = s_ref[r0:r0 + rb, 0:n_keys]
            if fix is not None:
                s = fix(s, r0)
            m = jnp.max(s, axis=-1, keepdims=True)
            p = jnp.exp2((s - m) * c)
            if not on_mxu(i):
                sums.append(jnp.sum(p, axis=-1, keepdims=True))
            p_ref[r0:r0 + rb, 0:n_keys] = p.astype(BF16)
        return None if on_mxu(i) else jnp.concatenate(sums, axis=0)

    def weighted_values(i, den):
        _, values, _, _, sink = maps[i]
        p_ref = p_scr.at[i % slots]
        offs, n_keys = staged.pop(i)
        acc = None
        for o, v in zip(offs, values()):
            part = _dot(p_ref[:, o:o + v.shape[0]], v)
            acc = part if acc is None else acc + part
        if on_mxu(i):
            den = _dot(p_ref[:, 0:n_keys], jnp.ones((n_keys, LANES), BF16))
        sink(acc / den)

    groups = [range(g, min(g + group, len(maps))) for g in range(0, len(maps), group)]
    for i in groups[0]:
        stage(i)
    for gi, grp in enumerate(groups):
        if gi + 1 < len(groups):
            for i in groups[gi + 1]:
                stage(i)
        dens = [softmax(i) for i in grp]
        for i, den in zip(grp, dens):
            weighted_values(i, den)


def _attn_scratch(tq, n_keys, group):
    return [pltpu.VMEM((2 * group, tq, n_keys), F32), pltpu.VMEM((2 * group, tq, n_keys), BF16)]


def _upper_half(shape):
    return lax.broadcasted_iota(jnp.int32, shape, 1) >= HEAD_DIM


def _keep_half(x, half):
    upper = _upper_half(x.shape)
    return jnp.where(upper if half else ~upper, x, jnp.zeros_like(x))


def _swap_halves(x):
    return jnp.concatenate([x[:, HEAD_DIM:], x[:, :HEAD_DIM]], axis=1)


def _pair_store(o_ref, c0, rows=slice(None)):
    got = {}

    def make(half):
        def sink(o):
            got[half] = o
            if len(got) == 2:
                o_ref[rows, c0:c0 + LANES] = jnp.where(_upper_half(o.shape), got[1], got[0]).astype(BF16)
        return sink
    return make(0), make(1)


def _out_proj_residual(o_scr, h_ref, wout_ref, mod_ref, nw1_ref, h1_ref):
    y = _dot(o_scr[...], wout_ref[...])
    gate = mod_ref[2:3, :]
    nw1 = nw1_ref[...]
    for c in range(y.shape[0] // NORM_ROWS):
        rs = slice(c * NORM_ROWS, (c + 1) * NORM_ROWS)
        h1_ref[rs, :] = h_ref[rs, :] + gate * _rms(y[rs], nw1)


def _mod_kernel(c_ref, w_ref, b_ref, o_ref):
    cv = c_ref[...]
    act = cv / (1.0 + jnp.exp(-cv))
    o_ref[...] = _dot(act.astype(BF16), w_ref[...].astype(BF16)) + b_ref[...]


def _modulation(cvecs, w_mod, b_mod):
    tn = 1024
    n = 6 * D_MODEL
    return pl.pallas_call(
        _mod_kernel,
        grid=(DEPTH, n // tn),
        in_specs=[
            pl.BlockSpec((MOD_ROWS, D_MODEL), lambda l, j: (0, 0)),
            pl.BlockSpec((None, D_MODEL, tn), lambda l, j: (l, 0, j)),
            pl.BlockSpec((None, 1, tn), lambda l, j: (l, 0, j)),
        ],
        out_specs=pl.BlockSpec((None, MOD_ROWS, tn), lambda l, j: (l, 0, j)),
        out_shape=jax.ShapeDtypeStruct((DEPTH, MOD_ROWS, n), F32),
        compiler_params=_cparams("parallel", "parallel"),
        name="adaln_mod",
    )(cvecs, w_mod, b_mod.reshape(DEPTH, 1, n))


def _mod_spec(layer, sample, tm):
    if sample:
        per = DEC_SEQ // tm
        return pl.BlockSpec((None, None, 6, D_MODEL), lambda i, *_: (layer, 1 + i // per, 0, 0))
    return pl.BlockSpec((None, None, 6, D_MODEL), lambda i, *_: (layer, 0, 0, 0))


def _full(shape):
    nd = len(shape)
    return pl.BlockSpec(shape, lambda *_: (0,) * nd)


def _rows(tm, width):
    return pl.BlockSpec((tm, width), lambda i, *_: (i, 0))


def _rope_spec(tm):
    per = DEC_SEQ // tm
    return pl.BlockSpec((tm, LANES), lambda i, *_: (i % per, 0))


EVEN_IN = 2048
MLA_SLOT = 128


def _proj_even_kernel(sample, *refs):
    if sample:
        (h_ref, mod_ref, nw_ref, win_ref, qnw_ref, kvnw_ref, wuq_ref, wuk_ref, wuv_ref,
         ca_ref, sa_ref, ch_ref, sh_ref,
         qa_ref, ka_ref, va_ref, qd_ref, kd_ref, vd_ref) = refs
    else:
        (h_ref, mod_ref, nw_ref, win_ref, qnw_ref, kvnw_ref, wuq_ref, wuk_ref, wuv_ref,
         qa_ref, ka_ref, va_ref, qd_ref, kd_ref, vd_ref,
         ckv_st_ref, kpe_st_ref, kd_st_ref, vd_st_ref) = refs
    mod = mod_ref[...]
    u = _rms(h_ref[...], nw_ref[...]) * (1.0 + mod[1:2]) + mod[0:1]
    z = _dot(u.astype(BF16), win_ref[...])
    cq = _rms(z[:, 0:256], qnw_ref[...]).astype(BF16)
    qa = _dot(cq, wuq_ref[...])
    ckv = _rms(z[:, 256:384], kvnw_ref[...])
    ckv_b = ckv.astype(BF16)
    kn = _dot(ckv_b, wuk_ref[...])
    va_ref[...] = _dot(ckv_b, wuv_ref[...]).astype(BF16)
    kpe_slot = z[:, 384:512]
    qd = z[:, 512:1024]
    kd = z[:, 1024:1536]
    vd = z[:, 1536:2048]
    vd_ref[...] = vd.astype(BF16)
    if sample:
        ca, sa, ch, sh = ca_ref[...], sa_ref[...], ch_ref[...], sh_ref[...]
        qa = _rope(qa, ca, sa, QK_ROPE // 2)
        kpe_rot = _rope(kpe_slot, ca, sa, QK_ROPE // 2)
        qd_ref[...] = _rope(qd, ch, sh, DH_B // 2).astype(BF16)
        kd_ref[...] = _rope(kd, ch, sh, DH_B // 2).astype(BF16)
    else:
        kpe_rot = kpe_slot
        qd_ref[...] = qd.astype(BF16)
        kd_ref[...] = kd.astype(BF16)
        ckv_st_ref[...] = ckv
        kpe_st_ref[...] = kpe_slot[:, QK_NOPE:QK_NOPE + QK_ROPE]
        kd_st_ref[...] = kd.reshape(kd.shape[0], H_B, 2 * DH_B)
        vd_st_ref[...] = vd.reshape(vd.shape[0], H_B, 2 * DH_B)
    qa_ref[...] = qa.astype(BF16)
    for hd in range(H_A):
        sl = slice(hd * MLA_SLOT, (hd + 1) * MLA_SLOT)
        ka_ref[:, sl] = (kn[:, sl] + kpe_rot).astype(BF16)


def _proj_even(h, mod, layer, sample, nw, w, rope):
    n = h.shape[0]
    tm = ROW_TILE
    wide = H_A * MLA_SLOT
    ins = [h, mod, nw, w["w_in"], w["q_norm_w"], w["kv_norm_w"], w["w_uq"], w["w_uk"], w["w_uv"]]
    in_specs = [_rows(tm, D_MODEL), _mod_spec(layer, sample, tm), _full((1, D_MODEL)), _full((D_MODEL, EVEN_IN)),
                _full((1, Q_LORA)), _full((1, KV_LORA)), _full((Q_LORA, wide)), _full((KV_LORA, wide)),
                _full((KV_LORA, 512))]
    widths = [wide, wide, 512, 512, 512, 512]
    out_shape = [jax.ShapeDtypeStruct((n, wd), BF16) for wd in widths]
    out_specs = [_rows(tm, wd) for wd in widths]
    if sample:
        ins += [rope["ca"], rope["sa"], rope["ch"], rope["sh"]]
        in_specs += [_rope_spec(tm)] * 4
    else:
        for wd in (KV_LORA, QK_ROPE):
            out_shape.append(jax.ShapeDtypeStruct((n, wd), F32))
            out_specs.append(_rows(tm, wd))
        for _ in range(2):
            out_shape.append(jax.ShapeDtypeStruct((n, H_B, 2 * DH_B), F32))
            out_specs.append(pl.BlockSpec((tm, H_B, 2 * DH_B), lambda i: (i, 0, 0)))
    return pl.pallas_call(
        functools.partial(_proj_even_kernel, sample),
        grid=(n // tm,),
        in_specs=in_specs,
        out_specs=out_specs,
        out_shape=out_shape,
        compiler_params=_cparams("parallel"),
        name="proj_even_s" if sample else "proj_even_p",
    )(*ins)


def _attn_even_kernel(sample, lam_init, *refs):
    if sample:
        (qa_ref, ka_ref, va_ref, qd_ref, kd_ref, vd_ref,
         cckv_ref, ckpe_ref, cdk_ref, cdv_ref, wuk_ref, wuv_ref, lam_ref, sub_ref,
         h_ref, wout_ref, mod_ref, nw1_ref, h1_ref, s_scr, p_scr, o_ref) = refs
    else:
        (qa_ref, ka_ref, va_ref, qd_ref, kd_ref, vd_ref, lam_ref, sub_ref,
         h_ref, wout_ref, mod_ref, nw1_ref, h1_ref, s_scr, p_scr, o_ref) = refs
    lf = lam_ref[...]
    lam = (jnp.exp(jnp.sum(lf[0:1] * lf[1:2], axis=-1, keepdims=True))
           - jnp.exp(jnp.sum(lf[2:3] * lf[3:4], axis=-1, keepdims=True)) + lam_init)
    if sample:
        cckv = cckv_ref[...].astype(BF16)
        kn_ctx = _dot(cckv, wuk_ref[...])
        va_ctx = _dot(cckv, wuv_ref[...]).astype(BF16)
        n_ctx = cckv.shape[0]
        kpe_ctx = jnp.concatenate([jnp.zeros((n_ctx, QK_NOPE), F32), ckpe_ref[...],
                                   jnp.zeros((n_ctx, MLA_SLOT - QK_NOPE - QK_ROPE), F32)], axis=1)
        kd_ctx = cdk_ref[...].astype(BF16)
        vd_ctx = cdv_ref[...].astype(BF16)
    maps = []
    c_a = (QK_NOPE + QK_ROPE) ** -0.5 * LOG2E
    for hd in range(H_A):
        sl = slice(hd * MLA_SLOT, (hd + 1) * MLA_SLOT)
        if hd % 2 == 0:
            sinks = _pair_store(o_ref, hd * V_A)

        def scores(sl=sl):
            q = qa_ref[:, sl]
            out = [_dot_nt(q, ka_ref[:, sl])]
            if sample:
                out.append(_dot_nt(q, (kn_ctx[:, sl] + kpe_ctx).astype(BF16)))
            return out

        def values(vs=slice(hd // 2 * LANES, (hd // 2 + 1) * LANES)):
            return [va_ref[:, vs]] + ([va_ctx[:, vs]] if sample else [])

        maps.append((scores, values, c_a, None, sinks[hd % 2]))
    c_b = DH_B ** -0.5 * LOG2E
    base = H_A * V_A
    sub_w = sub_ref[...]
    for hd in range(H_B):
        hs = slice(hd * 2 * DH_B, (hd + 1) * 2 * DH_B)
        outs = []

        def sink(o, outs=outs, hs=hs):
            outs.append(o)
            if len(outs) == 2:
                ob = _rms(outs[0] - lam * outs[1], sub_w) * (1.0 - lam_init)
                o_ref[:, base + hs.start:base + hs.stop] = ob.astype(BF16)

        def values(hs=hs):
            return [vd_ref[:, hs]] + ([vd_ctx[:, hs]] if sample else [])

        for comp in range(2):
            def scores(hs=hs, comp=comp):
                q = _keep_half(qd_ref[:, hs], comp)
                out = [_dot_nt(q, kd_ref[:, hs])]
                if sample:
                    out.append(_dot_nt(q, kd_ctx[:, hs]))
                return out

            maps.append((scores, values, c_b, None, sink))
    _attention(maps, s_scr, p_scr, s_scr.shape[0] // 2, mxu_sum=not sample)
    _out_proj_residual(o_ref, h_ref, wout_ref, mod_ref, nw1_ref, h1_ref)


def _attn_even(p, sample, lam_init, caches, w, h, w_out, mod, layer, nw1):
    qa, ka, va, qd, kd, vd = p
    n = qa.shape[0]
    wide = H_A * MLA_SLOT
    if sample:
        tq, per = Q_TILE, DEC_SEQ // Q_TILE
        grid = (DEC_BATCH, per)
        qspec = lambda wd: pl.BlockSpec((tq, wd), lambda b, i: (b * per + i, 0))
        kspec = lambda wd: pl.BlockSpec((DEC_SEQ, wd), lambda b, i: (b, 0))
        cspec = lambda wd: pl.BlockSpec((None, PAST_LEN, wd), lambda b, i: (b, 0, 0))
        ins = [qa, ka, va, qd, kd, vd, *caches, w["w_uk"], w["w_uv"], w["diff_lam"], w["diff_subln_w"]]
        in_specs = [qspec(wide), kspec(wide), kspec(512), qspec(512), kspec(512), kspec(512),
                    cspec(KV_LORA), cspec(QK_ROPE), cspec(512), cspec(512),
                    _full((KV_LORA, wide)), _full((KV_LORA, 512)), _full((4, DH_B)), _full((1, 2 * DH_B))]
        out_spec = qspec(D_MODEL)
        mod_spec = pl.BlockSpec((None, None, 6, D_MODEL), lambda b, i: (layer, 1 + b, 0, 0))
        sem = ("parallel", "parallel")
        n_keys, group = DEC_SEQ + PAST_LEN, LATENT_GROUP
    else:
        tq = SEQ
        grid = (BATCH,)
        spec = lambda wd: pl.BlockSpec((SEQ, wd), lambda b: (b, 0))
        ins = [qa, ka, va, qd, kd, vd, w["diff_lam"], w["diff_subln_w"]]
        in_specs = [spec(wide), spec(wide), spec(512), spec(512), spec(512), spec(512),
                    _full((4, DH_B)), _full((1, 2 * DH_B))]
        out_spec = spec(D_MODEL)
        mod_spec = pl.BlockSpec((None, None, 6, D_MODEL), lambda b: (layer, 0, 0, 0))
        sem = ("parallel",)
        n_keys, group = SEQ, CONTEXT_GROUP
    ins += [h, w_out, mod, nw1]
    in_specs += [out_spec, _full((D_MODEL, D_MODEL)), mod_spec, _full((1, D_MODEL))]
    return pl.pallas_call(
        functools.partial(_attn_even_kernel, sample, lam_init),
        grid=grid,
        in_specs=in_specs,
        out_specs=out_spec,
        out_shape=jax.ShapeDtypeStruct((n, D_MODEL), F32),
        scratch_shapes=_attn_scratch(tq, n_keys, group) + [pltpu.VMEM((tq, D_MODEL), BF16)],
        compiler_params=_cparams(*sem),
        name="attn_even_s" if sample else "attn_even_p",
    )(*ins)


ODD_IN = 2304


def _proj_odd_kernel(sample, *refs):
    if sample:
        (h_ref, mod_ref, nw_ref, win_ref, qw_ref, kw_ref, gq_ref, gk_ref, ch_ref, sh_ref,
         qc_ref, kc_ref, vc_ref, qn_ref, kn_ref, vn_ref) = refs
    else:
        (h_ref, mod_ref, nw_ref, win_ref, qw_ref, kw_ref, gq_ref, gk_ref,
         qc_ref, kc_ref, vc_ref, qn_ref, kn_ref, vn_ref,
         kc_st_ref, vc_st_ref, kn_st_ref, vn_st_ref) = refs
    mod = mod_ref[...]
    u = _rms(h_ref[...], nw_ref[...]) * (1.0 + mod[1:2]) + mod[0:1]
    z = _dot(u.astype(BF16), win_ref[...])
    qc = _group_rms(z[:, 0:512], qw_ref[...], gq_ref[...])
    kc = _group_rms(z[:, 512:640], kw_ref[...], gk_ref[...])
    vc = z[:, 640:768]
    kn = z[:, 1280:1792]
    vn = z[:, 1792:2304]
    vc_ref[...] = vc.astype(BF16)
    qn_ref[...] = z[:, 768:1280].astype(BF16)
    kn_ref[...] = kn.astype(BF16)
    vn_ref[...] = vn.astype(BF16)
    if sample:
        ch, sh = ch_ref[...], sh_ref[...]
        qc_ref[...] = _rope(qc, ch, sh, DH_C // 2).astype(BF16)
        kc_ref[...] = _rope(kc, ch, sh, DH_C // 2).astype(BF16)
    else:
        qc_ref[...] = qc.astype(BF16)
        kc_ref[...] = kc.astype(BF16)
        kc_st_ref[...] = kc.reshape(kc.shape[0], KV_C, DH_C)
        vc_st_ref[...] = vc.reshape(vc.shape[0], KV_C, DH_C)
        kn_st_ref[...] = kn.reshape(kn.shape[0], H_D, DH_D)
        vn_st_ref[...] = vn.reshape(vn.shape[0], H_D, DH_D)


def _proj_odd(h, mod, layer, sample, nw, w, rope):
    n = h.shape[0]
    tm = ROW_TILE
    ins = [h, mod, nw, w["w_in"], w["q_w"], w["k_w"], w["gq"], w["gk"]]
    in_specs = [_rows(tm, D_MODEL), _mod_spec(layer, sample, tm), _full((1, D_MODEL)), _full((D_MODEL, ODD_IN)),
                _full((1, 512)), _full((1, 128)), _full((512, 512)), _full((128, 128))]
    widths = [512, 128, 128, 512, 512, 512]
    out_shape = [jax.ShapeDtypeStruct((n, wd), BF16) for wd in widths]
    out_specs = [_rows(tm, wd) for wd in widths]
    if sample:
        ins += [rope["ch"], rope["sh"]]
        in_specs += [_rope_spec(tm)] * 2
    else:
        for heads, dh in ((KV_C, DH_C), (KV_C, DH_C), (H_D, DH_D), (H_D, DH_D)):
            out_shape.append(jax.ShapeDtypeStruct((n, heads, dh), F32))
            out_specs.append(pl.BlockSpec((tm, heads, dh), lambda i: (i, 0, 0)))
    return pl.pallas_call(
        functools.partial(_proj_odd_kernel, sample),
        grid=(n // tm,),
        in_specs=in_specs,
        out_specs=out_specs,
        out_shape=out_shape,
        compiler_params=_cparams("parallel"),
        name="proj_odd_s" if sample else "proj_odd_p",
    )(*ins)


def _attn_odd_prompt_kernel(qc_ref, kc_ref, vc_ref, qn_ref, kn_ref, vn_ref, h_ref, wout_ref, mod_ref, nw1_ref,
                            h1_ref, s_scr, p_scr, o_ref):
    c = DH_C ** -0.5 * LOG2E
    group = H_C // KV_C
    kc = (kc_ref[...], _swap_halves(kc_ref[...]))
    vc = (vc_ref[...], _swap_halves(vc_ref[...]))
    maps = []
    for hd in range(H_C):
        half, swap = hd % 2, (hd // group) != (hd % 2)
        ps = slice(hd // 2 * LANES, (hd // 2 + 1) * LANES)
        if half == 0:
            sinks = _pair_store(o_ref, ps.start)
        maps.append((lambda ps=ps, half=half, swap=swap: [_dot_nt(_keep_half(qc_ref[:, ps], half), kc[swap])],
                     lambda swap=swap: [vc[swap]], c, None, sinks[half]))
    base = H_C * DH_C
    c = DH_D ** -0.5 * LOG2E
    for hd in range(H_D):
        half = hd % 2
        ps = slice(hd // 2 * LANES, (hd // 2 + 1) * LANES)
        if half == 0:
            sinks = _pair_store(o_ref, base + ps.start)
        maps.append((lambda ps=ps, half=half: [_dot_nt(_keep_half(qn_ref[:, ps], half), kn_ref[:, ps])],
                     lambda ps=ps: [vn_ref[:, ps]], c, None, sinks[half]))
    _attention(maps, s_scr, p_scr, s_scr.shape[0] // 2, mxu_sum="alternate")
    _out_proj_residual(o_ref, h_ref, wout_ref, mod_ref, nw1_ref, h1_ref)


def _attn_odd_prompt(p, h, w_out, mod, layer, nw1):
    qc, kc, vc, qn, kn, vn = p
    spec = lambda wd: pl.BlockSpec((SEQ, wd), lambda b: (b, 0))
    return pl.pallas_call(
        _attn_odd_prompt_kernel,
        grid=(BATCH,),
        in_specs=[spec(512), spec(128), spec(128), spec(512), spec(512), spec(512),
                  spec(D_MODEL), _full((D_MODEL, D_MODEL)),
                  pl.BlockSpec((None, None, 6, D_MODEL), lambda b: (layer, 0, 0, 0)), _full((1, D_MODEL))],
        out_specs=spec(D_MODEL),
        out_shape=jax.ShapeDtypeStruct((qc.shape[0], D_MODEL), F32),
        scratch_shapes=_attn_scratch(SEQ, SEQ, CONTEXT_GROUP) + [pltpu.VMEM((SEQ, D_MODEL), BF16)],
        compiler_params=_cparams("parallel"),
        name="attn_odd_p",
    )(qc, kc, vc, qn, kn, vn, h, w_out, mod, nw1)


def _gqa_sample_kernel(q_ref, k_ref, v_ref, ck_ref, cv_ref, o_ref, s_scr, p_scr):
    c = DH_C ** -0.5 * LOG2E
    group = H_C // KV_C
    k_loc = (k_ref[...], _swap_halves(k_ref[...]))
    v_loc = (v_ref[...], _swap_halves(v_ref[...]))
    k_ctx = ck_ref[...].astype(BF16)
    v_ctx = cv_ref[...].astype(BF16)
    k_ctx = (k_ctx, _swap_halves(k_ctx))
    v_ctx = (v_ctx, _swap_halves(v_ctx))
    maps = []
    for hd in range(H_C):
        half, swap = hd % 2, (hd // group) != (hd % 2)
        ps = slice(hd // 2 * LANES, (hd // 2 + 1) * LANES)
        if half == 0:
            sinks = _pair_store(o_ref, ps.start)

        def scores(ps=ps, half=half, swap=swap):
            q = _keep_half(q_ref[:, ps], half)
            return [_dot_nt(q, k_loc[swap]), _dot_nt(q, k_ctx[swap])]

        maps.append((scores, lambda swap=swap: [v_loc[swap], v_ctx[swap]], c, None, sinks[half]))
    _attention(maps, s_scr, p_scr, s_scr.shape[0] // 2)


def _gqa_sample(qc, kc, vc, cache_k, cache_v):
    tq, per = Q_TILE, DEC_SEQ // Q_TILE
    return pl.pallas_call(
        _gqa_sample_kernel,
        grid=(DEC_BATCH, per),
        in_specs=[pl.BlockSpec((tq, 512), lambda b, i: (b * per + i, 0)),
                  pl.BlockSpec((DEC_SEQ, 128), lambda b, i: (b, 0)),
                  pl.BlockSpec((DEC_SEQ, 128), lambda b, i: (b, 0)),
                  pl.BlockSpec((None, PAST_LEN, 128), lambda b, i: (b, 0, 0)),
                  pl.BlockSpec((None, PAST_LEN, 128), lambda b, i: (b, 0, 0))],
        out_specs=pl.BlockSpec((tq, 512), lambda b, i: (b * per + i, 0)),
        out_shape=jax.ShapeDtypeStruct((qc.shape[0], 512), BF16),
        scratch_shapes=_attn_scratch(tq, DEC_SEQ + PAST_LEN, LATENT_GROUP),
        compiler_params=_cparams("parallel", "parallel"),
        name="gqa_s",
    )(qc, kc, vc, cache_k, cache_v)


NA_ROWS = DEC_SEQ // GRID_W
NA_KR = min(NA_WIN_ROWS, NA_ROWS)
NA_LOC = NA_KR * GRID_W


NA_STEP_ROWS = 8


def _na_first_key_row(r):
    return jnp.clip(r - NA_KR // 2, 0, NA_ROWS - NA_KR)


def _na_sample_kernel(q_ref, k_ref, v_ref, ck_ref, cv_ref, bias_ref, o_ref, s_scr, p_scr):
    scale = DH_D ** -0.5
    k_ctx = ck_ref[...].astype(BF16)
    v_ctx = cv_ref[...].astype(BF16)
    rb = _softmax_block_rows(NA_LOC + PAST_LEN)
    col_ok = {}
    for r0 in range(0, GRID_W, rb):
        wq = lax.broadcasted_iota(jnp.int32, (rb, NA_LOC), 0) + r0
        wk = lax.broadcasted_iota(jnp.int32, (rb, NA_LOC), 1) % GRID_W
        cs = jnp.clip(wq - NA_WIN_COLS // 2, 0, GRID_W - NA_WIN_COLS)
        col_ok[r0] = (wk >= cs) & (wk < cs + NA_WIN_COLS)
    maps = []
    for t in range(NA_STEP_ROWS):
        r = pl.program_id(1) * NA_STEP_ROWS + t
        keys = pl.ds(pl.multiple_of(_na_first_key_row(r) * GRID_W, GRID_W), NA_LOC)
        band = _na_first_key_row(r) - r + NA_WIN_ROWS - 1
        rows = slice(t * GRID_W, (t + 1) * GRID_W)
        for hd in range(H_D):
            half = hd % 2
            ps = slice(hd // 2 * LANES, (hd // 2 + 1) * LANES)
            if half == 0:
                sinks = _pair_store(o_ref, ps.start, rows)

            def fix(s, r0, hd=hd, band=band):
                loc = s[:, :NA_LOC] * scale + bias_ref[hd, band, r0:r0 + rb, :]
                return jnp.concatenate([jnp.where(col_ok[r0], loc, NEG_INF), s[:, NA_LOC:] * scale], axis=1)

            def scores(ps=ps, half=half, rows=rows, keys=keys):
                q = _keep_half(q_ref[rows, ps], half)
                return [_dot_nt(q, k_ref[keys, ps]), _dot_nt(q, k_ctx[:, ps])]

            maps.append((scores, lambda ps=ps, keys=keys: [v_ref[keys, ps], v_ctx[:, ps]], LOG2E, fix, sinks[half]))
    _attention(maps, s_scr, p_scr, s_scr.shape[0] // 2)


def _na_sample(qn, kn, vn, cache_k, cache_v, bias):
    steps = NA_ROWS // NA_STEP_ROWS
    tq = NA_STEP_ROWS * GRID_W

    return pl.pallas_call(
        _na_sample_kernel,
        grid=(DEC_BATCH, steps),
        in_specs=[pl.BlockSpec((tq, 512), lambda b, i: (b * steps + i, 0)),
                  pl.BlockSpec((DEC_SEQ, 512), lambda b, i: (b, 0)),
                  pl.BlockSpec((DEC_SEQ, 512), lambda b, i: (b, 0)),
                  pl.BlockSpec((None, PAST_LEN, 512), lambda b, i: (b, 0, 0)),
                  pl.BlockSpec((None, PAST_LEN, 512), lambda b, i: (b, 0, 0)),
                  pl.BlockSpec((H_D, NA_WIN_ROWS, GRID_W, NA_LOC), lambda b, i: (0, 0, 0, 0),
                               pipeline_mode=pl.Buffered(1))],
        out_specs=pl.BlockSpec((tq, 512), lambda b, i: (b * steps + i, 0)),
        out_shape=jax.ShapeDtypeStruct((qn.shape[0], 512), BF16),
        scratch_shapes=_attn_scratch(GRID_W, NA_LOC + PAST_LEN, CONTEXT_GROUP),
        compiler_params=_cparams("parallel", "parallel"),
        name="na_s",
    )(qn, kn, vn, cache_k, cache_v, bias)


def _na_bias_table(rpb):
    edge = GRID_W - NA_WIN_COLS
    n_dr = 2 * NA_WIN_ROWS - 1
    v = jnp.pad(rpb.astype(F32), ((0, 0), (0, 0), (edge, edge + 1)), mode="edge")
    skew = jnp.tile(v, (1, 1, GRID_W))[:, :, :GRID_W * (2 * GRID_W - 1)]
    toep = skew.reshape(H_D, n_dr, GRID_W, 2 * GRID_W - 1)[:, :, :, GRID_W - 1:]
    flat = toep.transpose(0, 2, 1, 3).reshape(H_D, GRID_W, n_dr * GRID_W)
    return jnp.stack([flat[:, :, d0 * GRID_W:d0 * GRID_W + NA_LOC] for d0 in range(NA_WIN_ROWS)], axis=1)


def _post_attn_kernel(n_parts, *refs):
    o_refs = refs[:n_parts]
    h_ref, wout_ref, mod_ref, nw1_ref, h1_ref = refs[n_parts:]
    y = None
    off = 0
    for o_ref in o_refs:
        wd = o_ref.shape[1]
        part = _dot(o_ref[...], wout_ref[off:off + wd, :])
        y = part if y is None else y + part
        off += wd
    mod = mod_ref[...]
    h1_ref[...] = h_ref[...] + mod[2:3] * _rms(y, nw1_ref[...])


def _post_attn(o_parts, h, w_out, mod, layer, sample, nw1):
    n = h.shape[0]
    tm = ROW_TILE
    in_specs = [_rows(tm, o.shape[1]) for o in o_parts]
    in_specs += [_rows(tm, D_MODEL), _full((D_MODEL, D_MODEL)), _mod_spec(layer, sample, tm), _full((1, D_MODEL))]
    return pl.pallas_call(
        functools.partial(_post_attn_kernel, len(o_parts)),
        grid=(n // tm,),
        in_specs=in_specs,
        out_specs=_rows(tm, D_MODEL),
        out_shape=jax.ShapeDtypeStruct((n, D_MODEL), F32),
        compiler_params=_cparams("parallel"),
        name="post_attn_s" if sample else "post_attn_p",
    )(*o_parts, h, w_out, mod, nw1)


FF_PAIR = 2 * FF_CHUNK
N_FF = D_FF // FF_CHUNK
SUBLANES = 8
ACT_TILES = 4
ROW_BLOCK = 1024
NORM_ROWS = 32


def _ffn_kernel(seq_len, h_ref, wup_ref, cw_ref, cb_ref, wd_ref, mod_ref, nw2_ref, nw3_ref, o_ref,
                u_ref, z0_ref, z1_ref, a_ref, a_last_ref):
    tm = h_ref.shape[0]
    n_blocks = tm // ROW_BLOCK
    rows = ACT_TILES * SUBLANES
    mod = mod_ref[...]
    sub = lax.broadcasted_iota(jnp.int32, (SUBLANES, LANES), 0)
    zero_rows = jnp.zeros((SUBLANES, FF_PAIR), F32)
    for z_ref in (z0_ref, z1_ref):
        z_ref[0:SUBLANES, :] = zero_rows
        z_ref[SUBLANES + tm:2 * SUBLANES + tm, :] = zero_rows

    nw2 = nw2_ref[...]

    def pre_norm(blk):
        for c in range(ROW_BLOCK // NORM_ROWS):
            rs = slice(blk * ROW_BLOCK + c * NORM_ROWS, blk * ROW_BLOCK + (c + 1) * NORM_ROWS)
            u = _rms(h_ref[rs, :], nw2) * (1.0 + mod[4:5]) + mod[3:4]
            u_ref[rs, :] = u.astype(BF16)

    def up(j, z_ref, blk):
        r0 = blk * ROW_BLOCK
        u = u_ref[r0:r0 + ROW_BLOCK, :]
        rows_ = slice(SUBLANES + r0, SUBLANES + r0 + ROW_BLOCK)
        for half in range(2):
            c0 = half * D_FF + j * FF_CHUNK
            if not isinstance(c0, int):
                c0 = pl.multiple_of(c0, FF_CHUNK)
            z_ref[rows_, half * FF_CHUNK:(half + 1) * FF_CHUNK] = _dot(u, wup_ref[:, pl.ds(c0, FF_CHUNK)])

    def act(j, z_ref, col, blk, dst_ref=a_ref):
        cw = cw_ref[j]
        cb = cb_ref[j]
        for lc in range(FF_CHUNK // LANES):
            taps = []
            for lane0 in (lc * LANES, FF_CHUNK + lc * LANES):
                lanes = slice(lane0, lane0 + LANES)
                taps.append([jnp.broadcast_to(cw[k:k + 1, lanes], (SUBLANES, LANES)) for k in range(3)]
                            + [jnp.broadcast_to(cb[:, lanes], (SUBLANES, LANES))])
            for c in range(ROW_BLOCK // rows):
                r = blk * ROW_BLOCK + c * rows
                first = r % seq_len == 0
                last = (r + rows) % seq_len == 0

                def conv(lane0, tap):
                    ext = z_ref[r:r + rows + 2 * SUBLANES, lane0:lane0 + LANES]
                    tiles = [ext[t * SUBLANES:(t + 1) * SUBLANES] for t in range(ACT_TILES + 2)]
                    down = [pltpu.roll(t, 1, 0) for t in tiles[:-1]]
                    up_ = [pltpu.roll(t, SUBLANES - 1, 0) for t in tiles[1:]]
                    out = []
                    for t in range(ACT_TILES):
                        above = 0.0 if (first and t == 0) else down[t]
                        below = 0.0 if (last and t == ACT_TILES - 1) else up_[t + 1]
                        prev = jnp.where(sub == 0, above, down[t + 1])
                        nxt = jnp.where(sub == SUBLANES - 1, below, up_[t])
                        out.append(prev * tap[0] + tiles[t + 1] * tap[1] + nxt * tap[2] + tap[3])
                    return jnp.concatenate(out, axis=0)

                g = conv(lc * LANES, taps[0])
                v = conv(FF_CHUNK + lc * LANES, taps[1])
                a = (g / (1.0 + jnp.exp2(g * -LOG2E))) * v
                lane = col + lc * LANES
                if not isinstance(lane, int):
                    lane = pl.multiple_of(lane, LANES)
                dst_ref[r:r + rows, pl.ds(lane, LANES)] = a.astype(BF16)

    for blk in range(n_blocks):
        pre_norm(blk)
        up(0, z0_ref, blk)

    def pair(i, carry):
        j = 2 * i
        col = pl.multiple_of(j * FF_CHUNK, FF_CHUNK)
        for blk in range(n_blocks):
            up(j + 1, z1_ref, blk)
            act(j, z0_ref, col, blk)
        for blk in range(n_blocks):
            up(j + 2, z0_ref, blk)
            act(j + 1, z1_ref, col + FF_CHUNK, blk)
        return carry

    lax.fori_loop(0, (N_FF - 1) // 2, pair, 0)
    nw3 = nw3_ref[...]
    k_main = (N_FF - 1) * FF_CHUNK
    for blk in range(n_blocks):
        r0 = blk * ROW_BLOCK
        y = _dot(a_ref[r0:r0 + ROW_BLOCK, 0:k_main], wd_ref[0:k_main, :])
        act(N_FF - 1, z0_ref, 0, blk, a_last_ref)
        y = y + _dot(a_last_ref[r0:r0 + ROW_BLOCK, :], wd_ref[k_main:D_FF, :])
        for c in range(ROW_BLOCK // NORM_ROWS):
            rs = slice(r0 + c * NORM_ROWS, r0 + (c + 1) * NORM_ROWS)
            o_ref[rs, :] = h_ref[rs, :] + mod[5:6] * _rms(y[c * NORM_ROWS:(c + 1) * NORM_ROWS], nw3)


def _ffn(h1, w, mod, layer, sample, nw2, nw3):
    n = h1.shape[0]
    tm = FFN_ROW_TILE
    seq_len = DEC_SEQ if sample else SEQ
    once = pl.Buffered(1)
    in_specs = [
        _rows(tm, D_MODEL),
        pl.BlockSpec((None, D_MODEL, 2 * D_FF), lambda i: (layer, 0, 0), pipeline_mode=once),
        pl.BlockSpec((N_FF, 3, FF_PAIR), lambda i: (0, 0, 0), pipeline_mode=once),
        pl.BlockSpec((N_FF, 1, FF_PAIR), lambda i: (0, 0, 0), pipeline_mode=once),
        pl.BlockSpec((None, D_FF, D_MODEL), lambda i: (layer, 0, 0), pipeline_mode=once),
        _mod_spec(layer, sample, tm),
        _full((1, D_MODEL)),
        _full((1, D_MODEL)),
    ]
    return pl.pallas_call(
        functools.partial(_ffn_kernel, seq_len),
        grid=(n // tm,),
        in_specs=in_specs,
        out_specs=_rows(tm, D_MODEL),
        out_shape=jax.ShapeDtypeStruct((n, D_MODEL), F32),
        scratch_shapes=[pltpu.VMEM((tm, D_MODEL), BF16), pltpu.VMEM((tm + 2 * SUBLANES, FF_PAIR), F32),
                        pltpu.VMEM((tm + 2 * SUBLANES, FF_PAIR), F32), pltpu.VMEM((tm, D_FF - FF_CHUNK), BF16),
                        pltpu.VMEM((tm, FF_CHUNK), BF16)],
        compiler_params=pltpu.CompilerParams(dimension_semantics=("parallel",), vmem_limit_bytes=FFN_VMEM_LIMIT),
        name="ffn_s" if sample else "ffn_p",
    )(h1, w["w_up"], w["conv_w"], w["conv_b"], w["w_down"], mod, nw2, nw3)


def _pair_chunks(x):
    lead = x.shape[:-1]
    x = x.reshape(lead + (2, N_FF, FF_CHUNK))
    x = jnp.moveaxis(x, -2, 0)
    return x.reshape((N_FF,) + lead + (FF_PAIR,))


def _rope_tables():
    def table(rot_dim):
        t = np.arange(DEC_SEQ)
        n_freq = rot_dim // 4
        inv = 1.0 / (ROPE_THETA ** (np.arange(n_freq) / n_freq))
        ang = np.concatenate([(t // GRID_W)[:, None] * inv[None, :], (t % GRID_W)[:, None] * inv[None, :]], axis=-1)
        cos = np.cos(ang).astype(np.float32)
        sin = np.sin(ang).astype(np.float32)
        reps = LANES // rot_dim
        return (np.tile(np.concatenate([cos, cos], axis=-1), (1, reps)),
                np.tile(np.concatenate([-sin, sin], axis=-1), (1, reps)))
    ca, sa = table(QK_ROPE)
    ch, sh = table(HEAD_DIM)
    rope_lanes = (np.arange(MLA_SLOT) >= QK_NOPE) & (np.arange(MLA_SLOT) < QK_NOPE + QK_ROPE)
    ca = np.where(rope_lanes[None, :], ca, 1.0).astype(np.float32)
    sa = np.where(rope_lanes[None, :], sa, 0.0).astype(np.float32)
    return {"ca": jnp.asarray(ca), "sa": jnp.asarray(sa), "ch": jnp.asarray(ch), "sh": jnp.asarray(sh)}


def _group_mean_matrix(width):
    idx = np.arange(width) // HEAD_DIM
    return jnp.asarray((idx[:, None] == idx[None, :]).astype(np.float32) / HEAD_DIM, BF16)


def kernel(x_prompt, x_sample, c, cache_mla_ckv, cache_mla_kpe, cache_diff_k, cache_diff_v, cache_gqa_k, cache_gqa_v, cache_na_k, cache_na_v, c_ctx, norm_w, w_mod, b_mod, w_in_even, w_out_even, w_uq, q_norm_w, kv_norm_w, w_uk, w_uv, diff_lam, diff_subln_w, w_in_odd, w_out_odd, qk_norm_w, na_rpb, w_up, conv_w, conv_b, w_down):
    rope = _rope_tables()
    n_p = BATCH * SEQ
    n_s = DEC_BATCH * DEC_SEQ
    cvecs = jnp.concatenate([c_ctx[None, :], c, jnp.zeros((MOD_ROWS - 1 - DEC_BATCH, D_MODEL), F32)], axis=0)
    mod = _modulation(cvecs, w_mod, b_mod).reshape(DEPTH, MOD_ROWS, 6, D_MODEL)
    hp = x_prompt.reshape(n_p, D_MODEL)
    hs = x_sample.reshape(n_s, D_MODEL)
    even_states, odd_states = [], []
    w_up_b = w_up.astype(BF16)
    w_down_b = w_down.astype(BF16)
    for l in range(DEPTH):
        i = l // 2
        nw = [norm_w[l, k][None, :] for k in range(4)]
        if l % 2 == 0:
            lam_init = 0.8 - 0.6 * math.exp(-0.3 * l)
            wi = w_in_even[i]
            w_uq3 = w_uq[i].reshape(Q_LORA, H_A, QK_NOPE + QK_ROPE)
            w = {
                "w_in": jnp.concatenate([wi[:, :384], jnp.zeros((D_MODEL, QK_NOPE), F32), wi[:, 384:416],
                                         jnp.zeros((D_MODEL, MLA_SLOT - QK_NOPE - QK_ROPE), F32), wi[:, 416:]],
                                        axis=1).astype(BF16),
                "q_norm_w": q_norm_w[i][None, :],
                "kv_norm_w": kv_norm_w[i][None, :],
                "w_uq": jnp.pad(w_uq3, ((0, 0), (0, 0), (0, MLA_SLOT - QK_NOPE - QK_ROPE))
                                ).reshape(Q_LORA, H_A * MLA_SLOT).astype(BF16),
                "w_uk": jnp.pad(w_uk[i].reshape(KV_LORA, H_A, QK_NOPE), ((0, 0), (0, 0), (0, MLA_SLOT - QK_NOPE))
                                ).reshape(KV_LORA, H_A * MLA_SLOT).astype(BF16),
                "w_uv": w_uv[i].astype(BF16),
                "diff_lam": diff_lam[i],
                "diff_subln_w": diff_subln_w[i][None, :],
            }
            outs_p = _proj_even(hp, mod, l, False, nw[0], w, rope)
            outs_s = _proj_even(hs, mod, l, True, nw[0], w, rope)
            even_states.append(outs_p[6:])
            w_out = w_out_even[i].astype(BF16)
            h1p = _attn_even(outs_p[:6], False, lam_init, None, w, hp, w_out, mod, l, nw[1])
            caches = (cache_mla_ckv[:, i], cache_mla_kpe[:, i],
                      cache_diff_k[:, i].reshape(DEC_BATCH, PAST_LEN, 512),
                      cache_diff_v[:, i].reshape(DEC_BATCH, PAST_LEN, 512))
            h1s = _attn_even(outs_s, True, lam_init, caches, w, hs, w_out, mod, l, nw[1])
        else:
            q_w = jnp.tile(qk_norm_w[i, 0], H_C)[None, :]
            k_w = jnp.tile(qk_norm_w[i, 1], KV_C)[None, :]
            w = {"w_in": w_in_odd[i].astype(BF16), "q_w": q_w, "k_w": k_w,
                 "gq": _group_mean_matrix(512), "gk": _group_mean_matrix(128)}
            outs_p = _proj_odd(hp, mod, l, False, nw[0], w, rope)
            outs_s = _proj_odd(hs, mod, l, True, nw[0], w, rope)
            odd_states.append(outs_p[6:])
            w_out = w_out_odd[i].astype(BF16)
            h1p = _attn_odd_prompt(outs_p[:6], hp, w_out, mod, l, nw[1])
            qc, kc, vc, qn, kn, vn = outs_s
            o_c = _gqa_sample(qc, kc, vc, cache_gqa_k[:, i].reshape(DEC_BATCH, PAST_LEN, 128),
                              cache_gqa_v[:, i].reshape(DEC_BATCH, PAST_LEN, 128))
            o_d = _na_sample(qn, kn, vn, cache_na_k[:, i].reshape(DEC_BATCH, PAST_LEN, 512),
                             cache_na_v[:, i].reshape(DEC_BATCH, PAST_LEN, 512), _na_bias_table(na_rpb[i]))
            h1s = _post_attn([o_c, o_d], hs, w_out, mod, l, True, nw[1])
        wf = {"w_up": w_up_b, "conv_w": _pair_chunks(conv_w[l]),
              "conv_b": _pair_chunks(conv_b[l][None, :]), "w_down": w_down_b}
        hp = _ffn(h1p, wf, mod, l, False, nw[2], nw[3])
        hs = _ffn(h1s, wf, mod, l, True, nw[2], nw[3])

    def stack(states, k, shape):
        return jnp.stack([st[k].reshape((BATCH, SEQ) + shape) for st in states], axis=1)

    new_mla_ckv = stack(even_states, 0, (KV_LORA,))
    new_mla_kpe = stack(even_states, 1, (QK_ROPE,))
    new_diff_k = stack(even_states, 2, (H_B, 2 * DH_B))
    new_diff_v = stack(even_states, 3, (H_B, 2 * DH_B))
    new_gqa_k = stack(odd_states, 0, (KV_C, DH_C))
    new_gqa_v = stack(odd_states, 1, (KV_C, DH_C))
    new_na_k = stack(odd_states, 2, (H_D, DH_D))
    new_na_v = stack(odd_states, 3, (H_D, DH_D))
    return (hp.reshape(BATCH, SEQ, D_MODEL), hs.reshape(DEC_BATCH, DEC_SEQ, D_MODEL),
            new_mla_ckv, new_mla_kpe, new_diff_k, new_diff_v, new_gqa_k, new_gqa_v, new_na_k, new_na_v)
```
